```python
import functools
import jax, jax.numpy as jnp
from jax import lax
import numpy as np


D_MODEL = 2048
BATCH = 8
SEQ = 8192
DEPTH = 4

N_EVEN = (DEPTH + 1) // 2
N_ODD = DEPTH // 2

D_FF = 4096

GDN_HEADS = 8
GDN_DK = 128
GDN_DV = 128
GDN_QK_W = GDN_HEADS * GDN_DK
GDN_V_W = GDN_HEADS * GDN_DV
CONV_K = 4
CHUNK = 64
POOL_WINDOWS = (2, 4, 8, 16)
POOL_GROUPS = 4
POOL_W = D_MODEL // 2
POOL_GROUP_W = POOL_W // POOL_GROUPS
EVEN_IN = 2 * GDN_QK_W + 2 * GDN_V_W + 2 * GDN_HEADS + POOL_W
EVEN_MIX = GDN_V_W + POOL_W

MLA_HEADS = 16
Q_LORA = 512
KV_LORA = 512
NOPE = 128
ROPE = 64
V_HEAD = 128
QK_HEAD = NOPE + ROPE
ODD_IN = Q_LORA + KV_LORA + ROPE
ROPE_THETA = 10000.0
Q_BLOCK = 128

EPS = 1e-6

kernel_name = 'hybrid_gdn_pool_mla_macaron'


def rms_norm(x, gain):
    xf = x.astype(jnp.float32)
    y = xf * lax.rsqrt(jnp.mean(xf * xf, axis=-1, keepdims=True) + EPS)
    return (y * gain.astype(jnp.float32)).astype(x.dtype)


def l2_norm(x):
    xf = x.astype(jnp.float32)
    return xf * lax.rsqrt(jnp.sum(xf * xf, axis=-1, keepdims=True) + EPS)


def swiglu(h, w_gate, w_up, w_down):
    return (jax.nn.silu(h @ w_gate) * (h @ w_up)) @ w_down


def causal_dwconv(x, w):
    c = x.shape[-1]
    return lax.conv_general_dilated(x, w[:, None, :].astype(x.dtype), window_strides=(1,),
                                    padding=[(CONV_K - 1, 0)],
                                    dimension_numbers=('NWC', 'WIO', 'NWC'),
                                    feature_group_count=c)


def gated_delta_rule(q, k, v, g, beta):
    B, S, H, DK = q.shape
    N = S // CHUNK

    def chunks(t):
        return t.reshape(B, N, CHUNK, H, -1).transpose(0, 3, 1, 2, 4)

    q = chunks(q) * DK ** -0.5
    k = chunks(k)
    v = chunks(v)
    g = chunks(g[..., None])[..., 0]
    beta = chunks(beta[..., None])[..., 0]
    gc = jnp.cumsum(g, axis=-1)
    idx = jnp.arange(CHUNK)
    causal = idx[:, None] >= idx[None, :]
    strict = idx[:, None] > idx[None, :]
    decay = jnp.exp(jnp.where(causal, gc[..., :, None] - gc[..., None, :], -jnp.inf))
    kb = k * beta[..., None]
    vb = v * beta[..., None]
    m = jnp.where(strict, jnp.einsum('bhnid,bhnjd->bhnij', kb, k) * decay, 0.0)
    tri = m + jnp.eye(CHUNK, dtype=m.dtype)
    solve = functools.partial(lax.linalg.triangular_solve, left_side=True, lower=True,
                              unit_diagonal=True)
    u = solve(tri, vb)
    w = solve(tri, kb * jnp.exp(gc)[..., None])
    attn = jnp.einsum('bhnid,bhnjd->bhnij', q, k) * decay
    g_last = gc[..., -1]
    q_dec = q * jnp.exp(gc)[..., None]
    k_dec = k * jnp.exp(g_last[..., None] - gc)[..., None]

    def step(state, xs):
        qd, kd, uc, wc, ac, gl = xs
        v_new = uc - jnp.einsum('bhck,bhkv->bhcv', wc, state)
        out = jnp.einsum('bhck,bhkv->bhcv', qd, state) + jnp.einsum('bhij,bhjv->bhiv', ac, v_new)
        state = state * jnp.exp(gl)[..., None, None] + jnp.einsum('bhck,bhcv->bhkv', kd, v_new)
        return state, out

    xs = (jnp.moveaxis(q_dec, 2, 0), jnp.moveaxis(k_dec, 2, 0), jnp.moveaxis(u, 2, 0),
          jnp.moveaxis(w, 2, 0), jnp.moveaxis(attn, 2, 0), jnp.moveaxis(g_last, 2, 0))
    state0 = jnp.zeros((B, H, DK, v.shape[-1]), jnp.float32)
    _, o = lax.scan(step, state0, xs)
    return o.transpose(1, 0, 3, 2, 4).reshape(B, S, H, -1)


def multiscale_pool(u, pool_w, pool_scale):
    B, S, C = u.shape
    ug = u.astype(jnp.float32).reshape(B, S, POOL_GROUPS, POOL_GROUP_W)
    cs = jnp.cumsum(ug, axis=1)
    t = jnp.arange(S)
    pooled = []
    for gi, win in enumerate(POOL_WINDOWS):
        c = cs[:, :, gi]
        prev = jnp.pad(c, ((0, 0), (win, 0), (0, 0)))[:, :S]
        cnt = jnp.minimum(t + 1, win).astype(jnp.float32)[None, :, None]
        pooled.append((c - prev) / cnt)
    diff = (jnp.stack(pooled, axis=2) - ug).astype(u.dtype)
    y = jnp.einsum('bsgc,gcd->bsgd', diff, pool_w)
    return y.reshape(B, S, C) * pool_scale


def delta_pool_mixer(h, w_in, conv_w, a_log, dt_bias, out_norm, pool_w, pool_scale, w_out):
    B, S, _ = h.shape
    proj = h @ w_in
    o1 = 2 * GDN_QK_W + GDN_V_W
    o2 = o1 + GDN_V_W
    o3 = o2 + GDN_HEADS
    o4 = o3 + GDN_HEADS
    qkv, z, a, b, u = jnp.split(proj, [o1, o2, o3, o4], axis=-1)
    qkv = jax.nn.silu(causal_dwconv(qkv, conv_w))
    q, k, v = jnp.split(qkv, [GDN_QK_W, 2 * GDN_QK_W], axis=-1)
    q = l2_norm(q.reshape(B, S, GDN_HEADS, GDN_DK))
    k = l2_norm(k.reshape(B, S, GDN_HEADS, GDN_DK))
    v = v.reshape(B, S, GDN_HEADS, GDN_DV).astype(jnp.float32)
    beta = jax.nn.sigmoid(b.astype(jnp.float32))
    g = -jnp.exp(a_log.astype(jnp.float32)) * jax.nn.softplus(a.astype(jnp.float32) + dt_bias.astype(jnp.float32))
    o = gated_delta_rule(q, k, v, g, beta)
    zf = z.reshape(B, S, GDN_HEADS, GDN_DV).astype(jnp.float32)
    o = (rms_norm(o, out_norm) * jax.nn.silu(zf)).reshape(B, S, GDN_V_W).astype(h.dtype)
    p = multiscale_pool(u, pool_w, pool_scale)
    return jnp.concatenate([o, p], axis=-1) @ w_out


def seg_head_norm(t, gain):
    tf = t.astype(jnp.float32)
    nope, pe = tf[..., :NOPE], tf[..., NOPE:]
    nope = nope * lax.rsqrt(jnp.mean(nope * nope, axis=-1, keepdims=True) + EPS)
    pe = pe * lax.rsqrt(jnp.mean(pe * pe, axis=-1, keepdims=True) + EPS)
    return (jnp.concatenate([nope, pe], axis=-1) * gain.astype(jnp.float32)).astype(t.dtype)


def apply_rope_tail(t, cos, sin):
    nope, pe = t[..., :NOPE], t[..., NOPE:].astype(jnp.float32)
    x1, x2 = pe[..., :ROPE // 2], pe[..., ROPE // 2:]
    rot = jnp.concatenate([x1 * cos - x2 * sin, x2 * cos + x1 * sin], axis=-1)
    return jnp.concatenate([nope, rot.astype(t.dtype)], axis=-1)


def causal_block_attention(q, k, v):
    B, S, H, Dqk = q.shape
    nb = S // Q_BLOCK
    qb = q.reshape(B, nb, Q_BLOCK, H, Dqk).transpose(1, 0, 2, 3, 4)
    starts = jnp.arange(nb, dtype=jnp.int32) * Q_BLOCK
    kpos = jnp.arange(S, dtype=jnp.int32)
    scale = Dqk ** -0.5
    neg = jnp.finfo(jnp.float32).min

    def one_block(args):
        qi, s0 = args
        s = jnp.einsum('bqhd,bkhd->bhqk', qi, k, preferred_element_type=jnp.float32) * scale
        qpos = s0 + jnp.arange(Q_BLOCK, dtype=jnp.int32)
        s = jnp.where(qpos[:, None] >= kpos[None, :], s, neg)
        p = jax.nn.softmax(s, axis=-1)
        return jnp.einsum('bhqk,bkhd->bqhd', p.astype(v.dtype), v)

    o = lax.map(one_block, (qb, starts))
    return o.transpose(1, 0, 2, 3, 4).reshape(B, S, H, v.shape[-1])


def mla_mixer(h, positions, w_in, q_norm, kv_norm, w_q_up, w_kv_up, q_head_norm, k_head_norm, w_out):
    B, S, _ = h.shape
    proj = h @ w_in
    q_lat, kv_lat, k_pe = jnp.split(proj, [Q_LORA, Q_LORA + KV_LORA], axis=-1)
    q = (rms_norm(q_lat, q_norm) @ w_q_up).reshape(B, S, MLA_HEADS, QK_HEAD)
    kv = (rms_norm(kv_lat, kv_norm) @ w_kv_up).reshape(B, S, MLA_HEADS, NOPE + V_HEAD)
    k_nope, v = kv[..., :NOPE], kv[..., NOPE:]
    k_pe = jnp.broadcast_to(k_pe[:, :, None, :], (B, S, MLA_HEADS, ROPE))
    k = jnp.concatenate([k_nope, k_pe], axis=-1)
    q = seg_head_norm(q, q_head_norm)
    k = seg_head_norm(k, k_head_norm)
    inv_freq = ROPE_THETA ** (-jnp.arange(0, ROPE, 2, dtype=jnp.float32) / ROPE)
    ang = positions.astype(jnp.float32)[..., None] * inv_freq
    cos, sin = jnp.cos(ang)[:, :, None, :], jnp.sin(ang)[:, :, None, :]
    q = apply_rope_tail(q, cos, sin)
    k = apply_rope_tail(k, cos, sin)
    o = causal_block_attention(q, k, v)
    return o.reshape(B, S, MLA_HEADS * V_HEAD) @ w_out


def _fwd_setup_inputs(seed: int = 0) -> dict:
    key = jax.random.key(seed)
    ks = jax.random.split(key, 32)

    def dense(k, shape, fan_in):
        return jax.random.normal(k, shape, jnp.float32) * fan_in ** -0.5

    def gain(k, shape):
        return 1.0 + 0.02 * jax.random.normal(k, shape, jnp.float32)

    x = jax.random.normal(ks[0], (BATCH, SEQ, D_MODEL), jnp.float32)
    positions = jnp.broadcast_to(jnp.arange(SEQ, dtype=jnp.int32)[None, :], (BATCH, SEQ))
    dt = jnp.exp(jax.random.uniform(ks[14], (N_EVEN, GDN_HEADS), jnp.float32,
                                    np.log(1e-3), np.log(1e-1)))
    return {
        'x': x,
        'positions': positions,
        'ffn1_norm': gain(ks[1], (DEPTH, D_MODEL)),
        'ffn1_w_gate': dense(ks[2], (DEPTH, D_MODEL, D_FF), D_MODEL),
        'ffn1_w_up': dense(ks[3], (DEPTH, D_MODEL, D_FF), D_MODEL),
        'ffn1_w_down': dense(ks[4], (DEPTH, D_FF, D_MODEL), D_FF),
        'mix_norm': gain(ks[5], (DEPTH, D_MODEL)),
        'ffn2_norm': gain(ks[6], (DEPTH, D_MODEL)),
        'ffn2_w_gate': dense(ks[7], (DEPTH, D_MODEL, D_FF), D_MODEL),
        'ffn2_w_up': dense(ks[8], (DEPTH, D_MODEL, D_FF), D_MODEL),
        'ffn2_w_down': dense(ks[9], (DEPTH, D_FF, D_MODEL), D_FF),
        'hyb_w_in': dense(ks[10], (N_EVEN, D_MODEL, EVEN_IN), D_MODEL),
        'gdn_conv': dense(ks[11], (N_EVEN, CONV_K, 2 * GDN_QK_W + GDN_V_W), CONV_K),
        'gdn_a_log': jnp.log(jax.random.uniform(ks[12], (N_EVEN, GDN_HEADS), jnp.float32, 1.0, 16.0)),
        'gdn_dt_bias': dt + jnp.log(-jnp.expm1(-dt)),
        'gdn_out_norm': gain(ks[13], (N_EVEN, GDN_DV)),
        'pool_w': dense(ks[15], (N_EVEN, POOL_GROUPS, POOL_GROUP_W, POOL_GROUP_W), POOL_GROUP_W),
        'pool_scale': gain(ks[16], (N_EVEN, POOL_W)),
        'hyb_w_out': dense(ks[17], (N_EVEN, EVEN_MIX, D_MODEL), EVEN_MIX),
        'mla_w_in': dense(ks[18], (N_ODD, D_MODEL, ODD_IN), D_MODEL),
        'mla_q_norm': gain(ks[19], (N_ODD, Q_LORA)),
        'mla_kv_norm': gain(ks[20], (N_ODD, KV_LORA)),
        'mla_w_q_up': dense(ks[21], (N_ODD, Q_LORA, MLA_HEADS * QK_HEAD), Q_LORA),
        'mla_w_kv_up': dense(ks[22], (N_ODD, KV_LORA, MLA_HEADS * (NOPE + V_HEAD)), KV_LORA),
        'mla_q_head_norm': gain(ks[23], (N_ODD, QK_HEAD)),
        'mla_k_head_norm': gain(ks[24], (N_ODD, QK_HEAD)),
        'mla_w_out': dense(ks[25], (N_ODD, MLA_HEADS * V_HEAD, D_MODEL), MLA_HEADS * V_HEAD),
    }


def _fwd_reference(x, positions, ffn1_norm, ffn1_w_gate, ffn1_w_up, ffn1_w_down, mix_norm,
              ffn2_norm, ffn2_w_gate, ffn2_w_up, ffn2_w_down,
              hyb_w_in, gdn_conv, gdn_a_log, gdn_dt_bias, gdn_out_norm, pool_w, pool_scale, hyb_w_out,
              mla_w_in, mla_q_norm, mla_kv_norm, mla_w_q_up, mla_w_kv_up,
              mla_q_head_norm, mla_k_head_norm, mla_w_out):
    for layer in range(DEPTH):
        x = x + 0.5 * swiglu(rms_norm(x, ffn1_norm[layer]), ffn1_w_gate[layer], ffn1_w_up[layer], ffn1_w_down[layer])
        h = rms_norm(x, mix_norm[layer])
        i = layer // 2
        if layer % 2 == 0:
            x = x + delta_pool_mixer(h, hyb_w_in[i], gdn_conv[i], gdn_a_log[i], gdn_dt_bias[i],
                                     gdn_out_norm[i], pool_w[i], pool_scale[i], hyb_w_out[i])
        else:
            x = x + mla_mixer(h, positions, mla_w_in[i], mla_q_norm[i], mla_kv_norm[i], mla_w_q_up[i],
                              mla_w_kv_up[i], mla_q_head_norm[i], mla_k_head_norm[i], mla_w_out[i])
        x = x + 0.5 * swiglu(rms_norm(x, ffn2_norm[layer]), ffn2_w_gate[layer], ffn2_w_up[layer], ffn2_w_down[layer])
    return x


import jax as _jax
import jax.numpy as _jnp

TWIN_FORMAT = 'train_step'
FWD_PARAMS = ['x', 'positions', 'ffn1_norm', 'ffn1_w_gate', 'ffn1_w_up', 'ffn1_w_down', 'mix_norm', 'ffn2_norm', 'ffn2_w_gate', 'ffn2_w_up', 'ffn2_w_down', 'hyb_w_in', 'gdn_conv', 'gdn_a_log', 'gdn_dt_bias', 'gdn_out_norm', 'pool_w', 'pool_scale', 'hyb_w_out', 'mla_w_in', 'mla_q_norm', 'mla_kv_norm', 'mla_w_q_up', 'mla_w_kv_up', 'mla_q_head_norm', 'mla_k_head_norm', 'mla_w_out']
TWIN_WEIGHTS = ['ffn1_norm', 'ffn1_w_gate', 'ffn1_w_up', 'ffn1_w_down', 'mix_norm', 'ffn2_norm', 'ffn2_w_gate', 'ffn2_w_up', 'ffn2_w_down', 'hyb_w_in', 'gdn_conv', 'gdn_a_log', 'gdn_dt_bias', 'gdn_out_norm', 'pool_w', 'pool_scale', 'hyb_w_out', 'mla_w_in', 'mla_q_norm', 'mla_kv_norm', 'mla_w_q_up', 'mla_w_kv_up', 'mla_q_head_norm', 'mla_k_head_norm', 'mla_w_out']
TWIN_DIFF_INPUT = 'x'
TWIN_INPUTS = ['x', 'positions', 'ffn1_norm', 'ffn1_w_gate', 'ffn1_w_up', 'ffn1_w_down', 'mix_norm', 'ffn2_norm', 'ffn2_w_gate', 'ffn2_w_up', 'ffn2_w_down', 'hyb_w_in', 'gdn_conv', 'gdn_a_log', 'gdn_dt_bias', 'gdn_out_norm', 'pool_w', 'pool_scale', 'hyb_w_out', 'mla_w_in', 'mla_q_norm', 'mla_kv_norm', 'mla_w_q_up', 'mla_w_kv_up', 'mla_q_head_norm', 'mla_k_head_norm', 'mla_w_out', 'loss_target', 'm_ffn1_norm', 'm_ffn1_w_gate', 'm_ffn1_w_up', 'm_ffn1_w_down', 'm_mix_norm', 'm_ffn2_norm', 'm_ffn2_w_gate', 'm_ffn2_w_up', 'm_ffn2_w_down', 'm_hyb_w_in', 'm_gdn_conv', 'm_gdn_a_log', 'm_gdn_dt_bias', 'm_gdn_out_norm', 'm_pool_w', 'm_pool_scale', 'm_hyb_w_out', 'm_mla_w_in', 'm_mla_q_norm', 'm_mla_kv_norm', 'm_mla_w_q_up', 'm_mla_w_kv_up', 'm_mla_q_head_norm', 'm_mla_k_head_norm', 'm_mla_w_out', 'v_ffn1_norm', 'v_ffn1_w_gate', 'v_ffn1_w_up', 'v_ffn1_w_down', 'v_mix_norm', 'v_ffn2_norm', 'v_ffn2_w_gate', 'v_ffn2_w_up', 'v_ffn2_w_down', 'v_hyb_w_in', 'v_gdn_conv', 'v_gdn_a_log', 'v_gdn_dt_bias', 'v_gdn_out_norm', 'v_pool_w', 'v_pool_scale', 'v_hyb_w_out', 'v_mla_w_in', 'v_mla_q_norm', 'v_mla_kv_norm', 'v_mla_w_q_up', 'v_mla_w_kv_up', 'v_mla_q_head_norm', 'v_mla_k_head_norm', 'v_mla_w_out']
TWIN_OUTPUTS = ['loss', 'grad_x', 'grad_ffn1_norm', 'grad_ffn1_w_gate', 'grad_ffn1_w_up', 'grad_ffn1_w_down', 'grad_mix_norm', 'grad_ffn2_norm', 'grad_ffn2_w_gate', 'grad_ffn2_w_up', 'grad_ffn2_w_down', 'grad_hyb_w_in', 'grad_gdn_conv', 'grad_gdn_a_log', 'grad_gdn_dt_bias', 'grad_gdn_out_norm', 'grad_pool_w', 'grad_pool_scale', 'grad_hyb_w_out', 'grad_mla_w_in', 'grad_mla_q_norm', 'grad_mla_kv_norm', 'grad_mla_w_q_up', 'grad_mla_w_kv_up', 'grad_mla_q_head_norm', 'grad_mla_k_head_norm', 'grad_mla_w_out', 'delta_ffn1_norm', 'delta_ffn1_w_gate', 'delta_ffn1_w_up', 'delta_ffn1_w_down', 'delta_mix_norm', 'delta_ffn2_norm', 'delta_ffn2_w_gate', 'delta_ffn2_w_up', 'delta_ffn2_w_down', 'delta_hyb_w_in', 'delta_gdn_conv', 'delta_gdn_a_log', 'delta_gdn_dt_bias', 'delta_gdn_out_norm', 'delta_pool_w', 'delta_pool_scale', 'delta_hyb_w_out', 'delta_mla_w_in', 'delta_mla_q_norm', 'delta_mla_kv_norm', 'delta_mla_w_q_up', 'delta_mla_w_kv_up', 'delta_mla_q_head_norm', 'delta_mla_k_head_norm', 'delta_mla_w_out', 'new_m_ffn1_norm', 'new_m_ffn1_w_gate', 'new_m_ffn1_w_up', 'new_m_ffn1_w_down', 'new_m_mix_norm', 'new_m_ffn2_norm', 'new_m_ffn2_w_gate', 'new_m_ffn2_w_up', 'new_m_ffn2_w_down', 'new_m_hyb_w_in', 'new_m_gdn_conv', 'new_m_gdn_a_log', 'new_m_gdn_dt_bias', 'new_m_gdn_out_norm', 'new_m_pool_w', 'new_m_pool_scale', 'new_m_hyb_w_out', 'new_m_mla_w_in', 'new_m_mla_q_norm', 'new_m_mla_kv_norm', 'new_m_mla_w_q_up', 'new_m_mla_w_kv_up', 'new_m_mla_q_head_norm', 'new_m_mla_k_head_norm', 'new_m_mla_w_out', 'new_v_ffn1_norm', 'new_v_ffn1_w_gate', 'new_v_ffn1_w_up', 'new_v_ffn1_w_down', 'new_v_mix_norm', 'new_v_ffn2_norm', 'new_v_ffn2_w_gate', 'new_v_ffn2_w_up', 'new_v_ffn2_w_down', 'new_v_hyb_w_in', 'new_v_gdn_conv', 'new_v_gdn_a_log', 'new_v_gdn_dt_bias', 'new_v_gdn_out_norm', 'new_v_pool_w', 'new_v_pool_scale', 'new_v_hyb_w_out', 'new_v_mla_w_in', 'new_v_mla_q_norm', 'new_v_mla_kv_norm', 'new_v_mla_w_q_up', 'new_v_mla_w_kv_up', 'new_v_mla_q_head_norm', 'new_v_mla_k_head_norm', 'new_v_mla_w_out']
TWIN_LEAF_KINDS = {'loss': 'loss', 'grad_x': 'grad_x', 'grad_ffn1_norm': 'grad_w', 'grad_ffn1_w_gate': 'grad_w', 'grad_ffn1_w_up': 'grad_w', 'grad_ffn1_w_down': 'grad_w', 'grad_mix_norm': 'grad_w', 'grad_ffn2_norm': 'grad_w', 'grad_ffn2_w_gate': 'grad_w', 'grad_ffn2_w_up': 'grad_w', 'grad_ffn2_w_down': 'grad_w', 'grad_hyb_w_in': 'grad_w', 'grad_gdn_conv': 'grad_w', 'grad_gdn_a_log': 'grad_w', 'grad_gdn_dt_bias': 'grad_w', 'grad_gdn_out_norm': 'grad_w', 'grad_pool_w': 'grad_w', 'grad_pool_scale': 'grad_w', 'grad_hyb_w_out': 'grad_w', 'grad_mla_w_in': 'grad_w', 'grad_mla_q_norm': 'grad_w', 'grad_mla_kv_norm': 'grad_w', 'grad_mla_w_q_up': 'grad_w', 'grad_mla_w_kv_up': 'grad_w', 'grad_mla_q_head_norm': 'grad_w', 'grad_mla_k_head_norm': 'grad_w', 'grad_mla_w_out': 'grad_w', 'delta_ffn1_norm': 'delta_w', 'delta_ffn1_w_gate': 'delta_w', 'delta_ffn1_w_up': 'delta_w', 'delta_ffn1_w_down': 'delta_w', 'delta_mix_norm': 'delta_w', 'delta_ffn2_norm': 'delta_w', 'delta_ffn2_w_gate': 'delta_w', 'delta_ffn2_w_up': 'delta_w', 'delta_ffn2_w_down': 'delta_w', 'delta_hyb_w_in': 'delta_w', 'delta_gdn_conv': 'delta_w', 'delta_gdn_a_log': 'delta_w', 'delta_gdn_dt_bias': 'delta_w', 'delta_gdn_out_norm': 'delta_w', 'delta_pool_w': 'delta_w', 'delta_pool_scale': 'delta_w', 'delta_hyb_w_out': 'delta_w', 'delta_mla_w_in': 'delta_w', 'delta_mla_q_norm': 'delta_w', 'delta_mla_kv_norm': 'delta_w', 'delta_mla_w_q_up': 'delta_w', 'delta_mla_w_kv_up': 'delta_w', 'delta_mla_q_head_norm': 'delta_w', 'delta_mla_k_head_norm': 'delta_w', 'delta_mla_w_out': 'delta_w', 'new_m_ffn1_norm': 'new_m', 'new_m_ffn1_w_gate': 'new_m', 'new_m_ffn1_w_up': 'new_m', 'new_m_ffn1_w_down': 'new_m', 'new_m_mix_norm': 'new_m', 'new_m_ffn2_norm': 'new_m', 'new_m_ffn2_w_gate': 'new_m', 'new_m_ffn2_w_up': 'new_m', 'new_m_ffn2_w_down': 'new_m', 'new_m_hyb_w_in': 'new_m', 'new_m_gdn_conv': 'new_m', 'new_m_gdn_a_log': 'new_m', 'new_m_gdn_dt_bias': 'new_m', 'new_m_gdn_out_norm': 'new_m', 'new_m_pool_w': 'new_m', 'new_m_pool_scale': 'new_m', 'new_m_hyb_w_out': 'new_m', 'new_m_mla_w_in': 'new_m', 'new_m_mla_q_norm': 'new_m', 'new_m_mla_kv_norm': 'new_m', 'new_m_mla_w_q_up': 'new_m', 'new_m_mla_w_kv_up': 'new_m', 'new_m_mla_q_head_norm': 'new_m', 'new_m_mla_k_head_norm': 'new_m', 'new_m_mla_w_out': 'new_m', 'new_v_ffn1_norm': 'new_v', 'new_v_ffn1_w_gate': 'new_v', 'new_v_ffn1_w_up': 'new_v', 'new_v_ffn1_w_down': 'new_v', 'new_v_mix_norm': 'new_v', 'new_v_ffn2_norm': 'new_v', 'new_v_ffn2_w_gate': 'new_v', 'new_v_ffn2_w_up': 'new_v', 'new_v_ffn2_w_down': 'new_v', 'new_v_hyb_w_in': 'new_v', 'new_v_gdn_conv': 'new_v', 'new_v_gdn_a_log': 'new_v', 'new_v_gdn_dt_bias': 'new_v', 'new_v_gdn_out_norm': 'new_v', 'new_v_pool_w': 'new_v', 'new_v_pool_scale': 'new_v', 'new_v_hyb_w_out': 'new_v', 'new_v_mla_w_in': 'new_v', 'new_v_mla_q_norm': 'new_v', 'new_v_mla_kv_norm': 'new_v', 'new_v_mla_w_q_up': 'new_v', 'new_v_mla_w_kv_up': 'new_v', 'new_v_mla_q_head_norm': 'new_v', 'new_v_mla_k_head_norm': 'new_v', 'new_v_mla_w_out': 'new_v'}


def _forward(args):
    return _fwd_reference(*[args[k] for k in FWD_PARAMS])


def _output_shape():
    def fwd():
        inp = _fwd_setup_inputs(0)
        return _fwd_reference(*[inp[k] for k in FWD_PARAMS])
    out = _jax.eval_shape(fwd)
    return out.shape, out.dtype

N_MICROBATCH = 1
ADAM_LR = 0.001
ADAM_B1 = 0.9
ADAM_B2 = 0.999
ADAM_EPS = 1e-08
ADAM_WD = 0.01
ADAM_STEP = 10
PER_EXAMPLE_BATCH_AXIS = {'x': 0, 'positions': 0, 'loss_target': 0}
SHARED_INPUTS = []
_WEIGHT_DTYPES = {'ffn1_norm': _jnp.float32, 'ffn1_w_gate': _jnp.float32, 'ffn1_w_up': _jnp.float32, 'ffn1_w_down': _jnp.float32, 'mix_norm': _jnp.float32, 'ffn2_norm': _jnp.float32, 'ffn2_w_gate': _jnp.float32, 'ffn2_w_up': _jnp.float32, 'ffn2_w_down': _jnp.float32, 'hyb_w_in': _jnp.float32, 'gdn_conv': _jnp.float32, 'gdn_a_log': _jnp.float32, 'gdn_dt_bias': _jnp.float32, 'gdn_out_norm': _jnp.float32, 'pool_w': _jnp.float32, 'pool_scale': _jnp.float32, 'hyb_w_out': _jnp.float32, 'mla_w_in': _jnp.float32, 'mla_q_norm': _jnp.float32, 'mla_kv_norm': _jnp.float32, 'mla_w_q_up': _jnp.float32, 'mla_w_kv_up': _jnp.float32, 'mla_q_head_norm': _jnp.float32, 'mla_k_head_norm': _jnp.float32, 'mla_w_out': _jnp.float32}
MOMENT_SCALE = {'ffn1_norm': 6.107592e+00, 'ffn1_w_gate': 1.116006e-01, 'ffn1_w_up': 1.201211e-01, 'ffn1_w_down': 1.691058e-01, 'mix_norm': 1.366729e+01, 'ffn2_norm': 6.107553e+00, 'ffn2_w_gate': 1.011689e-01, 'ffn2_w_up': 1.101522e-01, 'ffn2_w_down': 1.532276e-01, 'hyb_w_in': 5.156651e-01, 'gdn_conv': 6.257011e-01, 'gdn_a_log': 3.320361e+01, 'gdn_dt_bias': 3.177926e+01, 'gdn_out_norm': 9.422884e+01, 'pool_w': 1.652188e+00, 'pool_scale': 2.512553e+01, 'hyb_w_out': 1.218455e+00, 'mla_w_in': 7.047304e-01, 'mla_q_norm': 1.377181e-01, 'mla_kv_norm': 1.404283e+00, 'mla_w_q_up': 5.877228e-02, 'mla_w_kv_up': 3.187217e-01, 'mla_q_head_norm': 1.614217e+00, 'mla_k_head_norm': 1.611908e+00, 'mla_w_out': 4.588535e-01}


def _to_microbatches(a, axis):
    t = _jnp.moveaxis(a, axis, 0)
    t = t.reshape((N_MICROBATCH, t.shape[0] // N_MICROBATCH) + t.shape[1:])
    return _jnp.moveaxis(t, 1, axis + 1)


def setup_inputs(seed: int = 0) -> dict:
    inp = _fwd_setup_inputs(seed)
    key = _jax.random.fold_in(_jax.random.key(seed), 7919)
    shape, _ = _output_shape()
    out = dict(inp)
    out["loss_target"] = _jax.random.normal(_jax.random.fold_in(key, 0), shape, _jnp.float32)
    for i, name in enumerate(TWIN_WEIGHTS):
        w = inp[name].astype(_jnp.float32)
        if MOMENT_SCALE is None:
            s = _jnp.sqrt(_jnp.mean(_jnp.square(w)) + 1e-30)
        else:
            s = MOMENT_SCALE[name]
        km, kv = _jax.random.split(_jax.random.fold_in(key, i + 1))
        out[name] = w
        out["m_" + name] = s * _jax.random.normal(km, w.shape, _jnp.float32)
        out["v_" + name] = (s * s) * _jax.random.uniform(kv, w.shape, _jnp.float32, 0.5, 1.5)
    if N_MICROBATCH > 1:
        for name, axis in PER_EXAMPLE_BATCH_AXIS.items():
            out[name] = _to_microbatches(out[name], axis)
    return {'x': out['x'], 'positions': out['positions'], 'ffn1_norm': out['ffn1_norm'], 'ffn1_w_gate': out['ffn1_w_gate'], 'ffn1_w_up': out['ffn1_w_up'], 'ffn1_w_down': out['ffn1_w_down'], 'mix_norm': out['mix_norm'], 'ffn2_norm': out['ffn2_norm'], 'ffn2_w_gate': out['ffn2_w_gate'], 'ffn2_w_up': out['ffn2_w_up'], 'ffn2_w_down': out['ffn2_w_down'], 'hyb_w_in': out['hyb_w_in'], 'gdn_conv': out['gdn_conv'], 'gdn_a_log': out['gdn_a_log'], 'gdn_dt_bias': out['gdn_dt_bias'], 'gdn_out_norm': out['gdn_out_norm'], 'pool_w': out['pool_w'], 'pool_scale': out['pool_scale'], 'hyb_w_out': out['hyb_w_out'], 'mla_w_in': out['mla_w_in'], 'mla_q_norm': out['mla_q_norm'], 'mla_kv_norm': out['mla_kv_norm'], 'mla_w_q_up': out['mla_w_q_up'], 'mla_w_kv_up': out['mla_w_kv_up'], 'mla_q_head_norm': out['mla_q_head_norm'], 'mla_k_head_norm': out['mla_k_head_norm'], 'mla_w_out': out['mla_w_out'], 'loss_target': out['loss_target'], 'm_ffn1_norm': out['m_ffn1_norm'], 'm_ffn1_w_gate': out['m_ffn1_w_gate'], 'm_ffn1_w_up': out['m_ffn1_w_up'], 'm_ffn1_w_down': out['m_ffn1_w_down'], 'm_mix_norm': out['m_mix_norm'], 'm_ffn2_norm': out['m_ffn2_norm'], 'm_ffn2_w_gate': out['m_ffn2_w_gate'], 'm_ffn2_w_up': out['m_ffn2_w_up'], 'm_ffn2_w_down': out['m_ffn2_w_down'], 'm_hyb_w_in': out['m_hyb_w_in'], 'm_gdn_conv': out['m_gdn_conv'], 'm_gdn_a_log': out['m_gdn_a_log'], 'm_gdn_dt_bias': out['m_gdn_dt_bias'], 'm_gdn_out_norm': out['m_gdn_out_norm'], 'm_pool_w': out['m_pool_w'], 'm_pool_scale': out['m_pool_scale'], 'm_hyb_w_out': out['m_hyb_w_out'], 'm_mla_w_in': out['m_mla_w_in'], 'm_mla_q_norm': out['m_mla_q_norm'], 'm_mla_kv_norm': out['m_mla_kv_norm'], 'm_mla_w_q_up': out['m_mla_w_q_up'], 'm_mla_w_kv_up': out['m_mla_w_kv_up'], 'm_mla_q_head_norm': out['m_mla_q_head_norm'], 'm_mla_k_head_norm': out['m_mla_k_head_norm'], 'm_mla_w_out': out['m_mla_w_out'], 'v_ffn1_norm': out['v_ffn1_norm'], 'v_ffn1_w_gate': out['v_ffn1_w_gate'], 'v_ffn1_w_up': out['v_ffn1_w_up'], 'v_ffn1_w_down': out['v_ffn1_w_down'], 'v_mix_norm': out['v_mix_norm'], 'v_ffn2_norm': out['v_ffn2_norm'], 'v_ffn2_w_gate': out['v_ffn2_w_gate'], 'v_ffn2_w_up': out['v_ffn2_w_up'], 'v_ffn2_w_down': out['v_ffn2_w_down'], 'v_hyb_w_in': out['v_hyb_w_in'], 'v_gdn_conv': out['v_gdn_conv'], 'v_gdn_a_log': out['v_gdn_a_log'], 'v_gdn_dt_bias': out['v_gdn_dt_bias'], 'v_gdn_out_norm': out['v_gdn_out_norm'], 'v_pool_w': out['v_pool_w'], 'v_pool_scale': out['v_pool_scale'], 'v_hyb_w_out': out['v_hyb_w_out'], 'v_mla_w_in': out['v_mla_w_in'], 'v_mla_q_norm': out['v_mla_q_norm'], 'v_mla_kv_norm': out['v_mla_kv_norm'], 'v_mla_w_q_up': out['v_mla_w_q_up'], 'v_mla_w_kv_up': out['v_mla_w_kv_up'], 'v_mla_q_head_norm': out['v_mla_q_head_norm'], 'v_mla_k_head_norm': out['v_mla_k_head_norm'], 'v_mla_w_out': out['v_mla_w_out']}


def _loss(weights, diff, rest, loss_target):
    with _jax.named_scope("forward"):
        args = {**rest, TWIN_DIFF_INPUT: diff, **{k: w.astype(_WEIGHT_DTYPES[k]) for k, w in weights.items()}}
        y = _forward(args)
    with _jax.named_scope("loss_head"):
        err = _jnp.square(y.astype(_jnp.float32) - loss_target)
        return 0.5 * _jnp.sum(_jnp.mean(err, axis=-1)) if err.ndim else 0.5 * err


def _adamw(w, g, m, v):
    m = ADAM_B1 * m + (1.0 - ADAM_B1) * g
    v = ADAM_B2 * v + (1.0 - ADAM_B2) * _jnp.square(g)
    m_hat = m / (1.0 - ADAM_B1 ** ADAM_STEP)
    v_hat = v / (1.0 - ADAM_B2 ** ADAM_STEP)
    delta = -ADAM_LR * (m_hat / (_jnp.sqrt(v_hat) + ADAM_EPS) + ADAM_WD * w)
    return delta, m, v


def reference(x, positions, ffn1_norm, ffn1_w_gate, ffn1_w_up, ffn1_w_down, mix_norm, ffn2_norm, ffn2_w_gate, ffn2_w_up, ffn2_w_down, hyb_w_in, gdn_conv, gdn_a_log, gdn_dt_bias, gdn_out_norm, pool_w, pool_scale, hyb_w_out, mla_w_in, mla_q_norm, mla_kv_norm, mla_w_q_up, mla_w_kv_up, mla_q_head_norm, mla_k_head_norm, mla_w_out, loss_target, m_ffn1_norm, m_ffn1_w_gate, m_ffn1_w_up, m_ffn1_w_down, m_mix_norm, m_ffn2_norm, m_ffn2_w_gate, m_ffn2_w_up, m_ffn2_w_down, m_hyb_w_in, m_gdn_conv, m_gdn_a_log, m_gdn_dt_bias, m_gdn_out_norm, m_pool_w, m_pool_scale, m_hyb_w_out, m_mla_w_in, m_mla_q_norm, m_mla_kv_norm, m_mla_w_q_up, m_mla_w_kv_up, m_mla_q_head_norm, m_mla_k_head_norm, m_mla_w_out, v_ffn1_norm, v_ffn1_w_gate, v_ffn1_w_up, v_ffn1_w_down, v_mix_norm, v_ffn2_norm, v_ffn2_w_gate, v_ffn2_w_up, v_ffn2_w_down, v_hyb_w_in, v_gdn_conv, v_gdn_a_log, v_gdn_dt_bias, v_gdn_out_norm, v_pool_w, v_pool_scale, v_hyb_w_out, v_mla_w_in, v_mla_q_norm, v_mla_kv_norm, v_mla_w_q_up, v_mla_w_kv_up, v_mla_q_head_norm, v_mla_k_head_norm, v_mla_w_out):
    given = dict(x=x, positions=positions, ffn1_norm=ffn1_norm, ffn1_w_gate=ffn1_w_gate, ffn1_w_up=ffn1_w_up, ffn1_w_down=ffn1_w_down, mix_norm=mix_norm, ffn2_norm=ffn2_norm, ffn2_w_gate=ffn2_w_gate, ffn2_w_up=ffn2_w_up, ffn2_w_down=ffn2_w_down, hyb_w_in=hyb_w_in, gdn_conv=gdn_conv, gdn_a_log=gdn_a_log, gdn_dt_bias=gdn_dt_bias, gdn_out_norm=gdn_out_norm, pool_w=pool_w, pool_scale=pool_scale, hyb_w_out=hyb_w_out, mla_w_in=mla_w_in, mla_q_norm=mla_q_norm, mla_kv_norm=mla_kv_norm, mla_w_q_up=mla_w_q_up, mla_w_kv_up=mla_w_kv_up, mla_q_head_norm=mla_q_head_norm, mla_k_head_norm=mla_k_head_norm, mla_w_out=mla_w_out, loss_target=loss_target, m_ffn1_norm=m_ffn1_norm, m_ffn1_w_gate=m_ffn1_w_gate, m_ffn1_w_up=m_ffn1_w_up, m_ffn1_w_down=m_ffn1_w_down, m_mix_norm=m_mix_norm, m_ffn2_norm=m_ffn2_norm, m_ffn2_w_gate=m_ffn2_w_gate, m_ffn2_w_up=m_ffn2_w_up, m_ffn2_w_down=m_ffn2_w_down, m_hyb_w_in=m_hyb_w_in, m_gdn_conv=m_gdn_conv, m_gdn_a_log=m_gdn_a_log, m_gdn_dt_bias=m_gdn_dt_bias, m_gdn_out_norm=m_gdn_out_norm, m_pool_w=m_pool_w, m_pool_scale=m_pool_scale, m_hyb_w_out=m_hyb_w_out, m_mla_w_in=m_mla_w_in, m_mla_q_norm=m_mla_q_norm, m_mla_kv_norm=m_mla_kv_norm, m_mla_w_q_up=m_mla_w_q_up, m_mla_w_kv_up=m_mla_w_kv_up, m_mla_q_head_norm=m_mla_q_head_norm, m_mla_k_head_norm=m_mla_k_head_norm, m_mla_w_out=m_mla_w_out, v_ffn1_norm=v_ffn1_norm, v_ffn1_w_gate=v_ffn1_w_gate, v_ffn1_w_up=v_ffn1_w_up, v_ffn1_w_down=v_ffn1_w_down, v_mix_norm=v_mix_norm, v_ffn2_norm=v_ffn2_norm, v_ffn2_w_gate=v_ffn2_w_gate, v_ffn2_w_up=v_ffn2_w_up, v_ffn2_w_down=v_ffn2_w_down, v_hyb_w_in=v_hyb_w_in, v_gdn_conv=v_gdn_conv, v_gdn_a_log=v_gdn_a_log, v_gdn_dt_bias=v_gdn_dt_bias, v_gdn_out_norm=v_gdn_out_norm, v_pool_w=v_pool_w, v_pool_scale=v_pool_scale, v_hyb_w_out=v_hyb_w_out, v_mla_w_in=v_mla_w_in, v_mla_q_norm=v_mla_q_norm, v_mla_kv_norm=v_mla_kv_norm, v_mla_w_q_up=v_mla_w_q_up, v_mla_w_kv_up=v_mla_w_kv_up, v_mla_q_head_norm=v_mla_q_head_norm, v_mla_k_head_norm=v_mla_k_head_norm, v_mla_w_out=v_mla_w_out)
    weights = {n: given[n] for n in TWIN_WEIGHTS}
    shared = {n: given[n] for n in SHARED_INPUTS}
    per_example = {n: given[n] for n in ['x', 'positions']}
    grad_fn = _jax.value_and_grad(_loss, argnums=(0, 1))

    def one_microbatch(ex, loss_target):
        ex = dict(ex)
        diff = ex.pop(TWIN_DIFF_INPUT)
        return grad_fn(weights, diff, {**shared, **ex}, loss_target)

    if N_MICROBATCH == 1:
        loss, (grad_w, grad_x) = one_microbatch(per_example, given["loss_target"])
    else:
        def body(carry, xs):
            loss_sum, grad_sum = carry
            l_k, (gw_k, gx_k) = one_microbatch(xs[0], xs[1])
            with _jax.named_scope("update"):
                return (loss_sum + l_k, _jax.tree.map(_jnp.add, grad_sum, gw_k)), gx_k

        init = (_jnp.zeros((), _jnp.float32), _jax.tree.map(_jnp.zeros_like, weights))
        (loss, grad_w), grad_x = _jax.lax.scan(body, init, (per_example, given["loss_target"]))
    with _jax.named_scope("update"):
        delta_w, new_m, new_v = {}, {}, {}
        for n in TWIN_WEIGHTS:
            delta_w[n], new_m[n], new_v[n] = _adamw(weights[n], grad_w[n], given["m_" + n], given["v_" + n])
    return (loss, grad_x, *[grad_w[n] for n in TWIN_WEIGHTS], *[delta_w[n] for n in TWIN_WEIGHTS],
            *[new_m[n] for n in TWIN_WEIGHTS], *[new_v[n] for n in TWIN_WEIGHTS])
```

```python
import math

import jax
import jax.numpy as jnp
from jax import lax
from jax.experimental import pallas as pl
from jax.experimental.pallas import tpu as pltpu

F32 = jnp.float32
BF16 = jnp.bfloat16
MXU_DTYPE = jnp.bfloat16
HI = lax.Precision.HIGHEST
VMEM_LIMIT = 52 * 1024 * 1024
LANES = 128
N_DEV = 8

EPS = 1e-6
GDN_HEADS = 8
GDN_D = 128
CHUNK = 64
CONV_K = 4
POOL_WINDOWS = (2, 4, 8, 16)
POOL_GW = 256
MLA_HEADS = 16
NOPE = 128
ROPE = 64
QK_HEAD = NOPE + ROPE
V_HEAD = 128
Q_LORA = 512
ROPE_THETA = 10000.0

ADAM_LR = 0.001
ADAM_B1 = 0.9
ADAM_B2 = 0.999
ADAM_EPS = 1e-08
ADAM_WD = 0.01
ADAM_STEP = 10


def _pick(n, cands):
    for c in cands:
        if n % c == 0:
            return c
    return n


def _params(sem=None):
    return pltpu.CompilerParams(dimension_semantics=sem, vmem_limit_bytes=VMEM_LIMIT)


def _sigmoid(x):
    return 1.0 / (1.0 + jnp.exp(-x))


def mm(a, b, *, ta=False, tb=False, add=None, scale=None, out_dtype=F32, name="mm"):
    if ta:
        K, M = a.shape
    else:
        M, K = a.shape
    if tb:
        N, Kb = b.shape
    else:
        Kb, N = b.shape
    assert K == Kb, (a.shape, b.shape, ta, tb)
    tm = _pick(M, (1024, 512, 256, 128))
    tn = _pick(N, (1024, 512, 256, 128))
    tk = _pick(K, (512, 256, 128))
    nk = K // tk
    a_spec = pl.BlockSpec((tk, tm), lambda i, j, k: (k, i)) if ta else pl.BlockSpec((tm, tk), lambda i, j, k: (i, k))
    b_spec = pl.BlockSpec((tn, tk), lambda i, j, k: (j, k)) if tb else pl.BlockSpec((tk, tn), lambda i, j, k: (k, j))
    o_spec = pl.BlockSpec((tm, tn), lambda i, j, k: (i, j))
    dims = (((0 if ta else 1,), (1 if tb else 0,)), ((), ()))
    has_add = add is not None

    def body(*refs):
        if has_add:
            a_ref, b_ref, c_ref, o_ref, acc_ref = refs
        else:
            a_ref, b_ref, o_ref, acc_ref = refs
        k = pl.program_id(2)

        @pl.when(k == 0)
        def _():
            acc_ref[...] = jnp.zeros_like(acc_ref)

        acc_ref[...] += lax.dot_general(a_ref[...].astype(MXU_DTYPE), b_ref[...].astype(MXU_DTYPE), dims,
                                        preferred_element_type=F32)

        @pl.when(k == nk - 1)
        def _():
            r = acc_ref[...]
            if scale is not None:
                r = r * scale
            if has_add:
                r = r + c_ref[...].astype(F32)
            o_ref[...] = r.astype(out_dtype)

    ins = [a, b] + ([add] if has_add else [])
    specs = [a_spec, b_spec] + ([o_spec] if has_add else [])
    return pl.pallas_call(
        body, name=name, grid=(M // tm, N // tn, nk), in_specs=specs, out_specs=o_spec,
        out_shape=jax.ShapeDtypeStruct((M, N), out_dtype), scratch_shapes=[pltpu.VMEM((tm, tn), F32)],
        compiler_params=_params(("parallel", "parallel", "arbitrary")))(*ins)


def rms_fwd(x, gain, *, width=None, col_block=0, out_dtype=BF16, name="rms_fwd"):
    T = x.shape[0]
    W = x.shape[1] if width is None else width
    tt = _pick(T, (512, 256, 128, 64, 8))

    def body(x_ref, g_ref, o_ref):
        xv = x_ref[...]
        r = lax.rsqrt(jnp.mean(xv * xv, axis=-1, keepdims=True) + EPS)
        o_ref[...] = (xv * r * g_ref[...]).astype(out_dtype)

    return pl.pallas_call(
        body, name=name, grid=(T // tt,),
        in_specs=[pl.BlockSpec((tt, W), lambda i: (i, col_block)), pl.BlockSpec((1, W), lambda i: (0, 0))],
        out_specs=pl.BlockSpec((tt, W), lambda i: (i, 0)), out_shape=jax.ShapeDtypeStruct((T, W), out_dtype),
        compiler_params=_params(("parallel",)))(x, gain.reshape(1, W))


def rms_bwd(x, gain, dh, res=None, *, width=None, col_block=0, name="rms_bwd"):
    T = x.shape[0]
    W = x.shape[1] if width is None else width
    tt = _pick(T, (256, 128, 64, 8))
    has_res = res is not None

    def body(*refs):
        if has_res:
            x_ref, g_ref, dh_ref, res_ref, dx_ref, dg_ref = refs
        else:
            x_ref, g_ref, dh_ref, dx_ref, dg_ref = refs
        xv = x_ref[...]
        r = lax.rsqrt(jnp.mean(xv * xv, axis=-1, keepdims=True) + EPS)
        xhat = xv * r
        dy = dh_ref[...].astype(F32)
        dxhat = dy * g_ref[...]
        dx = r * (dxhat - xhat * jnp.mean(dxhat * xhat, axis=-1, keepdims=True))
        if has_res:
            dx = dx + res_ref[...]
        dx_ref[...] = dx

        @pl.when(pl.program_id(0) == 0)
        def _():
            dg_ref[...] = jnp.zeros_like(dg_ref)

        dg_ref[...] += jnp.sum(dy * xhat, axis=0, keepdims=True)

    row = pl.BlockSpec((tt, W), lambda i: (i, 0))
    ins = [x, gain.reshape(1, W), dh] + ([res] if has_res else [])
    specs = [pl.BlockSpec((tt, W), lambda i: (i, col_block)), pl.BlockSpec((1, W), lambda i: (0, 0)), row] + ([row] if has_res else [])
    return pl.pallas_call(
        body, name=name, grid=(T // tt,), in_specs=specs,
        out_specs=(row, pl.BlockSpec((1, W), lambda i: (0, 0))),
        out_shape=(jax.ShapeDtypeStruct((T, W), F32), jax.ShapeDtypeStruct((1, W), F32)),
        compiler_params=_params(("arbitrary",)))(*ins)


def swiglu_fwd(g, u, name="swiglu_fwd"):
    T, F = g.shape
    tt = _pick(T, (512, 256, 128, 64, 8))
    tf = _pick(F, (2048, 1024, 512, 256, 128))

    def body(g_ref, u_ref, a_ref):
        gv = g_ref[...].astype(F32)
        a_ref[...] = (gv * _sigmoid(gv) * u_ref[...].astype(F32)).astype(a_ref.dtype)

    blk = pl.BlockSpec((tt, tf), lambda i, j: (i, j))
    return pl.pallas_call(body, name=name, grid=(T // tt, F // tf), in_specs=[blk, blk], out_specs=blk,
                          out_shape=jax.ShapeDtypeStruct((T, F), BF16), compiler_params=_params(("parallel", "parallel")))(g, u)


def swiglu_bwd(da, g, u, name="swiglu_bwd"):
    T, F = g.shape
    tt = _pick(T, (512, 256, 128, 64, 8))
    tf = _pick(F, (2048, 1024, 512, 256, 128))

    def body(da_ref, g_ref, u_ref, dg_ref, du_ref):
        gv = g_ref[...].astype(F32)
        uv = u_ref[...].astype(F32)
        dav = da_ref[...].astype(F32)
        s = _sigmoid(gv)
        dg_ref[...] = (dav * uv * (s * (1.0 + gv * (1.0 - s)))).astype(dg_ref.dtype)
        du_ref[...] = (dav * (gv * s)).astype(du_ref.dtype)

    blk = pl.BlockSpec((tt, tf), lambda i, j: (i, j))
    return pl.pallas_call(body, name=name, grid=(T // tt, F // tf), in_specs=[blk, blk, blk], out_specs=(blk, blk),
                          out_shape=(jax.ShapeDtypeStruct((T, F), BF16), jax.ShapeDtypeStruct((T, F), BF16)),
                          compiler_params=_params(("parallel", "parallel")))(da, g, u)


def loss_head(y, target, name="loss_head"):
    T, D = y.shape
    tt = _pick(T, (512, 256, 128, 64, 8))

    def body(y_ref, t_ref, dy_ref, l_ref):
        e = y_ref[...] - t_ref[...]
        dy_ref[...] = e * (1.0 / D)

        @pl.when(pl.program_id(0) == 0)
        def _():
            l_ref[...] = jnp.zeros_like(l_ref)

        l_ref[...] += 0.5 * jnp.sum(jnp.mean(e * e, axis=-1, keepdims=True))

    row = pl.BlockSpec((tt, D), lambda i: (i, 0))
    dy, l = pl.pallas_call(body, name=name, grid=(T // tt,), in_specs=[row, row],
                           out_specs=(row, pl.BlockSpec((8, LANES), lambda i: (0, 0))),
                           out_shape=(jax.ShapeDtypeStruct((T, D), F32), jax.ShapeDtypeStruct((8, LANES), F32)),
                           compiler_params=_params(("arbitrary",)))(y, target)
    return dy, l[0, 0]


def _shifted(cur, prev, k, row):
    if k == 0:
        return cur
    return jnp.where(row < k, pltpu.roll(prev, k, 0), pltpu.roll(cur, k, 0))


def _conv_pre(x_ref, xp_ref, w_ref, first):
    cur = x_ref[...]
    prev = jnp.where(first, 0.0, xp_ref[...])
    row = lax.broadcasted_iota(jnp.int32, cur.shape, 0)
    xs = [_shifted(cur, prev, CONV_K - 1 - j, row) for j in range(CONV_K)]
    y = xs[0] * w_ref[0:1, :]
    for j in range(1, CONV_K):
        y = y + xs[j] * w_ref[j:j + 1, :]
    return y, xs


def gdn_prep_fwd(pm, conv_w, name="gdn_prep_fwd"):
    T = pm.shape[0]
    HW = GDN_HEADS * GDN_D
    tt = _pick(T, (256, 128, 64, 8))
    qscale = GDN_D ** -0.5

    def body(x_ref, xp_ref, w_ref, q_ref, k_ref, v_ref):
        y, _ = _conv_pre(x_ref, xp_ref, w_ref, pl.program_id(0) == 0)
        s = y * _sigmoid(y)
        for h in range(GDN_HEADS):
            for part, o_ref, sc in ((0, q_ref, qscale), (1, k_ref, 1.0)):
                sl = s[:, part * HW + h * GDN_D: part * HW + (h + 1) * GDN_D]
                r = lax.rsqrt(jnp.sum(sl * sl, axis=-1, keepdims=True) + EPS)
                o_ref[:, h * GDN_D:(h + 1) * GDN_D] = sl * (r * sc)
        v_ref[...] = s[:, 2 * HW:]

    blk = pl.BlockSpec((tt, 3 * HW), lambda i: (i, 0))
    blkp = pl.BlockSpec((tt, 3 * HW), lambda i: (jnp.maximum(i - 1, 0), 0))
    out = pl.BlockSpec((tt, HW), lambda i: (i, 0))
    sh = jax.ShapeDtypeStruct((T, HW), F32)
    return pl.pallas_call(body, name=name, grid=(T // tt,), in_specs=[blk, blkp, pl.BlockSpec((CONV_K, 3 * HW), lambda i: (0, 0))],
                          out_specs=(out, out, out), out_shape=(sh, sh, sh), compiler_params=_params(("parallel",)))(pm, pm, conv_w)


def gdn_prep_bwd(pm, conv_w, dq, dk, dv, name="gdn_prep_bwd"):
    T = pm.shape[0]
    HW = GDN_HEADS * GDN_D
    tt = _pick(T, (256, 128, 64, 8))
    qscale = GDN_D ** -0.5

    def body(x_ref, xp_ref, w_ref, dq_ref, dk_ref, dv_ref, dy_ref, dw_ref):
        y, xs = _conv_pre(x_ref, xp_ref, w_ref, pl.program_id(0) == 0)
        sg = _sigmoid(y)
        s = y * sg
        dsilu = sg * (1.0 + y * (1.0 - sg))
        for h in range(GDN_HEADS):
            for part, d_ref, sc in ((0, dq_ref, qscale), (1, dk_ref, 1.0)):
                lo = part * HW + h * GDN_D
                sl = s[:, lo:lo + GDN_D]
                r = lax.rsqrt(jnp.sum(sl * sl, axis=-1, keepdims=True) + EPS)
                n = sl * r
                dn = d_ref[:, h * GDN_D:(h + 1) * GDN_D] * sc
                ds = r * (dn - n * jnp.sum(dn * n, axis=-1, keepdims=True))
                dy_ref[:, lo:lo + GDN_D] = ds * dsilu[:, lo:lo + GDN_D]
        dy_ref[:, 2 * HW:] = dv_ref[...] * dsilu[:, 2 * HW:]

        @pl.when(pl.program_id(0) == 0)
        def _():
            dw_ref[...] = jnp.zeros_like(dw_ref)

        dyv = dy_ref[...]
        for j in range(CONV_K):
            dw_ref[j:j + 1, :] += jnp.sum(dyv * xs[j], axis=0, keepdims=True)

    blk = pl.BlockSpec((tt, 3 * HW), lambda i: (i, 0))
    blkp = pl.BlockSpec((tt, 3 * HW), lambda i: (jnp.maximum(i - 1, 0), 0))
    hb = pl.BlockSpec((tt, HW), lambda i: (i, 0))
    wb = pl.BlockSpec((CONV_K, 3 * HW), lambda i: (0, 0))
    return pl.pallas_call(body, name=name, grid=(T // tt,), in_specs=[blk, blkp, wb, hb, hb, hb], out_specs=(blk, wb),
                          out_shape=(jax.ShapeDtypeStruct((T, 3 * HW), F32), jax.ShapeDtypeStruct((CONV_K, 3 * HW), F32)),
                          compiler_params=_params(("arbitrary",)))(pm, pm, conv_w, dq, dk, dv)


def conv_dx(dy, conv_w, name="conv_dx"):
    T, W = dy.shape
    tt = _pick(T, (256, 128, 64, 8))
    nt = T // tt

    def body(d_ref, dn_ref, w_ref, dx_ref):
        cur = d_ref[...]
        nxt = jnp.where(pl.program_id(0) == nt - 1, 0.0, dn_ref[...])
        row = lax.broadcasted_iota(jnp.int32, cur.shape, 0)
        acc = cur * w_ref[CONV_K - 1:CONV_K, :]
        for j in range(CONV_K - 1):
            k = CONV_K - 1 - j
            sh = jnp.where(row >= tt - k, pltpu.roll(nxt, tt - k, 0), pltpu.roll(cur, tt - k, 0))
            acc = acc + sh * w_ref[j:j + 1, :]
        dx_ref[...] = acc

    blk = pl.BlockSpec((tt, W), lambda i: (i, 0))
    blkn = pl.BlockSpec((tt, W), lambda i: (jnp.minimum(i + 1, nt - 1), 0))
    return pl.pallas_call(body, name=name, grid=(nt,), in_specs=[blk, blkn, pl.BlockSpec((CONV_K, W), lambda i: (0, 0))], out_specs=blk,
                          out_shape=jax.ShapeDtypeStruct((T, W), F32), compiler_params=_params(("parallel",)))(dy, dy, conv_w)


def _upper_ones(c):
    return (lax.broadcasted_iota(jnp.int32, (c, c), 0) <= lax.broadcasted_iota(jnp.int32, (c, c), 1)).astype(F32)


def gdn_gates_fwd(a_r, b_r, alog_c, dt_c, name="gdn_gates_fwd"):
    R, C = a_r.shape

    def body(a_ref, b_ref, al_ref, dt_ref, gc_ref, beta_ref):
        x = a_ref[...] + dt_ref[...]
        sp = jnp.maximum(x, 0.0) + jnp.log1p(jnp.exp(-jnp.abs(x)))
        g = -jnp.exp(al_ref[...]) * sp
        gc_ref[...] = jnp.dot(g, _upper_ones(C), preferred_element_type=F32, precision=HI)
        beta_ref[...] = _sigmoid(b_ref[...])

    sh = jax.ShapeDtypeStruct((R, C), F32)
    return pl.pallas_call(body, name=name, out_shape=(sh, sh), compiler_params=_params())(a_r, b_r, alog_c, dt_c)


def gdn_gates_bwd(a_r, b_r, alog_c, dt_c, dgc, dbeta, name="gdn_gates_bwd"):
    R, C = a_r.shape

    def body(a_ref, b_ref, al_ref, dt_ref, dgc_ref, dbeta_ref, da_ref, db_ref, dal_ref, ddt_ref):
        x = a_ref[...] + dt_ref[...]
        sp = jnp.maximum(x, 0.0) + jnp.log1p(jnp.exp(-jnp.abs(x)))
        ea = jnp.exp(al_ref[...])
        dg = lax.dot_general(dgc_ref[...], _upper_ones(C), (((1,), (1,)), ((), ())), preferred_element_type=F32, precision=HI)
        dsp = dg * (-ea)
        da = dsp * _sigmoid(x)
        da_ref[...] = da
        beta = _sigmoid(b_ref[...])
        db_ref[...] = dbeta_ref[...] * beta * (1.0 - beta)
        sel = (lax.broadcasted_iota(jnp.int32, (GDN_HEADS, R), 1) % GDN_HEADS == lax.broadcasted_iota(jnp.int32, (GDN_HEADS, R), 0)).astype(F32)
        dal_ref[...] = jnp.sum(jnp.dot(sel, dg * (-ea * sp), preferred_element_type=F32, precision=HI), axis=1, keepdims=True)
        ddt_ref[...] = jnp.sum(jnp.dot(sel, da, preferred_element_type=F32, precision=HI), axis=1, keepdims=True)

    sh = jax.ShapeDtypeStruct((R, C), F32)
    s8 = jax.ShapeDtypeStruct((GDN_HEADS, 1), F32)
    return pl.pallas_call(body, name=name, out_shape=(sh, sh, s8, s8), compiler_params=_params())(a_r, b_r, alog_c, dt_c, dgc, dbeta)


def _dot(a, b):
    return jnp.dot(a, b, preferred_element_type=F32, precision=HI)


def _dot_nt(a, b):
    return lax.dot_general(a, b, (((1,), (1,)), ((), ())), preferred_element_type=F32, precision=HI)


def _dot_tn(a, b):
    return lax.dot_general(a, b, (((0,), (0,)), ((), ())), preferred_element_type=F32, precision=HI)


def _unit_lower_inverse(m):
    c = m.shape[0]
    eye = (lax.broadcasted_iota(jnp.int32, (c, c), 0) == lax.broadcasted_iota(jnp.int32, (c, c), 1)).astype(F32)
    p = -m
    t = eye + p
    n = 2
    while n < c:
        p = _dot(p, p)
        t = t + _dot(t, p)
        n *= 2
    return t


def _col(row, eye):
    c = eye.shape[0]
    return jnp.sum(jnp.where(eye, jnp.broadcast_to(row, (c, c)), 0.0), axis=1, keepdims=True)


def _row(col, eye):
    c = eye.shape[0]
    return jnp.sum(jnp.where(eye, jnp.broadcast_to(col, (c, c)), 0.0), axis=0, keepdims=True)


def _gdr_chunk(qh, kh, vh, gcr, br, eye, ii, jj):
    C = eye.shape[0]
    gcc = _col(gcr, eye)
    bc = _col(br, eye)
    causal = ii >= jj
    decay = jnp.where(causal, jnp.exp(jnp.where(causal, gcc - gcr, 0.0)), 0.0)
    decay_t = jnp.where(ii <= jj, jnp.exp(jnp.where(ii <= jj, gcr - gcc, 0.0)), 0.0)
    kb = kh * bc
    vb = vh * bc
    eg = jnp.exp(gcc)
    glast = gcr[:, C - 1:C]
    el = jnp.exp(glast)
    ekd = jnp.exp(glast - gcc)
    m = jnp.where(ii > jj, _dot_nt(kb, kh) * decay, 0.0)
    tinv = _unit_lower_inverse(m)
    kbg = kb * eg
    u = _dot(tinv, vb)
    w = _dot(tinv, kbg)
    a = _dot_nt(qh, kh) * decay
    qd = qh * eg
    kd = kh * ekd
    return dict(gcc=gcc, bc=bc, decay=decay, decay_t=decay_t, kb=kb, vb=vb, eg=eg, el=el, ekd=ekd, m=m, tinv=tinv,
                kbg=kbg, u=u, w=w, a=a, qd=qd, kd=kd)


def gdr_fwd(q, k, v, gc, beta, name="gdr_fwd"):
    T = q.shape[0]
    H, DK, C = GDN_HEADS, GDN_D, CHUNK
    N = T // C

    def body(q_ref, k_ref, v_ref, gc_ref, b_ref, o_ref, st_ref, s_ref):
        @pl.when(pl.program_id(0) == 0)
        def _():
            s_ref[...] = jnp.zeros_like(s_ref)

        ii = lax.broadcasted_iota(jnp.int32, (C, C), 0)
        jj = lax.broadcasted_iota(jnp.int32, (C, C), 1)
        eye = ii == jj
        for h in range(H):
            sl = slice(h * DK, (h + 1) * DK)
            f = _gdr_chunk(q_ref[:, sl], k_ref[:, sl], v_ref[:, sl], gc_ref[0, h:h + 1, :], b_ref[0, h:h + 1, :], eye, ii, jj)
            s = s_ref[h]
            st_ref[0, h] = s
            vnew = f["u"] - _dot(f["w"], s)
            o_ref[:, sl] = _dot(f["qd"], s) + _dot(f["a"], vnew)
            s_ref[h] = s * f["el"] + _dot_tn(f["kd"], vnew)

    tok = pl.BlockSpec((C, H * DK), lambda n: (n, 0))
    gate = pl.BlockSpec((1, H, C), lambda n: (n, 0, 0))
    return pl.pallas_call(
        body, name=name, grid=(N,), in_specs=[tok, tok, tok, gate, gate],
        out_specs=(tok, pl.BlockSpec((1, H, DK, DK), lambda n: (n, 0, 0, 0))),
        out_shape=(jax.ShapeDtypeStruct((T, H * DK), F32), jax.ShapeDtypeStruct((N, H, DK, DK), F32)),
        scratch_shapes=[pltpu.VMEM((H, DK, DK), F32)], compiler_params=_params(("arbitrary",)))(q, k, v, gc, beta)


def gdr_bwd(q, k, v, gc, beta, states, do, name="gdr_bwd"):
    T = q.shape[0]
    H, DK, C = GDN_HEADS, GDN_D, CHUNK
    N = T // C

    def body(q_ref, k_ref, v_ref, gc_ref, b_ref, st_ref, do_ref, dq_ref, dk_ref, dv_ref, dgc_ref, db_ref, ds_ref):
        @pl.when(pl.program_id(0) == 0)
        def _():
            ds_ref[...] = jnp.zeros_like(ds_ref)

        ii = lax.broadcasted_iota(jnp.int32, (C, C), 0)
        jj = lax.broadcasted_iota(jnp.int32, (C, C), 1)
        eye = ii == jj
        lastj = lax.broadcasted_iota(jnp.int32, (1, C), 1) == C - 1
        for h in range(H):
            sl = slice(h * DK, (h + 1) * DK)
            qh, kh, vh = q_ref[:, sl], k_ref[:, sl], v_ref[:, sl]
            f = _gdr_chunk(qh, kh, vh, gc_ref[0, h:h + 1, :], b_ref[0, h:h + 1, :], eye, ii, jj)
            s = st_ref[0, h]
            dsn = ds_ref[h]
            dout = do_ref[:, sl]
            decay, decay_t, tinv = f["decay"], f["decay_t"], f["tinv"]
            vnew = f["u"] - _dot(f["w"], s)
            m_t = jnp.where(ii < jj, _dot_nt(kh, f["kb"]) * decay_t, 0.0)
            tinv_t = _unit_lower_inverse(m_t)
            a_t = _dot_nt(kh, qh) * decay_t
            dvnew = _dot(a_t, dout) + _dot(f["kd"], dsn)
            da = _dot_nt(dout, vnew)
            da_t = _dot_nt(vnew, dout)
            dqd = _dot_nt(dout, s)
            dkd = _dot_nt(vnew, dsn)
            ds_ref[h] = _dot_tn(f["qd"], dout) - _dot_tn(f["w"], dvnew) + dsn * f["el"]
            dglast = jnp.sum(dsn * s) * f["el"]
            dw = -_dot_nt(dvnew, s)
            dvb = _dot(tinv_t, dvnew)
            dkbg = _dot(tinv_t, dw)
            dt = _dot_nt(dvnew, f["vb"]) + _dot_nt(dw, f["kbg"])
            dt_t = _dot_nt(f["vb"], dvnew) + _dot_nt(f["kbg"], dw)
            dm = jnp.where(ii > jj, -_dot(tinv_t, _dot(dt, tinv_t)), 0.0)
            dm_t = jnp.where(ii < jj, -_dot(tinv, _dot(dt_t, tinv)), 0.0)
            dkk = dm * decay
            dkk_t = dm_t * decay_t
            dqk = da * decay
            dqk_t = da_t * decay_t
            e = dm * f["m"] + da * f["a"]
            dkb = _dot(dkk, kh) + dkbg * f["eg"]
            dkh = _dot(dkk_t, f["kb"]) + _dot(dqk_t, qh) + dkd * f["ekd"] + dkb * f["bc"]
            dq_ref[:, sl] = _dot(dqk, kh) + dqd * f["eg"]
            dk_ref[:, sl] = dkh
            dv_ref[:, sl] = dvb * f["bc"]
            skd = jnp.sum(dkd * f["kd"], axis=1, keepdims=True)
            dgc_col = (jnp.sum(e, axis=1, keepdims=True) + jnp.sum(dqd * f["qd"], axis=1, keepdims=True)
                       + jnp.sum(dkbg * f["kbg"], axis=1, keepdims=True) - skd)
            dglast = dglast + jnp.sum(skd)
            dgc_row = _row(dgc_col, eye) - jnp.sum(e, axis=0, keepdims=True)
            dgc_ref[0, h:h + 1, :] = dgc_row + jnp.where(lastj, dglast, 0.0)
            dbeta_col = jnp.sum(dkb * kh, axis=1, keepdims=True) + jnp.sum(dvb * vh, axis=1, keepdims=True)
            db_ref[0, h:h + 1, :] = _row(dbeta_col, eye)

    rev = lambda n: (N - 1 - n, 0)
    tok = pl.BlockSpec((C, H * DK), rev)
    gate = pl.BlockSpec((1, H, C), lambda n: (N - 1 - n, 0, 0))
    tsh = jax.ShapeDtypeStruct((T, H * DK), F32)
    gsh = jax.ShapeDtypeStruct((N, H, C), F32)
    return pl.pallas_call(
        body, name=name, grid=(N,),
        in_specs=[tok, tok, tok, gate, gate, pl.BlockSpec((1, H, DK, DK), lambda n: (N - 1 - n, 0, 0, 0)), tok],
        out_specs=(tok, tok, tok, gate, gate), out_shape=(tsh, tsh, tsh, gsh, gsh),
        scratch_shapes=[pltpu.VMEM((H, DK, DK), F32)], compiler_params=_params(("arbitrary",)))(q, k, v, gc, beta, states, do)


def gdn_gate_fwd(o, pm, out_norm, *, z_block, name="gdn_gate_fwd"):
    T, HW = o.shape
    tt = _pick(T, (256, 128, 64, 8))

    def body(o_ref, z_ref, g_ref, y_ref):
        for h in range(GDN_HEADS):
            sl = slice(h * GDN_D, (h + 1) * GDN_D)
            ov = o_ref[:, sl]
            zv = z_ref[:, sl]
            r = lax.rsqrt(jnp.mean(ov * ov, axis=-1, keepdims=True) + EPS)
            y_ref[:, sl] = (ov * r * g_ref[...] * (zv * _sigmoid(zv))).astype(y_ref.dtype)

    blk = pl.BlockSpec((tt, HW), lambda i: (i, 0))
    return pl.pallas_call(body, name=name, grid=(T // tt,),
                          in_specs=[blk, pl.BlockSpec((tt, HW), lambda i: (i, z_block)), pl.BlockSpec((1, GDN_D), lambda i: (0, 0))],
                          out_specs=blk, out_shape=jax.ShapeDtypeStruct((T, HW), BF16),
                          compiler_params=_params(("parallel",)))(o, pm, out_norm.reshape(1, GDN_D))


def gdn_gate_bwd(o, pm, out_norm, dy, *, z_block, name="gdn_gate_bwd"):
    T, HW = o.shape
    tt = _pick(T, (256, 128, 64, 8))

    def body(o_ref, z_ref, g_ref, dy_ref, do_ref, dz_ref, dg_ref):
        @pl.when(pl.program_id(0) == 0)
        def _():
            dg_ref[...] = jnp.zeros_like(dg_ref)

        acc = jnp.zeros((1, GDN_D), F32)
        for h in range(GDN_HEADS):
            sl = slice(h * GDN_D, (h + 1) * GDN_D)
            ov = o_ref[:, sl]
            zv = z_ref[:, sl]
            dyv = dy_ref[:, sl]
            r = lax.rsqrt(jnp.mean(ov * ov, axis=-1, keepdims=True) + EPS)
            xhat = ov * r
            sg = _sigmoid(zv)
            sz = zv * sg
            dn = dyv * sz
            dz_ref[:, sl] = dyv * (xhat * g_ref[...]) * (sg * (1.0 + zv * (1.0 - sg)))
            acc = acc + jnp.sum(dn * xhat, axis=0, keepdims=True)
            dxhat = dn * g_ref[...]
            do_ref[:, sl] = r * (dxhat - xhat * jnp.mean(dxhat * xhat, axis=-1, keepdims=True))
        dg_ref[...] += acc

    blk = pl.BlockSpec((tt, HW), lambda i: (i, 0))
    gb = pl.BlockSpec((1, GDN_D), lambda i: (0, 0))
    sh = jax.ShapeDtypeStruct((T, HW), F32)
    return pl.pallas_call(body, name=name, grid=(T // tt,),
                          in_specs=[blk, pl.BlockSpec((tt, HW), lambda i: (i, z_block)), gb, blk],
                          out_specs=(blk, blk, gb), out_shape=(sh, sh, jax.ShapeDtypeStruct((1, GDN_D), F32)),
                          compiler_params=_params(("arbitrary",)))(o, pm, out_norm.reshape(1, GDN_D), dy)


def _pool_bands(tt, win, t0):
    t = lax.broadcasted_iota(jnp.int32, (tt, tt), 0)
    s = lax.broadcasted_iota(jnp.int32, (tt, tt), 1)
    inv = 1.0 / jnp.minimum(t + t0 + 1, win).astype(F32)
    cur = jnp.where((s <= t) & (s > t - win), inv, 0.0)
    prev = jnp.where(s - tt > t - win, inv, 0.0)
    return cur, prev


def _pool_diff(u_ref, up_ref, g, tt, t0, first):
    sl = slice(g * POOL_GW, (g + 1) * POOL_GW)
    cur, prev = _pool_bands(tt, POOL_WINDOWS[g], t0)
    ug = u_ref[:, sl]
    upg = jnp.where(first, 0.0, up_ref[:, sl])
    return _dot(cur, ug) + _dot(prev, upg) - ug


def pool_fwd(pm, pool_w, pool_scale, *, u_block, name="pool_fwd"):
    T = pm.shape[0]
    PW = len(POOL_WINDOWS) * POOL_GW
    tt = _pick(T, (256, 128, 64, 16))

    def body(u_ref, up_ref, w_ref, s_ref, p_ref):
        i = pl.program_id(0)
        for g in range(len(POOL_WINDOWS)):
            sl = slice(g * POOL_GW, (g + 1) * POOL_GW)
            diff = _pool_diff(u_ref, up_ref, g, tt, i * tt, i == 0)
            y = jnp.dot(diff.astype(MXU_DTYPE), w_ref[g].astype(MXU_DTYPE), preferred_element_type=F32)
            p_ref[:, sl] = (y * s_ref[:, sl]).astype(p_ref.dtype)

    return pl.pallas_call(
        body, name=name, grid=(T // tt,),
        in_specs=[pl.BlockSpec((tt, PW), lambda i: (i, u_block)), pl.BlockSpec((tt, PW), lambda i: (jnp.maximum(i - 1, 0), u_block)),
                  pl.BlockSpec((len(POOL_WINDOWS), POOL_GW, POOL_GW), lambda i: (0, 0, 0)), pl.BlockSpec((1, PW), lambda i: (0, 0))],
        out_specs=pl.BlockSpec((tt, PW), lambda i: (i, 0)), out_shape=jax.ShapeDtypeStruct((T, PW), BF16),
        compiler_params=_params(("parallel",)))(pm, pm, pool_w, pool_scale.reshape(1, PW))


def pool_bwd_a(pm, pool_w, pool_scale, dp, *, u_block, name="pool_bwd_a"):
    T = pm.shape[0]
    G = len(POOL_WINDOWS)
    PW = G * POOL_GW
    tt = _pick(T, (256, 128, 64, 16))

    def body(u_ref, up_ref, w_ref, s_ref, dp_ref, dd_ref, dw_ref, dsc_ref):
        i = pl.program_id(0)

        @pl.when(i == 0)
        def _():
            dw_ref[...] = jnp.zeros_like(dw_ref)
            dsc_ref[...] = jnp.zeros_like(dsc_ref)

        for g in range(G):
            sl = slice(g * POOL_GW, (g + 1) * POOL_GW)
            diff = _pool_diff(u_ref, up_ref, g, tt, i * tt, i == 0).astype(MXU_DTYPE)
            wg = w_ref[g].astype(MXU_DTYPE)
            dpv = dp_ref[:, sl]
            y = jnp.dot(diff, wg, preferred_element_type=F32)
            dsc_ref[:, sl] += jnp.sum(dpv * y, axis=0, keepdims=True)
            dy = (dpv * s_ref[:, sl]).astype(MXU_DTYPE)
            dd_ref[:, sl] = lax.dot_general(dy, wg, (((1,), (1,)), ((), ())), preferred_element_type=F32)
            dw_ref[g] += lax.dot_general(diff, dy, (((0,), (0,)), ((), ())), preferred_element_type=F32)

    wb = pl.BlockSpec((G, POOL_GW, POOL_GW), lambda i: (0, 0, 0))
    sb = pl.BlockSpec((1, PW), lambda i: (0, 0))
    blk = pl.BlockSpec((tt, PW), lambda i: (i, 0))
    return pl.pallas_call(
        body, name=name, grid=(T // tt,),
        in_specs=[pl.BlockSpec((tt, PW), lambda i: (i, u_block)), pl.BlockSpec((tt, PW), lambda i: (jnp.maximum(i - 1, 0), u_block)), wb, sb, blk],
        out_specs=(blk, wb, sb),
        out_shape=(jax.ShapeDtypeStruct((T, PW), F32), jax.ShapeDtypeStruct((G, POOL_GW, POOL_GW), F32), jax.ShapeDtypeStruct((1, PW), F32)),
        compiler_params=_params(("arbitrary",)))(pm, pm, pool_w, pool_scale.reshape(1, PW), dp)


def pool_bwd_b(dd, name="pool_bwd_b"):
    T, PW = dd.shape
    tt = _pick(T, (256, 128, 64, 16))
    nt = T // tt

    def body(d_ref, dn_ref, du_ref):
        i = pl.program_id(0)
        s = lax.broadcasted_iota(jnp.int32, (tt, tt), 0)
        t = lax.broadcasted_iota(jnp.int32, (tt, tt), 1)
        for g, win in enumerate(POOL_WINDOWS):
            sl = slice(g * POOL_GW, (g + 1) * POOL_GW)
            inv_c = 1.0 / jnp.minimum(t + i * tt + 1, win).astype(F32)
            cur = jnp.where((t >= s) & (t < s + win), inv_c, 0.0)
            nxt = jnp.where(t + tt < s + win, 1.0 / win, 0.0)
            dg = d_ref[:, sl]
            dng = jnp.where(i == nt - 1, 0.0, dn_ref[:, sl])
            du_ref[:, sl] = _dot(cur, dg) + _dot(nxt, dng) - dg

    blk = pl.BlockSpec((tt, PW), lambda i: (i, 0))
    return pl.pallas_call(body, name=name, grid=(nt,), in_specs=[blk, pl.BlockSpec((tt, PW), lambda i: (jnp.minimum(i + 1, nt - 1), 0))],
                          out_specs=blk, out_shape=jax.ShapeDtypeStruct((T, PW), F32), compiler_params=_params(("parallel",)))(dd, dd)


def _rope_consts():
    j = jnp.arange(QK_HEAD)
    inv_freq = ROPE_THETA ** (-jnp.arange(0, ROPE, 2, dtype=F32) / ROPE)
    freq = jnp.where(j >= NOPE, inv_freq[(j - NOPE) % (ROPE // 2)], 0.0).astype(F32)
    half = ROPE // 2
    src = jnp.arange(QK_HEAD)[:, None]
    dst = jnp.arange(QK_HEAD)[None, :]
    first = (dst >= NOPE) & (dst < NOPE + half)
    second = dst >= NOPE + half
    p = jnp.where(first & (src == dst + half), -1.0, 0.0) + jnp.where(second & (src == dst - half), 1.0, 0.0)
    return freq.reshape(1, QK_HEAD), p.astype(F32)


def rope_tables(pos_col, freq, name="rope_tables"):
    T = pos_col.shape[0]
    tt = _pick(T, (512, 256, 128, 64, 8))

    def body(p_ref, f_ref, c_ref, s_ref):
        ang = p_ref[...] * f_ref[...]
        rot = lax.broadcasted_iota(jnp.int32, ang.shape, 1) >= NOPE
        c_ref[...] = jnp.where(rot, jnp.cos(ang), 1.0)
        s_ref[...] = jnp.where(rot, jnp.sin(ang), 0.0)

    blk = pl.BlockSpec((tt, QK_HEAD), lambda i: (i, 0))
    sh = jax.ShapeDtypeStruct((T, QK_HEAD), F32)
    return pl.pallas_call(body, name=name, grid=(T // tt,), in_specs=[pl.BlockSpec((tt, 1), lambda i: (i, 0)), pl.BlockSpec((1, QK_HEAD), lambda i: (0, 0))],
                          out_specs=(blk, blk), out_shape=(sh, sh), compiler_params=_params(("parallel",)))(pos_col, freq)


def _seg_stats(t):
    lane = lax.broadcasted_iota(jnp.int32, t.shape, 1)
    nope = lane < NOPE
    sq = t * t
    r = jnp.where(nope, lax.rsqrt(jnp.sum(jnp.where(nope, sq, 0.0), axis=-1, keepdims=True) / NOPE + EPS),
                  lax.rsqrt(jnp.sum(jnp.where(nope, 0.0, sq), axis=-1, keepdims=True) / ROPE + EPS))
    return nope, r


def headnorm_rope_fwd(t, gain, cos, sin, pmat, name="headnorm_rope_fwd"):
    H, T, W = t.shape
    tt = _pick(T, (512, 256, 128, 64, 8))

    def body(t_ref, g_ref, c_ref, s_ref, p_ref, o_ref):
        tv = t_ref[0]
        _, r = _seg_stats(tv)
        y = tv * r * g_ref[...]
        o_ref[0] = (y * c_ref[...] + _dot(y, p_ref[...]) * s_ref[...]).astype(o_ref.dtype)

    blk = pl.BlockSpec((1, tt, W), lambda h, i: (h, i, 0))
    tab = pl.BlockSpec((tt, W), lambda h, i: (i, 0))
    return pl.pallas_call(body, name=name, grid=(H, T // tt),
                          in_specs=[blk, pl.BlockSpec((1, W), lambda h, i: (0, 0)), tab, tab, pl.BlockSpec((W, W), lambda h, i: (0, 0))],
                          out_specs=blk, out_shape=jax.ShapeDtypeStruct((H, T, W), BF16),
                          compiler_params=_params(("parallel", "parallel")))(t, gain.reshape(1, W), cos, sin, pmat)


def headnorm_rope_bwd(t, gain, cos, sin, pmat_t, dout, name="headnorm_rope_bwd"):
    H, T, W = t.shape
    tt = _pick(T, (512, 256, 128, 64, 8))

    def body(t_ref, g_ref, c_ref, s_ref, p_ref, do_ref, dt_ref, dg_ref):
        @pl.when((pl.program_id(0) == 0) & (pl.program_id(1) == 0))
        def _():
            dg_ref[...] = jnp.zeros_like(dg_ref)

        tv = t_ref[0]
        dov = do_ref[0]
        nope, r = _seg_stats(tv)
        dy = dov * c_ref[...] + _dot(dov * s_ref[...], p_ref[...])
        xhat = tv * r
        dg_ref[...] += jnp.sum(dy * xhat, axis=0, keepdims=True)
        dxhat = dy * g_ref[...]
        pr = dxhat * xhat
        mean = jnp.where(nope, jnp.sum(jnp.where(nope, pr, 0.0), axis=-1, keepdims=True) / NOPE,
                         jnp.sum(jnp.where(nope, 0.0, pr), axis=-1, keepdims=True) / ROPE)
        dt_ref[0] = r * (dxhat - xhat * mean)

    blk = pl.BlockSpec((1, tt, W), lambda h, i: (h, i, 0))
    tab = pl.BlockSpec((tt, W), lambda h, i: (i, 0))
    gb = pl.BlockSpec((1, W), lambda h, i: (0, 0))
    return pl.pallas_call(body, name=name, grid=(H, T // tt),
                          in_specs=[blk, gb, tab, tab, pl.BlockSpec((W, W), lambda h, i: (0, 0)), blk],
                          out_specs=(blk, gb), out_shape=(jax.ShapeDtypeStruct((H, T, W), F32), jax.ShapeDtypeStruct((1, W), F32)),
                          compiler_params=_params(("arbitrary", "arbitrary")))(t, gain.reshape(1, W), cos, sin, pmat_t, dout)


def sum_heads(x, name="sum_heads"):
    H, T, W = x.shape
    tt = _pick(T, (512, 256, 128, 64, 8))

    def body(x_ref, o_ref):
        @pl.when(pl.program_id(1) == 0)
        def _():
            o_ref[...] = jnp.zeros_like(o_ref)

        o_ref[...] += x_ref[0]

    return pl.pallas_call(body, name=name, grid=(T // tt, H), in_specs=[pl.BlockSpec((1, tt, W), lambda i, h: (h, i, 0))],
                          out_specs=pl.BlockSpec((tt, W), lambda i, h: (i, 0)), out_shape=jax.ShapeDtypeStruct((T, W), F32),
                          compiler_params=_params(("parallel", "arbitrary")))(x)


def _attn_tile(T):
    return _pick(T, (512, 256, 128, 64))


def attn_fwd(q, k, v, name="attn_fwd"):
    H, T, DQ = q.shape
    DV = v.shape[2]
    tq = tk = _attn_tile(T)
    nq, nk = T // tq, T // tk
    scale = DQ ** -0.5

    def body(q_ref, k_ref, v_ref, o_ref, l_ref, m_s, l_s, acc_s):
        i, j = pl.program_id(1), pl.program_id(2)

        @pl.when(j == 0)
        def _():
            m_s[...] = jnp.full_like(m_s, -1e30)
            l_s[...] = jnp.zeros_like(l_s)
            acc_s[...] = jnp.zeros_like(acc_s)

        @pl.when(j <= i)
        def _():
            s = lax.dot_general(q_ref[0], k_ref[0], (((1,), (1,)), ((), ())), preferred_element_type=F32) * scale
            row = lax.broadcasted_iota(jnp.int32, s.shape, 0) + i * tq
            col = lax.broadcasted_iota(jnp.int32, s.shape, 1) + j * tk
            s = jnp.where(row >= col, s, -1e30)
            m_old = m_s[...]
            m_new = jnp.maximum(m_old, jnp.max(s, axis=-1, keepdims=True))
            p = jnp.exp(s - m_new)
            alpha = jnp.exp(m_old - m_new)
            l_s[...] = alpha * l_s[...] + jnp.sum(p, axis=-1, keepdims=True)
            acc_s[...] = alpha * acc_s[...] + jnp.dot(p.astype(MXU_DTYPE), v_ref[0], preferred_element_type=F32)
            m_s[...] = m_new

        @pl.when(j == nk - 1)
        def _():
            o_ref[0] = acc_s[...] / l_s[...]
            l_ref[0] = jnp.broadcast_to(m_s[...] + jnp.log(l_s[...]), (tq, DV))

    qb = pl.BlockSpec((1, tq, DQ), lambda h, i, j: (h, i, 0))
    kb = pl.BlockSpec((1, tk, DQ), lambda h, i, j: (h, jnp.minimum(j, i), 0))
    vb = pl.BlockSpec((1, tk, DV), lambda h, i, j: (h, jnp.minimum(j, i), 0))
    ob = pl.BlockSpec((1, tq, DV), lambda h, i, j: (h, i, 0))
    sh = jax.ShapeDtypeStruct((H, T, DV), F32)
    return pl.pallas_call(body, name=name, grid=(H, nq, nk), in_specs=[qb, kb, vb], out_specs=(ob, ob), out_shape=(sh, sh),
                          scratch_shapes=[pltpu.VMEM((tq, 1), F32), pltpu.VMEM((tq, 1), F32), pltpu.VMEM((tq, DV), F32)],
                          compiler_params=_params(("parallel", "parallel", "arbitrary")))(q, k, v)


def attn_bwd(q, k, v, o, lse, do, name="attn_bwd"):
    H, T, DQ = q.shape
    DV = v.shape[2]
    tq = tk = _attn_tile(T)
    nq, nk = T // tq, T // tk
    scale = DQ ** -0.5

    def body(q_ref, k_ref, v_ref, o_ref, l_ref, do_ref, dq_ref, dk_ref, dv_ref, dk_s, dv_s):
        j, i = pl.program_id(1), pl.program_id(2)

        @pl.when((j == 0) & (i == 0))
        def _():
            dq_ref[...] = jnp.zeros_like(dq_ref)

        @pl.when(i == 0)
        def _():
            dk_s[...] = jnp.zeros_like(dk_s)
            dv_s[...] = jnp.zeros_like(dv_s)

        @pl.when(i >= j)
        def _():
            qv, kv, vv = q_ref[0], k_ref[0], v_ref[0]
            dov = do_ref[0]
            s = lax.dot_general(qv, kv, (((1,), (1,)), ((), ())), preferred_element_type=F32) * scale
            row = lax.broadcasted_iota(jnp.int32, s.shape, 0) + i * tq
            col = lax.broadcasted_iota(jnp.int32, s.shape, 1) + j * tk
            p = jnp.where(row >= col, jnp.exp(s - l_ref[0][:, 0:1]), 0.0)
            delta = jnp.sum(dov * o_ref[0], axis=-1, keepdims=True)
            dob = dov.astype(MXU_DTYPE)
            pb = p.astype(MXU_DTYPE)
            dv_s[...] += lax.dot_general(pb, dob, (((0,), (0,)), ((), ())), preferred_element_type=F32)
            dp = lax.dot_general(dob, vv, (((1,), (1,)), ((), ())), preferred_element_type=F32)
            ds = (p * (dp - delta) * scale).astype(MXU_DTYPE)
            dk_s[...] += lax.dot_general(ds, qv, (((0,), (0,)), ((), ())), preferred_element_type=F32)
            rows = pl.ds(pl.multiple_of(i * tq, tq), tq)
            dq_ref[0, rows, :] += jnp.dot(ds, kv, preferred_element_type=F32)

        @pl.when(i == nq - 1)
        def _():
            dk_ref[0] = dk_s[...]
            dv_ref[0] = dv_s[...]

    qi = lambda h, j, i: (h, jnp.maximum(i, j), 0)
    qb = pl.BlockSpec((1, tq, DQ), qi)
    ob = pl.BlockSpec((1, tq, DV), qi)
    kb = pl.BlockSpec((1, tk, DQ), lambda h, j, i: (h, j, 0))
    vb = pl.BlockSpec((1, tk, DV), lambda h, j, i: (h, j, 0))
    dqb = pl.BlockSpec((1, T, DQ), lambda h, j, i: (h, 0, 0))
    return pl.pallas_call(
        body, name=name, grid=(H, nk, nq), in_specs=[qb, kb, vb, ob, ob, ob], out_specs=(dqb, kb, vb),
        out_shape=(jax.ShapeDtypeStruct((H, T, DQ), F32), jax.ShapeDtypeStruct((H, T, DQ), F32), jax.ShapeDtypeStruct((H, T, DV), F32)),
        scratch_shapes=[pltpu.VMEM((tk, DQ), F32), pltpu.VMEM((tk, DV), F32)],
        compiler_params=_params(("parallel", "arbitrary", "arbitrary")))(q, k, v, o, lse, do)


def _mesh_place():
    return lax.axis_index("x"), lax.axis_index("y"), lax.axis_index("c")


def _flip(v, bit):
    return 1 - v if bit else v


def _relations():
    return [((r >> 2) & 1, (r >> 1) & 1, r & 1) for r in range(1, N_DEV)]


def all_gather(xs, name="all_gather"):
    R, L = xs.shape

    def body(x_ref, o_ref, send_sems, recv_sems, local_sem):
        x, y, c = _mesh_place()
        me = 4 * x + 2 * y + c
        local = pltpu.make_async_copy(x_ref, o_ref.at[me], local_sem)
        local.start()
        copies = []
        for r, (bx, by, bc) in enumerate(_relations()):
            cp = pltpu.make_async_remote_copy(
                src_ref=x_ref, dst_ref=o_ref.at[me], send_sem=send_sems.at[r], recv_sem=recv_sems.at[r],
                device_id=(_flip(x, bx), _flip(y, by), _flip(c, bc)), device_id_type=pl.DeviceIdType.MESH)
            cp.start()
            copies.append(cp)
        for cp in copies:
            cp.wait_recv()
        for cp in copies:
            cp.wait_send()
        local.wait()

    return pl.pallas_call(
        body, name=name, out_shape=jax.ShapeDtypeStruct((N_DEV, R, L), xs.dtype),
        in_specs=[pl.BlockSpec(memory_space=pl.ANY)], out_specs=pl.BlockSpec(memory_space=pl.ANY),
        scratch_shapes=[pltpu.SemaphoreType.DMA((N_DEV - 1,)), pltpu.SemaphoreType.DMA((N_DEV - 1,)), pltpu.SemaphoreType.DMA(())])(xs)


def exchange(xs, name="exchange"):
    _, R, L = xs.shape

    def body(x_ref, o_ref, send_sems, recv_sems, local_sem):
        x, y, c = _mesh_place()
        me = 4 * x + 2 * y + c
        local = pltpu.make_async_copy(x_ref.at[me], o_ref.at[me], local_sem)
        local.start()
        copies = []
        for r, (bx, by, bc) in enumerate(_relations()):
            px, py, pc = _flip(x, bx), _flip(y, by), _flip(c, bc)
            cp = pltpu.make_async_remote_copy(
                src_ref=x_ref.at[4 * px + 2 * py + pc], dst_ref=o_ref.at[me], send_sem=send_sems.at[r], recv_sem=recv_sems.at[r],
                device_id=(px, py, pc), device_id_type=pl.DeviceIdType.MESH)
            cp.start()
            copies.append(cp)
        for cp in copies:
            cp.wait_recv()
        for cp in copies:
            cp.wait_send()
        local.wait()

    return pl.pallas_call(
        body, name=name, out_shape=jax.ShapeDtypeStruct((N_DEV, R, L), xs.dtype),
        in_specs=[pl.BlockSpec(memory_space=pl.ANY)], out_specs=pl.BlockSpec(memory_space=pl.ANY),
        scratch_shapes=[pltpu.SemaphoreType.DMA((N_DEV - 1,)), pltpu.SemaphoreType.DMA((N_DEV - 1,)), pltpu.SemaphoreType.DMA(())])(xs)


def adamw(recv, w, m, v, name="adamw"):
    _, R, L = recv.shape
    tr = _pick(R, (1024, 512, 256, 128, 64, 32, 16, 8))

    def body(r_ref, w_ref, m_ref, v_ref, g_ref, d_ref, nm_ref, nv_ref):
        g = r_ref[0]
        for s in range(1, N_DEV):
            g = g + r_ref[s]
        m_new = ADAM_B1 * m_ref[...] + (1.0 - ADAM_B1) * g
        v_new = ADAM_B2 * v_ref[...] + (1.0 - ADAM_B2) * jnp.square(g)
        m_hat = m_new / (1.0 - ADAM_B1 ** ADAM_STEP)
        v_hat = v_new / (1.0 - ADAM_B2 ** ADAM_STEP)
        g_ref[...] = g
        d_ref[...] = -ADAM_LR * (m_hat / (jnp.sqrt(v_hat) + ADAM_EPS) + ADAM_WD * w_ref[...])
        nm_ref[...] = m_new
        nv_ref[...] = v_new

    row = pl.BlockSpec((tr, L), lambda i: (i, 0))
    sh = jax.ShapeDtypeStruct((R, L), F32)
    return pl.pallas_call(body, name=name, grid=(R // tr,), in_specs=[pl.BlockSpec((N_DEV, tr, L), lambda i: (0, i, 0)), row, row, row],
                          out_specs=(row, row, row, row), out_shape=(sh, sh, sh, sh), compiler_params=_params(("parallel",)))(recv, w, m, v)


def ffn_fwd(x, gain, wg, wu, wd):
    h = rms_fwd(x, gain, name="ffn_rms_fwd")
    g = mm(h, wg, out_dtype=BF16, name="ffn_gate")
    u = mm(h, wu, out_dtype=BF16, name="ffn_up")
    a = swiglu_fwd(g, u)
    y = mm(a, wd, add=x, scale=0.5, name="ffn_down")
    return y, (x, h, g, u, a)


def ffn_bwd(dy, saved, gain, wg, wu, wd):
    x, h, g, u, a = saved
    da = mm(dy, wd, tb=True, scale=0.5, out_dtype=BF16, name="ffn_dact")
    dwd = mm(a, dy, ta=True, scale=0.5, name="ffn_dwd")
    dg, du = swiglu_bwd(da, g, u)
    dh = mm(dg, wg, tb=True, name="ffn_dh_gate")
    dh = mm(du, wu, tb=True, add=dh, name="ffn_dh_up")
    dwg = mm(h, dg, ta=True, name="ffn_dwg")
    dwu = mm(h, du, ta=True, name="ffn_dwu")
    dx, dgain = rms_bwd(x, gain, dh, dy, name="ffn_rms_bwd")
    return dx, dgain[0], dwg, dwu, dwd


HYB_QKVZ = 4 * GDN_HEADS * GDN_D
HYB_AB = 2 * GDN_HEADS
HYB_U = len(POOL_WINDOWS) * POOL_GW


def _hyb_split_w_in(w_in):
    main = jnp.concatenate([w_in[:, :HYB_QKVZ], w_in[:, HYB_QKVZ + HYB_AB:]], axis=1)
    ab = jnp.pad(w_in[:, HYB_QKVZ:HYB_QKVZ + HYB_AB], ((0, 0), (0, LANES - HYB_AB)))
    return main, ab


def _gate_rows(t, n):
    return t.reshape(n, CHUNK, GDN_HEADS).transpose(0, 2, 1).reshape(n * GDN_HEADS, CHUNK)


def _gate_cols(r, n):
    return r.reshape(n, GDN_HEADS, CHUNK).transpose(0, 2, 1).reshape(n * CHUNK, GDN_HEADS)


def hyb_fwd(x, p):
    T = x.shape[0]
    n = T // CHUNK
    h = rms_fwd(x, p["mix_norm"], name="mix_rms_fwd")
    w_main, w_ab = _hyb_split_w_in(p["w_in"])
    pm = mm(h, w_main, name="hyb_in_main")
    pab = mm(h, w_ab, name="hyb_in_gates")
    q, k, v = gdn_prep_fwd(pm, p["conv"])
    a_r = _gate_rows(pab[:, :GDN_HEADS], n)
    b_r = _gate_rows(pab[:, GDN_HEADS:HYB_AB], n)
    alog_c = jnp.tile(p["a_log"], n).reshape(n * GDN_HEADS, 1)
    dt_c = jnp.tile(p["dt_bias"], n).reshape(n * GDN_HEADS, 1)
    gc, beta = gdn_gates_fwd(a_r, b_r, alog_c, dt_c)
    gc3 = gc.reshape(n, GDN_HEADS, CHUNK)
    beta3 = beta.reshape(n, GDN_HEADS, CHUNK)
    o, states = gdr_fwd(q, k, v, gc3, beta3)
    og = gdn_gate_fwd(o, pm, p["out_norm"], z_block=3)
    pool = pool_fwd(pm, p["pool_w"], p["pool_scale"], u_block=4)
    half = GDN_HEADS * GDN_D
    y = mm(og, p["w_out"][:half], add=x, name="hyb_out_gdn")
    y = mm(pool, p["w_out"][half:], add=y, name="hyb_out_pool")
    return y, dict(x=x, h=h, pm=pm, q=q, k=k, v=v, a_r=a_r, b_r=b_r, alog_c=alog_c, dt_c=dt_c, gc3=gc3, beta3=beta3,
                   o=o, states=states, og=og, pool=pool)


def hyb_bwd(dy, s, p):
    T = dy.shape[0]
    n = T // CHUNK
    half = GDN_HEADS * GDN_D
    w_main, w_ab = _hyb_split_w_in(p["w_in"])
    dog = mm(dy, p["w_out"][:half], tb=True, name="hyb_dog")
    dpool = mm(dy, p["w_out"][half:], tb=True, name="hyb_dpool")
    dw_out = jnp.concatenate([mm(s["og"], dy, ta=True, name="hyb_dwout_gdn"), mm(s["pool"], dy, ta=True, name="hyb_dwout_pool")], axis=0)
    do, dz, dout_norm = gdn_gate_bwd(s["o"], s["pm"], p["out_norm"], dog, z_block=3)
    dq, dk, dv, dgc, dbeta = gdr_bwd(s["q"], s["k"], s["v"], s["gc3"], s["beta3"], s["states"], do)
    da_r, db_r, dalog, ddt = gdn_gates_bwd(s["a_r"], s["b_r"], s["alog_c"], s["dt_c"],
                                           dgc.reshape(n * GDN_HEADS, CHUNK), dbeta.reshape(n * GDN_HEADS, CHUNK))
    dpab = jnp.pad(jnp.concatenate([_gate_cols(da_r, n), _gate_cols(db_r, n)], axis=1), ((0, 0), (0, LANES - HYB_AB)))
    dyc, dconv = gdn_prep_bwd(s["pm"], p["conv"], dq, dk, dv)
    dqkv = conv_dx(dyc, p["conv"])
    dd, dpool_w, dpool_scale = pool_bwd_a(s["pm"], p["pool_w"], p["pool_scale"], dpool, u_block=4)
    du = pool_bwd_b(dd)
    dpm = jnp.concatenate([dqkv, dz, du], axis=1)
    dh = mm(dpm, w_main, tb=True, name="hyb_dh_main")
    dh = mm(dpab, w_ab, tb=True, add=dh, name="hyb_dh_gates")
    dw_main = mm(s["h"], dpm, ta=True, name="hyb_dwin_main")
    dw_ab = mm(s["h"], dpab, ta=True, name="hyb_dwin_gates")
    dw_in = jnp.concatenate([dw_main[:, :HYB_QKVZ], dw_ab[:, :HYB_AB], dw_main[:, HYB_QKVZ:]], axis=1)
    dx, dmix = rms_bwd(s["x"], p["mix_norm"], dh, dy, name="mix_rms_bwd")
    grads = dict(mix_norm=dmix[0], w_in=dw_in, conv=dconv, a_log=dalog[:, 0], dt_bias=ddt[:, 0], out_norm=dout_norm[0],
                 pool_w=dpool_w, pool_scale=dpool_scale[0], w_out=dw_out)
    return dx, grads


MLA_LAT = 2 * Q_LORA


def _mla_split_w_in(w_in):
    return w_in[:, :MLA_LAT], jnp.pad(w_in[:, MLA_LAT:], ((0, 0), (0, LANES - ROPE)))


def _to_heads(t, width):
    T = t.shape[0]
    return t.reshape(T, MLA_HEADS, width).transpose(1, 0, 2)


def _from_heads(t):
    H, T, W = t.shape
    return t.transpose(1, 0, 2).reshape(T, H * W)


def mla_fwd(x, p, rope):
    T = x.shape[0]
    cos, sin, pmat = rope
    h = rms_fwd(x, p["mix_norm"], name="mix_rms_fwd")
    w_main, w_pe = _mla_split_w_in(p["w_in"])
    pm = mm(h, w_main, name="mla_in_main")
    ppe = mm(h, w_pe, name="mla_in_pe")
    qn = rms_fwd(pm, p["q_norm"], width=Q_LORA, col_block=0, name="mla_lat_rms_fwd")
    kvn = rms_fwd(pm, p["kv_norm"], width=Q_LORA, col_block=1, name="mla_lat_rms_fwd")
    q3 = _to_heads(mm(qn, p["w_q_up"], name="mla_q_up"), QK_HEAD)
    kv3 = _to_heads(mm(kvn, p["w_kv_up"], name="mla_kv_up"), NOPE + V_HEAD)
    kpe = jnp.broadcast_to(ppe[None, :, :ROPE], (MLA_HEADS, T, ROPE))
    k3 = jnp.concatenate([kv3[..., :NOPE], kpe], axis=-1)
    v3 = kv3[..., NOPE:].astype(MXU_DTYPE)
    qr = headnorm_rope_fwd(q3, p["q_head_norm"], cos, sin, pmat)
    kr = headnorm_rope_fwd(k3, p["k_head_norm"], cos, sin, pmat)
    o3, lse = attn_fwd(qr, kr, v3)
    o = _from_heads(o3).astype(MXU_DTYPE)
    y = mm(o, p["w_out"], add=x, name="mla_out")
    return y, dict(x=x, h=h, pm=pm, qn=qn, kvn=kvn, q3=q3, k3=k3, v3=v3, qr=qr, kr=kr, o3=o3, lse=lse, o=o)


def mla_bwd(dy, s, p, rope):
    cos, sin, pmat = rope
    w_main, w_pe = _mla_split_w_in(p["w_in"])
    do3 = _to_heads(mm(dy, p["w_out"], tb=True, name="mla_do"), V_HEAD)
    dw_out = mm(s["o"], dy, ta=True, name="mla_dwout")
    dqr, dkr, dv3 = attn_bwd(s["qr"], s["kr"], s["v3"], s["o3"], s["lse"], do3)
    dq3, dqhn = headnorm_rope_bwd(s["q3"], p["q_head_norm"], cos, sin, pmat.T, dqr)
    dk3, dkhn = headnorm_rope_bwd(s["k3"], p["k_head_norm"], cos, sin, pmat.T, dkr)
    dppe = jnp.pad(sum_heads(dk3)[:, NOPE:], ((0, 0), (0, LANES - ROPE)))
    dq = _from_heads(dq3)
    dkv = _from_heads(jnp.concatenate([dk3[..., :NOPE], dv3], axis=-1))
    dqn = mm(dq, p["w_q_up"], tb=True, name="mla_dqn")
    dkvn = mm(dkv, p["w_kv_up"], tb=True, name="mla_dkvn")
    dw_q_up = mm(s["qn"], dq, ta=True, name="mla_dwq_up")
    dw_kv_up = mm(s["kvn"], dkv, ta=True, name="mla_dwkv_up")
    dqlat, dq_norm = rms_bwd(s["pm"], p["q_norm"], dqn, width=Q_LORA, col_block=0, name="mla_lat_rms_bwd")
    dkvlat, dkv_norm = rms_bwd(s["pm"], p["kv_norm"], dkvn, width=Q_LORA, col_block=1, name="mla_lat_rms_bwd")
    dpm = jnp.concatenate([dqlat, dkvlat], axis=1)
    dh = mm(dpm, w_main, tb=True, name="mla_dh_main")
    dh = mm(dppe, w_pe, tb=True, add=dh, name="mla_dh_pe")
    dw_in = jnp.concatenate([mm(s["h"], dpm, ta=True, name="mla_dwin_main"), mm(s["h"], dppe, ta=True, name="mla_dwin_pe")[:, :ROPE]], axis=1)
    dx, dmix = rms_bwd(s["x"], p["mix_norm"], dh, dy, name="mix_rms_bwd")
    grads = dict(mix_norm=dmix[0], w_in=dw_in, q_norm=dq_norm[0], kv_norm=dkv_norm[0], w_q_up=dw_q_up, w_kv_up=dw_kv_up,
                 q_head_norm=dqhn[0], k_head_norm=dkhn[0], w_out=dw_out)
    return dx, grads


WEIGHTS = ['ffn1_norm', 'ffn1_w_gate', 'ffn1_w_up', 'ffn1_w_down', 'mix_norm', 'ffn2_norm', 'ffn2_w_gate', 'ffn2_w_up',
           'ffn2_w_down', 'hyb_w_in', 'gdn_conv', 'gdn_a_log', 'gdn_dt_bias', 'gdn_out_norm', 'pool_w', 'pool_scale',
           'hyb_w_out', 'mla_w_in', 'mla_q_norm', 'mla_kv_norm', 'mla_w_q_up', 'mla_w_kv_up', 'mla_q_head_norm',
           'mla_k_head_norm', 'mla_w_out']
SHARD_AXIS = dict(ffn1_norm=None, ffn1_w_gate=2, ffn1_w_up=2, ffn1_w_down=1, mix_norm=None, ffn2_norm=None, ffn2_w_gate=2,
                  ffn2_w_up=2, ffn2_w_down=1, hyb_w_in=2, gdn_conv=2, gdn_a_log=None, gdn_dt_bias=None, gdn_out_norm=None,
                  pool_w=2, pool_scale=None, hyb_w_out=1, mla_w_in=1, mla_q_norm=1, mla_kv_norm=1, mla_w_q_up=2,
                  mla_w_kv_up=2, mla_q_head_norm=None, mla_k_head_norm=None, mla_w_out=1)
GATHER_BF16 = ['ffn1_w_gate', 'ffn1_w_up', 'ffn1_w_down', 'ffn2_w_gate', 'ffn2_w_up', 'ffn2_w_down', 'hyb_w_in', 'pool_w',
               'hyb_w_out', 'mla_w_in', 'mla_w_q_up', 'mla_w_kv_up', 'mla_w_out']
GATHER_F32 = ['gdn_conv', 'mla_q_norm', 'mla_kv_norm']
N_CHUNKS = 8
CHUNK_ROW_UNIT = 1024


def _pack(flat_list, lead=(), n_chunks=N_CHUNKS, unit=CHUNK_ROW_UNIT):
    flat = jnp.concatenate(flat_list, axis=-1)
    rows = -(-flat.shape[-1] // LANES)
    rc = -(-rows // (n_chunks * unit)) * unit
    pad = n_chunks * rc * LANES - flat.shape[-1]
    flat = jnp.pad(flat, [(0, 0)] * len(lead) + [(0, pad)])
    return [flat[..., i * rc * LANES:(i + 1) * rc * LANES].reshape(*lead, rc, LANES) for i in range(n_chunks)]


def _unpack(chunks, shapes, lead=()):
    flat = jnp.concatenate([c.reshape(*lead, -1) for c in chunks], axis=-1)
    out, off = [], 0
    for sh in shapes:
        n = math.prod(sh)
        out.append(flat[..., off:off + n].reshape(*lead, *sh))
        off += n
    return out


def _to_slabs(g, axis):
    if axis is None:
        return jnp.broadcast_to(g.reshape(1, -1), (N_DEV, g.size))
    sh = g.shape
    g = g.reshape(*sh[:axis], N_DEV, sh[axis] // N_DEV, *sh[axis + 1:])
    return jnp.moveaxis(g, axis, 0).reshape(N_DEV, -1)


def _from_shards(t, axis):
    t = jnp.moveaxis(t, 0, axis)
    sh = t.shape
    return t.reshape(*sh[:axis], sh[axis] * sh[axis + 1], *sh[axis + 2:])


def _gather_weights(local, names, dtype, name, **how):
    chunks = _pack([local[n].astype(dtype).reshape(-1) for n in names], **how)
    got = [all_gather(c, name=name) for c in chunks]
    parts = _unpack(got, [local[n].shape for n in names], lead=(N_DEV,))
    return {n: _from_shards(t, SHARD_AXIS[n]) for n, t in zip(names, parts)}


def kernel(x, positions, ffn1_norm, ffn1_w_gate, ffn1_w_up, ffn1_w_down, mix_norm, ffn2_norm, ffn2_w_gate, ffn2_w_up, ffn2_w_down, hyb_w_in, gdn_conv, gdn_a_log, gdn_dt_bias, gdn_out_norm, pool_w, pool_scale, hyb_w_out, mla_w_in, mla_q_norm, mla_kv_norm, mla_w_q_up, mla_w_kv_up, mla_q_head_norm, mla_k_head_norm, mla_w_out, loss_target, m_ffn1_norm, m_ffn1_w_gate, m_ffn1_w_up, m_ffn1_w_down, m_mix_norm, m_ffn2_norm, m_ffn2_w_gate, m_ffn2_w_up, m_ffn2_w_down, m_hyb_w_in, m_gdn_conv, m_gdn_a_log, m_gdn_dt_bias, m_gdn_out_norm, m_pool_w, m_pool_scale, m_hyb_w_out, m_mla_w_in, m_mla_q_norm, m_mla_kv_norm, m_mla_w_q_up, m_mla_w_kv_up, m_mla_q_head_norm, m_mla_k_head_norm, m_mla_w_out, v_ffn1_norm, v_ffn1_w_gate, v_ffn1_w_up, v_ffn1_w_down, v_mix_norm, v_ffn2_norm, v_ffn2_w_gate, v_ffn2_w_up, v_ffn2_w_down, v_hyb_w_in, v_gdn_conv, v_gdn_a_log, v_gdn_dt_bias, v_gdn_out_norm, v_pool_w, v_pool_scale, v_hyb_w_out, v_mla_w_in, v_mla_q_norm, v_mla_kv_norm, v_mla_w_q_up, v_mla_w_kv_up, v_mla_q_head_norm, v_mla_k_head_norm, v_mla_w_out):
    given = dict(locals())
    local = {n: given[n] for n in WEIGHTS}
    depth = ffn1_norm.shape[0]
    xs = x[0]
    T = xs.shape[0]

    full = dict(local)
    full.update(_gather_weights(local, GATHER_BF16, MXU_DTYPE, "gather_bf16"))
    full.update(_gather_weights(local, GATHER_F32, F32, "gather_f32", n_chunks=1, unit=8))

    freq, pmat = _rope_consts()
    cos, sin = rope_tables(positions[0].astype(F32).reshape(T, 1), freq)
    rope = (cos, sin, pmat)

    def mixer_params(layer):
        i = layer // 2
        if layer % 2 == 0:
            return dict(mix_norm=full["mix_norm"][layer], w_in=full["hyb_w_in"][i], conv=full["gdn_conv"][i], a_log=full["gdn_a_log"][i],
                        dt_bias=full["gdn_dt_bias"][i], out_norm=full["gdn_out_norm"][i], pool_w=full["pool_w"][i],
                        pool_scale=full["pool_scale"][i], w_out=full["hyb_w_out"][i])
        return dict(mix_norm=full["mix_norm"][layer], w_in=full["mla_w_in"][i], q_norm=full["mla_q_norm"][i], kv_norm=full["mla_kv_norm"][i],
                    w_q_up=full["mla_w_q_up"][i], w_kv_up=full["mla_w_kv_up"][i], q_head_norm=full["mla_q_head_norm"][i],
                    k_head_norm=full["mla_k_head_norm"][i], w_out=full["mla_w_out"][i])

    def ffn_params(which, layer):
        return (full[which + "_norm"][layer], full[which + "_w_gate"][layer], full[which + "_w_up"][layer], full[which + "_w_down"][layer])

    saved = []
    cur = xs
    for layer in range(depth):
        cur, s1 = ffn_fwd(cur, *ffn_params("ffn1", layer))
        if layer % 2 == 0:
            cur, sm = hyb_fwd(cur, mixer_params(layer))
        else:
            cur, sm = mla_fwd(cur, mixer_params(layer), rope)
        cur, s2 = ffn_fwd(cur, *ffn_params("ffn2", layer))
        saved.append((s1, sm, s2))

    dcur, loss_local = loss_head(cur, loss_target[0])
    loss = lax.psum(loss_local, ("x", "y", "c"))

    per_layer = {n: [None] * (depth if n.startswith(("ffn", "mix")) else 0) for n in WEIGHTS}
    mix_grads = [None] * depth
    for layer in reversed(range(depth)):
        s1, sm, s2 = saved[layer]
        dcur, dn, dwg, dwu, dwd = ffn_bwd(dcur, s2, *ffn_params("ffn2", layer))
        per_layer["ffn2_norm"][layer], per_layer["ffn2_w_gate"][layer], per_layer["ffn2_w_up"][layer], per_layer["ffn2_w_down"][layer] = dn, dwg, dwu, dwd
        if layer % 2 == 0:
            dcur, mg = hyb_bwd(dcur, sm, mixer_params(layer))
        else:
            dcur, mg = mla_bwd(dcur, sm, mixer_params(layer), rope)
        mix_grads[layer] = mg
        per_layer["mix_norm"][layer] = mg["mix_norm"]
        dcur, dn, dwg, dwu, dwd = ffn_bwd(dcur, s1, *ffn_params("ffn1", layer))
        per_layer["ffn1_norm"][layer], per_layer["ffn1_w_gate"][layer], per_layer["ffn1_w_up"][layer], per_layer["ffn1_w_down"][layer] = dn, dwg, dwu, dwd
    grad_x = dcur[None]

    hyb_names = dict(hyb_w_in="w_in", gdn_conv="conv", gdn_a_log="a_log", gdn_dt_bias="dt_bias", gdn_out_norm="out_norm",
                     pool_w="pool_w", pool_scale="pool_scale", hyb_w_out="w_out")
    mla_names = dict(mla_w_in="w_in", mla_q_norm="q_norm", mla_kv_norm="kv_norm", mla_w_q_up="w_q_up", mla_w_kv_up="w_kv_up",
                     mla_q_head_norm="q_head_norm", mla_k_head_norm="k_head_norm", mla_w_out="w_out")
    full_grads = {}
    for n in WEIGHTS:
        if n in hyb_names:
            full_grads[n] = jnp.stack([mix_grads[l][hyb_names[n]] for l in range(0, depth, 2)])
        elif n in mla_names:
            full_grads[n] = jnp.stack([mix_grads[l][mla_names[n]] for l in range(1, depth, 2)])
        else:
            full_grads[n] = jnp.stack(per_layer[n])

    send = _pack([_to_slabs(full_grads[n], SHARD_AXIS[n]) for n in WEIGHTS], lead=(N_DEV,))
    w_chunks = _pack([local[n].reshape(-1) for n in WEIGHTS])
    m_chunks = _pack([given["m_" + n].reshape(-1) for n in WEIGHTS])
    v_chunks = _pack([given["v_" + n].reshape(-1) for n in WEIGHTS])
    outs = [adamw(exchange(send[i]), w_chunks[i], m_chunks[i], v_chunks[i]) for i in range(N_CHUNKS)]
    shapes = [local[n].shape for n in WEIGHTS]
    grads, deltas, new_m, new_v = (_unpack([o[j] for o in outs], shapes) for j in range(4))
    return (loss, grad_x, *grads, *deltas, *new_m, *new_v)
```

```python
import math

import jax
import jax.numpy as jnp
from jax import lax
from jax.experimental import pallas as pl
from jax.experimental.pallas import tpu as pltpu

F32 = jnp.float32
BF16 = jnp.bfloat16
MXU_DTYPE = jnp.bfloat16
HI = lax.Precision.HIGHEST
VMEM_LIMIT = 52 * 1024 * 1024
LANES = 128
N_DEV = 8

EPS = 1e-6
GDN_HEADS = 8
GDN_D = 128
CHUNK = 64
CONV_K = 4
POOL_WINDOWS = (2, 4, 8, 16)
POOL_GW = 256
MLA_HEADS = 16
NOPE = 128
ROPE = 64
QK_HEAD = NOPE + ROPE
V_HEAD = 128
Q_LORA = 512
ROPE_THETA = 10000.0

ADAM_LR = 0.001
ADAM_B1 = 0.9
ADAM_B2 = 0.999
ADAM_EPS = 1e-08
ADAM_WD = 0.01
ADAM_STEP = 10
ADAMW_BLOCK_ELEMS = 128 * 1024


def _pick(n, cands):
    for c in cands:
        if n % c == 0:
            return c
    return n


def _params(sem=None):
    return pltpu.CompilerParams(dimension_semantics=sem, vmem_limit_bytes=VMEM_LIMIT)


def _sigmoid(x):
    return 1.0 / (1.0 + jnp.exp(-x))


def mm(a, b, *, ta=False, tb=False, add=None, scale=None, out_dtype=F32, col_slabs=None, name="mm"):
    if ta:
        K, M = a.shape
    else:
        M, K = a.shape
    if tb:
        N, Kb = b.shape
    else:
        Kb, N = b.shape
    assert K == Kb, (a.shape, b.shape, ta, tb)
    tm = _pick(M, (1024, 512, 256, 128))
    tn = _pick(N if col_slabs is None else N // col_slabs, (1024, 512, 384, 256, 128))
    tk = _pick(K, (512, 256, 128))
    nk = K // tk
    a_spec = pl.BlockSpec((tk, tm), lambda i, j, k: (k, i)) if ta else pl.BlockSpec((tm, tk), lambda i, j, k: (i, k))
    b_spec = pl.BlockSpec((tn, tk), lambda i, j, k: (j, k)) if tb else pl.BlockSpec((tk, tn), lambda i, j, k: (k, j))
    if col_slabs is None:
        o_spec = pl.BlockSpec((tm, tn), lambda i, j, k: (i, j))
        o_shape = (M, N)
    else:
        assert add is None
        per = N // col_slabs // tn
        o_spec = pl.BlockSpec((None, tm, tn), lambda i, j, k: (j // per, i, j % per))
        o_shape = (col_slabs, M, N // col_slabs)
    dims = (((0 if ta else 1,), (1 if tb else 0,)), ((), ()))
    has_add = add is not None

    def body(*refs):
        if has_add:
            a_ref, b_ref, c_ref, o_ref, acc_ref = refs
        else:
            a_ref, b_ref, o_ref, acc_ref = refs
        k = pl.program_id(2)

        @pl.when(k == 0)
        def _():
            acc_ref[...] = jnp.zeros_like(acc_ref)

        acc_ref[...] += lax.dot_general(a_ref[...].astype(MXU_DTYPE), b_ref[...].astype(MXU_DTYPE), dims,
                                        preferred_element_type=F32)

        @pl.when(k == nk - 1)
        def _():
            r = acc_ref[...]
            if scale is not None:
                r = r * scale
            if has_add:
                r = r + c_ref[...].astype(F32)
            o_ref[...] = r.astype(out_dtype)

    ins = [a, b] + ([add] if has_add else [])
    specs = [a_spec, b_spec] + ([o_spec] if has_add else [])
    return pl.pallas_call(
        body, name=name, grid=(M // tm, N // tn, nk), in_specs=specs, out_specs=o_spec,
        out_shape=jax.ShapeDtypeStruct(o_shape, out_dtype), scratch_shapes=[pltpu.VMEM((tm, tn), F32)],
        compiler_params=_params(("parallel", "parallel", "arbitrary")))(*ins)


def rms_fwd(x, gain, *, width=None, col_block=0, out_dtype=BF16, name="rms_fwd"):
    T = x.shape[0]
    W = x.shape[1] if width is None else width
    tt = _pick(T, (512, 256, 128, 64, 8))

    def body(x_ref, g_ref, o_ref):
        xv = x_ref[...]
        r = lax.rsqrt(jnp.mean(xv * xv, axis=-1, keepdims=True) + EPS)
        o_ref[...] = (xv * r * g_ref[...]).astype(out_dtype)

    return pl.pallas_call(
        body, name=name, grid=(T // tt,),
        in_specs=[pl.BlockSpec((tt, W), lambda i: (i, col_block)), pl.BlockSpec((1, W), lambda i: (0, 0))],
        out_specs=pl.BlockSpec((tt, W), lambda i: (i, 0)), out_shape=jax.ShapeDtypeStruct((T, W), out_dtype),
        compiler_params=_params(("parallel",)))(x, gain.reshape(1, W))


def rms_bwd(x, gain, dh, res=None, *, width=None, col_block=0, name="rms_bwd"):
    T = x.shape[0]
    W = x.shape[1] if width is None else width
    tt = _pick(T, (256, 128, 64, 8))
    has_res = res is not None

    def body(*refs):
        if has_res:
            x_ref, g_ref, dh_ref, res_ref, dx_ref, dg_ref = refs
        else:
            x_ref, g_ref, dh_ref, dx_ref, dg_ref = refs
        xv = x_ref[...]
        r = lax.rsqrt(jnp.mean(xv * xv, axis=-1, keepdims=True) + EPS)
        xhat = xv * r
        dy = dh_ref[...].astype(F32)
        dxhat = dy * g_ref[...]
        dx = r * (dxhat - xhat * jnp.mean(dxhat * xhat, axis=-1, keepdims=True))
        if has_res:
            dx = dx + res_ref[...]
        dx_ref[...] = dx

        @pl.when(pl.program_id(0) == 0)
        def _():
            dg_ref[...] = jnp.zeros_like(dg_ref)

        dg_ref[...] += jnp.sum(dy * xhat, axis=0, keepdims=True)

    row = pl.BlockSpec((tt, W), lambda i: (i, 0))
    ins = [x, gain.reshape(1, W), dh] + ([res] if has_res else [])
    specs = [pl.BlockSpec((tt, W), lambda i: (i, col_block)), pl.BlockSpec((1, W), lambda i: (0, 0)), row] + ([row] if has_res else [])
    return pl.pallas_call(
        body, name=name, grid=(T // tt,), in_specs=specs,
        out_specs=(row, pl.BlockSpec((1, W), lambda i: (0, 0))),
        out_shape=(jax.ShapeDtypeStruct((T, W), F32), jax.ShapeDtypeStruct((1, W), F32)),
        compiler_params=_params(("arbitrary",)))(*ins)


def swiglu_fwd(g, u, name="swiglu_fwd"):
    T, F = g.shape
    tt = _pick(T, (512, 256, 128, 64, 8))
    tf = _pick(F, (2048, 1024, 512, 256, 128))

    def body(g_ref, u_ref, a_ref):
        gv = g_ref[...].astype(F32)
        a_ref[...] = (gv * _sigmoid(gv) * u_ref[...].astype(F32)).astype(a_ref.dtype)

    blk = pl.BlockSpec((tt, tf), lambda i, j: (i, j))
    return pl.pallas_call(body, name=name, grid=(T // tt, F // tf), in_specs=[blk, blk], out_specs=blk,
                          out_shape=jax.ShapeDtypeStruct((T, F), BF16), compiler_params=_params(("parallel", "parallel")))(g, u)


def swiglu_bwd(da, g, u, name="swiglu_bwd"):
    T, F = g.shape
    tt = _pick(T, (512, 256, 128, 64, 8))
    tf = _pick(F, (2048, 1024, 512, 256, 128))

    def body(da_ref, g_ref, u_ref, dg_ref, du_ref):
        gv = g_ref[...].astype(F32)
        uv = u_ref[...].astype(F32)
        dav = da_ref[...].astype(F32)
        s = _sigmoid(gv)
        dg_ref[...] = (dav * uv * (s * (1.0 + gv * (1.0 - s)))).astype(dg_ref.dtype)
        du_ref[...] = (dav * (gv * s)).astype(du_ref.dtype)

    blk = pl.BlockSpec((tt, tf), lambda i, j: (i, j))
    return pl.pallas_call(body, name=name, grid=(T // tt, F // tf), in_specs=[blk, blk, blk], out_specs=(blk, blk),
                          out_shape=(jax.ShapeDtypeStruct((T, F), BF16), jax.ShapeDtypeStruct((T, F), BF16)),
                          compiler_params=_params(("parallel", "parallel")))(da, g, u)


def loss_head(y, target, name="loss_head"):
    T, D = y.shape
    tt = _pick(T, (512, 256, 128, 64, 8))

    def body(y_ref, t_ref, dy_ref, l_ref):
        e = y_ref[...] - t_ref[...]
        dy_ref[...] = e * (1.0 / D)

        @pl.when(pl.program_id(0) == 0)
        def _():
            l_ref[...] = jnp.zeros_like(l_ref)

        l_ref[...] += 0.5 * jnp.sum(jnp.mean(e * e, axis=-1, keepdims=True))

    row = pl.BlockSpec((tt, D), lambda i: (i, 0))
    dy, l = pl.pallas_call(body, name=name, grid=(T // tt,), in_specs=[row, row],
                           out_specs=(row, pl.BlockSpec((8, LANES), lambda i: (0, 0))),
                           out_shape=(jax.ShapeDtypeStruct((T, D), F32), jax.ShapeDtypeStruct((8, LANES), F32)),
                           compiler_params=_params(("arbitrary",)))(y, target)
    return dy, l[0, 0]


def _shifted(cur, prev, k, row):
    if k == 0:
        return cur
    return jnp.where(row < k, pltpu.roll(prev, k, 0), pltpu.roll(cur, k, 0))


def _conv_pre(x_ref, xp_ref, w_ref, first):
    cur = x_ref[...]
    prev = jnp.where(first, 0.0, xp_ref[...])
    row = lax.broadcasted_iota(jnp.int32, cur.shape, 0)
    xs = [_shifted(cur, prev, CONV_K - 1 - j, row) for j in range(CONV_K)]
    y = xs[0] * w_ref[0:1, :]
    for j in range(1, CONV_K):
        y = y + xs[j] * w_ref[j:j + 1, :]
    return y, xs


def gdn_prep_fwd(pm, conv_w, name="gdn_prep_fwd"):
    T = pm.shape[0]
    HW = GDN_HEADS * GDN_D
    tt = _pick(T, (256, 128, 64, 8))
    qscale = GDN_D ** -0.5

    def body(x_ref, xp_ref, w_ref, q_ref, k_ref, v_ref):
        y, _ = _conv_pre(x_ref, xp_ref, w_ref, pl.program_id(0) == 0)
        s = y * _sigmoid(y)
        for h in range(GDN_HEADS):
            for part, o_ref, sc in ((0, q_ref, qscale), (1, k_ref, 1.0)):
                sl = s[:, part * HW + h * GDN_D: part * HW + (h + 1) * GDN_D]
                r = lax.rsqrt(jnp.sum(sl * sl, axis=-1, keepdims=True) + EPS)
                o_ref[:, h * GDN_D:(h + 1) * GDN_D] = sl * (r * sc)
        v_ref[...] = s[:, 2 * HW:]

    blk = pl.BlockSpec((tt, 3 * HW), lambda i: (i, 0))
    blkp = pl.BlockSpec((tt, 3 * HW), lambda i: (jnp.maximum(i - 1, 0), 0))
    out = pl.BlockSpec((tt, HW), lambda i: (i, 0))
    sh = jax.ShapeDtypeStruct((T, HW), F32)
    return pl.pallas_call(body, name=name, grid=(T // tt,), in_specs=[blk, blkp, pl.BlockSpec((CONV_K, 3 * HW), lambda i: (0, 0))],
                          out_specs=(out, out, out), out_shape=(sh, sh, sh), compiler_params=_params(("parallel",)))(pm, pm, conv_w)


def gdn_prep_bwd(pm, conv_w, dq, dk, dv, name="gdn_prep_bwd"):
    T = pm.shape[0]
    HW = GDN_HEADS * GDN_D
    tt = _pick(T, (256, 128, 64, 8))
    qscale = GDN_D ** -0.5

    def body(x_ref, xp_ref, w_ref, dq_ref, dk_ref, dv_ref, dy_ref, dw_ref):
        y, xs = _conv_pre(x_ref, xp_ref, w_ref, pl.program_id(0) == 0)
        sg = _sigmoid(y)
        s = y * sg
        dsilu = sg * (1.0 + y * (1.0 - sg))
        for h in range(GDN_HEADS):
            for part, d_ref, sc in ((0, dq_ref, qscale), (1, dk_ref, 1.0)):
                lo = part * HW + h * GDN_D
                sl = s[:, lo:lo + GDN_D]
                r = lax.rsqrt(jnp.sum(sl * sl, axis=-1, keepdims=True) + EPS)
                n = sl * r
                dn = d_ref[:, h * GDN_D:(h + 1) * GDN_D] * sc
                ds = r * (dn - n * jnp.sum(dn * n, axis=-1, keepdims=True))
                dy_ref[:, lo:lo + GDN_D] = ds * dsilu[:, lo:lo + GDN_D]
        dy_ref[:, 2 * HW:] = dv_ref[...] * dsilu[:, 2 * HW:]

        @pl.when(pl.program_id(0) == 0)
        def _():
            dw_ref[...] = jnp.zeros_like(dw_ref)

        dyv = dy_ref[...]
        for j in range(CONV_K):
            dw_ref[j:j + 1, :] += jnp.sum(dyv * xs[j], axis=0, keepdims=True)

    blk = pl.BlockSpec((tt, 3 * HW), lambda i: (i, 0))
    blkp = pl.BlockSpec((tt, 3 * HW), lambda i: (jnp.maximum(i - 1, 0), 0))
    hb = pl.BlockSpec((tt, HW), lambda i: (i, 0))
    wb = pl.BlockSpec((CONV_K, 3 * HW), lambda i: (0, 0))
    return pl.pallas_call(body, name=name, grid=(T // tt,), in_specs=[blk, blkp, wb, hb, hb, hb], out_specs=(blk, wb),
                          out_shape=(jax.ShapeDtypeStruct((T, 3 * HW), F32), jax.ShapeDtypeStruct((CONV_K, 3 * HW), F32)),
                          compiler_params=_params(("arbitrary",)))(pm, pm, conv_w, dq, dk, dv)


def conv_dx(dy, conv_w, name="conv_dx"):
    T, W = dy.shape
    tt = _pick(T, (256, 128, 64, 8))
    nt = T // tt

    def body(d_ref, dn_ref, w_ref, dx_ref):
        cur = d_ref[...]
        nxt = jnp.where(pl.program_id(0) == nt - 1, 0.0, dn_ref[...])
        row = lax.broadcasted_iota(jnp.int32, cur.shape, 0)
        acc = cur * w_ref[CONV_K - 1:CONV_K, :]
        for j in range(CONV_K - 1):
            k = CONV_K - 1 - j
            sh = jnp.where(row >= tt - k, pltpu.roll(nxt, tt - k, 0), pltpu.roll(cur, tt - k, 0))
            acc = acc + sh * w_ref[j:j + 1, :]
        dx_ref[...] = acc

    blk = pl.BlockSpec((tt, W), lambda i: (i, 0))
    blkn = pl.BlockSpec((tt, W), lambda i: (jnp.minimum(i + 1, nt - 1), 0))
    return pl.pallas_call(body, name=name, grid=(nt,), in_specs=[blk, blkn, pl.BlockSpec((CONV_K, W), lambda i: (0, 0))], out_specs=blk,
                          out_shape=jax.ShapeDtypeStruct((T, W), F32), compiler_params=_params(("parallel",)))(dy, dy, conv_w)


def _upper_ones(c):
    return (lax.broadcasted_iota(jnp.int32, (c, c), 0) <= lax.broadcasted_iota(jnp.int32, (c, c), 1)).astype(F32)


def gdn_gates_fwd(a_r, b_r, alog_c, dt_c, name="gdn_gates_fwd"):
    R, C = a_r.shape

    def body(a_ref, b_ref, al_ref, dt_ref, gc_ref, beta_ref):
        x = a_ref[...] + dt_ref[...]
        sp = jnp.maximum(x, 0.0) + jnp.log1p(jnp.exp(-jnp.abs(x)))
        g = -jnp.exp(al_ref[...]) * sp
        gc_ref[...] = jnp.dot(g, _upper_ones(C), preferred_element_type=F32, precision=HI)
        beta_ref[...] = _sigmoid(b_ref[...])

    sh = jax.ShapeDtypeStruct((R, C), F32)
    return pl.pallas_call(body, name=name, out_shape=(sh, sh), compiler_params=_params())(a_r, b_r, alog_c, dt_c)


def gdn_gates_bwd(a_r, b_r, alog_c, dt_c, dgc, dbeta, name="gdn_gates_bwd"):
    R, C = a_r.shape

    def body(a_ref, b_ref, al_ref, dt_ref, dgc_ref, dbeta_ref, da_ref, db_ref, dal_ref, ddt_ref):
        x = a_ref[...] + dt_ref[...]
        sp = jnp.maximum(x, 0.0) + jnp.log1p(jnp.exp(-jnp.abs(x)))
        ea = jnp.exp(al_ref[...])
        dg = lax.dot_general(dgc_ref[...], _upper_ones(C), (((1,), (1,)), ((), ())), preferred_element_type=F32, precision=HI)
        dsp = dg * (-ea)
        da = dsp * _sigmoid(x)
        da_ref[...] = da
        beta = _sigmoid(b_ref[...])
        db_ref[...] = dbeta_ref[...] * beta * (1.0 - beta)
        sel = (lax.broadcasted_iota(jnp.int32, (GDN_HEADS, R), 1) % GDN_HEADS == lax.broadcasted_iota(jnp.int32, (GDN_HEADS, R), 0)).astype(F32)
        dal_ref[...] = jnp.sum(jnp.dot(sel, dg * (-ea * sp), preferred_element_type=F32, precision=HI), axis=1, keepdims=True)
        ddt_ref[...] = jnp.sum(jnp.dot(sel, da, preferred_element_type=F32, precision=HI), axis=1, keepdims=True)

    sh = jax.ShapeDtypeStruct((R, C), F32)
    s8 = jax.ShapeDtypeStruct((GDN_HEADS, 1), F32)
    return pl.pallas_call(body, name=name, out_shape=(sh, sh, s8, s8), compiler_params=_params())(a_r, b_r, alog_c, dt_c, dgc, dbeta)


def _dot(a, b):
    return jnp.dot(a, b, preferred_element_type=F32, precision=HI)


def _dot_nt(a, b):
    return lax.dot_general(a, b, (((1,), (1,)), ((), ())), preferred_element_type=F32, precision=HI)


def _dot_tn(a, b):
    return lax.dot_general(a, b, (((0,), (0,)), ((), ())), preferred_element_type=F32, precision=HI)


def _bdot(a, b, dims=(((1,), (0,)), ((), ()))):
    return lax.dot_general(a.astype(MXU_DTYPE), b.astype(MXU_DTYPE), dims, preferred_element_type=F32)


def _bdot_nt(a, b):
    return _bdot(a, b, (((1,), (1,)), ((), ())))


def _bdot_tn(a, b):
    return _bdot(a, b, (((0,), (0,)), ((), ())))


def _unit_lower_inverses(ms):
    c = ms[0].shape[0]
    eye = (lax.broadcasted_iota(jnp.int32, (c, c), 0) == lax.broadcasted_iota(jnp.int32, (c, c), 1)).astype(F32)
    ps = [-m for m in ms]
    ts = [eye + p for p in ps]
    n = 2
    while n < c:
        ps = [_dot(p, p) for p in ps]
        ts = [t + _dot(t, p) for t, p in zip(ts, ps)]
        n *= 2
    return ts


def _col(row, eye):
    c = eye.shape[0]
    return jnp.sum(jnp.where(eye, jnp.broadcast_to(row, (c, c)), 0.0), axis=1, keepdims=True)


def _row(col, eye):
    c = eye.shape[0]
    return jnp.sum(jnp.where(eye, jnp.broadcast_to(col, (c, c)), 0.0), axis=0, keepdims=True)


def _gdr_chunks(q_ref, k_ref, v_ref, gc_ref, b_ref, eye, ii, jj):
    C = eye.shape[0]
    fs = []
    for h in range(GDN_HEADS):
        sl = slice(h * GDN_D, (h + 1) * GDN_D)
        qh, kh, vh = q_ref[:, sl], k_ref[:, sl], v_ref[:, sl]
        gcr, br = gc_ref[0, h:h + 1, :], b_ref[0, h:h + 1, :]
        gcc = _col(gcr, eye)
        bc = _col(br, eye)
        causal = ii >= jj
        decay = jnp.where(causal, jnp.exp(jnp.where(causal, gcc - gcr, 0.0)), 0.0)
        decay_t = jnp.where(ii <= jj, jnp.exp(jnp.where(ii <= jj, gcr - gcc, 0.0)), 0.0)
        kb = kh * bc
        eg = jnp.exp(gcc)
        glast = gcr[:, C - 1:C]
        fs.append(dict(sl=sl, qh=qh, kh=kh, vh=vh, gcc=gcc, bc=bc, decay=decay, decay_t=decay_t, kb=kb, vb=vh * bc, eg=eg,
                       el=jnp.exp(glast), ekd=jnp.exp(glast - gcc), kbg=kb * eg,
                       m=jnp.where(ii > jj, _bdot_nt(kb, kh) * decay, 0.0)))
    for f, tinv in zip(fs, _unit_lower_inverses([f["m"] for f in fs])):
        f["tinv"] = tinv
    for f in fs:
        f["u"] = _dot(f["tinv"], f["vb"])
        f["w"] = _dot(f["tinv"], f["kbg"])
        f["a"] = _bdot_nt(f["qh"], f["kh"]) * f["decay"]
        f["qd"] = f["qh"] * f["eg"]
        f["kd"] = f["kh"] * f["ekd"]
    return fs


def gdr_fwd(q, k, v, gc, beta, name="gdr_fwd"):
    T = q.shape[0]
    H, DK, C = GDN_HEADS, GDN_D, CHUNK
    N = T // C

    def body(q_ref, k_ref, v_ref, gc_ref, b_ref, o_ref, st_ref, s_ref):
        @pl.when(pl.program_id(0) == 0)
        def _():
            s_ref[...] = jnp.zeros_like(s_ref)

        ii = lax.broadcasted_iota(jnp.int32, (C, C), 0)
        jj = lax.broadcasted_iota(jnp.int32, (C, C), 1)
        eye = ii == jj
        fs = _gdr_chunks(q_ref, k_ref, v_ref, gc_ref, b_ref, eye, ii, jj)
        ss = [s_ref[h] for h in range(H)]
        vnews = [f["u"] - _bdot(f["w"], s) for f, s in zip(fs, ss)]
        for h, (f, s, vnew) in enumerate(zip(fs, ss, vnews)):
            st_ref[0, h] = s
            o_ref[:, f["sl"]] = _bdot(f["qd"], s) + _bdot(f["a"], vnew)
            s_ref[h] = s * f["el"] + _bdot_tn(f["kd"], vnew)

    tok = pl.BlockSpec((C, H * DK), lambda n: (n, 0))
    gate = pl.BlockSpec((1, H, C), lambda n: (n, 0, 0))
    return pl.pallas_call(
        body, name=name, grid=(N,), in_specs=[tok, tok, tok, gate, gate],
        out_specs=(tok, pl.BlockSpec((1, H, DK, DK), lambda n: (n, 0, 0, 0))),
        out_shape=(jax.ShapeDtypeStruct((T, H * DK), F32), jax.ShapeDtypeStruct((N, H, DK, DK), F32)),
        scratch_shapes=[pltpu.VMEM((H, DK, DK), F32)], compiler_params=_params(("arbitrary",)))(q, k, v, gc, beta)


def gdr_bwd(q, k, v, gc, beta, states, do, name="gdr_bwd"):
    T = q.shape[0]
    H, DK, C = GDN_HEADS, GDN_D, CHUNK
    N = T // C

    def body(q_ref, k_ref, v_ref, gc_ref, b_ref, st_ref, do_ref, dq_ref, dk_ref, dv_ref, dgc_ref, db_ref, ds_ref):
        @pl.when(pl.program_id(0) == 0)
        def _():
            ds_ref[...] = jnp.zeros_like(ds_ref)

        ii = lax.broadcasted_iota(jnp.int32, (C, C), 0)
        jj = lax.broadcasted_iota(jnp.int32, (C, C), 1)
        eye = ii == jj
        lastj = lax.broadcasted_iota(jnp.int32, (1, C), 1) == C - 1
        fs = _gdr_chunks(q_ref, k_ref, v_ref, gc_ref, b_ref, eye, ii, jj)
        for h, f in enumerate(fs):
            f["s"] = st_ref[0, h]
            f["dsn"] = ds_ref[h]
            f["dout"] = do_ref[:, f["sl"]]
        for f in fs:
            f["vnew"] = f["u"] - _bdot(f["w"], f["s"])
            f["tinv_t"] = f["tinv"].T
            f["a_t"] = _bdot_nt(f["kh"], f["qh"]) * f["decay_t"]
        for f in fs:
            f["dvnew"] = _bdot(f["a_t"], f["dout"]) + _bdot(f["kd"], f["dsn"])
            f["da"] = _bdot_nt(f["dout"], f["vnew"])
            f["da_t"] = _bdot_nt(f["vnew"], f["dout"])
            f["dqd"] = _bdot_nt(f["dout"], f["s"])
            f["dkd"] = _bdot_nt(f["vnew"], f["dsn"])
        for h, f in enumerate(fs):
            ds_ref[h] = _bdot_tn(f["qd"], f["dout"]) - _bdot_tn(f["w"], f["dvnew"]) + f["dsn"] * f["el"]
            f["dw"] = -_bdot_nt(f["dvnew"], f["s"])
        for f in fs:
            f["dvb"] = _dot(f["tinv_t"], f["dvnew"])
            f["dkbg"] = _dot(f["tinv_t"], f["dw"])
        for f in fs:
            f["dm"] = jnp.where(ii > jj, -(_bdot_nt(f["dvb"], f["u"]) + _bdot_nt(f["dkbg"], f["w"])), 0.0)
            f["dm_t"] = jnp.where(ii < jj, -(_bdot_nt(f["u"], f["dvb"]) + _bdot_nt(f["w"], f["dkbg"])), 0.0)
        for h, f in enumerate(fs):
            sl, qh, kh, vh = f["sl"], f["qh"], f["kh"], f["vh"]
            dkd, dqd, dkbg, dvb = f["dkd"], f["dqd"], f["dkbg"], f["dvb"]
            dkk = f["dm"] * f["decay"]
            dkk_t = f["dm_t"] * f["decay_t"]
            dqk = f["da"] * f["decay"]
            dqk_t = f["da_t"] * f["decay_t"]
            e = f["dm"] * f["m"] + f["da"] * f["a"]
            dkb = _bdot(dkk, kh) + dkbg * f["eg"]
            dq_ref[:, sl] = _bdot(dqk, kh) + dqd * f["eg"]
            dk_ref[:, sl] = _bdot(dkk_t, f["kb"]) + _bdot(dqk_t, qh) + dkd * f["ekd"] + dkb * f["bc"]
            dv_ref[:, sl] = dvb * f["bc"]
            skd = jnp.sum(dkd * f["kd"], axis=1, keepdims=True)
            dgc_col = (jnp.sum(e, axis=1, keepdims=True) + jnp.sum(dqd * f["qd"], axis=1, keepdims=True)
                       + jnp.sum(dkbg * f["kbg"], axis=1, keepdims=True) - skd)
            dglast = jnp.sum(f["dsn"] * f["s"]) * f["el"] + jnp.sum(skd)
            dgc_row = _row(dgc_col, eye) - jnp.sum(e, axis=0, keepdims=True)
            dgc_ref[0, h:h + 1, :] = dgc_row + jnp.where(lastj, dglast, 0.0)
            dbeta_col = jnp.sum(dkb * kh, axis=1, keepdims=True) + jnp.sum(dvb * vh, axis=1, keepdims=True)
            db_ref[0, h:h + 1, :] = _row(dbeta_col, eye)

    rev = lambda n: (N - 1 - n, 0)
    tok = pl.BlockSpec((C, H * DK), rev)
    gate = pl.BlockSpec((1, H, C), lambda n: (N - 1 - n, 0, 0))
    tsh = jax.ShapeDtypeStruct((T, H * DK), F32)
    gsh = jax.ShapeDtypeStruct((N, H, C), F32)
    return pl.pallas_call(
        body, name=name, grid=(N,),
        in_specs=[tok, tok, tok, gate, gate, pl.BlockSpec((1, H, DK, DK), lambda n: (N - 1 - n, 0, 0, 0)), tok],
        out_specs=(tok, tok, tok, gate, gate), out_shape=(tsh, tsh, tsh, gsh, gsh),
        scratch_shapes=[pltpu.VMEM((H, DK, DK), F32)], compiler_params=_params(("arbitrary",)))(q, k, v, gc, beta, states, do)


def gdn_gate_fwd(o, pm, out_norm, *, z_block, name="gdn_gate_fwd"):
    T, HW = o.shape
    tt = _pick(T, (256, 128, 64, 8))

    def body(o_ref, z_ref, g_ref, y_ref):
        for h in range(GDN_HEADS):
            sl = slice(h * GDN_D, (h + 1) * GDN_D)
            ov = o_ref[:, sl]
            zv = z_ref[:, sl]
            r = lax.rsqrt(jnp.mean(ov * ov, axis=-1, keepdims=True) + EPS)
            y_ref[:, sl] = (ov * r * g_ref[...] * (zv * _sigmoid(zv))).astype(y_ref.dtype)

    blk = pl.BlockSpec((tt, HW), lambda i: (i, 0))
    return pl.pallas_call(body, name=name, grid=(T // tt,),
                          in_specs=[blk, pl.BlockSpec((tt, HW), lambda i: (i, z_block)), pl.BlockSpec((1, GDN_D), lambda i: (0, 0))],
                          out_specs=blk, out_shape=jax.ShapeDtypeStruct((T, HW), BF16),
                          compiler_params=_params(("parallel",)))(o, pm, out_norm.reshape(1, GDN_D))


def gdn_gate_bwd(o, pm, out_norm, dy, *, z_block, name="gdn_gate_bwd"):
    T, HW = o.shape
    tt = _pick(T, (256, 128, 64, 8))

    def body(o_ref, z_ref, g_ref, dy_ref, do_ref, dz_ref, dg_ref):
        @pl.when(pl.program_id(0) == 0)
        def _():
            dg_ref[...] = jnp.zeros_like(dg_ref)

        acc = jnp.zeros((1, GDN_D), F32)
        for h in range(GDN_HEADS):
            sl = slice(h * GDN_D, (h + 1) * GDN_D)
            ov = o_ref[:, sl]
            zv = z_ref[:, sl]
            dyv = dy_ref[:, sl]
            r = lax.rsqrt(jnp.mean(ov * ov, axis=-1, keepdims=True) + EPS)
            xhat = ov * r
            sg = _sigmoid(zv)
            sz = zv * sg
            dn = dyv * sz
            dz_ref[:, sl] = dyv * (xhat * g_ref[...]) * (sg * (1.0 + zv * (1.0 - sg)))
            acc = acc + jnp.sum(dn * xhat, axis=0, keepdims=True)
            dxhat = dn * g_ref[...]
            do_ref[:, sl] = r * (dxhat - xhat * jnp.mean(dxhat * xhat, axis=-1, keepdims=True))
        dg_ref[...] += acc

    blk = pl.BlockSpec((tt, HW), lambda i: (i, 0))
    gb = pl.BlockSpec((1, GDN_D), lambda i: (0, 0))
    sh = jax.ShapeDtypeStruct((T, HW), F32)
    return pl.pallas_call(body, name=name, grid=(T // tt,),
                          in_specs=[blk, pl.BlockSpec((tt, HW), lambda i: (i, z_block)), gb, blk],
                          out_specs=(blk, blk, gb), out_shape=(sh, sh, jax.ShapeDtypeStruct((1, GDN_D), F32)),
                          compiler_params=_params(("arbitrary",)))(o, pm, out_norm.reshape(1, GDN_D), dy)


def _pool_bands(tt, win, t0):
    t = lax.broadcasted_iota(jnp.int32, (tt, tt), 0)
    s = lax.broadcasted_iota(jnp.int32, (tt, tt), 1)
    inv = 1.0 / jnp.minimum(t + t0 + 1, win).astype(F32)
    cur = jnp.where((s <= t) & (s > t - win), inv, 0.0)
    prev = jnp.where(s - tt > t - win, inv, 0.0)
    return cur, prev


def _pool_diff(u_ref, up_ref, g, tt, t0, first):
    sl = slice(g * POOL_GW, (g + 1) * POOL_GW)
    cur, prev = _pool_bands(tt, POOL_WINDOWS[g], t0)
    ug = u_ref[:, sl]
    upg = jnp.where(first, 0.0, up_ref[:, sl])
    return _dot(cur, ug) + _dot(prev, upg) - ug


def pool_fwd(pm, pool_w, pool_scale, *, u_block, name="pool_fwd"):
    T = pm.shape[0]
    PW = len(POOL_WINDOWS) * POOL_GW
    tt = _pick(T, (256, 128, 64, 16))

    def body(u_ref, up_ref, w_ref, s_ref, p_ref):
        i = pl.program_id(0)
        for g in range(len(POOL_WINDOWS)):
            sl = slice(g * POOL_GW, (g + 1) * POOL_GW)
            diff = _pool_diff(u_ref, up_ref, g, tt, i * tt, i == 0)
            y = jnp.dot(diff.astype(MXU_DTYPE), w_ref[g].astype(MXU_DTYPE), preferred_element_type=F32)
            p_ref[:, sl] = (y * s_ref[:, sl]).astype(p_ref.dtype)

    return pl.pallas_call(
        body, name=name, grid=(T // tt,),
        in_specs=[pl.BlockSpec((tt, PW), lambda i: (i, u_block)), pl.BlockSpec((tt, PW), lambda i: (jnp.maximum(i - 1, 0), u_block)),
                  pl.BlockSpec((len(POOL_WINDOWS), POOL_GW, POOL_GW), lambda i: (0, 0, 0)), pl.BlockSpec((1, PW), lambda i: (0, 0))],
        out_specs=pl.BlockSpec((tt, PW), lambda i: (i, 0)), out_shape=jax.ShapeDtypeStruct((T, PW), BF16),
        compiler_params=_params(("parallel",)))(pm, pm, pool_w, pool_scale.reshape(1, PW))


def pool_bwd_a(pm, pool_w, pool_scale, dp, *, u_block, name="pool_bwd_a"):
    T = pm.shape[0]
    G = len(POOL_WINDOWS)
    PW = G * POOL_GW
    tt = _pick(T, (256, 128, 64, 16))

    def body(u_ref, up_ref, w_ref, s_ref, dp_ref, dd_ref, dw_ref, dsc_ref):
        i = pl.program_id(0)

        @pl.when(i == 0)
        def _():
            dw_ref[...] = jnp.zeros_like(dw_ref)
            dsc_ref[...] = jnp.zeros_like(dsc_ref)

        for g in range(G):
            sl = slice(g * POOL_GW, (g + 1) * POOL_GW)
            diff = _pool_diff(u_ref, up_ref, g, tt, i * tt, i == 0).astype(MXU_DTYPE)
            wg = w_ref[g].astype(MXU_DTYPE)
            dpv = dp_ref[:, sl]
            y = jnp.dot(diff, wg, preferred_element_type=F32)
            dsc_ref[:, sl] += jnp.sum(dpv * y, axis=0, keepdims=True)
            dy = (dpv * s_ref[:, sl]).astype(MXU_DTYPE)
            dd_ref[:, sl] = lax.dot_general(dy, wg, (((1,), (1,)), ((), ())), preferred_element_type=F32)
            dw_ref[g] += lax.dot_general(diff, dy, (((0,), (0,)), ((), ())), preferred_element_type=F32)

    wb = pl.BlockSpec((G, POOL_GW, POOL_GW), lambda i: (0, 0, 0))
    sb = pl.BlockSpec((1, PW), lambda i: (0, 0))
    blk = pl.BlockSpec((tt, PW), lambda i: (i, 0))
    return pl.pallas_call(
        body, name=name, grid=(T // tt,),
        in_specs=[pl.BlockSpec((tt, PW), lambda i: (i, u_block)), pl.BlockSpec((tt, PW), lambda i: (jnp.maximum(i - 1, 0), u_block)), wb, sb, blk],
        out_specs=(blk, wb, sb),
        out_shape=(jax.ShapeDtypeStruct((T, PW), F32), jax.ShapeDtypeStruct((G, POOL_GW, POOL_GW), F32), jax.ShapeDtypeStruct((1, PW), F32)),
        compiler_params=_params(("arbitrary",)))(pm, pm, pool_w, pool_scale.reshape(1, PW), dp)


def pool_bwd_b(dd, name="pool_bwd_b"):
    T, PW = dd.shape
    tt = _pick(T, (256, 128, 64, 16))
    nt = T // tt

    def body(d_ref, dn_ref, du_ref):
        i = pl.program_id(0)
        s = lax.broadcasted_iota(jnp.int32, (tt, tt), 0)
        t = lax.broadcasted_iota(jnp.int32, (tt, tt), 1)
        for g, win in enumerate(POOL_WINDOWS):
            sl = slice(g * POOL_GW, (g + 1) * POOL_GW)
            inv_c = 1.0 / jnp.minimum(t + i * tt + 1, win).astype(F32)
            cur = jnp.where((t >= s) & (t < s + win), inv_c, 0.0)
            nxt = jnp.where(t + tt < s + win, 1.0 / win, 0.0)
            dg = d_ref[:, sl]
            dng = jnp.where(i == nt - 1, 0.0, dn_ref[:, sl])
            du_ref[:, sl] = _dot(cur, dg) + _dot(nxt, dng) - dg

    blk = pl.BlockSpec((tt, PW), lambda i: (i, 0))
    return pl.pallas_call(body, name=name, grid=(nt,), in_specs=[blk, pl.BlockSpec((tt, PW), lambda i: (jnp.minimum(i + 1, nt - 1), 0))],
                          out_specs=blk, out_shape=jax.ShapeDtypeStruct((T, PW), F32), compiler_params=_params(("parallel",)))(dd, dd)


def _rope_consts():
    j = jnp.arange(QK_HEAD)
    inv_freq = ROPE_THETA ** (-jnp.arange(0, ROPE, 2, dtype=F32) / ROPE)
    freq = jnp.where(j >= NOPE, inv_freq[(j - NOPE) % (ROPE // 2)], 0.0).astype(F32)
    half = ROPE // 2
    src = jnp.arange(QK_HEAD)[:, None]
    dst = jnp.arange(QK_HEAD)[None, :]
    first = (dst >= NOPE) & (dst < NOPE + half)
    second = dst >= NOPE + half
    p = jnp.where(first & (src == dst + half), -1.0, 0.0) + jnp.where(second & (src == dst - half), 1.0, 0.0)
    return freq.reshape(1, QK_HEAD), p.astype(F32)


def rope_tables(pos_col, freq, name="rope_tables"):
    T = pos_col.shape[0]
    tt = _pick(T, (512, 256, 128, 64, 8))

    def body(p_ref, f_ref, c_ref, s_ref):
        ang = p_ref[...] * f_ref[...]
        rot = lax.broadcasted_iota(jnp.int32, ang.shape, 1) >= NOPE
        c_ref[...] = jnp.where(rot, jnp.cos(ang), 1.0)
        s_ref[...] = jnp.where(rot, jnp.sin(ang), 0.0)

    blk = pl.BlockSpec((tt, QK_HEAD), lambda i: (i, 0))
    sh = jax.ShapeDtypeStruct((T, QK_HEAD), F32)
    return pl.pallas_call(body, name=name, grid=(T // tt,), in_specs=[pl.BlockSpec((tt, 1), lambda i: (i, 0)), pl.BlockSpec((1, QK_HEAD), lambda i: (0, 0))],
                          out_specs=(blk, blk), out_shape=(sh, sh), compiler_params=_params(("parallel",)))(pos_col, freq)


def _seg_stats(t):
    lane = lax.broadcasted_iota(jnp.int32, t.shape, 1)
    nope = lane < NOPE
    sq = t * t
    r = jnp.where(nope, lax.rsqrt(jnp.sum(jnp.where(nope, sq, 0.0), axis=-1, keepdims=True) / NOPE + EPS),
                  lax.rsqrt(jnp.sum(jnp.where(nope, 0.0, sq), axis=-1, keepdims=True) / ROPE + EPS))
    return nope, r


def headnorm_rope_fwd(t, gain, cos, sin, pmat, name="headnorm_rope_fwd"):
    H, T, W = t.shape
    tt = _pick(T, (512, 256, 128, 64, 8))

    def body(t_ref, g_ref, c_ref, s_ref, p_ref, o_ref):
        tv = t_ref[0]
        _, r = _seg_stats(tv)
        y = tv * r * g_ref[...]
        o_ref[0] = (y * c_ref[...] + _dot(y, p_ref[...]) * s_ref[...]).astype(o_ref.dtype)

    blk = pl.BlockSpec((1, tt, W), lambda h, i: (h, i, 0))
    tab = pl.BlockSpec((tt, W), lambda h, i: (i, 0))
    return pl.pallas_call(body, name=name, grid=(H, T // tt),
                          in_specs=[blk, pl.BlockSpec((1, W), lambda h, i: (0, 0)), tab, tab, pl.BlockSpec((W, W), lambda h, i: (0, 0))],
                          out_specs=blk, out_shape=jax.ShapeDtypeStruct((H, T, W), BF16),
                          compiler_params=_params(("parallel", "parallel")))(t, gain.reshape(1, W), cos, sin, pmat)


def headnorm_rope_bwd(t, gain, cos, sin, pmat_t, dout, name="headnorm_rope_bwd"):
    H, T, W = t.shape
    tt = _pick(T, (512, 256, 128, 64, 8))

    def body(t_ref, g_ref, c_ref, s_ref, p_ref, do_ref, dt_ref, dg_ref):
        @pl.when((pl.program_id(0) == 0) & (pl.program_id(1) == 0))
        def _():
            dg_ref[...] = jnp.zeros_like(dg_ref)

        tv = t_ref[0]
        dov = do_ref[0]
        nope, r = _seg_stats(tv)
        dy = dov * c_ref[...] + _dot(dov * s_ref[...], p_ref[...])
        xhat = tv * r
        dg_ref[...] += jnp.sum(dy * xhat, axis=0, keepdims=True)
        dxhat = dy * g_ref[...]
        pr = dxhat * xhat
        mean = jnp.where(nope, jnp.sum(jnp.where(nope, pr, 0.0), axis=-1, keepdims=True) / NOPE,
                         jnp.sum(jnp.where(nope, 0.0, pr), axis=-1, keepdims=True) / ROPE)
        dt_ref[0] = r * (dxhat - xhat * mean)

    blk = pl.BlockSpec((1, tt, W), lambda h, i: (h, i, 0))
    tab = pl.BlockSpec((tt, W), lambda h, i: (i, 0))
    gb = pl.BlockSpec((1, W), lambda h, i: (0, 0))
    return pl.pallas_call(body, name=name, grid=(H, T // tt),
                          in_specs=[blk, gb, tab, tab, pl.BlockSpec((W, W), lambda h, i: (0, 0)), blk],
                          out_specs=(blk, gb), out_shape=(jax.ShapeDtypeStruct((H, T, W), F32), jax.ShapeDtypeStruct((1, W), F32)),
                          compiler_params=_params(("arbitrary", "arbitrary")))(t, gain.reshape(1, W), cos, sin, pmat_t, dout)


def sum_heads(x, name="sum_heads"):
    H, T, W = x.shape
    tt = _pick(T, (512, 256, 128, 64, 8))

    def body(x_ref, o_ref):
        @pl.when(pl.program_id(1) == 0)
        def _():
            o_ref[...] = jnp.zeros_like(o_ref)

        o_ref[...] += x_ref[0]

    return pl.pallas_call(body, name=name, grid=(T // tt, H), in_specs=[pl.BlockSpec((1, tt, W), lambda i, h: (h, i, 0))],
                          out_specs=pl.BlockSpec((tt, W), lambda i, h: (i, 0)), out_shape=jax.ShapeDtypeStruct((T, W), F32),
                          compiler_params=_params(("parallel", "arbitrary")))(x)


_NT = (((1,), (1,)), ((), ()))
_TN = (((0,), (0,)), ((), ()))
ATTN_SUB = 256


def _attn_fwd_tiles(T):
    tq = _pick(T, (512, 256, 128, 64))
    tk = _pick(T, (2048, 1024, 512, 256, 128, 64))
    return tq, tk, min(ATTN_SUB, tq, tk)


def _attn_bwd_tiles(T):
    tq = _pick(T, (1024, 512, 256, 128, 64))
    tk = _pick(T, (512, 256, 128, 64))
    return tq, tk, min(ATTN_SUB, tq, tk)


def attn_fwd(q, k, v, name="attn_fwd", tiles=None):
    H, T, DQ = q.shape
    DV = v.shape[2]
    tq, tk, sub = tiles or _attn_fwd_tiles(T)
    nq, nk, nsub = T // tq, T // tk, tk // sub
    scale = DQ ** -0.5

    def body(q_ref, k_ref, v_ref, o_ref, l_ref, m_s, l_s, acc_s):
        i, j = pl.program_id(1), pl.program_id(2)

        @pl.when(j == 0)
        def _():
            m_s[...] = jnp.full_like(m_s, -1e30)
            l_s[...] = jnp.zeros_like(l_s)
            acc_s[...] = jnp.zeros_like(acc_s)

        def tile(rel):
            qv = q_ref[0]
            m_old = m_s[...]
            m_new = m_old
            ss = {}
            for c in range(nsub):
                if rel is not None and c * sub > rel + tq - 1:
                    continue
                s = lax.dot_general(qv, k_ref[0, c * sub:(c + 1) * sub, :], _NT, preferred_element_type=F32) * scale
                if rel is not None and (c + 1) * sub - 1 > rel:
                    row = lax.broadcasted_iota(jnp.int32, s.shape, 0) + rel
                    col = lax.broadcasted_iota(jnp.int32, s.shape, 1) + c * sub
                    s = jnp.where(row >= col, s, -1e30)
                ss[c] = s
                m_new = jnp.maximum(m_new, jnp.max(s, axis=-1, keepdims=True))
            alpha = jnp.exp(m_old - m_new)
            l_new = alpha * l_s[...]
            acc = alpha * acc_s[...]
            for c, s in ss.items():
                p = jnp.exp(s - m_new)
                l_new = l_new + jnp.sum(p, axis=-1, keepdims=True)
                acc = acc + jnp.dot(p.astype(MXU_DTYPE), v_ref[0, c * sub:(c + 1) * sub, :], preferred_element_type=F32)
            l_s[...] = l_new
            acc_s[...] = acc
            m_s[...] = m_new

        rel = i * tq - j * tk
        pl.when(rel >= tk - 1)(lambda: tile(None))
        for r0 in range(0, tk - 1, tq):
            pl.when(rel == r0)(lambda r0=r0: tile(r0))

        @pl.when(j == nk - 1)
        def _():
            o_ref[0] = acc_s[...] / l_s[...]
            l_ref[0] = jnp.broadcast_to(m_s[...] + jnp.log(l_s[...]), (tq, DV))

    last = lambda i: (i * tq + (tq - 1)) // tk
    qb = pl.BlockSpec((1, tq, DQ), lambda h, i, j: (h, i, 0))
    kb = pl.BlockSpec((1, tk, DQ), lambda h, i, j: (h, jnp.minimum(j, last(i)), 0))
    vb = pl.BlockSpec((1, tk, DV), lambda h, i, j: (h, jnp.minimum(j, last(i)), 0))
    ob = pl.BlockSpec((1, tq, DV), lambda h, i, j: (h, i, 0))
    sh = jax.ShapeDtypeStruct((H, T, DV), F32)
    return pl.pallas_call(body, name=name, grid=(H, nq, nk), in_specs=[qb, kb, vb], out_specs=(ob, ob), out_shape=(sh, sh),
                          scratch_shapes=[pltpu.VMEM((tq, 1), F32), pltpu.VMEM((tq, 1), F32), pltpu.VMEM((tq, DV), F32)],
                          compiler_params=_params(("parallel", "parallel", "arbitrary")))(q, k, v)


def attn_bwd(q, k, v, o, lse, do, name="attn_bwd", tiles=None):
    H, T, DQ = q.shape
    DV = v.shape[2]
    tq, tk, sub = tiles or _attn_bwd_tiles(T)
    nq, nk, nsub = T // tq, T // tk, tq // sub
    scale = DQ ** -0.5

    def body(q_ref, k_ref, v_ref, o_ref, l_ref, do_ref, dq_ref, dk_ref, dv_ref, dk_s, dv_s):
        j, i = pl.program_id(1), pl.program_id(2)

        @pl.when((j == 0) & (i == 0))
        def _():
            dq_ref[...] = jnp.zeros_like(dq_ref)

        @pl.when(i == 0)
        def _():
            dk_s[...] = jnp.zeros_like(dk_s)
            dv_s[...] = jnp.zeros_like(dv_s)

        def tile(rel):
            kv, vv = k_ref[0], v_ref[0]
            live = [r for r in range(nsub) if rel is None or (r + 1) * sub - 1 >= rel]
            qs, dos, ss, dps = {}, {}, {}, {}
            for r in live:
                rs = slice(r * sub, (r + 1) * sub)
                qs[r] = q_ref[0, rs, :]
                dos[r] = do_ref[0, rs, :]
                ss[r] = lax.dot_general(qs[r], kv, _NT, preferred_element_type=F32) * scale
                dps[r] = lax.dot_general(dos[r].astype(MXU_DTYPE), vv, _NT, preferred_element_type=F32)
            dk_acc = dk_s[...]
            dv_acc = dv_s[...]
            for r in live:
                rs = slice(r * sub, (r + 1) * sub)
                p = jnp.exp(ss[r] - l_ref[0, rs, 0:1])
                if rel is not None and r * sub < rel + tk - 1:
                    row = lax.broadcasted_iota(jnp.int32, p.shape, 0) + r * sub
                    col = lax.broadcasted_iota(jnp.int32, p.shape, 1) + rel
                    p = jnp.where(row >= col, p, 0.0)
                delta = jnp.sum(dos[r] * o_ref[0, rs, :], axis=-1, keepdims=True)
                ds = (p * (dps[r] - delta) * scale).astype(MXU_DTYPE)
                dv_acc = dv_acc + lax.dot_general(p.astype(MXU_DTYPE), dos[r].astype(MXU_DTYPE), _TN, preferred_element_type=F32)
                dk_acc = dk_acc + lax.dot_general(ds, qs[r], _TN, preferred_element_type=F32)
                rows = pl.ds(pl.multiple_of(i * tq + r * sub, sub), sub)
                dq_ref[0, rows, :] += jnp.dot(ds, kv, preferred_element_type=F32)
            dk_s[...] = dk_acc
            dv_s[...] = dv_acc

        rel = j * tk - i * tq
        pl.when(rel <= 1 - tk)(lambda: tile(None))
        for r0 in range(0, tq, tk):
            pl.when(rel == r0)(lambda r0=r0: tile(r0))

        @pl.when(i == nq - 1)
        def _():
            dk_ref[0] = dk_s[...]
            dv_ref[0] = dv_s[...]

    first = lambda j: (j * tk) // tq
    qi = lambda h, j, i: (h, jnp.maximum(i, first(j)), 0)
    qb = pl.BlockSpec((1, tq, DQ), qi)
    ob = pl.BlockSpec((1, tq, DV), qi)
    kb = pl.BlockSpec((1, tk, DQ), lambda h, j, i: (h, j, 0))
    vb = pl.BlockSpec((1, tk, DV), lambda h, j, i: (h, j, 0))
    dqb = pl.BlockSpec((1, T, DQ), lambda h, j, i: (h, 0, 0))
    return pl.pallas_call(
        body, name=name, grid=(H, nk, nq), in_specs=[qb, kb, vb, ob, ob, ob], out_specs=(dqb, kb, vb),
        out_shape=(jax.ShapeDtypeStruct((H, T, DQ), F32), jax.ShapeDtypeStruct((H, T, DQ), F32), jax.ShapeDtypeStruct((H, T, DV), F32)),
        scratch_shapes=[pltpu.VMEM((tk, DQ), F32), pltpu.VMEM((tk, DV), F32)],
        compiler_params=_params(("parallel", "arbitrary", "arbitrary")))(q, k, v, o, lse, do)


def _mesh_place():
    return lax.axis_index("x"), lax.axis_index("y"), lax.axis_index("c")


def _flip(v, bit):
    return 1 - v if bit else v


def _relations():
    return [((r >> 2) & 1, (r >> 1) & 1, r & 1) for r in range(1, N_DEV)]


def all_gather(xs, name="all_gather"):
    def body(x_ref, o_ref, send_sems, recv_sems, local_sem):
        x, y, c = _mesh_place()
        me, sibling = (x, y, c), (x, y, 1 - c)
        chips = [(1 - x, y), (x, 1 - y), (1 - x, 1 - y)]

        def slot(px, py, pc):
            return o_ref.at[4 * px + 2 * py + pc]

        def copy(k, block, to, src=None):
            return pltpu.make_async_remote_copy(
                src_ref=slot(*block) if src is None else src, dst_ref=slot(*block), send_sem=send_sems.at[k], recv_sem=recv_sems.at[k],
                device_id=to, device_id_type=pl.DeviceIdType.MESH)

        mine = pltpu.make_async_copy(x_ref, slot(*me), local_sem)
        mine.start()
        first = [copy(0, me, sibling, src=x_ref)] + [copy(1 + j, me, (*chip, c), src=x_ref) for j, chip in enumerate(chips)]
        for cp in first:
            cp.start()
        passed = [copy(4 + j, (*chip, c), sibling) for j, chip in enumerate(chips)]
        for j, chip in enumerate(chips):
            copy(1 + j, (*chip, c), me).wait_recv()
            passed[j].start()
        copy(0, sibling, me).wait_recv()
        for j, chip in enumerate(chips):
            copy(4 + j, (*chip, 1 - c), me).wait_recv()
        for cp in first + passed:
            cp.wait_send()
        mine.wait()

    return pl.pallas_call(
        body, name=name, out_shape=jax.ShapeDtypeStruct((N_DEV, *xs.shape), xs.dtype),
        in_specs=[pl.BlockSpec(memory_space=pl.ANY)], out_specs=pl.BlockSpec(memory_space=pl.ANY),
        scratch_shapes=[pltpu.SemaphoreType.DMA((N_DEV - 1,)), pltpu.SemaphoreType.DMA((N_DEV - 1,)), pltpu.SemaphoreType.DMA(())])(xs)


def exchange(xs, name="exchange"):
    def body(x_ref, o_ref, send_sems, recv_sems, local_sem):
        x, y, c = _mesh_place()
        me = 4 * x + 2 * y + c
        local = pltpu.make_async_copy(x_ref.at[me], o_ref.at[me], local_sem)
        local.start()
        copies = []
        for r, (bx, by, bc) in enumerate(_relations()):
            px, py, pc = _flip(x, bx), _flip(y, by), _flip(c, bc)
            cp = pltpu.make_async_remote_copy(
                src_ref=x_ref.at[4 * px + 2 * py + pc], dst_ref=o_ref.at[me], send_sem=send_sems.at[r], recv_sem=recv_sems.at[r],
                device_id=(px, py, pc), device_id_type=pl.DeviceIdType.MESH)
            cp.start()
            copies.append(cp)
        for cp in copies:
            cp.wait_recv()
        for cp in copies:
            cp.wait_send()
        local.wait()

    return pl.pallas_call(
        body, name=name, out_shape=jax.ShapeDtypeStruct(xs.shape, xs.dtype),
        in_specs=[pl.BlockSpec(memory_space=pl.ANY)], out_specs=pl.BlockSpec(memory_space=pl.ANY),
        scratch_shapes=[pltpu.SemaphoreType.DMA((N_DEV - 1,)), pltpu.SemaphoreType.DMA((N_DEV - 1,)), pltpu.SemaphoreType.DMA(())])(xs)


def adamw(recv, w, m, v, layer=None, name="adamw"):
    _, R, L = recv.shape
    tr = _pick(R, [c for c in (1024, 512, 256, 128, 64, 32, 16) if c * L <= ADAMW_BLOCK_ELEMS] + [8])

    def body(r_ref, w_ref, m_ref, v_ref, g_ref, d_ref, nm_ref, nv_ref):
        g = r_ref[0].astype(F32)
        for s in range(1, N_DEV):
            g = g + r_ref[s].astype(F32)
        m_new = ADAM_B1 * m_ref[...] + (1.0 - ADAM_B1) * g
        v_new = ADAM_B2 * v_ref[...] + (1.0 - ADAM_B2) * jnp.square(g)
        m_hat = m_new / (1.0 - ADAM_B1 ** ADAM_STEP)
        v_hat = v_new / (1.0 - ADAM_B2 ** ADAM_STEP)
        g_ref[...] = g
        d_ref[...] = -ADAM_LR * (m_hat / (jnp.sqrt(v_hat) + ADAM_EPS) + ADAM_WD * w_ref[...])
        nm_ref[...] = m_new
        nv_ref[...] = v_new

    row = pl.BlockSpec((tr, L), lambda i: (i, 0))
    state = row if layer is None else pl.BlockSpec((None, tr, L), lambda i: (layer, i, 0))
    sh = jax.ShapeDtypeStruct((R, L), F32)
    return pl.pallas_call(body, name=name, grid=(R // tr,), in_specs=[pl.BlockSpec((N_DEV, tr, L), lambda i: (0, i, 0)), state, state, state],
                          out_specs=(row, row, row, row), out_shape=(sh, sh, sh, sh), compiler_params=_params(("parallel",)))(recv, w, m, v)


GRAD_WIRE_DTYPE = BF16


def _row_slabs(g):
    return g.reshape(N_DEV, g.shape[0] // N_DEV, g.shape[1])


def _col_slabs(g):
    return g.reshape(g.shape[0], N_DEV, g.shape[1] // N_DEV).transpose(1, 0, 2)


def ffn_fwd(x, gain, wg, wu, wd):
    h = rms_fwd(x, gain, name="ffn_rms_fwd")
    g = mm(h, wg, out_dtype=BF16, name="ffn_gate")
    u = mm(h, wu, out_dtype=BF16, name="ffn_up")
    a = swiglu_fwd(g, u)
    y = mm(a, wd, add=x, scale=0.5, name="ffn_down")
    return y, (x, h, g, u, a)


def ffn_bwd(dy, saved, gain, wg, wu, wd):
    x, h, g, u, a = saved
    da = mm(dy, wd, tb=True, scale=0.5, out_dtype=BF16, name="ffn_dact")
    dwd = mm(a, dy, ta=True, scale=0.5, out_dtype=GRAD_WIRE_DTYPE, name="ffn_dwd")
    dg, du = swiglu_bwd(da, g, u)
    dh = mm(dg, wg, tb=True, name="ffn_dh_gate")
    dh = mm(du, wu, tb=True, add=dh, name="ffn_dh_up")
    dwg = mm(h, dg, ta=True, out_dtype=GRAD_WIRE_DTYPE, col_slabs=N_DEV, name="ffn_dwg")
    dwu = mm(h, du, ta=True, out_dtype=GRAD_WIRE_DTYPE, col_slabs=N_DEV, name="ffn_dwu")
    dx, dgain = rms_bwd(x, gain, dh, dy, name="ffn_rms_bwd")
    return dx, dgain[0], dwg, dwu, _row_slabs(dwd)


HYB_QKVZ = 4 * GDN_HEADS * GDN_D
HYB_AB = 2 * GDN_HEADS
HYB_U = len(POOL_WINDOWS) * POOL_GW


def _hyb_split_w_in(w_in):
    main = jnp.concatenate([w_in[:, :HYB_QKVZ], w_in[:, HYB_QKVZ + HYB_AB:]], axis=1)
    ab = jnp.pad(w_in[:, HYB_QKVZ:HYB_QKVZ + HYB_AB], ((0, 0), (0, LANES - HYB_AB)))
    return main, ab


def _gate_rows(t, n):
    return t.reshape(n, CHUNK, GDN_HEADS).transpose(0, 2, 1).reshape(n * GDN_HEADS, CHUNK)


def _gate_cols(r, n):
    return r.reshape(n, GDN_HEADS, CHUNK).transpose(0, 2, 1).reshape(n * CHUNK, GDN_HEADS)


def hyb_fwd(x, p):
    T = x.shape[0]
    n = T // CHUNK
    h = rms_fwd(x, p["mix_norm"], name="mix_rms_fwd")
    w_main, w_ab = _hyb_split_w_in(p["w_in"])
    pm = mm(h, w_main, name="hyb_in_main")
    pab = mm(h, w_ab, name="hyb_in_gates")
    q, k, v = gdn_prep_fwd(pm, p["conv"])
    a_r = _gate_rows(pab[:, :GDN_HEADS], n)
    b_r = _gate_rows(pab[:, GDN_HEADS:HYB_AB], n)
    alog_c = jnp.tile(p["a_log"], n).reshape(n * GDN_HEADS, 1)
    dt_c = jnp.tile(p["dt_bias"], n).reshape(n * GDN_HEADS, 1)
    gc, beta = gdn_gates_fwd(a_r, b_r, alog_c, dt_c)
    gc3 = gc.reshape(n, GDN_HEADS, CHUNK)
    beta3 = beta.reshape(n, GDN_HEADS, CHUNK)
    o, states = gdr_fwd(q, k, v, gc3, beta3)
    og = gdn_gate_fwd(o, pm, p["out_norm"], z_block=3)
    pool = pool_fwd(pm, p["pool_w"], p["pool_scale"], u_block=4)
    mix = jnp.concatenate([og, pool], axis=1)
    y = mm(mix, p["w_out"], add=x, name="hyb_out")
    return y, dict(x=x, h=h, pm=pm, q=q, k=k, v=v, a_r=a_r, b_r=b_r, alog_c=alog_c, dt_c=dt_c, gc3=gc3, beta3=beta3,
                   o=o, states=states, mix=mix)


def hyb_bwd(dy, s, p):
    T = dy.shape[0]
    n = T // CHUNK
    half = GDN_HEADS * GDN_D
    w_main, w_ab = _hyb_split_w_in(p["w_in"])
    dmix = mm(dy, p["w_out"], tb=True, name="hyb_dmix")
    dog, dpool = dmix[:, :half], dmix[:, half:]
    dw_out = _row_slabs(mm(s["mix"], dy, ta=True, out_dtype=GRAD_WIRE_DTYPE, name="hyb_dwout"))
    do, dz, dout_norm = gdn_gate_bwd(s["o"], s["pm"], p["out_norm"], dog, z_block=3)
    dq, dk, dv, dgc, dbeta = gdr_bwd(s["q"], s["k"], s["v"], s["gc3"], s["beta3"], s["states"], do)
    da_r, db_r, dalog, ddt = gdn_gates_bwd(s["a_r"], s["b_r"], s["alog_c"], s["dt_c"],
                                           dgc.reshape(n * GDN_HEADS, CHUNK), dbeta.reshape(n * GDN_HEADS, CHUNK))
    dpab = jnp.pad(jnp.concatenate([_gate_cols(da_r, n), _gate_cols(db_r, n)], axis=1), ((0, 0), (0, LANES - HYB_AB)))
    dyc, dconv = gdn_prep_bwd(s["pm"], p["conv"], dq, dk, dv)
    dqkv = conv_dx(dyc, p["conv"])
    dd, dpool_w, dpool_scale = pool_bwd_a(s["pm"], p["pool_w"], p["pool_scale"], dpool, u_block=4)
    du = pool_bwd_b(dd)
    dpm = jnp.concatenate([dqkv, dz, du], axis=1)
    dh = mm(dpm, w_main, tb=True, name="hyb_dh_main")
    dh = mm(dpab, w_ab, tb=True, add=dh, name="hyb_dh_gates")
    dw_main = mm(s["h"], dpm, ta=True, name="hyb_dwin_main")
    dw_ab = mm(s["h"], dpab, ta=True, name="hyb_dwin_gates")
    dw_in = _col_slabs(jnp.concatenate([dw_main[:, :HYB_QKVZ], dw_ab[:, :HYB_AB], dw_main[:, HYB_QKVZ:]], axis=1)).astype(GRAD_WIRE_DTYPE)
    dx, dmix = rms_bwd(s["x"], p["mix_norm"], dh, dy, name="mix_rms_bwd")
    grads = dict(mix_norm=dmix[0], w_in=dw_in, conv=dconv, a_log=dalog[:, 0], dt_bias=ddt[:, 0], out_norm=dout_norm[0],
                 pool_w=dpool_w, pool_scale=dpool_scale[0], w_out=dw_out)
    return dx, grads


MLA_LAT = 2 * Q_LORA


def _mla_split_w_in(w_in):
    return w_in[:, :MLA_LAT], jnp.pad(w_in[:, MLA_LAT:], ((0, 0), (0, LANES - ROPE)))


def _to_heads(t, width):
    T = t.shape[0]
    return t.reshape(T, MLA_HEADS, width).transpose(1, 0, 2)


def _from_heads(t):
    H, T, W = t.shape
    return t.transpose(1, 0, 2).reshape(T, H * W)


def mla_fwd(x, p, rope):
    T = x.shape[0]
    cos, sin, pmat = rope
    h = rms_fwd(x, p["mix_norm"], name="mix_rms_fwd")
    w_main, w_pe = _mla_split_w_in(p["w_in"])
    pm = mm(h, w_main, name="mla_in_main")
    ppe = mm(h, w_pe, name="mla_in_pe")
    qn = rms_fwd(pm, p["q_norm"], width=Q_LORA, col_block=0, name="mla_lat_rms_fwd")
    kvn = rms_fwd(pm, p["kv_norm"], width=Q_LORA, col_block=1, name="mla_lat_rms_fwd")
    q3 = _to_heads(mm(qn, p["w_q_up"], name="mla_q_up"), QK_HEAD)
    kv3 = _to_heads(mm(kvn, p["w_kv_up"], name="mla_kv_up"), NOPE + V_HEAD)
    kpe = jnp.broadcast_to(ppe[None, :, :ROPE], (MLA_HEADS, T, ROPE))
    k3 = jnp.concatenate([kv3[..., :NOPE], kpe], axis=-1)
    v3 = kv3[..., NOPE:].astype(MXU_DTYPE)
    qr = headnorm_rope_fwd(q3, p["q_head_norm"], cos, sin, pmat)
    kr = headnorm_rope_fwd(k3, p["k_head_norm"], cos, sin, pmat)
    o3, lse = attn_fwd(qr, kr, v3)
    o = _from_heads(o3).astype(MXU_DTYPE)
    y = mm(o, p["w_out"], add=x, name="mla_out")
    return y, dict(x=x, h=h, pm=pm, qn=qn, kvn=kvn, q3=q3, k3=k3, v3=v3, qr=qr, kr=kr, o3=o3, lse=lse, o=o)


def mla_bwd(dy, s, p, rope):
    cos, sin, pmat = rope
    w_main, w_pe = _mla_split_w_in(p["w_in"])
    do3 = _to_heads(mm(dy, p["w_out"], tb=True, name="mla_do"), V_HEAD)
    dw_out = _row_slabs(mm(s["o"], dy, ta=True, out_dtype=GRAD_WIRE_DTYPE, name="mla_dwout"))
    dqr, dkr, dv3 = attn_bwd(s["qr"], s["kr"], s["v3"], s["o3"], s["lse"], do3)
    dq3, dqhn = headnorm_rope_bwd(s["q3"], p["q_head_norm"], cos, sin, pmat.T, dqr)
    dk3, dkhn = headnorm_rope_bwd(s["k3"], p["k_head_norm"], cos, sin, pmat.T, dkr)
    dppe = jnp.pad(sum_heads(dk3)[:, NOPE:], ((0, 0), (0, LANES - ROPE)))
    dq = _from_heads(dq3)
    dkv = _from_heads(jnp.concatenate([dk3[..., :NOPE], dv3], axis=-1))
    dqn = mm(dq, p["w_q_up"], tb=True, name="mla_dqn")
    dkvn = mm(dkv, p["w_kv_up"], tb=True, name="mla_dkvn")
    dw_q_up = mm(s["qn"], dq, ta=True, out_dtype=GRAD_WIRE_DTYPE, col_slabs=N_DEV, name="mla_dwq_up")
    dw_kv_up = mm(s["kvn"], dkv, ta=True, out_dtype=GRAD_WIRE_DTYPE, col_slabs=N_DEV, name="mla_dwkv_up")
    dqlat, dq_norm = rms_bwd(s["pm"], p["q_norm"], dqn, width=Q_LORA, col_block=0, name="mla_lat_rms_bwd")
    dkvlat, dkv_norm = rms_bwd(s["pm"], p["kv_norm"], dkvn, width=Q_LORA, col_block=1, name="mla_lat_rms_bwd")
    dpm = jnp.concatenate([dqlat, dkvlat], axis=1)
    dh = mm(dpm, w_main, tb=True, name="mla_dh_main")
    dh = mm(dppe, w_pe, tb=True, add=dh, name="mla_dh_pe")
    dw_in = _row_slabs(jnp.concatenate([mm(s["h"], dpm, ta=True, name="mla_dwin_main"),
                                        mm(s["h"], dppe, ta=True, name="mla_dwin_pe")[:, :ROPE]], axis=1)).astype(GRAD_WIRE_DTYPE)
    dx, dmix = rms_bwd(s["x"], p["mix_norm"], dh, dy, name="mix_rms_bwd")
    grads = dict(mix_norm=dmix[0], w_in=dw_in, q_norm=dq_norm[0], kv_norm=dkv_norm[0], w_q_up=dw_q_up, w_kv_up=dw_kv_up,
                 q_head_norm=dqhn[0], k_head_norm=dkhn[0], w_out=dw_out)
    return dx, grads


WEIGHTS = ['ffn1_norm', 'ffn1_w_gate', 'ffn1_w_up', 'ffn1_w_down', 'mix_norm', 'ffn2_norm', 'ffn2_w_gate', 'ffn2_w_up',
           'ffn2_w_down', 'hyb_w_in', 'gdn_conv', 'gdn_a_log', 'gdn_dt_bias', 'gdn_out_norm', 'pool_w', 'pool_scale',
           'hyb_w_out', 'mla_w_in', 'mla_q_norm', 'mla_kv_norm', 'mla_w_q_up', 'mla_w_kv_up', 'mla_q_head_norm',
           'mla_k_head_norm', 'mla_w_out']
SHARD_AXIS = dict(ffn1_norm=None, ffn1_w_gate=2, ffn1_w_up=2, ffn1_w_down=1, mix_norm=None, ffn2_norm=None, ffn2_w_gate=2,
                  ffn2_w_up=2, ffn2_w_down=1, hyb_w_in=2, gdn_conv=2, gdn_a_log=None, gdn_dt_bias=None, gdn_out_norm=None,
                  pool_w=2, pool_scale=None, hyb_w_out=1, mla_w_in=1, mla_q_norm=1, mla_kv_norm=1, mla_w_q_up=2,
                  mla_w_kv_up=2, mla_q_head_norm=None, mla_k_head_norm=None, mla_w_out=1)
GATHER_BF16 = ['ffn1_w_gate', 'ffn1_w_up', 'ffn1_w_down', 'ffn2_w_gate', 'ffn2_w_up', 'ffn2_w_down', 'hyb_w_in', 'pool_w',
               'hyb_w_out', 'mla_w_in', 'mla_w_q_up', 'mla_w_kv_up', 'mla_w_out']
GATHER_F32 = ['gdn_conv', 'mla_q_norm', 'mla_kv_norm']
LARGE = ['ffn1_w_gate', 'ffn1_w_up', 'ffn1_w_down', 'ffn2_w_gate', 'ffn2_w_up', 'ffn2_w_down', 'hyb_w_in', 'hyb_w_out',
         'mla_w_in', 'mla_w_q_up', 'mla_w_kv_up', 'mla_w_out']
SMALL = [n for n in WEIGHTS if n not in LARGE]
SUBLANES = 8


def _pack(flat_list, lead=()):
    flat = jnp.concatenate(flat_list, axis=-1)
    rows = -(-flat.shape[-1] // (LANES * SUBLANES)) * SUBLANES
    flat = jnp.pad(flat, [(0, 0)] * len(lead) + [(0, rows * LANES - flat.shape[-1])])
    return flat.reshape(*lead, rows, LANES)


def _unpack(packed, shapes, lead=()):
    flat = packed.reshape(*lead, -1)
    out, off = [], 0
    for sh in shapes:
        n = math.prod(sh)
        out.append(flat[..., off:off + n].reshape(*lead, *sh))
        off += n
    return out


def _to_slabs(g, axis):
    if axis is None:
        return jnp.broadcast_to(g.reshape(1, -1), (N_DEV, g.size))
    sh = g.shape
    g = g.reshape(*sh[:axis], N_DEV, sh[axis] // N_DEV, *sh[axis + 1:])
    return jnp.moveaxis(g, axis, 0).reshape(N_DEV, -1)


def _from_shards(t, axis):
    t = jnp.moveaxis(t, 0, axis)
    sh = t.shape
    return t.reshape(*sh[:axis], sh[axis] * sh[axis + 1], *sh[axis + 2:])


def _gather_packed(local, names, dtype, name):
    got = all_gather(_pack([local[n].astype(dtype).reshape(-1) for n in names]), name=name)
    parts = _unpack(got, [local[n].shape for n in names], lead=(N_DEV,))
    return {n: _from_shards(t, SHARD_AXIS[n]) for n, t in zip(names, parts)}


def kernel(x, positions, ffn1_norm, ffn1_w_gate, ffn1_w_up, ffn1_w_down, mix_norm, ffn2_norm, ffn2_w_gate, ffn2_w_up, ffn2_w_down, hyb_w_in, gdn_conv, gdn_a_log, gdn_dt_bias, gdn_out_norm, pool_w, pool_scale, hyb_w_out, mla_w_in, mla_q_norm, mla_kv_norm, mla_w_q_up, mla_w_kv_up, mla_q_head_norm, mla_k_head_norm, mla_w_out, loss_target, m_ffn1_norm, m_ffn1_w_gate, m_ffn1_w_up, m_ffn1_w_down, m_mix_norm, m_ffn2_norm, m_ffn2_w_gate, m_ffn2_w_up, m_ffn2_w_down, m_hyb_w_in, m_gdn_conv, m_gdn_a_log, m_gdn_dt_bias, m_gdn_out_norm, m_pool_w, m_pool_scale, m_hyb_w_out, m_mla_w_in, m_mla_q_norm, m_mla_kv_norm, m_mla_w_q_up, m_mla_w_kv_up, m_mla_q_head_norm, m_mla_k_head_norm, m_mla_w_out, v_ffn1_norm, v_ffn1_w_gate, v_ffn1_w_up, v_ffn1_w_down, v_mix_norm, v_ffn2_norm, v_ffn2_w_gate, v_ffn2_w_up, v_ffn2_w_down, v_hyb_w_in, v_gdn_conv, v_gdn_a_log, v_gdn_dt_bias, v_gdn_out_norm, v_pool_w, v_pool_scale, v_hyb_w_out, v_mla_w_in, v_mla_q_norm, v_mla_kv_norm, v_mla_w_q_up, v_mla_w_kv_up, v_mla_q_head_norm, v_mla_k_head_norm, v_mla_w_out):
    given = dict(locals())
    local = {n: given[n] for n in WEIGHTS}
    depth = ffn1_norm.shape[0]
    xs = x[0]
    T = xs.shape[0]

    full = dict(local)
    for n in GATHER_BF16:
        full[n] = _from_shards(all_gather(local[n].astype(MXU_DTYPE), name="gather_" + n), SHARD_AXIS[n])
    full.update(_gather_packed(local, GATHER_F32, F32, "gather_f32"))

    freq, pmat = _rope_consts()
    cos, sin = rope_tables(positions[0].astype(F32).reshape(T, 1), freq)
    rope = (cos, sin, pmat)

    def mixer_params(layer):
        i = layer // 2
        if layer % 2 == 0:
            return dict(mix_norm=full["mix_norm"][layer], w_in=full["hyb_w_in"][i], conv=full["gdn_conv"][i], a_log=full["gdn_a_log"][i],
                        dt_bias=full["gdn_dt_bias"][i], out_norm=full["gdn_out_norm"][i], pool_w=full["pool_w"][i],
                        pool_scale=full["pool_scale"][i], w_out=full["hyb_w_out"][i])
        return dict(mix_norm=full["mix_norm"][layer], w_in=full["mla_w_in"][i], q_norm=full["mla_q_norm"][i], kv_norm=full["mla_kv_norm"][i],
                    w_q_up=full["mla_w_q_up"][i], w_kv_up=full["mla_w_kv_up"][i], q_head_norm=full["mla_q_head_norm"][i],
                    k_head_norm=full["mla_k_head_norm"][i], w_out=full["mla_w_out"][i])

    def ffn_params(which, layer):
        return (full[which + "_norm"][layer], full[which + "_w_gate"][layer], full[which + "_w_up"][layer], full[which + "_w_down"][layer])

    saved = []
    cur = xs
    for layer in range(depth):
        cur, s1 = ffn_fwd(cur, *ffn_params("ffn1", layer))
        if layer % 2 == 0:
            cur, sm = hyb_fwd(cur, mixer_params(layer))
        else:
            cur, sm = mla_fwd(cur, mixer_params(layer), rope)
        cur, s2 = ffn_fwd(cur, *ffn_params("ffn2", layer))
        saved.append((s1, sm, s2))

    dcur, loss_local = loss_head(cur, loss_target[0])
    loss = lax.psum(loss_local, ("x", "y", "c"))

    hyb_names = dict(hyb_w_in="w_in", gdn_conv="conv", gdn_a_log="a_log", gdn_dt_bias="dt_bias", gdn_out_norm="out_norm",
                     pool_w="pool_w", pool_scale="pool_scale", hyb_w_out="w_out")
    mla_names = dict(mla_w_in="w_in", mla_q_norm="q_norm", mla_kv_norm="kv_norm", mla_w_q_up="w_q_up", mla_w_kv_up="w_kv_up",
                     mla_q_head_norm="q_head_norm", mla_k_head_norm="k_head_norm", mla_w_out="w_out")
    per_layer = {n: [None] * local[n].shape[0] for n in WEIGHTS}

    def put(n, idx, g):
        if n in LARGE:
            g = adamw(exchange(g, name="exchange_" + n), local[n], given["m_" + n], given["v_" + n], layer=idx, name="adamw_" + n)
        per_layer[n][idx] = g

    for layer in reversed(range(depth)):
        s1, sm, s2 = saved[layer]
        dcur, *gs = ffn_bwd(dcur, s2, *ffn_params("ffn2", layer))
        for n, g in zip(("ffn2_norm", "ffn2_w_gate", "ffn2_w_up", "ffn2_w_down"), gs):
            put(n, layer, g)
        if layer % 2 == 0:
            dcur, mg = hyb_bwd(dcur, sm, mixer_params(layer))
            names = hyb_names
        else:
            dcur, mg = mla_bwd(dcur, sm, mixer_params(layer), rope)
            names = mla_names
        put("mix_norm", layer, mg["mix_norm"])
        for n, key in names.items():
            put(n, layer // 2, mg[key])
        dcur, *gs = ffn_bwd(dcur, s1, *ffn_params("ffn1", layer))
        for n, g in zip(("ffn1_norm", "ffn1_w_gate", "ffn1_w_up", "ffn1_w_down"), gs):
            put(n, layer, g)
    grad_x = dcur[None]

    send = _pack([_to_slabs(jnp.stack(per_layer[n]), SHARD_AXIS[n]) for n in SMALL], lead=(N_DEV,))
    state = [_pack([src[n].reshape(-1) for n in SMALL]) for src in
             (local, {n: given["m_" + n] for n in SMALL}, {n: given["v_" + n] for n in SMALL})]
    small = [_unpack(o, [local[n].shape for n in SMALL]) for o in adamw(exchange(send, name="exchange_small"), *state, name="adamw_small")]

    outs = []
    for j in range(4):
        for n in WEIGHTS:
            outs.append(jnp.stack([t[j] for t in per_layer[n]]) if n in LARGE else small[j][SMALL.index(n)])
    return (loss, grad_x, *outs)
```

```python
import math

import jax
import jax.numpy as jnp
from jax import lax
from jax.experimental import pallas as pl
from jax.experimental.pallas import tpu as pltpu

F32 = jnp.float32
BF16 = jnp.bfloat16
MXU_DTYPE = jnp.bfloat16
HI = lax.Precision.HIGHEST
VMEM_LIMIT = 52 * 1024 * 1024
MM_VMEM_BUDGET = 40 * 1024 * 1024
LANES = 128
N_DEV = 8

EPS = 1e-6
GDN_HEADS = 8
GDN_D = 128
CHUNK = 64
CONV_K = 4
POOL_WINDOWS = (2, 4, 8, 16)
POOL_GW = 256
MLA_HEADS = 16
NOPE = 128
ROPE = 64
QK_HEAD = NOPE + ROPE
V_HEAD = 128
Q_LORA = 512
ROPE_THETA = 10000.0

ADAM_LR = 0.001
ADAM_B1 = 0.9
ADAM_B2 = 0.999
ADAM_EPS = 1e-08
ADAM_WD = 0.01
ADAM_STEP = 10
ADAMW_BLOCK_ELEMS = 128 * 1024


def _pick(n, cands):
    for c in cands:
        if n % c == 0:
            return c
    return n


def _params(sem=None):
    return pltpu.CompilerParams(dimension_semantics=sem, vmem_limit_bytes=VMEM_LIMIT)


def _sigmoid(x):
    return 1.0 / (1.0 + jnp.exp(-x))


def mm(a, b, *, ta=False, tb=False, add=None, scale=None, out_dtype=F32, col_slabs=None, send=None, tiles=None, name="mm"):
    if ta:
        K, M = a.shape
    else:
        M, K = a.shape
    if tb:
        N, Kb = b.shape
    else:
        Kb, N = b.shape
    assert K == Kb, (a.shape, b.shape, ta, tb)
    tm = _pick(M, (1024, 512, 256, 128))
    tn = _pick(N if col_slabs is None else N // col_slabs, (1024, 512, 384, 256, 128))

    def vmem_bytes(tk):
        return (2 * tk * (tm * a.dtype.itemsize + tn * b.dtype.itemsize) + tm * tn * (4 + 2 * jnp.dtype(out_dtype).itemsize)
                + (2 * tm * tn * add.dtype.itemsize if add is not None else 0))

    tk = next(c for c in (2048, 1024, 512, 256, 128, K) if K % c == 0 and (c <= 128 or vmem_bytes(c) <= MM_VMEM_BUDGET))
    if tiles is not None:
        tm, tn, tk = tiles
    nk = K // tk
    a_spec = pl.BlockSpec((tk, tm), lambda i, j, k: (k, i)) if ta else pl.BlockSpec((tm, tk), lambda i, j, k: (i, k))
    b_spec = pl.BlockSpec((tn, tk), lambda i, j, k: (j, k)) if tb else pl.BlockSpec((tk, tn), lambda i, j, k: (k, j))
    if col_slabs is None:
        o_spec = pl.BlockSpec((tm, tn), lambda i, j, k: (i, j))
        o_shape = (M, N)
    else:
        assert add is None
        per = N // col_slabs // tn
        o_spec = pl.BlockSpec((None, tm, tn), lambda i, j, k: (j // per, i, j % per))
        o_shape = (col_slabs, M, N // col_slabs)
    dims = (((0 if ta else 1,), (1 if tb else 0,)), ((), ()))
    has_add = add is not None
    has_send = send is not None
    grid = (M // tm, N // tn, nk)

    def body(*refs):
        refs = list(refs)
        a_ref, b_ref = refs[:2]
        c_ref = refs[2] if has_add else None
        n_in = 2 + has_add + has_send
        x_ref = refs[n_in - 1] if has_send else None
        o_ref = refs[n_in]
        r_ref = refs[n_in + 1] if has_send else None
        scratch = refs[n_in + 1 + has_send:]
        acc_ref = scratch[0] if nk > 1 else None
        i, j, k = pl.program_id(0), pl.program_id(1), pl.program_id(2)

        if has_send:
            copies = _exchange_copies(x_ref, r_ref, *scratch[-3:])

            @pl.when((i == 0) & (j == 0) & (k == 0))
            def _():
                for cp in copies:
                    cp.start()

        prod = lax.dot_general(a_ref[...].astype(MXU_DTYPE), b_ref[...].astype(MXU_DTYPE), dims, preferred_element_type=F32)

        def finish(r):
            if scale is not None:
                r = r * scale
            if has_add:
                r = r + c_ref[...].astype(F32)
            o_ref[...] = r.astype(out_dtype)

        if nk == 1:
            finish(prod)
        else:
            @pl.when(k == 0)
            def _():
                acc_ref[...] = prod

            @pl.when(k > 0)
            def _():
                acc_ref[...] += prod

            @pl.when(k == nk - 1)
            def _():
                finish(acc_ref[...])

        if has_send:
            @pl.when((i == grid[0] - 1) & (j == grid[1] - 1) & (k == nk - 1))
            def _():
                _exchange_waits(copies)

    hbm = pl.BlockSpec(memory_space=pl.ANY)
    ins = [a, b] + ([add] if has_add else []) + ([send] if has_send else [])
    specs = [a_spec, b_spec] + ([o_spec] if has_add else []) + ([hbm] if has_send else [])
    scratch_shapes = ([pltpu.VMEM((tm, tn), F32)] if nk > 1 else []) + (_exchange_semaphores() if has_send else [])
    o_sds = jax.ShapeDtypeStruct(o_shape, out_dtype)
    return pl.pallas_call(
        body, name=name, grid=grid, in_specs=specs, out_specs=(o_spec, hbm) if has_send else o_spec,
        out_shape=(o_sds, jax.ShapeDtypeStruct(send.shape, send.dtype)) if has_send else o_sds, scratch_shapes=scratch_shapes,
        compiler_params=_params(("arbitrary",) * 3 if has_send else ("parallel", "parallel", "arbitrary")))(*ins)


def rms_fwd(x, gain, *, width=None, col_block=0, out_dtype=BF16, name="rms_fwd"):
    T = x.shape[0]
    W = x.shape[1] if width is None else width
    tt = _pick(T, (512, 256, 128, 64, 8))

    def body(x_ref, g_ref, o_ref):
        xv = x_ref[...]
        r = lax.rsqrt(jnp.mean(xv * xv, axis=-1, keepdims=True) + EPS)
        o_ref[...] = (xv * r * g_ref[...]).astype(out_dtype)

    return pl.pallas_call(
        body, name=name, grid=(T // tt,),
        in_specs=[pl.BlockSpec((tt, W), lambda i: (i, col_block)), pl.BlockSpec((1, W), lambda i: (0, 0))],
        out_specs=pl.BlockSpec((tt, W), lambda i: (i, 0)), out_shape=jax.ShapeDtypeStruct((T, W), out_dtype),
        compiler_params=_params(("parallel",)))(x, gain.reshape(1, W))


def rms_bwd(x, gain, dh, res=None, *, width=None, col_block=0, name="rms_bwd"):
    T = x.shape[0]
    W = x.shape[1] if width is None else width
    tt = _pick(T, (256, 128, 64, 8))
    has_res = res is not None

    def body(*refs):
        if has_res:
            x_ref, g_ref, dh_ref, res_ref, dx_ref, dg_ref = refs
        else:
            x_ref, g_ref, dh_ref, dx_ref, dg_ref = refs
        xv = x_ref[...]
        r = lax.rsqrt(jnp.mean(xv * xv, axis=-1, keepdims=True) + EPS)
        xhat = xv * r
        dy = dh_ref[...].astype(F32)
        dxhat = dy * g_ref[...]
        dx = r * (dxhat - xhat * jnp.mean(dxhat * xhat, axis=-1, keepdims=True))
        if has_res:
            dx = dx + res_ref[...]
        dx_ref[...] = dx

        @pl.when(pl.program_id(0) == 0)
        def _():
            dg_ref[...] = jnp.zeros_like(dg_ref)

        dg_ref[...] += jnp.sum(dy * xhat, axis=0, keepdims=True)

    row = pl.BlockSpec((tt, W), lambda i: (i, 0))
    ins = [x, gain.reshape(1, W), dh] + ([res] if has_res else [])
    specs = [pl.BlockSpec((tt, W), lambda i: (i, col_block)), pl.BlockSpec((1, W), lambda i: (0, 0)), row] + ([row] if has_res else [])
    return pl.pallas_call(
        body, name=name, grid=(T // tt,), in_specs=specs,
        out_specs=(row, pl.BlockSpec((1, W), lambda i: (0, 0))),
        out_shape=(jax.ShapeDtypeStruct((T, W), F32), jax.ShapeDtypeStruct((1, W), F32)),
        compiler_params=_params(("arbitrary",)))(*ins)


def ffn_hidden_fwd(h, wg, wu, name="ffn_hidden_fwd"):
    T, D = h.shape
    F = wg.shape[1]
    tm = _pick(T, (1024, 512, 256, 128, 64, 8))
    tn = _pick(F, (512, 256, 128))

    def body(h_ref, wg_ref, wu_ref, g_ref, u_ref, a_ref):
        hv = h_ref[...].astype(MXU_DTYPE)
        g = jnp.dot(hv, wg_ref[...].astype(MXU_DTYPE), preferred_element_type=F32)
        u = jnp.dot(hv, wu_ref[...].astype(MXU_DTYPE), preferred_element_type=F32)
        g_ref[...] = g.astype(g_ref.dtype)
        u_ref[...] = u.astype(u_ref.dtype)
        a_ref[...] = (g * _sigmoid(g) * u).astype(a_ref.dtype)

    wb = pl.BlockSpec((D, tn), lambda i, j: (0, j))
    ob = pl.BlockSpec((tm, tn), lambda i, j: (i, j))
    sh = jax.ShapeDtypeStruct((T, F), BF16)
    return pl.pallas_call(body, name=name, grid=(T // tm, F // tn), in_specs=[pl.BlockSpec((tm, D), lambda i, j: (i, 0)), wb, wb],
                          out_specs=(ob, ob, ob), out_shape=(sh, sh, sh), compiler_params=_params(("parallel", "parallel")))(h, wg, wu)


def ffn_hidden_bwd(dy, wd, g, u, scale, name="ffn_hidden_bwd"):
    T, D = dy.shape
    F = wd.shape[0]
    tm = _pick(T, (1024, 512, 256, 128, 64, 8))
    tn = _pick(F, (512, 256, 128))

    def body(dy_ref, wd_ref, g_ref, u_ref, dg_ref, du_ref):
        da = lax.dot_general(dy_ref[...].astype(MXU_DTYPE), wd_ref[...].astype(MXU_DTYPE), _NT, preferred_element_type=F32) * scale
        gv = g_ref[...].astype(F32)
        s = _sigmoid(gv)
        dg_ref[...] = (da * u_ref[...].astype(F32) * (s * (1.0 + gv * (1.0 - s)))).astype(dg_ref.dtype)
        du_ref[...] = (da * (gv * s)).astype(du_ref.dtype)

    ob = pl.BlockSpec((tm, tn), lambda i, j: (i, j))
    sh = jax.ShapeDtypeStruct((T, F), BF16)
    return pl.pallas_call(body, name=name, grid=(T // tm, F // tn),
                          in_specs=[pl.BlockSpec((tm, D), lambda i, j: (i, 0)), pl.BlockSpec((tn, D), lambda i, j: (j, 0)), ob, ob],
                          out_specs=(ob, ob), out_shape=(sh, sh), compiler_params=_params(("parallel", "parallel")))(dy, wd, g, u)


def loss_head(y, target, name="loss_head"):
    T, D = y.shape
    tt = _pick(T, (512, 256, 128, 64, 8))

    def body(y_ref, t_ref, dy_ref, l_ref):
        e = y_ref[...] - t_ref[...]
        dy_ref[...] = e * (1.0 / D)

        @pl.when(pl.program_id(0) == 0)
        def _():
            l_ref[...] = jnp.zeros_like(l_ref)

        l_ref[...] += 0.5 * jnp.sum(jnp.mean(e * e, axis=-1, keepdims=True))

    row = pl.BlockSpec((tt, D), lambda i: (i, 0))
    dy, l = pl.pallas_call(body, name=name, grid=(T // tt,), in_specs=[row, row],
                           out_specs=(row, pl.BlockSpec((8, LANES), lambda i: (0, 0))),
                           out_shape=(jax.ShapeDtypeStruct((T, D), F32), jax.ShapeDtypeStruct((8, LANES), F32)),
                           compiler_params=_params(("arbitrary",)))(y, target)
    return dy, l[0, 0]


def _shifted(cur, prev, k, row):
    if k == 0:
        return cur
    return jnp.where(row < k, pltpu.roll(prev, k, 0), pltpu.roll(cur, k, 0))


def _conv_pre(x_ref, xp_ref, w_ref, first):
    cur = x_ref[...]
    prev = jnp.where(first, 0.0, xp_ref[...])
    row = lax.broadcasted_iota(jnp.int32, cur.shape, 0)
    xs = [_shifted(cur, prev, CONV_K - 1 - j, row) for j in range(CONV_K)]
    y = xs[0] * w_ref[0:1, :]
    for j in range(1, CONV_K):
        y = y + xs[j] * w_ref[j:j + 1, :]
    return y, xs


def gdn_prep_fwd(pm, conv_w, name="gdn_prep_fwd"):
    T = pm.shape[0]
    HW = GDN_HEADS * GDN_D
    tt = _pick(T, (256, 128, 64, 8))
    qscale = GDN_D ** -0.5

    def body(x_ref, xp_ref, w_ref, q_ref, k_ref, v_ref):
        y, _ = _conv_pre(x_ref, xp_ref, w_ref, pl.program_id(0) == 0)
        s = y * _sigmoid(y)
        for h in range(GDN_HEADS):
            for part, o_ref, sc in ((0, q_ref, qscale), (1, k_ref, 1.0)):
                sl = s[:, part * HW + h * GDN_D: part * HW + (h + 1) * GDN_D]
                r = lax.rsqrt(jnp.sum(sl * sl, axis=-1, keepdims=True) + EPS)
                o_ref[:, h * GDN_D:(h + 1) * GDN_D] = sl * (r * sc)
        v_ref[...] = s[:, 2 * HW:]

    blk = pl.BlockSpec((tt, 3 * HW), lambda i: (i, 0))
    blkp = pl.BlockSpec((tt, 3 * HW), lambda i: (jnp.maximum(i - 1, 0), 0))
    out = pl.BlockSpec((tt, HW), lambda i: (i, 0))
    sh = jax.ShapeDtypeStruct((T, HW), F32)
    return pl.pallas_call(body, name=name, grid=(T // tt,), in_specs=[blk, blkp, pl.BlockSpec((CONV_K, 3 * HW), lambda i: (0, 0))],
                          out_specs=(out, out, out), out_shape=(sh, sh, sh), compiler_params=_params(("parallel",)))(pm, pm, conv_w)


def gdn_prep_bwd(pm, conv_w, dq, dk, dv, name="gdn_prep_bwd"):
    T = pm.shape[0]
    HW = GDN_HEADS * GDN_D
    tt = _pick(T, (256, 128, 64, 8))
    qscale = GDN_D ** -0.5

    def body(x_ref, xp_ref, w_ref, dq_ref, dk_ref, dv_ref, dy_ref, dw_ref):
        y, xs = _conv_pre(x_ref, xp_ref, w_ref, pl.program_id(0) == 0)
        sg = _sigmoid(y)
        s = y * sg
        dsilu = sg * (1.0 + y * (1.0 - sg))
        for h in range(GDN_HEADS):
            for part, d_ref, sc in ((0, dq_ref, qscale), (1, dk_ref, 1.0)):
                lo = part * HW + h * GDN_D
                sl = s[:, lo:lo + GDN_D]
                r = lax.rsqrt(jnp.sum(sl * sl, axis=-1, keepdims=True) + EPS)
                n = sl * r
                dn = d_ref[:, h * GDN_D:(h + 1) * GDN_D] * sc
                ds = r * (dn - n * jnp.sum(dn * n, axis=-1, keepdims=True))
                dy_ref[:, lo:lo + GDN_D] = ds * dsilu[:, lo:lo + GDN_D]
        dy_ref[:, 2 * HW:] = dv_ref[...] * dsilu[:, 2 * HW:]

        @pl.when(pl.program_id(0) == 0)
        def _():
            dw_ref[...] = jnp.zeros_like(dw_ref)

        dyv = dy_ref[...]
        for j in range(CONV_K):
            dw_ref[j:j + 1, :] += jnp.sum(dyv * xs[j], axis=0, keepdims=True)

    blk = pl.BlockSpec((tt, 3 * HW), lambda i: (i, 0))
    blkp = pl.BlockSpec((tt, 3 * HW), lambda i: (jnp.maximum(i - 1, 0), 0))
    hb = pl.BlockSpec((tt, HW), lambda i: (i, 0))
    wb = pl.BlockSpec((CONV_K, 3 * HW), lambda i: (0, 0))
    return pl.pallas_call(body, name=name, grid=(T // tt,), in_specs=[blk, blkp, wb, hb, hb, hb], out_specs=(blk, wb),
                          out_shape=(jax.ShapeDtypeStruct((T, 3 * HW), F32), jax.ShapeDtypeStruct((CONV_K, 3 * HW), F32)),
                          compiler_params=_params(("arbitrary",)))(pm, pm, conv_w, dq, dk, dv)


def conv_dx(dy, conv_w, name="conv_dx"):
    T, W = dy.shape
    tt = _pick(T, (256, 128, 64, 8))
    nt = T // tt

    def body(d_ref, dn_ref, w_ref, dx_ref):
        cur = d_ref[...]
        nxt = jnp.where(pl.program_id(0) == nt - 1, 0.0, dn_ref[...])
        row = lax.broadcasted_iota(jnp.int32, cur.shape, 0)
        acc = cur * w_ref[CONV_K - 1:CONV_K, :]
        for j in range(CONV_K - 1):
            k = CONV_K - 1 - j
            sh = jnp.where(row >= tt - k, pltpu.roll(nxt, tt - k, 0), pltpu.roll(cur, tt - k, 0))
            acc = acc + sh * w_ref[j:j + 1, :]
        dx_ref[...] = acc

    blk = pl.BlockSpec((tt, W), lambda i: (i, 0))
    blkn = pl.BlockSpec((tt, W), lambda i: (jnp.minimum(i + 1, nt - 1), 0))
    return pl.pallas_call(body, name=name, grid=(nt,), in_specs=[blk, blkn, pl.BlockSpec((CONV_K, W), lambda i: (0, 0))], out_specs=blk,
                          out_shape=jax.ShapeDtypeStruct((T, W), F32), compiler_params=_params(("parallel",)))(dy, dy, conv_w)


def _upper_ones(c):
    return (lax.broadcasted_iota(jnp.int32, (c, c), 0) <= lax.broadcasted_iota(jnp.int32, (c, c), 1)).astype(F32)


def gdn_gates_fwd(a_r, b_r, alog_c, dt_c, name="gdn_gates_fwd"):
    R, C = a_r.shape

    def body(a_ref, b_ref, al_ref, dt_ref, gc_ref, beta_ref):
        x = a_ref[...] + dt_ref[...]
        sp = jnp.maximum(x, 0.0) + jnp.log1p(jnp.exp(-jnp.abs(x)))
        g = -jnp.exp(al_ref[...]) * sp
        gc_ref[...] = jnp.dot(g, _upper_ones(C), preferred_element_type=F32, precision=HI)
        beta_ref[...] = _sigmoid(b_ref[...])

    sh = jax.ShapeDtypeStruct((R, C), F32)
    return pl.pallas_call(body, name=name, out_shape=(sh, sh), compiler_params=_params())(a_r, b_r, alog_c, dt_c)


def gdn_gates_bwd(a_r, b_r, alog_c, dt_c, dgc, dbeta, name="gdn_gates_bwd"):
    R, C = a_r.shape

    def body(a_ref, b_ref, al_ref, dt_ref, dgc_ref, dbeta_ref, da_ref, db_ref, dal_ref, ddt_ref):
        x = a_ref[...] + dt_ref[...]
        sp = jnp.maximum(x, 0.0) + jnp.log1p(jnp.exp(-jnp.abs(x)))
        ea = jnp.exp(al_ref[...])
        dg = lax.dot_general(dgc_ref[...], _upper_ones(C), (((1,), (1,)), ((), ())), preferred_element_type=F32, precision=HI)
        dsp = dg * (-ea)
        da = dsp * _sigmoid(x)
        da_ref[...] = da
        beta = _sigmoid(b_ref[...])
        db_ref[...] = dbeta_ref[...] * beta * (1.0 - beta)
        sel = (lax.broadcasted_iota(jnp.int32, (GDN_HEADS, R), 1) % GDN_HEADS == lax.broadcasted_iota(jnp.int32, (GDN_HEADS, R), 0)).astype(F32)
        dal_ref[...] = jnp.sum(jnp.dot(sel, dg * (-ea * sp), preferred_element_type=F32, precision=HI), axis=1, keepdims=True)
        ddt_ref[...] = jnp.sum(jnp.dot(sel, da, preferred_element_type=F32, precision=HI), axis=1, keepdims=True)

    sh = jax.ShapeDtypeStruct((R, C), F32)
    s8 = jax.ShapeDtypeStruct((GDN_HEADS, 1), F32)
    return pl.pallas_call(body, name=name, out_shape=(sh, sh, s8, s8), compiler_params=_params())(a_r, b_r, alog_c, dt_c, dgc, dbeta)


def _dot(a, b):
    return jnp.dot(a, b, preferred_element_type=F32, precision=HI)


def _dot_nt(a, b):
    return lax.dot_general(a, b, (((1,), (1,)), ((), ())), preferred_element_type=F32, precision=HI)


def _dot_tn(a, b):
    return lax.dot_general(a, b, (((0,), (0,)), ((), ())), preferred_element_type=F32, precision=HI)


def _bdot(a, b, dims=(((1,), (0,)), ((), ()))):
    return lax.dot_general(a.astype(MXU_DTYPE), b.astype(MXU_DTYPE), dims, preferred_element_type=F32)


def _bdot_nt(a, b):
    return _bdot(a, b, (((1,), (1,)), ((), ())))


def _bdot_tn(a, b):
    return _bdot(a, b, (((0,), (0,)), ((), ())))


def _unit_lower_inverses(ms):
    c = ms[0].shape[0]
    eye = (lax.broadcasted_iota(jnp.int32, (c, c), 0) == lax.broadcasted_iota(jnp.int32, (c, c), 1)).astype(F32)
    ps = [-m for m in ms]
    ts = [eye + p for p in ps]
    n = 2
    while n < c:
        ps = [_dot(p, p) for p in ps]
        ts = [t + _dot(t, p) for t, p in zip(ts, ps)]
        n *= 2
    return ts


def _col(row, eye):
    c = eye.shape[0]
    return jnp.sum(jnp.where(eye, jnp.broadcast_to(row, (c, c)), 0.0), axis=1, keepdims=True)


def _row(col, eye):
    c = eye.shape[0]
    return jnp.sum(jnp.where(eye, jnp.broadcast_to(col, (c, c)), 0.0), axis=0, keepdims=True)


def _gdr_chunks(q_ref, k_ref, v_ref, gc_ref, b_ref, eye, ii, jj):
    C = eye.shape[0]
    fs = []
    for h in range(GDN_HEADS):
        sl = slice(h * GDN_D, (h + 1) * GDN_D)
        qh, kh, vh = q_ref[:, sl], k_ref[:, sl], v_ref[:, sl]
        gcr, br = gc_ref[0, h:h + 1, :], b_ref[0, h:h + 1, :]
        gcc = _col(gcr, eye)
        bc = _col(br, eye)
        causal = ii >= jj
        decay = jnp.where(causal, jnp.exp(jnp.where(causal, gcc - gcr, 0.0)), 0.0)
        decay_t = jnp.where(ii <= jj, jnp.exp(jnp.where(ii <= jj, gcr - gcc, 0.0)), 0.0)
        kb = kh * bc
        eg = jnp.exp(gcc)
        glast = gcr[:, C - 1:C]
        fs.append(dict(sl=sl, qh=qh, kh=kh, vh=vh, gcc=gcc, bc=bc, decay=decay, decay_t=decay_t, kb=kb, vb=vh * bc, eg=eg,
                       el=jnp.exp(glast), ekd=jnp.exp(glast - gcc), kbg=kb * eg,
                       m=jnp.where(ii > jj, _bdot_nt(kb, kh) * decay, 0.0)))
    for f, tinv in zip(fs, _unit_lower_inverses([f["m"] for f in fs])):
        f["tinv"] = tinv
    for f in fs:
        f["u"] = _dot(f["tinv"], f["vb"])
        f["w"] = _dot(f["tinv"], f["kbg"])
        f["a"] = _bdot_nt(f["qh"], f["kh"]) * f["decay"]
        f["qd"] = f["qh"] * f["eg"]
        f["kd"] = f["kh"] * f["ekd"]
    return fs


def gdr_fwd(q, k, v, gc, beta, name="gdr_fwd"):
    T = q.shape[0]
    H, DK, C = GDN_HEADS, GDN_D, CHUNK
    N = T // C

    def body(q_ref, k_ref, v_ref, gc_ref, b_ref, o_ref, st_ref, s_ref):
        @pl.when(pl.program_id(0) == 0)
        def _():
            s_ref[...] = jnp.zeros_like(s_ref)

        ii = lax.broadcasted_iota(jnp.int32, (C, C), 0)
        jj = lax.broadcasted_iota(jnp.int32, (C, C), 1)
        eye = ii == jj
        fs = _gdr_chunks(q_ref, k_ref, v_ref, gc_ref, b_ref, eye, ii, jj)
        ss = [s_ref[h] for h in range(H)]
        vnews = [f["u"] - _bdot(f["w"], s) for f, s in zip(fs, ss)]
        for h, (f, s, vnew) in enumerate(zip(fs, ss, vnews)):
            st_ref[0, h] = s
            o_ref[:, f["sl"]] = _bdot(f["qd"], s) + _bdot(f["a"], vnew)
            s_ref[h] = s * f["el"] + _bdot_tn(f["kd"], vnew)

    tok = pl.BlockSpec((C, H * DK), lambda n: (n, 0))
    gate = pl.BlockSpec((1, H, C), lambda n: (n, 0, 0))
    return pl.pallas_call(
        body, name=name, grid=(N,), in_specs=[tok, tok, tok, gate, gate],
        out_specs=(tok, pl.BlockSpec((1, H, DK, DK), lambda n: (n, 0, 0, 0))),
        out_shape=(jax.ShapeDtypeStruct((T, H * DK), F32), jax.ShapeDtypeStruct((N, H, DK, DK), F32)),
        scratch_shapes=[pltpu.VMEM((H, DK, DK), F32)], compiler_params=_params(("arbitrary",)))(q, k, v, gc, beta)


def gdr_bwd(q, k, v, gc, beta, states, do, name="gdr_bwd"):
    T = q.shape[0]
    H, DK, C = GDN_HEADS, GDN_D, CHUNK
    N = T // C

    def body(q_ref, k_ref, v_ref, gc_ref, b_ref, st_ref, do_ref, dq_ref, dk_ref, dv_ref, dgc_ref, db_ref, ds_ref):
        @pl.when(pl.program_id(0) == 0)
        def _():
            ds_ref[...] = jnp.zeros_like(ds_ref)

        ii = lax.broadcasted_iota(jnp.int32, (C, C), 0)
        jj = lax.broadcasted_iota(jnp.int32, (C, C), 1)
        eye = ii == jj
        lastj = lax.broadcasted_iota(jnp.int32, (1, C), 1) == C - 1
        fs = _gdr_chunks(q_ref, k_ref, v_ref, gc_ref, b_ref, eye, ii, jj)
        for h, f in enumerate(fs):
            f["s"] = st_ref[0, h]
            f["dsn"] = ds_ref[h]
            f["dout"] = do_ref[:, f["sl"]]
        for f in fs:
            f["vnew"] = f["u"] - _bdot(f["w"], f["s"])
            f["tinv_t"] = f["tinv"].T
            f["a_t"] = _bdot_nt(f["kh"], f["qh"]) * f["decay_t"]
        for f in fs:
            f["dvnew"] = _bdot(f["a_t"], f["dout"]) + _bdot(f["kd"], f["dsn"])
            f["da"] = _bdot_nt(f["dout"], f["vnew"])
            f["da_t"] = _bdot_nt(f["vnew"], f["dout"])
            f["dqd"] = _bdot_nt(f["dout"], f["s"])
            f["dkd"] = _bdot_nt(f["vnew"], f["dsn"])
        for h, f in enumerate(fs):
            ds_ref[h] = _bdot_tn(f["qd"], f["dout"]) - _bdot_tn(f["w"], f["dvnew"]) + f["dsn"] * f["el"]
            f["dw"] = -_bdot_nt(f["dvnew"], f["s"])
        for f in fs:
            f["dvb"] = _dot(f["tinv_t"], f["dvnew"])
            f["dkbg"] = _dot(f["tinv_t"], f["dw"])
        for f in fs:
            f["dm"] = jnp.where(ii > jj, -(_bdot_nt(f["dvb"], f["u"]) + _bdot_nt(f["dkbg"], f["w"])), 0.0)
            f["dm_t"] = jnp.where(ii < jj, -(_bdot_nt(f["u"], f["dvb"]) + _bdot_nt(f["w"], f["dkbg"])), 0.0)
        for h, f in enumerate(fs):
            sl, qh, kh, vh = f["sl"], f["qh"], f["kh"], f["vh"]
            dkd, dqd, dkbg, dvb = f["dkd"], f["dqd"], f["dkbg"], f["dvb"]
            dkk = f["dm"] * f["decay"]
            dkk_t = f["dm_t"] * f["decay_t"]
            dqk = f["da"] * f["decay"]
            dqk_t = f["da_t"] * f["decay_t"]
            e = f["dm"] * f["m"] + f["da"] * f["a"]
            dkb = _bdot(dkk, kh) + dkbg * f["eg"]
            dq_ref[:, sl] = _bdot(dqk, kh) + dqd * f["eg"]
            dk_ref[:, sl] = _bdot(dkk_t, f["kb"]) + _bdot(dqk_t, qh) + dkd * f["ekd"] + dkb * f["bc"]
            dv_ref[:, sl] = dvb * f["bc"]
            skd = jnp.sum(dkd * f["kd"], axis=1, keepdims=True)
            dgc_col = (jnp.sum(e, axis=1, keepdims=True) + jnp.sum(dqd * f["qd"], axis=1, keepdims=True)
                       + jnp.sum(dkbg * f["kbg"], axis=1, keepdims=True) - skd)
            dglast = jnp.sum(f["dsn"] * f["s"]) * f["el"] + jnp.sum(skd)
            dgc_row = _row(dgc_col, eye) - jnp.sum(e, axis=0, keepdims=True)
            dgc_ref[0, h:h + 1, :] = dgc_row + jnp.where(lastj, dglast, 0.0)
            dbeta_col = jnp.sum(dkb * kh, axis=1, keepdims=True) + jnp.sum(dvb * vh, axis=1, keepdims=True)
            db_ref[0, h:h + 1, :] = _row(dbeta_col, eye)

    rev = lambda n: (N - 1 - n, 0)
    tok = pl.BlockSpec((C, H * DK), rev)
    gate = pl.BlockSpec((1, H, C), lambda n: (N - 1 - n, 0, 0))
    tsh = jax.ShapeDtypeStruct((T, H * DK), F32)
    gsh = jax.ShapeDtypeStruct((N, H, C), F32)
    return pl.pallas_call(
        body, name=name, grid=(N,),
        in_specs=[tok, tok, tok, gate, gate, pl.BlockSpec((1, H, DK, DK), lambda n: (N - 1 - n, 0, 0, 0)), tok],
        out_specs=(tok, tok, tok, gate, gate), out_shape=(tsh, tsh, tsh, gsh, gsh),
        scratch_shapes=[pltpu.VMEM((H, DK, DK), F32)], compiler_params=_params(("arbitrary",)))(q, k, v, gc, beta, states, do)


def gdn_gate_fwd(o, pm, out_norm, *, z_block, name="gdn_gate_fwd"):
    T, HW = o.shape
    tt = _pick(T, (256, 128, 64, 8))

    def body(o_ref, z_ref, g_ref, y_ref):
        for h in range(GDN_HEADS):
            sl = slice(h * GDN_D, (h + 1) * GDN_D)
            ov = o_ref[:, sl]
            zv = z_ref[:, sl]
            r = lax.rsqrt(jnp.mean(ov * ov, axis=-1, keepdims=True) + EPS)
            y_ref[:, sl] = (ov * r * g_ref[...] * (zv * _sigmoid(zv))).astype(y_ref.dtype)

    blk = pl.BlockSpec((tt, HW), lambda i: (i, 0))
    return pl.pallas_call(body, name=name, grid=(T // tt,),
                          in_specs=[blk, pl.BlockSpec((tt, HW), lambda i: (i, z_block)), pl.BlockSpec((1, GDN_D), lambda i: (0, 0))],
                          out_specs=blk, out_shape=jax.ShapeDtypeStruct((T, HW), BF16),
                          compiler_params=_params(("parallel",)))(o, pm, out_norm.reshape(1, GDN_D))


def gdn_gate_bwd(o, pm, out_norm, dy, *, z_block, name="gdn_gate_bwd"):
    T, HW = o.shape
    tt = _pick(T, (256, 128, 64, 8))

    def body(o_ref, z_ref, g_ref, dy_ref, do_ref, dz_ref, dg_ref):
        @pl.when(pl.program_id(0) == 0)
        def _():
            dg_ref[...] = jnp.zeros_like(dg_ref)

        acc = jnp.zeros((1, GDN_D), F32)
        for h in range(GDN_HEADS):
            sl = slice(h * GDN_D, (h + 1) * GDN_D)
            ov = o_ref[:, sl]
            zv = z_ref[:, sl]
            dyv = dy_ref[:, sl]
            r = lax.rsqrt(jnp.mean(ov * ov, axis=-1, keepdims=True) + EPS)
            xhat = ov * r
            sg = _sigmoid(zv)
            sz = zv * sg
            dn = dyv * sz
            dz_ref[:, sl] = dyv * (xhat * g_ref[...]) * (sg * (1.0 + zv * (1.0 - sg)))
            acc = acc + jnp.sum(dn * xhat, axis=0, keepdims=True)
            dxhat = dn * g_ref[...]
            do_ref[:, sl] = r * (dxhat - xhat * jnp.mean(dxhat * xhat, axis=-1, keepdims=True))
        dg_ref[...] += acc

    blk = pl.BlockSpec((tt, HW), lambda i: (i, 0))
    gb = pl.BlockSpec((1, GDN_D), lambda i: (0, 0))
    sh = jax.ShapeDtypeStruct((T, HW), F32)
    return pl.pallas_call(body, name=name, grid=(T // tt,),
                          in_specs=[blk, pl.BlockSpec((tt, HW), lambda i: (i, z_block)), gb, blk],
                          out_specs=(blk, blk, gb), out_shape=(sh, sh, jax.ShapeDtypeStruct((1, GDN_D), F32)),
                          compiler_params=_params(("arbitrary",)))(o, pm, out_norm.reshape(1, GDN_D), dy)


def _pool_bands(tt, win, t0):
    t = lax.broadcasted_iota(jnp.int32, (tt, tt), 0)
    s = lax.broadcasted_iota(jnp.int32, (tt, tt), 1)
    inv = 1.0 / jnp.minimum(t + t0 + 1, win).astype(F32)
    cur = jnp.where((s <= t) & (s > t - win), inv, 0.0)
    prev = jnp.where(s - tt > t - win, inv, 0.0)
    return cur, prev


def _pool_diff(u_ref, up_ref, g, tt, t0, first):
    sl = slice(g * POOL_GW, (g + 1) * POOL_GW)
    cur, prev = _pool_bands(tt, POOL_WINDOWS[g], t0)
    ug = u_ref[:, sl]
    upg = jnp.where(first, 0.0, up_ref[:, sl])
    return _dot(cur, ug) + _dot(prev, upg) - ug


def pool_fwd(pm, pool_w, pool_scale, *, u_block, name="pool_fwd"):
    T = pm.shape[0]
    PW = len(POOL_WINDOWS) * POOL_GW
    tt = _pick(T, (256, 128, 64, 16))

    def body(u_ref, up_ref, w_ref, s_ref, p_ref):
        i = pl.program_id(0)
        for g in range(len(POOL_WINDOWS)):
            sl = slice(g * POOL_GW, (g + 1) * POOL_GW)
            diff = _pool_diff(u_ref, up_ref, g, tt, i * tt, i == 0)
            y = jnp.dot(diff.astype(MXU_DTYPE), w_ref[g].astype(MXU_DTYPE), preferred_element_type=F32)
            p_ref[:, sl] = (y * s_ref[:, sl]).astype(p_ref.dtype)

    return pl.pallas_call(
        body, name=name, grid=(T // tt,),
        in_specs=[pl.BlockSpec((tt, PW), lambda i: (i, u_block)), pl.BlockSpec((tt, PW), lambda i: (jnp.maximum(i - 1, 0), u_block)),
                  pl.BlockSpec((len(POOL_WINDOWS), POOL_GW, POOL_GW), lambda i: (0, 0, 0)), pl.BlockSpec((1, PW), lambda i: (0, 0))],
        out_specs=pl.BlockSpec((tt, PW), lambda i: (i, 0)), out_shape=jax.ShapeDtypeStruct((T, PW), BF16),
        compiler_params=_params(("parallel",)))(pm, pm, pool_w, pool_scale.reshape(1, PW))


def pool_bwd_a(pm, pool_w, pool_scale, dp, *, u_block, name="pool_bwd_a"):
    T = pm.shape[0]
    G = len(POOL_WINDOWS)
    PW = G * POOL_GW
    tt = _pick(T, (256, 128, 64, 16))

    def body(u_ref, up_ref, w_ref, s_ref, dp_ref, dd_ref, dw_ref, dsc_ref):
        i = pl.program_id(0)

        @pl.when(i == 0)
        def _():
            dw_ref[...] = jnp.zeros_like(dw_ref)
            dsc_ref[...] = jnp.zeros_like(dsc_ref)

        for g in range(G):
            sl = slice(g * POOL_GW, (g + 1) * POOL_GW)
            diff = _pool_diff(u_ref, up_ref, g, tt, i * tt, i == 0).astype(MXU_DTYPE)
            wg = w_ref[g].astype(MXU_DTYPE)
            dpv = dp_ref[:, sl]
            y = jnp.dot(diff, wg, preferred_element_type=F32)
            dsc_ref[:, sl] += jnp.sum(dpv * y, axis=0, keepdims=True)
            dy = (dpv * s_ref[:, sl]).astype(MXU_DTYPE)
            dd_ref[:, sl] = lax.dot_general(dy, wg, (((1,), (1,)), ((), ())), preferred_element_type=F32)
            dw_ref[g] += lax.dot_general(diff, dy, (((0,), (0,)), ((), ())), preferred_element_type=F32)

    wb = pl.BlockSpec((G, POOL_GW, POOL_GW), lambda i: (0, 0, 0))
    sb = pl.BlockSpec((1, PW), lambda i: (0, 0))
    blk = pl.BlockSpec((tt, PW), lambda i: (i, 0))
    return pl.pallas_call(
        body, name=name, grid=(T // tt,),
        in_specs=[pl.BlockSpec((tt, PW), lambda i: (i, u_block)), pl.BlockSpec((tt, PW), lambda i: (jnp.maximum(i - 1, 0), u_block)), wb, sb, blk],
        out_specs=(blk, wb, sb),
        out_shape=(jax.ShapeDtypeStruct((T, PW), F32), jax.ShapeDtypeStruct((G, POOL_GW, POOL_GW), F32), jax.ShapeDtypeStruct((1, PW), F32)),
        compiler_params=_params(("arbitrary",)))(pm, pm, pool_w, pool_scale.reshape(1, PW), dp)


def pool_bwd_b(dd, name="pool_bwd_b"):
    T, PW = dd.shape
    tt = _pick(T, (256, 128, 64, 16))
    nt = T // tt

    def body(d_ref, dn_ref, du_ref):
        i = pl.program_id(0)
        s = lax.broadcasted_iota(jnp.int32, (tt, tt), 0)
        t = lax.broadcasted_iota(jnp.int32, (tt, tt), 1)
        for g, win in enumerate(POOL_WINDOWS):
            sl = slice(g * POOL_GW, (g + 1) * POOL_GW)
            inv_c = 1.0 / jnp.minimum(t + i * tt + 1, win).astype(F32)
            cur = jnp.where((t >= s) & (t < s + win), inv_c, 0.0)
            nxt = jnp.where(t + tt < s + win, 1.0 / win, 0.0)
            dg = d_ref[:, sl]
            dng = jnp.where(i == nt - 1, 0.0, dn_ref[:, sl])
            du_ref[:, sl] = _dot(cur, dg) + _dot(nxt, dng) - dg

    blk = pl.BlockSpec((tt, PW), lambda i: (i, 0))
    return pl.pallas_call(body, name=name, grid=(nt,), in_specs=[blk, pl.BlockSpec((tt, PW), lambda i: (jnp.minimum(i + 1, nt - 1), 0))],
                          out_specs=blk, out_shape=jax.ShapeDtypeStruct((T, PW), F32), compiler_params=_params(("parallel",)))(dd, dd)


def _rope_consts():
    j = jnp.arange(QK_HEAD)
    inv_freq = ROPE_THETA ** (-jnp.arange(0, ROPE, 2, dtype=F32) / ROPE)
    freq = jnp.where(j >= NOPE, inv_freq[(j - NOPE) % (ROPE // 2)], 0.0).astype(F32)
    half = ROPE // 2
    src = jnp.arange(QK_HEAD)[:, None]
    dst = jnp.arange(QK_HEAD)[None, :]
    first = (dst >= NOPE) & (dst < NOPE + half)
    second = dst >= NOPE + half
    p = jnp.where(first & (src == dst + half), -1.0, 0.0) + jnp.where(second & (src == dst - half), 1.0, 0.0)
    return freq.reshape(1, QK_HEAD), p.astype(F32)


def rope_tables(pos_col, freq, name="rope_tables"):
    T = pos_col.shape[0]
    tt = _pick(T, (512, 256, 128, 64, 8))

    def body(p_ref, f_ref, c_ref, s_ref):
        ang = p_ref[...] * f_ref[...]
        rot = lax.broadcasted_iota(jnp.int32, ang.shape, 1) >= NOPE
        c_ref[...] = jnp.where(rot, jnp.cos(ang), 1.0)
        s_ref[...] = jnp.where(rot, jnp.sin(ang), 0.0)

    blk = pl.BlockSpec((tt, QK_HEAD), lambda i: (i, 0))
    sh = jax.ShapeDtypeStruct((T, QK_HEAD), F32)
    return pl.pallas_call(body, name=name, grid=(T // tt,), in_specs=[pl.BlockSpec((tt, 1), lambda i: (i, 0)), pl.BlockSpec((1, QK_HEAD), lambda i: (0, 0))],
                          out_specs=(blk, blk), out_shape=(sh, sh), compiler_params=_params(("parallel",)))(pos_col, freq)


def _permute(x, p):
    hi = x.astype(BF16)
    lo = (x - hi.astype(F32)).astype(BF16)
    pb = p.astype(BF16)
    return jnp.dot(hi, pb, preferred_element_type=F32) + jnp.dot(lo, pb, preferred_element_type=F32)


def _seg_stats(t):
    lane = lax.broadcasted_iota(jnp.int32, t.shape, 1)
    nope = lane < NOPE
    sq = t * t
    r = jnp.where(nope, lax.rsqrt(jnp.sum(jnp.where(nope, sq, 0.0), axis=-1, keepdims=True) / NOPE + EPS),
                  lax.rsqrt(jnp.sum(jnp.where(nope, 0.0, sq), axis=-1, keepdims=True) / ROPE + EPS))
    return nope, r


def headnorm_rope_fwd(t, gain, cos, sin, pmat, name="headnorm_rope_fwd"):
    H, T, W = t.shape
    tt = _pick(T, (512, 256, 128, 64, 8))

    def body(t_ref, g_ref, c_ref, s_ref, p_ref, o_ref):
        tv = t_ref[0]
        _, r = _seg_stats(tv)
        y = tv * r * g_ref[...]
        o_ref[0] = (y * c_ref[...] + _permute(y, p_ref[...]) * s_ref[...]).astype(o_ref.dtype)

    blk = pl.BlockSpec((1, tt, W), lambda h, i: (h, i, 0))
    tab = pl.BlockSpec((tt, W), lambda h, i: (i, 0))
    return pl.pallas_call(body, name=name, grid=(H, T // tt),
                          in_specs=[blk, pl.BlockSpec((1, W), lambda h, i: (0, 0)), tab, tab, pl.BlockSpec((W, W), lambda h, i: (0, 0))],
                          out_specs=blk, out_shape=jax.ShapeDtypeStruct((H, T, W), BF16),
                          compiler_params=_params(("parallel", "parallel")))(t, gain.reshape(1, W), cos, sin, pmat)


def headnorm_rope_bwd(t, gain, cos, sin, pmat_t, dout, name="headnorm_rope_bwd"):
    H, T, W = t.shape
    tt = _pick(T, (512, 256, 128, 64, 8))

    def body(t_ref, g_ref, c_ref, s_ref, p_ref, do_ref, dt_ref, dg_ref):
        @pl.when((pl.program_id(0) == 0) & (pl.program_id(1) == 0))
        def _():
            dg_ref[...] = jnp.zeros_like(dg_ref)

        tv = t_ref[0]
        dov = do_ref[0]
        nope, r = _seg_stats(tv)
        dy = dov * c_ref[...] + _permute(dov * s_ref[...], p_ref[...])
        xhat = tv * r
        dg_ref[...] += jnp.sum(dy * xhat, axis=0, keepdims=True)
        dxhat = dy * g_ref[...]
        pr = dxhat * xhat
        mean = jnp.where(nope, jnp.sum(jnp.where(nope, pr, 0.0), axis=-1, keepdims=True) / NOPE,
                         jnp.sum(jnp.where(nope, 0.0, pr), axis=-1, keepdims=True) / ROPE)
        dt_ref[0] = r * (dxhat - xhat * mean)

    blk = pl.BlockSpec((1, tt, W), lambda h, i: (h, i, 0))
    tab = pl.BlockSpec((tt, W), lambda h, i: (i, 0))
    gb = pl.BlockSpec((1, W), lambda h, i: (0, 0))
    return pl.pallas_call(body, name=name, grid=(H, T // tt),
                          in_specs=[blk, gb, tab, tab, pl.BlockSpec((W, W), lambda h, i: (0, 0)), blk],
                          out_specs=(blk, gb), out_shape=(jax.ShapeDtypeStruct((H, T, W), F32), jax.ShapeDtypeStruct((1, W), F32)),
                          compiler_params=_params(("arbitrary", "arbitrary")))(t, gain.reshape(1, W), cos, sin, pmat_t, dout)


def sum_heads(x, name="sum_heads"):
    H, T, W = x.shape
    tt = _pick(T, (512, 256, 128, 64, 8))

    def body(x_ref, o_ref):
        @pl.when(pl.program_id(1) == 0)
        def _():
            o_ref[...] = jnp.zeros_like(o_ref)

        o_ref[...] += x_ref[0]

    return pl.pallas_call(body, name=name, grid=(T // tt, H), in_specs=[pl.BlockSpec((1, tt, W), lambda i, h: (h, i, 0))],
                          out_specs=pl.BlockSpec((tt, W), lambda i, h: (i, 0)), out_shape=jax.ShapeDtypeStruct((T, W), F32),
                          compiler_params=_params(("parallel", "arbitrary")))(x)


_NT = (((1,), (1,)), ((), ()))
_TN = (((0,), (0,)), ((), ()))
ATTN_SUB = 256


def _attn_fwd_tiles(T):
    tq = _pick(T, (512, 256, 128, 64))
    tk = _pick(T, (2048, 1024, 512, 256, 128, 64))
    return tq, tk, min(ATTN_SUB, tq, tk)


def _attn_bwd_tiles(T):
    tq = _pick(T, (1024, 512, 256, 128, 64))
    tk = _pick(T, (512, 256, 128, 64))
    return tq, tk, min(ATTN_SUB, tq, tk)


def attn_fwd(q, k, v, name="attn_fwd", tiles=None):
    H, T, DQ = q.shape
    DV = v.shape[2]
    tq, tk, sub = tiles or _attn_fwd_tiles(T)
    nq, nk, nsub = T // tq, T // tk, tk // sub
    scale = DQ ** -0.5

    def body(q_ref, k_ref, v_ref, o_ref, l_ref, m_s, l_s, acc_s):
        i, j = pl.program_id(1), pl.program_id(2)

        @pl.when(j == 0)
        def _():
            m_s[...] = jnp.full_like(m_s, -1e30)
            l_s[...] = jnp.zeros_like(l_s)
            acc_s[...] = jnp.zeros_like(acc_s)

        def tile(rel):
            qv = q_ref[0]
            m_old = m_s[...]
            m_new = m_old
            ss = {}
            for c in range(nsub):
                if rel is not None and c * sub > rel + tq - 1:
                    continue
                s = lax.dot_general(qv, k_ref[0, c * sub:(c + 1) * sub, :], _NT, preferred_element_type=F32) * scale
                if rel is not None and (c + 1) * sub - 1 > rel:
                    row = lax.broadcasted_iota(jnp.int32, s.shape, 0) + rel
                    col = lax.broadcasted_iota(jnp.int32, s.shape, 1) + c * sub
                    s = jnp.where(row >= col, s, -1e30)
                ss[c] = s
                m_new = jnp.maximum(m_new, jnp.max(s, axis=-1, keepdims=True))
            alpha = jnp.exp(m_old - m_new)
            l_new = alpha * l_s[...]
            acc = alpha * acc_s[...]
            for c, s in ss.items():
                p = jnp.exp(s - m_new)
                l_new = l_new + jnp.sum(p, axis=-1, keepdims=True)
                acc = acc + jnp.dot(p.astype(MXU_DTYPE), v_ref[0, c * sub:(c + 1) * sub, :], preferred_element_type=F32)
            l_s[...] = l_new
            acc_s[...] = acc
            m_s[...] = m_new

        rel = i * tq - j * tk
        pl.when(rel >= tk - 1)(lambda: tile(None))
        for r0 in range(0, tk - 1, tq):
            pl.when(rel == r0)(lambda r0=r0: tile(r0))

        @pl.when(j == nk - 1)
        def _():
            o_ref[0] = acc_s[...] / l_s[...]
            l_ref[0] = jnp.broadcast_to(m_s[...] + jnp.log(l_s[...]), (tq, DV))

    last = lambda i: (i * tq + (tq - 1)) // tk
    qb = pl.BlockSpec((1, tq, DQ), lambda h, i, j: (h, i, 0))
    kb = pl.BlockSpec((1, tk, DQ), lambda h, i, j: (h, jnp.minimum(j, last(i)), 0))
    vb = pl.BlockSpec((1, tk, DV), lambda h, i, j: (h, jnp.minimum(j, last(i)), 0))
    ob = pl.BlockSpec((1, tq, DV), lambda h, i, j: (h, i, 0))
    sh = jax.ShapeDtypeStruct((H, T, DV), F32)
    return pl.pallas_call(body, name=name, grid=(H, nq, nk), in_specs=[qb, kb, vb], out_specs=(ob, ob), out_shape=(sh, sh),
                          scratch_shapes=[pltpu.VMEM((tq, 1), F32), pltpu.VMEM((tq, 1), F32), pltpu.VMEM((tq, DV), F32)],
                          compiler_params=_params(("parallel", "parallel", "arbitrary")))(q, k, v)


def attn_bwd(q, k, v, o, lse, do, name="attn_bwd", tiles=None):
    H, T, DQ = q.shape
    DV = v.shape[2]
    tq, tk, sub = tiles or _attn_bwd_tiles(T)
    nq, nk, nsub = T // tq, T // tk, tq // sub
    scale = DQ ** -0.5

    def body(q_ref, k_ref, v_ref, o_ref, l_ref, do_ref, dq_ref, dk_ref, dv_ref, dk_s, dv_s):
        j, i = pl.program_id(1), pl.program_id(2)

        @pl.when((j == 0) & (i == 0))
        def _():
            dq_ref[...] = jnp.zeros_like(dq_ref)

        @pl.when(i == 0)
        def _():
            dk_s[...] = jnp.zeros_like(dk_s)
            dv_s[...] = jnp.zeros_like(dv_s)

        def tile(rel):
            kv, vv = k_ref[0], v_ref[0]
            live = [r for r in range(nsub) if rel is None or (r + 1) * sub - 1 >= rel]
            qs, dos, ss, dps = {}, {}, {}, {}
            for r in live:
                rs = slice(r * sub, (r + 1) * sub)
                qs[r] = q_ref[0, rs, :]
                dos[r] = do_ref[0, rs, :]
                ss[r] = lax.dot_general(qs[r], kv, _NT, preferred_element_type=F32) * scale
                dps[r] = lax.dot_general(dos[r].astype(MXU_DTYPE), vv, _NT, preferred_element_type=F32)
            dk_acc = dk_s[...]
            dv_acc = dv_s[...]
            for r in live:
                rs = slice(r * sub, (r + 1) * sub)
                p = jnp.exp(ss[r] - l_ref[0, rs, 0:1])
                if rel is not None and r * sub < rel + tk - 1:
                    row = lax.broadcasted_iota(jnp.int32, p.shape, 0) + r * sub
                    col = lax.broadcasted_iota(jnp.int32, p.shape, 1) + rel
                    p = jnp.where(row >= col, p, 0.0)
                delta = jnp.sum(dos[r] * o_ref[0, rs, :], axis=-1, keepdims=True)
                ds = (p * (dps[r] - delta) * scale).astype(MXU_DTYPE)
                dv_acc = dv_acc + lax.dot_general(p.astype(MXU_DTYPE), dos[r].astype(MXU_DTYPE), _TN, preferred_element_type=F32)
                dk_acc = dk_acc + lax.dot_general(ds, qs[r], _TN, preferred_element_type=F32)
                rows = pl.ds(pl.multiple_of(i * tq + r * sub, sub), sub)
                dq_ref[0, rows, :] += jnp.dot(ds, kv, preferred_element_type=F32)
            dk_s[...] = dk_acc
            dv_s[...] = dv_acc

        rel = j * tk - i * tq
        pl.when(rel <= 1 - tk)(lambda: tile(None))
        for r0 in range(0, tq, tk):
            pl.when(rel == r0)(lambda r0=r0: tile(r0))

        @pl.when(i == nq - 1)
        def _():
            dk_ref[0] = dk_s[...]
            dv_ref[0] = dv_s[...]

    first = lambda j: (j * tk) // tq
    qi = lambda h, j, i: (h, jnp.maximum(i, first(j)), 0)
    qb = pl.BlockSpec((1, tq, DQ), qi)
    ob = pl.BlockSpec((1, tq, DV), qi)
    kb = pl.BlockSpec((1, tk, DQ), lambda h, j, i: (h, j, 0))
    vb = pl.BlockSpec((1, tk, DV), lambda h, j, i: (h, j, 0))
    dqb = pl.BlockSpec((1, T, DQ), lambda h, j, i: (h, 0, 0))
    return pl.pallas_call(
        body, name=name, grid=(H, nk, nq), in_specs=[qb, kb, vb, ob, ob, ob], out_specs=(dqb, kb, vb),
        out_shape=(jax.ShapeDtypeStruct((H, T, DQ), F32), jax.ShapeDtypeStruct((H, T, DQ), F32), jax.ShapeDtypeStruct((H, T, DV), F32)),
        scratch_shapes=[pltpu.VMEM((tk, DQ), F32), pltpu.VMEM((tk, DV), F32)],
        compiler_params=_params(("parallel", "arbitrary", "arbitrary")))(q, k, v, o, lse, do)


def _mesh_place():
    return lax.axis_index("x"), lax.axis_index("y"), lax.axis_index("c")


def _flip(v, bit):
    return 1 - v if bit else v


def _relations():
    return [((r >> 2) & 1, (r >> 1) & 1, r & 1) for r in range(1, N_DEV)]


def all_gather(xs, name="all_gather"):
    def body(x_ref, o_ref, send_sems, recv_sems, local_sem):
        x, y, c = _mesh_place()
        me, sibling = (x, y, c), (x, y, 1 - c)
        chips = [(1 - x, y), (x, 1 - y), (1 - x, 1 - y)]

        def slot(px, py, pc):
            return o_ref.at[4 * px + 2 * py + pc]

        def copy(k, block, to, src=None):
            return pltpu.make_async_remote_copy(
                src_ref=slot(*block) if src is None else src, dst_ref=slot(*block), send_sem=send_sems.at[k], recv_sem=recv_sems.at[k],
                device_id=to, device_id_type=pl.DeviceIdType.MESH)

        mine = pltpu.make_async_copy(x_ref, slot(*me), local_sem)
        mine.start()
        first = [copy(0, me, sibling, src=x_ref)] + [copy(1 + j, me, (*chip, c), src=x_ref) for j, chip in enumerate(chips)]
        for cp in first:
            cp.start()
        passed = [copy(4 + j, (*chip, c), sibling) for j, chip in enumerate(chips)]
        for j, chip in enumerate(chips):
            copy(1 + j, (*chip, c), me).wait_recv()
            passed[j].start()
        copy(0, sibling, me).wait_recv()
        for j, chip in enumerate(chips):
            copy(4 + j, (*chip, 1 - c), me).wait_recv()
        for cp in first + passed:
            cp.wait_send()
        mine.wait()

    return pl.pallas_call(
        body, name=name, out_shape=jax.ShapeDtypeStruct((N_DEV, *xs.shape), xs.dtype),
        in_specs=[pl.BlockSpec(memory_space=pl.ANY)], out_specs=pl.BlockSpec(memory_space=pl.ANY),
        scratch_shapes=[pltpu.SemaphoreType.DMA((N_DEV - 1,)), pltpu.SemaphoreType.DMA((N_DEV - 1,)), pltpu.SemaphoreType.DMA(())])(xs)


def _exchange_semaphores():
    return [pltpu.SemaphoreType.DMA((N_DEV - 1,)), pltpu.SemaphoreType.DMA((N_DEV - 1,)), pltpu.SemaphoreType.DMA(())]


def _exchange_copies(x_ref, o_ref, send_sems, recv_sems, local_sem):
    x, y, c = _mesh_place()
    me = 4 * x + 2 * y + c
    copies = [pltpu.make_async_copy(x_ref.at[me], o_ref.at[me], local_sem)]
    for r, (bx, by, bc) in enumerate(_relations()):
        px, py, pc = _flip(x, bx), _flip(y, by), _flip(c, bc)
        copies.append(pltpu.make_async_remote_copy(
            src_ref=x_ref.at[4 * px + 2 * py + pc], dst_ref=o_ref.at[me], send_sem=send_sems.at[r], recv_sem=recv_sems.at[r],
            device_id=(px, py, pc), device_id_type=pl.DeviceIdType.MESH))
    return copies


def _exchange_waits(copies):
    local, remote = copies[0], copies[1:]
    for cp in remote:
        cp.wait_recv()
    for cp in remote:
        cp.wait_send()
    local.wait()


def exchange(xs, name="exchange"):
    def body(x_ref, o_ref, send_sems, recv_sems, local_sem):
        copies = _exchange_copies(x_ref, o_ref, send_sems, recv_sems, local_sem)
        for cp in copies:
            cp.start()
        _exchange_waits(copies)

    return pl.pallas_call(
        body, name=name, out_shape=jax.ShapeDtypeStruct(xs.shape, xs.dtype),
        in_specs=[pl.BlockSpec(memory_space=pl.ANY)], out_specs=pl.BlockSpec(memory_space=pl.ANY),
        scratch_shapes=_exchange_semaphores())(xs)


def adamw(recv, w, m, v, layer=None, name="adamw"):
    _, R, L = recv.shape
    tr = _pick(R, [c for c in (1024, 512, 256, 128, 64, 32, 16) if c * L <= ADAMW_BLOCK_ELEMS] + [8])

    def body(r_ref, w_ref, m_ref, v_ref, g_ref, d_ref, nm_ref, nv_ref):
        g = r_ref[0].astype(F32)
        for s in range(1, N_DEV):
            g = g + r_ref[s].astype(F32)
        m_new = ADAM_B1 * m_ref[...] + (1.0 - ADAM_B1) * g
        v_new = ADAM_B2 * v_ref[...] + (1.0 - ADAM_B2) * jnp.square(g)
        m_hat = m_new / (1.0 - ADAM_B1 ** ADAM_STEP)
        v_hat = v_new / (1.0 - ADAM_B2 ** ADAM_STEP)
        g_ref[...] = g
        d_ref[...] = -ADAM_LR * (m_hat / (jnp.sqrt(v_hat) + ADAM_EPS) + ADAM_WD * w_ref[...])
        nm_ref[...] = m_new
        nv_ref[...] = v_new

    row = pl.BlockSpec((tr, L), lambda i: (i, 0))
    state = row if layer is None else pl.BlockSpec((None, tr, L), lambda i: (layer, i, 0))
    sh = jax.ShapeDtypeStruct((R, L), F32)
    return pl.pallas_call(body, name=name, grid=(R // tr,), in_specs=[pl.BlockSpec((N_DEV, tr, L), lambda i: (0, i, 0)), state, state, state],
                          out_specs=(row, row, row, row), out_shape=(sh, sh, sh, sh), compiler_params=_params(("parallel",)))(recv, w, m, v)


GRAD_WIRE_DTYPE = BF16


class Wire:
    def __init__(self):
        self.waiting = []

    def post(self, slabs, deliver):
        self.waiting.append((slabs, deliver))

    def mm(self, a, b, **kw):
        if not self.waiting:
            return mm(a, b, **kw)
        slabs, deliver = self.waiting.pop(0)
        out, received = mm(a, b, send=slabs, **kw)
        deliver(received)
        return out

    def flush(self):
        for slabs, deliver in self.waiting:
            deliver(exchange(slabs, name="exchange_rest"))
        self.waiting = []


def _row_slabs(g):
    return g.reshape(N_DEV, g.shape[0] // N_DEV, g.shape[1])


def _col_slabs(g):
    return g.reshape(g.shape[0], N_DEV, g.shape[1] // N_DEV).transpose(1, 0, 2)


def ffn_fwd(x, gain, wg, wu, wd):
    h = rms_fwd(x, gain, name="ffn_rms_fwd")
    g, u, a = ffn_hidden_fwd(h, wg, wu)
    y = mm(a, wd, add=x, scale=0.5, name="ffn_down")
    return y, (x, h, g, u, a)


def ffn_bwd(dy, saved, gain, wg, wu, wd, wire):
    x, h, g, u, a = saved
    dg, du = ffn_hidden_bwd(dy, wd, g, u, 0.5)
    dwd = wire.mm(a, dy, ta=True, scale=0.5, out_dtype=GRAD_WIRE_DTYPE, name="ffn_dwd")
    dh = wire.mm(dg, wg, tb=True, name="ffn_dh_gate")
    dh = wire.mm(du, wu, tb=True, add=dh, name="ffn_dh_up")
    dwg = wire.mm(h, dg, ta=True, out_dtype=GRAD_WIRE_DTYPE, col_slabs=N_DEV, name="ffn_dwg")
    dwu = wire.mm(h, du, ta=True, out_dtype=GRAD_WIRE_DTYPE, col_slabs=N_DEV, name="ffn_dwu")
    dx, dgain = rms_bwd(x, gain, dh, dy, name="ffn_rms_bwd")
    return dx, dgain[0], dwg, dwu, _row_slabs(dwd)


HYB_QKVZ = 4 * GDN_HEADS * GDN_D
HYB_AB = 2 * GDN_HEADS
HYB_U = len(POOL_WINDOWS) * POOL_GW


def _hyb_split_w_in(w_in):
    main = jnp.concatenate([w_in[:, :HYB_QKVZ], w_in[:, HYB_QKVZ + HYB_AB:]], axis=1)
    ab = jnp.pad(w_in[:, HYB_QKVZ:HYB_QKVZ + HYB_AB], ((0, 0), (0, LANES - HYB_AB)))
    return main, ab


def _gate_rows(t, n):
    return t.reshape(n, CHUNK, GDN_HEADS).transpose(0, 2, 1).reshape(n * GDN_HEADS, CHUNK)


def _gate_cols(r, n):
    return r.reshape(n, GDN_HEADS, CHUNK).transpose(0, 2, 1).reshape(n * CHUNK, GDN_HEADS)


def hyb_fwd(x, p):
    T = x.shape[0]
    n = T // CHUNK
    h = rms_fwd(x, p["mix_norm"], name="mix_rms_fwd")
    w_main, w_ab = _hyb_split_w_in(p["w_in"])
    pm = mm(h, w_main, name="hyb_in_main")
    pab = mm(h, w_ab, name="hyb_in_gates")
    q, k, v = gdn_prep_fwd(pm, p["conv"])
    a_r = _gate_rows(pab[:, :GDN_HEADS], n)
    b_r = _gate_rows(pab[:, GDN_HEADS:HYB_AB], n)
    alog_c = jnp.tile(p["a_log"], n).reshape(n * GDN_HEADS, 1)
    dt_c = jnp.tile(p["dt_bias"], n).reshape(n * GDN_HEADS, 1)
    gc, beta = gdn_gates_fwd(a_r, b_r, alog_c, dt_c)
    gc3 = gc.reshape(n, GDN_HEADS, CHUNK)
    beta3 = beta.reshape(n, GDN_HEADS, CHUNK)
    o, states = gdr_fwd(q, k, v, gc3, beta3)
    og = gdn_gate_fwd(o, pm, p["out_norm"], z_block=3)
    pool = pool_fwd(pm, p["pool_w"], p["pool_scale"], u_block=4)
    mix = jnp.concatenate([og, pool], axis=1)
    y = mm(mix, p["w_out"], add=x, name="hyb_out")
    return y, dict(x=x, h=h, pm=pm, q=q, k=k, v=v, a_r=a_r, b_r=b_r, alog_c=alog_c, dt_c=dt_c, gc3=gc3, beta3=beta3,
                   o=o, states=states, mix=mix)


def hyb_bwd(dy, s, p, wire):
    T = dy.shape[0]
    n = T // CHUNK
    half = GDN_HEADS * GDN_D
    w_main, w_ab = _hyb_split_w_in(p["w_in"])
    dmix = wire.mm(dy, p["w_out"], tb=True, name="hyb_dmix")
    dog, dpool = dmix[:, :half], dmix[:, half:]
    dw_out = _row_slabs(wire.mm(s["mix"], dy, ta=True, out_dtype=GRAD_WIRE_DTYPE, name="hyb_dwout"))
    do, dz, dout_norm = gdn_gate_bwd(s["o"], s["pm"], p["out_norm"], dog, z_block=3)
    dq, dk, dv, dgc, dbeta = gdr_bwd(s["q"], s["k"], s["v"], s["gc3"], s["beta3"], s["states"], do)
    da_r, db_r, dalog, ddt = gdn_gates_bwd(s["a_r"], s["b_r"], s["alog_c"], s["dt_c"],
                                           dgc.reshape(n * GDN_HEADS, CHUNK), dbeta.reshape(n * GDN_HEADS, CHUNK))
    dpab = jnp.pad(jnp.concatenate([_gate_cols(da_r, n), _gate_cols(db_r, n)], axis=1), ((0, 0), (0, LANES - HYB_AB)))
    dyc, dconv = gdn_prep_bwd(s["pm"], p["conv"], dq, dk, dv)
    dqkv = conv_dx(dyc, p["conv"])
    dd, dpool_w, dpool_scale = pool_bwd_a(s["pm"], p["pool_w"], p["pool_scale"], dpool, u_block=4)
    du = pool_bwd_b(dd)
    dpm = jnp.concatenate([dqkv, dz, du], axis=1)
    dh = wire.mm(dpm, w_main, tb=True, name="hyb_dh_main")
    dh = mm(dpab, w_ab, tb=True, add=dh, name="hyb_dh_gates")
    dw_main = wire.mm(s["h"], dpm, ta=True, name="hyb_dwin_main")
    dw_ab = mm(s["h"], dpab, ta=True, name="hyb_dwin_gates")
    dw_in = _col_slabs(jnp.concatenate([dw_main[:, :HYB_QKVZ], dw_ab[:, :HYB_AB], dw_main[:, HYB_QKVZ:]], axis=1)).astype(GRAD_WIRE_DTYPE)
    dx, dmix = rms_bwd(s["x"], p["mix_norm"], dh, dy, name="mix_rms_bwd")
    grads = dict(mix_norm=dmix[0], w_in=dw_in, conv=dconv, a_log=dalog[:, 0], dt_bias=ddt[:, 0], out_norm=dout_norm[0],
                 pool_w=dpool_w, pool_scale=dpool_scale[0], w_out=dw_out)
    return dx, grads


MLA_LAT = 2 * Q_LORA


def _mla_split_w_in(w_in):
    return w_in[:, :MLA_LAT], jnp.pad(w_in[:, MLA_LAT:], ((0, 0), (0, LANES - ROPE)))


def _to_heads(t, width):
    T = t.shape[0]
    return t.reshape(T, MLA_HEADS, width).transpose(1, 0, 2)


def _from_heads(t):
    H, T, W = t.shape
    return t.transpose(1, 0, 2).reshape(T, H * W)


def mla_fwd(x, p, rope):
    T = x.shape[0]
    cos, sin, pmat = rope
    h = rms_fwd(x, p["mix_norm"], name="mix_rms_fwd")
    w_main, w_pe = _mla_split_w_in(p["w_in"])
    pm = mm(h, w_main, name="mla_in_main")
    ppe = mm(h, w_pe, name="mla_in_pe")
    qn = rms_fwd(pm, p["q_norm"], width=Q_LORA, col_block=0, name="mla_lat_rms_fwd")
    kvn = rms_fwd(pm, p["kv_norm"], width=Q_LORA, col_block=1, name="mla_lat_rms_fwd")
    q3 = _to_heads(mm(qn, p["w_q_up"], name="mla_q_up"), QK_HEAD)
    kv3 = _to_heads(mm(kvn, p["w_kv_up"], name="mla_kv_up"), NOPE + V_HEAD)
    kpe = jnp.broadcast_to(ppe[None, :, :ROPE], (MLA_HEADS, T, ROPE))
    k3 = jnp.concatenate([kv3[..., :NOPE], kpe], axis=-1)
    v3 = kv3[..., NOPE:].astype(MXU_DTYPE)
    qr = headnorm_rope_fwd(q3, p["q_head_norm"], cos, sin, pmat)
    kr = headnorm_rope_fwd(k3, p["k_head_norm"], cos, sin, pmat)
    o3, lse = attn_fwd(qr, kr, v3)
    o = _from_heads(o3).astype(MXU_DTYPE)
    y = mm(o, p["w_out"], add=x, name="mla_out")
    return y, dict(x=x, h=h, pm=pm, qn=qn, kvn=kvn, q3=q3, k3=k3, v3=v3, qr=qr, kr=kr, o3=o3, lse=lse, o=o)


def mla_bwd(dy, s, p, rope, wire):
    cos, sin, pmat = rope
    w_main, w_pe = _mla_split_w_in(p["w_in"])
    do3 = _to_heads(wire.mm(dy, p["w_out"], tb=True, name="mla_do"), V_HEAD)
    dw_out = _row_slabs(wire.mm(s["o"], dy, ta=True, out_dtype=GRAD_WIRE_DTYPE, name="mla_dwout"))
    dqr, dkr, dv3 = attn_bwd(s["qr"], s["kr"], s["v3"], s["o3"], s["lse"], do3)
    dq3, dqhn = headnorm_rope_bwd(s["q3"], p["q_head_norm"], cos, sin, pmat.T, dqr)
    dk3, dkhn = headnorm_rope_bwd(s["k3"], p["k_head_norm"], cos, sin, pmat.T, dkr)
    dppe = jnp.pad(sum_heads(dk3)[:, NOPE:], ((0, 0), (0, LANES - ROPE)))
    dq = _from_heads(dq3)
    dkv = _from_heads(jnp.concatenate([dk3[..., :NOPE], dv3], axis=-1))
    dqn = mm(dq, p["w_q_up"], tb=True, name="mla_dqn")
    dkvn = wire.mm(dkv, p["w_kv_up"], tb=True, name="mla_dkvn")
    dw_q_up = mm(s["qn"], dq, ta=True, out_dtype=GRAD_WIRE_DTYPE, col_slabs=N_DEV, name="mla_dwq_up")
    dw_kv_up = wire.mm(s["kvn"], dkv, ta=True, out_dtype=GRAD_WIRE_DTYPE, col_slabs=N_DEV, name="mla_dwkv_up")
    dqlat, dq_norm = rms_bwd(s["pm"], p["q_norm"], dqn, width=Q_LORA, col_block=0, name="mla_lat_rms_bwd")
    dkvlat, dkv_norm = rms_bwd(s["pm"], p["kv_norm"], dkvn, width=Q_LORA, col_block=1, name="mla_lat_rms_bwd")
    dpm = jnp.concatenate([dqlat, dkvlat], axis=1)
    dh = mm(dpm, w_main, tb=True, name="mla_dh_main")
    dh = mm(dppe, w_pe, tb=True, add=dh, name="mla_dh_pe")
    dw_in = _row_slabs(jnp.concatenate([mm(s["h"], dpm, ta=True, name="mla_dwin_main"),
                                        mm(s["h"], dppe, ta=True, name="mla_dwin_pe")[:, :ROPE]], axis=1)).astype(GRAD_WIRE_DTYPE)
    dx, dmix = rms_bwd(s["x"], p["mix_norm"], dh, dy, name="mix_rms_bwd")
    grads = dict(mix_norm=dmix[0], w_in=dw_in, q_norm=dq_norm[0], kv_norm=dkv_norm[0], w_q_up=dw_q_up, w_kv_up=dw_kv_up,
                 q_head_norm=dqhn[0], k_head_norm=dkhn[0], w_out=dw_out)
    return dx, grads


WEIGHTS = ['ffn1_norm', 'ffn1_w_gate', 'ffn1_w_up', 'ffn1_w_down', 'mix_norm', 'ffn2_norm', 'ffn2_w_gate', 'ffn2_w_up',
           'ffn2_w_down', 'hyb_w_in', 'gdn_conv', 'gdn_a_log', 'gdn_dt_bias', 'gdn_out_norm', 'pool_w', 'pool_scale',
           'hyb_w_out', 'mla_w_in', 'mla_q_norm', 'mla_kv_norm', 'mla_w_q_up', 'mla_w_kv_up', 'mla_q_head_norm',
           'mla_k_head_norm', 'mla_w_out']
SHARD_AXIS = dict(ffn1_norm=None, ffn1_w_gate=2, ffn1_w_up=2, ffn1_w_down=1, mix_norm=None, ffn2_norm=None, ffn2_w_gate=2,
                  ffn2_w_up=2, ffn2_w_down=1, hyb_w_in=2, gdn_conv=2, gdn_a_log=None, gdn_dt_bias=None, gdn_out_norm=None,
                  pool_w=2, pool_scale=None, hyb_w_out=1, mla_w_in=1, mla_q_norm=1, mla_kv_norm=1, mla_w_q_up=2,
                  mla_w_kv_up=2, mla_q_head_norm=None, mla_k_head_norm=None, mla_w_out=1)
GATHER_BF16 = ['ffn1_w_gate', 'ffn1_w_up', 'ffn1_w_down', 'ffn2_w_gate', 'ffn2_w_up', 'ffn2_w_down', 'hyb_w_in', 'pool_w',
               'hyb_w_out', 'mla_w_in', 'mla_w_q_up', 'mla_w_kv_up', 'mla_w_out']
GATHER_F32 = ['gdn_conv', 'mla_q_norm', 'mla_kv_norm']
LARGE = ['ffn1_w_gate', 'ffn1_w_up', 'ffn1_w_down', 'ffn2_w_gate', 'ffn2_w_up', 'ffn2_w_down', 'hyb_w_in', 'hyb_w_out',
         'mla_w_in', 'mla_w_q_up', 'mla_w_kv_up', 'mla_w_out']
SMALL = [n for n in WEIGHTS if n not in LARGE]
SUBLANES = 8


def _pack(flat_list, lead=()):
    flat = jnp.concatenate(flat_list, axis=-1)
    rows = -(-flat.shape[-1] // (LANES * SUBLANES)) * SUBLANES
    flat = jnp.pad(flat, [(0, 0)] * len(lead) + [(0, rows * LANES - flat.shape[-1])])
    return flat.reshape(*lead, rows, LANES)


def _unpack(packed, shapes, lead=()):
    flat = packed.reshape(*lead, -1)
    out, off = [], 0
    for sh in shapes:
        n = math.prod(sh)
        out.append(flat[..., off:off + n].reshape(*lead, *sh))
        off += n
    return out


def _to_slabs(g, axis):
    if axis is None:
        return jnp.broadcast_to(g.reshape(1, -1), (N_DEV, g.size))
    sh = g.shape
    g = g.reshape(*sh[:axis], N_DEV, sh[axis] // N_DEV, *sh[axis + 1:])
    return jnp.moveaxis(g, axis, 0).reshape(N_DEV, -1)


def _from_shards(t, axis):
    t = jnp.moveaxis(t, 0, axis)
    sh = t.shape
    return t.reshape(*sh[:axis], sh[axis] * sh[axis + 1], *sh[axis + 2:])


def _gather_packed(local, names, dtype, name):
    got = all_gather(_pack([local[n].astype(dtype).reshape(-1) for n in names]), name=name)
    parts = _unpack(got, [local[n].shape for n in names], lead=(N_DEV,))
    return {n: _from_shards(t, SHARD_AXIS[n]) for n, t in zip(names, parts)}


def kernel(x, positions, ffn1_norm, ffn1_w_gate, ffn1_w_up, ffn1_w_down, mix_norm, ffn2_norm, ffn2_w_gate, ffn2_w_up, ffn2_w_down, hyb_w_in, gdn_conv, gdn_a_log, gdn_dt_bias, gdn_out_norm, pool_w, pool_scale, hyb_w_out, mla_w_in, mla_q_norm, mla_kv_norm, mla_w_q_up, mla_w_kv_up, mla_q_head_norm, mla_k_head_norm, mla_w_out, loss_target, m_ffn1_norm, m_ffn1_w_gate, m_ffn1_w_up, m_ffn1_w_down, m_mix_norm, m_ffn2_norm, m_ffn2_w_gate, m_ffn2_w_up, m_ffn2_w_down, m_hyb_w_in, m_gdn_conv, m_gdn_a_log, m_gdn_dt_bias, m_gdn_out_norm, m_pool_w, m_pool_scale, m_hyb_w_out, m_mla_w_in, m_mla_q_norm, m_mla_kv_norm, m_mla_w_q_up, m_mla_w_kv_up, m_mla_q_head_norm, m_mla_k_head_norm, m_mla_w_out, v_ffn1_norm, v_ffn1_w_gate, v_ffn1_w_up, v_ffn1_w_down, v_mix_norm, v_ffn2_norm, v_ffn2_w_gate, v_ffn2_w_up, v_ffn2_w_down, v_hyb_w_in, v_gdn_conv, v_gdn_a_log, v_gdn_dt_bias, v_gdn_out_norm, v_pool_w, v_pool_scale, v_hyb_w_out, v_mla_w_in, v_mla_q_norm, v_mla_kv_norm, v_mla_w_q_up, v_mla_w_kv_up, v_mla_q_head_norm, v_mla_k_head_norm, v_mla_w_out):
    given = dict(locals())
    local = {n: given[n] for n in WEIGHTS}
    depth = ffn1_norm.shape[0]
    xs = x[0]
    T = xs.shape[0]

    full = dict(local)
    for n in GATHER_BF16:
        full[n] = _from_shards(all_gather(local[n].astype(MXU_DTYPE), name="gather_" + n), SHARD_AXIS[n])
    full.update(_gather_packed(local, GATHER_F32, F32, "gather_f32"))

    freq, pmat = _rope_consts()
    cos, sin = rope_tables(positions[0].astype(F32).reshape(T, 1), freq)
    rope = (cos, sin, pmat)

    def mixer_params(layer):
        i = layer // 2
        if layer % 2 == 0:
            return dict(mix_norm=full["mix_norm"][layer], w_in=full["hyb_w_in"][i], conv=full["gdn_conv"][i], a_log=full["gdn_a_log"][i],
                        dt_bias=full["gdn_dt_bias"][i], out_norm=full["gdn_out_norm"][i], pool_w=full["pool_w"][i],
                        pool_scale=full["pool_scale"][i], w_out=full["hyb_w_out"][i])
        return dict(mix_norm=full["mix_norm"][layer], w_in=full["mla_w_in"][i], q_norm=full["mla_q_norm"][i], kv_norm=full["mla_kv_norm"][i],
                    w_q_up=full["mla_w_q_up"][i], w_kv_up=full["mla_w_kv_up"][i], q_head_norm=full["mla_q_head_norm"][i],
                    k_head_norm=full["mla_k_head_norm"][i], w_out=full["mla_w_out"][i])

    def ffn_params(which, layer):
        return (full[which + "_norm"][layer], full[which + "_w_gate"][layer], full[which + "_w_up"][layer], full[which + "_w_down"][layer])

    saved = []
    cur = xs
    for layer in range(depth):
        cur, s1 = ffn_fwd(cur, *ffn_params("ffn1", layer))
        if layer % 2 == 0:
            cur, sm = hyb_fwd(cur, mixer_params(layer))
        else:
            cur, sm = mla_fwd(cur, mixer_params(layer), rope)
        cur, s2 = ffn_fwd(cur, *ffn_params("ffn2", layer))
        saved.append((s1, sm, s2))

    dcur, loss_local = loss_head(cur, loss_target[0])
    loss = lax.psum(loss_local, ("x", "y", "c"))

    hyb_names = dict(hyb_w_in="w_in", gdn_conv="conv", gdn_a_log="a_log", gdn_dt_bias="dt_bias", gdn_out_norm="out_norm",
                     pool_w="pool_w", pool_scale="pool_scale", hyb_w_out="w_out")
    mla_names = dict(mla_w_in="w_in", mla_q_norm="q_norm", mla_kv_norm="kv_norm", mla_w_q_up="w_q_up", mla_w_kv_up="w_kv_up",
                     mla_q_head_norm="q_head_norm", mla_k_head_norm="k_head_norm", mla_w_out="w_out")
    per_layer = {n: [None] * local[n].shape[0] for n in WEIGHTS}

    wire = Wire()

    def put(n, idx, g):
        if n not in LARGE:
            per_layer[n][idx] = g
            return

        def deliver(received):
            per_layer[n][idx] = adamw(received, local[n], given["m_" + n], given["v_" + n], layer=idx, name="adamw_" + n)

        wire.post(g, deliver)

    for layer in reversed(range(depth)):
        s1, sm, s2 = saved[layer]
        dcur, *gs = ffn_bwd(dcur, s2, *ffn_params("ffn2", layer), wire)
        for n, g in zip(("ffn2_norm", "ffn2_w_gate", "ffn2_w_up", "ffn2_w_down"), gs):
            put(n, layer, g)
        if layer % 2 == 0:
            dcur, mg = hyb_bwd(dcur, sm, mixer_params(layer), wire)
            names = hyb_names
        else:
            dcur, mg = mla_bwd(dcur, sm, mixer_params(layer), rope, wire)
            names = mla_names
        put("mix_norm", layer, mg["mix_norm"])
        for n, key in names.items():
            put(n, layer // 2, mg[key])
        dcur, *gs = ffn_bwd(dcur, s1, *ffn_params("ffn1", layer), wire)
        for n, g in zip(("ffn1_norm", "ffn1_w_gate", "ffn1_w_up", "ffn1_w_down"), gs):
            put(n, layer, g)
    wire.flush()
    grad_x = dcur[None]

    send = _pack([_to_slabs(jnp.stack(per_layer[n]), SHARD_AXIS[n]) for n in SMALL], lead=(N_DEV,))
    state = [_pack([src[n].reshape(-1) for n in SMALL]) for src in
             (local, {n: given["m_" + n] for n in SMALL}, {n: given["v_" + n] for n in SMALL})]
    small = [_unpack(o, [local[n].shape for n in SMALL]) for o in adamw(exchange(send, name="exchange_small"), *state, name="adamw_small")]

    outs = []
    for j in range(4):
        for n in WEIGHTS:
            outs.append(jnp.stack([t[j] for t in per_layer[n]]) if n in LARGE else small[j][SMALL.index(n)])
    return (loss, grad_x, *outs)
```

```python
import math

import jax
import jax.numpy as jnp
from jax import lax
from jax.experimental import pallas as pl
from jax.experimental.pallas import tpu as pltpu

F32 = jnp.float32
BF16 = jnp.bfloat16
MXU_DTYPE = jnp.bfloat16
HI = lax.Precision.HIGHEST
VMEM_LIMIT = 52 * 1024 * 1024
MM_VMEM_BUDGET = 40 * 1024 * 1024
LANES = 128
N_DEV = 8

EPS = 1e-6
GDN_HEADS = 8
GDN_D = 128
CHUNK = 64
CONV_K = 4
POOL_WINDOWS = (2, 4, 8, 16)
POOL_GW = 256
MLA_HEADS = 16
NOPE = 128
ROPE = 64
QK_HEAD = NOPE + ROPE
V_HEAD = 128
Q_LORA = 512
ROPE_THETA = 10000.0

ADAM_LR = 0.001
ADAM_B1 = 0.9
ADAM_B2 = 0.999
ADAM_EPS = 1e-08
ADAM_WD = 0.01
ADAM_STEP = 10
ADAMW_BLOCK_ELEMS = 128 * 1024


def _pick(n, cands):
    for c in cands:
        if n % c == 0:
            return c
    return n


def _params(sem=None):
    return pltpu.CompilerParams(dimension_semantics=sem, vmem_limit_bytes=VMEM_LIMIT)


def _sigmoid(x):
    return 1.0 / (1.0 + jnp.exp(-x))


def mm(a, b, *, ta=False, tb=False, add=None, scale=None, out_dtype=F32, col_slabs=None, send=None, gather=None, tiles=None, name="mm"):
    if ta:
        K, M = a.shape
    else:
        M, K = a.shape
    if tb:
        N, Kb = b.shape
    else:
        Kb, N = b.shape
    assert K == Kb, (a.shape, b.shape, ta, tb)
    tm = _pick(M, (1024, 512, 256, 128))
    tn = _pick(N if col_slabs is None else N // col_slabs, (1024, 512, 384, 256, 128))

    def vmem_bytes(tk):
        return (2 * tk * (tm * a.dtype.itemsize + tn * b.dtype.itemsize) + tm * tn * (4 + 2 * jnp.dtype(out_dtype).itemsize)
                + (2 * tm * tn * add.dtype.itemsize if add is not None else 0))

    tk = next(c for c in (2048, 1024, 512, 256, 128, K) if K % c == 0 and (c <= 128 or vmem_bytes(c) <= MM_VMEM_BUDGET))
    if tiles is not None:
        tm, tn, tk = tiles
    nk = K // tk
    a_spec = pl.BlockSpec((tk, tm), lambda i, j, k: (k, i)) if ta else pl.BlockSpec((tm, tk), lambda i, j, k: (i, k))
    b_spec = pl.BlockSpec((tn, tk), lambda i, j, k: (j, k)) if tb else pl.BlockSpec((tk, tn), lambda i, j, k: (k, j))
    if col_slabs is None:
        o_spec = pl.BlockSpec((tm, tn), lambda i, j, k: (i, j))
        o_shape = (M, N)
    else:
        assert add is None
        per = N // col_slabs // tn
        o_spec = pl.BlockSpec((None, tm, tn), lambda i, j, k: (j // per, i, j % per))
        o_shape = (col_slabs, M, N // col_slabs)
    dims = (((0 if ta else 1,), (1 if tb else 0,)), ((), ()))
    has_add = add is not None
    rider = _rider(send, gather)
    has_send = rider is not None
    grid = (M // tm, N // tn, nk)

    def body(*refs):
        refs = list(refs)
        a_ref, b_ref = refs[:2]
        c_ref = refs[2] if has_add else None
        n_in = 2 + has_add + has_send
        x_ref = refs[n_in - 1] if has_send else None
        o_ref = refs[n_in]
        r_ref = refs[n_in + 1] if has_send else None
        scratch = refs[n_in + 1 + has_send:]
        acc_ref = scratch[0] if nk > 1 else None
        i, j, k = pl.program_id(0), pl.program_id(1), pl.program_id(2)

        if has_send:
            ride_start, ride_finish = rider.plan(x_ref, r_ref, *scratch[-3:])
            pl.when((i == 0) & (j == 0) & (k == 0))(ride_start)

        prod = lax.dot_general(a_ref[...].astype(MXU_DTYPE), b_ref[...].astype(MXU_DTYPE), dims, preferred_element_type=F32)

        def finish(r):
            if scale is not None:
                r = r * scale
            if has_add:
                r = r + c_ref[...].astype(F32)
            o_ref[...] = r.astype(out_dtype)

        if nk == 1:
            finish(prod)
        else:
            @pl.when(k == 0)
            def _():
                acc_ref[...] = prod

            @pl.when(k > 0)
            def _():
                acc_ref[...] += prod

            @pl.when(k == nk - 1)
            def _():
                finish(acc_ref[...])

        if has_send:
            pl.when((i == grid[0] - 1) & (j == grid[1] - 1) & (k == nk - 1))(ride_finish)

    hbm = pl.BlockSpec(memory_space=pl.ANY)
    ins = [a, b] + ([add] if has_add else []) + ([rider.operand] if has_send else [])
    specs = [a_spec, b_spec] + ([o_spec] if has_add else []) + ([hbm] if has_send else [])
    scratch_shapes = ([pltpu.VMEM((tm, tn), F32)] if nk > 1 else []) + (_exchange_semaphores() if has_send else [])
    o_sds = jax.ShapeDtypeStruct(o_shape, out_dtype)
    return pl.pallas_call(
        body, name=name, grid=grid, in_specs=specs, out_specs=(o_spec, hbm) if has_send else o_spec,
        out_shape=(o_sds, rider.result) if has_send else o_sds, scratch_shapes=scratch_shapes,
        compiler_params=_params(("arbitrary",) * 3 if has_send else ("parallel", "parallel", "arbitrary")))(*ins)


def rms_fwd(x, gain, *, width=None, col_block=0, out_dtype=BF16, name="rms_fwd"):
    T = x.shape[0]
    W = x.shape[1] if width is None else width
    tt = _pick(T, (512, 256, 128, 64, 8))

    def body(x_ref, g_ref, o_ref):
        xv = x_ref[...]
        r = lax.rsqrt(jnp.mean(xv * xv, axis=-1, keepdims=True) + EPS)
        o_ref[...] = (xv * r * g_ref[...]).astype(out_dtype)

    return pl.pallas_call(
        body, name=name, grid=(T // tt,),
        in_specs=[pl.BlockSpec((tt, W), lambda i: (i, col_block)), pl.BlockSpec((1, W), lambda i: (0, 0))],
        out_specs=pl.BlockSpec((tt, W), lambda i: (i, 0)), out_shape=jax.ShapeDtypeStruct((T, W), out_dtype),
        compiler_params=_params(("parallel",)))(x, gain.reshape(1, W))


def rms_bwd(x, gain, dh, res=None, *, width=None, col_block=0, name="rms_bwd"):
    T = x.shape[0]
    W = x.shape[1] if width is None else width
    tt = _pick(T, (256, 128, 64, 8))
    has_res = res is not None

    def body(*refs):
        if has_res:
            x_ref, g_ref, dh_ref, res_ref, dx_ref, dg_ref = refs
        else:
            x_ref, g_ref, dh_ref, dx_ref, dg_ref = refs
        xv = x_ref[...]
        r = lax.rsqrt(jnp.mean(xv * xv, axis=-1, keepdims=True) + EPS)
        xhat = xv * r
        dy = dh_ref[...].astype(F32)
        dxhat = dy * g_ref[...]
        dx = r * (dxhat - xhat * jnp.mean(dxhat * xhat, axis=-1, keepdims=True))
        if has_res:
            dx = dx + res_ref[...]
        dx_ref[...] = dx

        @pl.when(pl.program_id(0) == 0)
        def _():
            dg_ref[...] = jnp.zeros_like(dg_ref)

        dg_ref[...] += jnp.sum(dy * xhat, axis=0, keepdims=True)

    row = pl.BlockSpec((tt, W), lambda i: (i, 0))
    ins = [x, gain.reshape(1, W), dh] + ([res] if has_res else [])
    specs = [pl.BlockSpec((tt, W), lambda i: (i, col_block)), pl.BlockSpec((1, W), lambda i: (0, 0)), row] + ([row] if has_res else [])
    return pl.pallas_call(
        body, name=name, grid=(T // tt,), in_specs=specs,
        out_specs=(row, pl.BlockSpec((1, W), lambda i: (0, 0))),
        out_shape=(jax.ShapeDtypeStruct((T, W), F32), jax.ShapeDtypeStruct((1, W), F32)),
        compiler_params=_params(("arbitrary",)))(*ins)


def ffn_hidden_fwd(h, wg, wu, gather=None, name="ffn_hidden_fwd"):
    T, D = h.shape
    F = wg.shape[1]
    tm = _pick(T, (1024, 512, 256, 128, 64, 8))
    tn = _pick(F, (512, 256, 128))
    grid = (T // tm, F // tn)
    rider = _rider(None, gather)

    def body(h_ref, wg_ref, wu_ref, *refs):
        if rider is not None:
            x_ref, g_ref, u_ref, a_ref, r_ref, *sems = refs
            ride_start, ride_finish = rider.plan(x_ref, r_ref, *sems)
            pl.when((pl.program_id(0) == 0) & (pl.program_id(1) == 0))(ride_start)
        else:
            g_ref, u_ref, a_ref = refs
        hv = h_ref[...].astype(MXU_DTYPE)
        g = jnp.dot(hv, wg_ref[...].astype(MXU_DTYPE), preferred_element_type=F32)
        u = jnp.dot(hv, wu_ref[...].astype(MXU_DTYPE), preferred_element_type=F32)
        g_ref[...] = g.astype(g_ref.dtype)
        u_ref[...] = u.astype(u_ref.dtype)
        a_ref[...] = (g * _sigmoid(g) * u).astype(a_ref.dtype)
        if rider is not None:
            pl.when((pl.program_id(0) == grid[0] - 1) & (pl.program_id(1) == grid[1] - 1))(ride_finish)

    hbm = pl.BlockSpec(memory_space=pl.ANY)
    wb = pl.BlockSpec((D, tn), lambda i, j: (0, j))
    ob = pl.BlockSpec((tm, tn), lambda i, j: (i, j))
    sh = jax.ShapeDtypeStruct((T, F), BF16)
    riding = rider is not None
    return pl.pallas_call(
        body, name=name, grid=grid, in_specs=[pl.BlockSpec((tm, D), lambda i, j: (i, 0)), wb, wb] + ([hbm] if riding else []),
        out_specs=(ob, ob, ob) + ((hbm,) if riding else ()), out_shape=(sh, sh, sh) + ((rider.result,) if riding else ()),
        scratch_shapes=_exchange_semaphores() if riding else [],
        compiler_params=_params(("arbitrary", "arbitrary") if riding else ("parallel", "parallel")))(*([h, wg, wu] + ([gather] if riding else [])))


def ffn_hidden_bwd(dy, wd, g, u, scale, name="ffn_hidden_bwd"):
    T, D = dy.shape
    F = wd.shape[0]
    tm = _pick(T, (1024, 512, 256, 128, 64, 8))
    tn = _pick(F, (512, 256, 128))

    def body(dy_ref, wd_ref, g_ref, u_ref, dg_ref, du_ref):
        da = lax.dot_general(dy_ref[...].astype(MXU_DTYPE), wd_ref[...].astype(MXU_DTYPE), _NT, preferred_element_type=F32) * scale
        gv = g_ref[...].astype(F32)
        s = _sigmoid(gv)
        dg_ref[...] = (da * u_ref[...].astype(F32) * (s * (1.0 + gv * (1.0 - s)))).astype(dg_ref.dtype)
        du_ref[...] = (da * (gv * s)).astype(du_ref.dtype)

    ob = pl.BlockSpec((tm, tn), lambda i, j: (i, j))
    sh = jax.ShapeDtypeStruct((T, F), BF16)
    return pl.pallas_call(body, name=name, grid=(T // tm, F // tn),
                          in_specs=[pl.BlockSpec((tm, D), lambda i, j: (i, 0)), pl.BlockSpec((tn, D), lambda i, j: (j, 0)), ob, ob],
                          out_specs=(ob, ob), out_shape=(sh, sh), compiler_params=_params(("parallel", "parallel")))(dy, wd, g, u)


def loss_head(y, target, name="loss_head"):
    T, D = y.shape
    tt = _pick(T, (512, 256, 128, 64, 8))

    def body(y_ref, t_ref, dy_ref, l_ref):
        e = y_ref[...] - t_ref[...]
        dy_ref[...] = e * (1.0 / D)

        @pl.when(pl.program_id(0) == 0)
        def _():
            l_ref[...] = jnp.zeros_like(l_ref)

        l_ref[...] += 0.5 * jnp.sum(jnp.mean(e * e, axis=-1, keepdims=True))

    row = pl.BlockSpec((tt, D), lambda i: (i, 0))
    dy, l = pl.pallas_call(body, name=name, grid=(T // tt,), in_specs=[row, row],
                           out_specs=(row, pl.BlockSpec((8, LANES), lambda i: (0, 0))),
                           out_shape=(jax.ShapeDtypeStruct((T, D), F32), jax.ShapeDtypeStruct((8, LANES), F32)),
                           compiler_params=_params(("arbitrary",)))(y, target)
    return dy, l[0, 0]


def _shifted(cur, prev, k, row):
    if k == 0:
        return cur
    return jnp.where(row < k, pltpu.roll(prev, k, 0), pltpu.roll(cur, k, 0))


def _conv_pre(x_ref, xp_ref, w_ref, first):
    cur = x_ref[...]
    prev = jnp.where(first, 0.0, xp_ref[...])
    row = lax.broadcasted_iota(jnp.int32, cur.shape, 0)
    xs = [_shifted(cur, prev, CONV_K - 1 - j, row) for j in range(CONV_K)]
    y = xs[0] * w_ref[0:1, :]
    for j in range(1, CONV_K):
        y = y + xs[j] * w_ref[j:j + 1, :]
    return y, xs


def gdn_prep_fwd(pm, conv_w, name="gdn_prep_fwd"):
    T = pm.shape[0]
    HW = GDN_HEADS * GDN_D
    tt = _pick(T, (256, 128, 64, 8))
    qscale = GDN_D ** -0.5

    def body(x_ref, xp_ref, w_ref, q_ref, k_ref, v_ref):
        y, _ = _conv_pre(x_ref, xp_ref, w_ref, pl.program_id(0) == 0)
        s = y * _sigmoid(y)
        for h in range(GDN_HEADS):
            for part, o_ref, sc in ((0, q_ref, qscale), (1, k_ref, 1.0)):
                sl = s[:, part * HW + h * GDN_D: part * HW + (h + 1) * GDN_D]
                r = lax.rsqrt(jnp.sum(sl * sl, axis=-1, keepdims=True) + EPS)
                o_ref[:, h * GDN_D:(h + 1) * GDN_D] = sl * (r * sc)
        v_ref[...] = s[:, 2 * HW:]

    blk = pl.BlockSpec((tt, 3 * HW), lambda i: (i, 0))
    blkp = pl.BlockSpec((tt, 3 * HW), lambda i: (jnp.maximum(i - 1, 0), 0))
    out = pl.BlockSpec((tt, HW), lambda i: (i, 0))
    sh = jax.ShapeDtypeStruct((T, HW), F32)
    return pl.pallas_call(body, name=name, grid=(T // tt,), in_specs=[blk, blkp, pl.BlockSpec((CONV_K, 3 * HW), lambda i: (0, 0))],
                          out_specs=(out, out, out), out_shape=(sh, sh, sh), compiler_params=_params(("parallel",)))(pm, pm, conv_w)


def gdn_prep_bwd(pm, conv_w, dq, dk, dv, name="gdn_prep_bwd"):
    T = pm.shape[0]
    HW = GDN_HEADS * GDN_D
    tt = _pick(T, (256, 128, 64, 8))
    qscale = GDN_D ** -0.5

    def body(x_ref, xp_ref, w_ref, dq_ref, dk_ref, dv_ref, dy_ref, dw_ref):
        y, xs = _conv_pre(x_ref, xp_ref, w_ref, pl.program_id(0) == 0)
        sg = _sigmoid(y)
        s = y * sg
        dsilu = sg * (1.0 + y * (1.0 - sg))
        for h in range(GDN_HEADS):
            for part, d_ref, sc in ((0, dq_ref, qscale), (1, dk_ref, 1.0)):
                lo = part * HW + h * GDN_D
                sl = s[:, lo:lo + GDN_D]
                r = lax.rsqrt(jnp.sum(sl * sl, axis=-1, keepdims=True) + EPS)
                n = sl * r
                dn = d_ref[:, h * GDN_D:(h + 1) * GDN_D] * sc
                ds = r * (dn - n * jnp.sum(dn * n, axis=-1, keepdims=True))
                dy_ref[:, lo:lo + GDN_D] = ds * dsilu[:, lo:lo + GDN_D]
        dy_ref[:, 2 * HW:] = dv_ref[...] * dsilu[:, 2 * HW:]

        @pl.when(pl.program_id(0) == 0)
        def _():
            dw_ref[...] = jnp.zeros_like(dw_ref)

        dyv = dy_ref[...]
        for j in range(CONV_K):
            dw_ref[j:j + 1, :] += jnp.sum(dyv * xs[j], axis=0, keepdims=True)

    blk = pl.BlockSpec((tt, 3 * HW), lambda i: (i, 0))
    blkp = pl.BlockSpec((tt, 3 * HW), lambda i: (jnp.maximum(i - 1, 0), 0))
    hb = pl.BlockSpec((tt, HW), lambda i: (i, 0))
    wb = pl.BlockSpec((CONV_K, 3 * HW), lambda i: (0, 0))
    return pl.pallas_call(body, name=name, grid=(T // tt,), in_specs=[blk, blkp, wb, hb, hb, hb], out_specs=(blk, wb),
                          out_shape=(jax.ShapeDtypeStruct((T, 3 * HW), F32), jax.ShapeDtypeStruct((CONV_K, 3 * HW), F32)),
                          compiler_params=_params(("arbitrary",)))(pm, pm, conv_w, dq, dk, dv)


def conv_dx(dy, conv_w, name="conv_dx"):
    T, W = dy.shape
    tt = _pick(T, (256, 128, 64, 8))
    nt = T // tt

    def body(d_ref, dn_ref, w_ref, dx_ref):
        cur = d_ref[...]
        nxt = jnp.where(pl.program_id(0) == nt - 1, 0.0, dn_ref[...])
        row = lax.broadcasted_iota(jnp.int32, cur.shape, 0)
        acc = cur * w_ref[CONV_K - 1:CONV_K, :]
        for j in range(CONV_K - 1):
            k = CONV_K - 1 - j
            sh = jnp.where(row >= tt - k, pltpu.roll(nxt, tt - k, 0), pltpu.roll(cur, tt - k, 0))
            acc = acc + sh * w_ref[j:j + 1, :]
        dx_ref[...] = acc

    blk = pl.BlockSpec((tt, W), lambda i: (i, 0))
    blkn = pl.BlockSpec((tt, W), lambda i: (jnp.minimum(i + 1, nt - 1), 0))
    return pl.pallas_call(body, name=name, grid=(nt,), in_specs=[blk, blkn, pl.BlockSpec((CONV_K, W), lambda i: (0, 0))], out_specs=blk,
                          out_shape=jax.ShapeDtypeStruct((T, W), F32), compiler_params=_params(("parallel",)))(dy, dy, conv_w)


def _upper_ones(c):
    return (lax.broadcasted_iota(jnp.int32, (c, c), 0) <= lax.broadcasted_iota(jnp.int32, (c, c), 1)).astype(F32)


def gdn_gates_fwd(a_r, b_r, alog_c, dt_c, name="gdn_gates_fwd"):
    R, C = a_r.shape

    def body(a_ref, b_ref, al_ref, dt_ref, gc_ref, beta_ref):
        x = a_ref[...] + dt_ref[...]
        sp = jnp.maximum(x, 0.0) + jnp.log1p(jnp.exp(-jnp.abs(x)))
        g = -jnp.exp(al_ref[...]) * sp
        gc_ref[...] = jnp.dot(g, _upper_ones(C), preferred_element_type=F32, precision=HI)
        beta_ref[...] = _sigmoid(b_ref[...])

    sh = jax.ShapeDtypeStruct((R, C), F32)
    return pl.pallas_call(body, name=name, out_shape=(sh, sh), compiler_params=_params())(a_r, b_r, alog_c, dt_c)


def gdn_gates_bwd(a_r, b_r, alog_c, dt_c, dgc, dbeta, name="gdn_gates_bwd"):
    R, C = a_r.shape

    def body(a_ref, b_ref, al_ref, dt_ref, dgc_ref, dbeta_ref, da_ref, db_ref, dal_ref, ddt_ref):
        x = a_ref[...] + dt_ref[...]
        sp = jnp.maximum(x, 0.0) + jnp.log1p(jnp.exp(-jnp.abs(x)))
        ea = jnp.exp(al_ref[...])
        dg = lax.dot_general(dgc_ref[...], _upper_ones(C), (((1,), (1,)), ((), ())), preferred_element_type=F32, precision=HI)
        dsp = dg * (-ea)
        da = dsp * _sigmoid(x)
        da_ref[...] = da
        beta = _sigmoid(b_ref[...])
        db_ref[...] = dbeta_ref[...] * beta * (1.0 - beta)
        sel = (lax.broadcasted_iota(jnp.int32, (GDN_HEADS, R), 1) % GDN_HEADS == lax.broadcasted_iota(jnp.int32, (GDN_HEADS, R), 0)).astype(F32)
        dal_ref[...] = jnp.sum(jnp.dot(sel, dg * (-ea * sp), preferred_element_type=F32, precision=HI), axis=1, keepdims=True)
        ddt_ref[...] = jnp.sum(jnp.dot(sel, da, preferred_element_type=F32, precision=HI), axis=1, keepdims=True)

    sh = jax.ShapeDtypeStruct((R, C), F32)
    s8 = jax.ShapeDtypeStruct((GDN_HEADS, 1), F32)
    return pl.pallas_call(body, name=name, out_shape=(sh, sh, s8, s8), compiler_params=_params())(a_r, b_r, alog_c, dt_c, dgc, dbeta)


def _dot(a, b):
    return jnp.dot(a, b, preferred_element_type=F32, precision=HI)


def _dot_nt(a, b):
    return lax.dot_general(a, b, (((1,), (1,)), ((), ())), preferred_element_type=F32, precision=HI)


def _dot_tn(a, b):
    return lax.dot_general(a, b, (((0,), (0,)), ((), ())), preferred_element_type=F32, precision=HI)


def _bdot(a, b, dims=(((1,), (0,)), ((), ()))):
    return lax.dot_general(a.astype(MXU_DTYPE), b.astype(MXU_DTYPE), dims, preferred_element_type=F32)


def _bdot_nt(a, b):
    return _bdot(a, b, (((1,), (1,)), ((), ())))


def _bdot_tn(a, b):
    return _bdot(a, b, (((0,), (0,)), ((), ())))


def _unit_lower_inverses(ms):
    c = ms[0].shape[0]
    eye = (lax.broadcasted_iota(jnp.int32, (c, c), 0) == lax.broadcasted_iota(jnp.int32, (c, c), 1)).astype(F32)
    ps = [-m for m in ms]
    ts = [eye + p for p in ps]
    n = 2
    while n < c:
        ps = [_dot(p, p) for p in ps]
        ts = [t + _dot(t, p) for t, p in zip(ts, ps)]
        n *= 2
    return ts


def _col(row, eye):
    c = eye.shape[0]
    return jnp.sum(jnp.where(eye, jnp.broadcast_to(row, (c, c)), 0.0), axis=1, keepdims=True)


def _row(col, eye):
    c = eye.shape[0]
    return jnp.sum(jnp.where(eye, jnp.broadcast_to(col, (c, c)), 0.0), axis=0, keepdims=True)


def _gdr_chunks(q_ref, k_ref, v_ref, gc_ref, b_ref, eye, ii, jj):
    C = eye.shape[0]
    fs = []
    for h in range(GDN_HEADS):
        sl = slice(h * GDN_D, (h + 1) * GDN_D)
        qh, kh, vh = q_ref[:, sl], k_ref[:, sl], v_ref[:, sl]
        gcr, br = gc_ref[0, h:h + 1, :], b_ref[0, h:h + 1, :]
        gcc = _col(gcr, eye)
        bc = _col(br, eye)
        causal = ii >= jj
        decay = jnp.where(causal, jnp.exp(jnp.where(causal, gcc - gcr, 0.0)), 0.0)
        decay_t = jnp.where(ii <= jj, jnp.exp(jnp.where(ii <= jj, gcr - gcc, 0.0)), 0.0)
        kb = kh * bc
        eg = jnp.exp(gcc)
        glast = gcr[:, C - 1:C]
        fs.append(dict(sl=sl, qh=qh, kh=kh, vh=vh, gcc=gcc, bc=bc, decay=decay, decay_t=decay_t, kb=kb, vb=vh * bc, eg=eg,
                       el=jnp.exp(glast), ekd=jnp.exp(glast - gcc), kbg=kb * eg,
                       m=jnp.where(ii > jj, _bdot_nt(kb, kh) * decay, 0.0)))
    for f, tinv in zip(fs, _unit_lower_inverses([f["m"] for f in fs])):
        f["tinv"] = tinv
    for f in fs:
        f["u"] = _dot(f["tinv"], f["vb"])
        f["w"] = _dot(f["tinv"], f["kbg"])
        f["a"] = _bdot_nt(f["qh"], f["kh"]) * f["decay"]
        f["qd"] = f["qh"] * f["eg"]
        f["kd"] = f["kh"] * f["ekd"]
    return fs


def gdr_fwd(q, k, v, gc, beta, name="gdr_fwd"):
    T = q.shape[0]
    H, DK, C = GDN_HEADS, GDN_D, CHUNK
    N = T // C

    def body(q_ref, k_ref, v_ref, gc_ref, b_ref, o_ref, st_ref, s_ref):
        @pl.when(pl.program_id(0) == 0)
        def _():
            s_ref[...] = jnp.zeros_like(s_ref)

        ii = lax.broadcasted_iota(jnp.int32, (C, C), 0)
        jj = lax.broadcasted_iota(jnp.int32, (C, C), 1)
        eye = ii == jj
        fs = _gdr_chunks(q_ref, k_ref, v_ref, gc_ref, b_ref, eye, ii, jj)
        ss = [s_ref[h] for h in range(H)]
        vnews = [f["u"] - _bdot(f["w"], s) for f, s in zip(fs, ss)]
        for h, (f, s, vnew) in enumerate(zip(fs, ss, vnews)):
            st_ref[0, h] = s
            o_ref[:, f["sl"]] = _bdot(f["qd"], s) + _bdot(f["a"], vnew)
            s_ref[h] = s * f["el"] + _bdot_tn(f["kd"], vnew)

    tok = pl.BlockSpec((C, H * DK), lambda n: (n, 0))
    gate = pl.BlockSpec((1, H, C), lambda n: (n, 0, 0))
    return pl.pallas_call(
        body, name=name, grid=(N,), in_specs=[tok, tok, tok, gate, gate],
        out_specs=(tok, pl.BlockSpec((1, H, DK, DK), lambda n: (n, 0, 0, 0))),
        out_shape=(jax.ShapeDtypeStruct((T, H * DK), F32), jax.ShapeDtypeStruct((N, H, DK, DK), F32)),
        scratch_shapes=[pltpu.VMEM((H, DK, DK), F32)], compiler_params=_params(("arbitrary",)))(q, k, v, gc, beta)


def gdr_bwd(q, k, v, gc, beta, states, do, name="gdr_bwd"):
    T = q.shape[0]
    H, DK, C = GDN_HEADS, GDN_D, CHUNK
    N = T // C

    def body(q_ref, k_ref, v_ref, gc_ref, b_ref, st_ref, do_ref, dq_ref, dk_ref, dv_ref, dgc_ref, db_ref, ds_ref):
        @pl.when(pl.program_id(0) == 0)
        def _():
            ds_ref[...] = jnp.zeros_like(ds_ref)

        ii = lax.broadcasted_iota(jnp.int32, (C, C), 0)
        jj = lax.broadcasted_iota(jnp.int32, (C, C), 1)
        eye = ii == jj
        lastj = lax.broadcasted_iota(jnp.int32, (1, C), 1) == C - 1
        fs = _gdr_chunks(q_ref, k_ref, v_ref, gc_ref, b_ref, eye, ii, jj)
        for h, f in enumerate(fs):
            f["s"] = st_ref[0, h]
            f["dsn"] = ds_ref[h]
            f["dout"] = do_ref[:, f["sl"]]
        for f in fs:
            f["vnew"] = f["u"] - _bdot(f["w"], f["s"])
            f["tinv_t"] = f["tinv"].T
            f["a_t"] = _bdot_nt(f["kh"], f["qh"]) * f["decay_t"]
        for f in fs:
            f["dvnew"] = _bdot(f["a_t"], f["dout"]) + _bdot(f["kd"], f["dsn"])
            f["da"] = _bdot_nt(f["dout"], f["vnew"])
            f["da_t"] = _bdot_nt(f["vnew"], f["dout"])
            f["dqd"] = _bdot_nt(f["dout"], f["s"])
            f["dkd"] = _bdot_nt(f["vnew"], f["dsn"])
        for h, f in enumerate(fs):
            ds_ref[h] = _bdot_tn(f["qd"], f["dout"]) - _bdot_tn(f["w"], f["dvnew"]) + f["dsn"] * f["el"]
            f["dw"] = -_bdot_nt(f["dvnew"], f["s"])
        for f in fs:
            f["dvb"] = _dot(f["tinv_t"], f["dvnew"])
            f["dkbg"] = _dot(f["tinv_t"], f["dw"])
        for f in fs:
            f["dm"] = jnp.where(ii > jj, -(_bdot_nt(f["dvb"], f["u"]) + _bdot_nt(f["dkbg"], f["w"])), 0.0)
            f["dm_t"] = jnp.where(ii < jj, -(_bdot_nt(f["u"], f["dvb"]) + _bdot_nt(f["w"], f["dkbg"])), 0.0)
        for h, f in enumerate(fs):
            sl, qh, kh, vh = f["sl"], f["qh"], f["kh"], f["vh"]
            dkd, dqd, dkbg, dvb = f["dkd"], f["dqd"], f["dkbg"], f["dvb"]
            dkk = f["dm"] * f["decay"]
            dkk_t = f["dm_t"] * f["decay_t"]
            dqk = f["da"] * f["decay"]
            dqk_t = f["da_t"] * f["decay_t"]
            e = f["dm"] * f["m"] + f["da"] * f["a"]
            dkb = _bdot(dkk, kh) + dkbg * f["eg"]
            dq_ref[:, sl] = _bdot(dqk, kh) + dqd * f["eg"]
            dk_ref[:, sl] = _bdot(dkk_t, f["kb"]) + _bdot(dqk_t, qh) + dkd * f["ekd"] + dkb * f["bc"]
            dv_ref[:, sl] = dvb * f["bc"]
            skd = jnp.sum(dkd * f["kd"], axis=1, keepdims=True)
            dgc_col = (jnp.sum(e, axis=1, keepdims=True) + jnp.sum(dqd * f["qd"], axis=1, keepdims=True)
                       + jnp.sum(dkbg * f["kbg"], axis=1, keepdims=True) - skd)
            dglast = jnp.sum(f["dsn"] * f["s"]) * f["el"] + jnp.sum(skd)
            dgc_row = _row(dgc_col, eye) - jnp.sum(e, axis=0, keepdims=True)
            dgc_ref[0, h:h + 1, :] = dgc_row + jnp.where(lastj, dglast, 0.0)
            dbeta_col = jnp.sum(dkb * kh, axis=1, keepdims=True) + jnp.sum(dvb * vh, axis=1, keepdims=True)
            db_ref[0, h:h + 1, :] = _row(dbeta_col, eye)

    rev = lambda n: (N - 1 - n, 0)
    tok = pl.BlockSpec((C, H * DK), rev)
    gate = pl.BlockSpec((1, H, C), lambda n: (N - 1 - n, 0, 0))
    tsh = jax.ShapeDtypeStruct((T, H * DK), F32)
    gsh = jax.ShapeDtypeStruct((N, H, C), F32)
    return pl.pallas_call(
        body, name=name, grid=(N,),
        in_specs=[tok, tok, tok, gate, gate, pl.BlockSpec((1, H, DK, DK), lambda n: (N - 1 - n, 0, 0, 0)), tok],
        out_specs=(tok, tok, tok, gate, gate), out_shape=(tsh, tsh, tsh, gsh, gsh),
        scratch_shapes=[pltpu.VMEM((H, DK, DK), F32)], compiler_params=_params(("arbitrary",)))(q, k, v, gc, beta, states, do)


def gdn_gate_fwd(o, pm, out_norm, *, z_block, name="gdn_gate_fwd"):
    T, HW = o.shape
    tt = _pick(T, (256, 128, 64, 8))

    def body(o_ref, z_ref, g_ref, y_ref):
        for h in range(GDN_HEADS):
            sl = slice(h * GDN_D, (h + 1) * GDN_D)
            ov = o_ref[:, sl]
            zv = z_ref[:, sl]
            r = lax.rsqrt(jnp.mean(ov * ov, axis=-1, keepdims=True) + EPS)
            y_ref[:, sl] = (ov * r * g_ref[...] * (zv * _sigmoid(zv))).astype(y_ref.dtype)

    blk = pl.BlockSpec((tt, HW), lambda i: (i, 0))
    return pl.pallas_call(body, name=name, grid=(T // tt,),
                          in_specs=[blk, pl.BlockSpec((tt, HW), lambda i: (i, z_block)), pl.BlockSpec((1, GDN_D), lambda i: (0, 0))],
                          out_specs=blk, out_shape=jax.ShapeDtypeStruct((T, HW), BF16),
                          compiler_params=_params(("parallel",)))(o, pm, out_norm.reshape(1, GDN_D))


def gdn_gate_bwd(o, pm, out_norm, dy, *, z_block, name="gdn_gate_bwd"):
    T, HW = o.shape
    tt = _pick(T, (256, 128, 64, 8))

    def body(o_ref, z_ref, g_ref, dy_ref, do_ref, dz_ref, dg_ref):
        @pl.when(pl.program_id(0) == 0)
        def _():
            dg_ref[...] = jnp.zeros_like(dg_ref)

        acc = jnp.zeros((1, GDN_D), F32)
        for h in range(GDN_HEADS):
            sl = slice(h * GDN_D, (h + 1) * GDN_D)
            ov = o_ref[:, sl]
            zv = z_ref[:, sl]
            dyv = dy_ref[:, sl]
            r = lax.rsqrt(jnp.mean(ov * ov, axis=-1, keepdims=True) + EPS)
            xhat = ov * r
            sg = _sigmoid(zv)
            sz = zv * sg
            dn = dyv * sz
            dz_ref[:, sl] = dyv * (xhat * g_ref[...]) * (sg * (1.0 + zv * (1.0 - sg)))
            acc = acc + jnp.sum(dn * xhat, axis=0, keepdims=True)
            dxhat = dn * g_ref[...]
            do_ref[:, sl] = r * (dxhat - xhat * jnp.mean(dxhat * xhat, axis=-1, keepdims=True))
        dg_ref[...] += acc

    blk = pl.BlockSpec((tt, HW), lambda i: (i, 0))
    gb = pl.BlockSpec((1, GDN_D), lambda i: (0, 0))
    sh = jax.ShapeDtypeStruct((T, HW), F32)
    return pl.pallas_call(body, name=name, grid=(T // tt,),
                          in_specs=[blk, pl.BlockSpec((tt, HW), lambda i: (i, z_block)), gb, blk],
                          out_specs=(blk, blk, gb), out_shape=(sh, sh, jax.ShapeDtypeStruct((1, GDN_D), F32)),
                          compiler_params=_params(("arbitrary",)))(o, pm, out_norm.reshape(1, GDN_D), dy)


def _pool_bands(tt, win, t0):
    t = lax.broadcasted_iota(jnp.int32, (tt, tt), 0)
    s = lax.broadcasted_iota(jnp.int32, (tt, tt), 1)
    inv = 1.0 / jnp.minimum(t + t0 + 1, win).astype(F32)
    cur = jnp.where((s <= t) & (s > t - win), inv, 0.0)
    prev = jnp.where(s - tt > t - win, inv, 0.0)
    return cur, prev


def _pool_diff(u_ref, up_ref, g, tt, t0, first):
    sl = slice(g * POOL_GW, (g + 1) * POOL_GW)
    cur, prev = _pool_bands(tt, POOL_WINDOWS[g], t0)
    ug = u_ref[:, sl]
    upg = jnp.where(first, 0.0, up_ref[:, sl])
    return _dot(cur, ug) + _dot(prev, upg) - ug


def pool_fwd(pm, pool_w, pool_scale, *, u_block, name="pool_fwd"):
    T = pm.shape[0]
    PW = len(POOL_WINDOWS) * POOL_GW
    tt = _pick(T, (256, 128, 64, 16))

    def body(u_ref, up_ref, w_ref, s_ref, p_ref):
        i = pl.program_id(0)
        for g in range(len(POOL_WINDOWS)):
            sl = slice(g * POOL_GW, (g + 1) * POOL_GW)
            diff = _pool_diff(u_ref, up_ref, g, tt, i * tt, i == 0)
            y = jnp.dot(diff.astype(MXU_DTYPE), w_ref[g].astype(MXU_DTYPE), preferred_element_type=F32)
            p_ref[:, sl] = (y * s_ref[:, sl]).astype(p_ref.dtype)

    return pl.pallas_call(
        body, name=name, grid=(T // tt,),
        in_specs=[pl.BlockSpec((tt, PW), lambda i: (i, u_block)), pl.BlockSpec((tt, PW), lambda i: (jnp.maximum(i - 1, 0), u_block)),
                  pl.BlockSpec((len(POOL_WINDOWS), POOL_GW, POOL_GW), lambda i: (0, 0, 0)), pl.BlockSpec((1, PW), lambda i: (0, 0))],
        out_specs=pl.BlockSpec((tt, PW), lambda i: (i, 0)), out_shape=jax.ShapeDtypeStruct((T, PW), BF16),
        compiler_params=_params(("parallel",)))(pm, pm, pool_w, pool_scale.reshape(1, PW))


def pool_bwd_a(pm, pool_w, pool_scale, dp, *, u_block, name="pool_bwd_a"):
    T = pm.shape[0]
    G = len(POOL_WINDOWS)
    PW = G * POOL_GW
    tt = _pick(T, (256, 128, 64, 16))

    def body(u_ref, up_ref, w_ref, s_ref, dp_ref, dd_ref, dw_ref, dsc_ref):
        i = pl.program_id(0)

        @pl.when(i == 0)
        def _():
            dw_ref[...] = jnp.zeros_like(dw_ref)
            dsc_ref[...] = jnp.zeros_like(dsc_ref)

        for g in range(G):
            sl = slice(g * POOL_GW, (g + 1) * POOL_GW)
            diff = _pool_diff(u_ref, up_ref, g, tt, i * tt, i == 0).astype(MXU_DTYPE)
            wg = w_ref[g].astype(MXU_DTYPE)
            dpv = dp_ref[:, sl]
            y = jnp.dot(diff, wg, preferred_element_type=F32)
            dsc_ref[:, sl] += jnp.sum(dpv * y, axis=0, keepdims=True)
            dy = (dpv * s_ref[:, sl]).astype(MXU_DTYPE)
            dd_ref[:, sl] = lax.dot_general(dy, wg, (((1,), (1,)), ((), ())), preferred_element_type=F32)
            dw_ref[g] += lax.dot_general(diff, dy, (((0,), (0,)), ((), ())), preferred_element_type=F32)

    wb = pl.BlockSpec((G, POOL_GW, POOL_GW), lambda i: (0, 0, 0))
    sb = pl.BlockSpec((1, PW), lambda i: (0, 0))
    blk = pl.BlockSpec((tt, PW), lambda i: (i, 0))
    return pl.pallas_call(
        body, name=name, grid=(T // tt,),
        in_specs=[pl.BlockSpec((tt, PW), lambda i: (i, u_block)), pl.BlockSpec((tt, PW), lambda i: (jnp.maximum(i - 1, 0), u_block)), wb, sb, blk],
        out_specs=(blk, wb, sb),
        out_shape=(jax.ShapeDtypeStruct((T, PW), F32), jax.ShapeDtypeStruct((G, POOL_GW, POOL_GW), F32), jax.ShapeDtypeStruct((1, PW), F32)),
        compiler_params=_params(("arbitrary",)))(pm, pm, pool_w, pool_scale.reshape(1, PW), dp)


def pool_bwd_b(dd, name="pool_bwd_b"):
    T, PW = dd.shape
    tt = _pick(T, (256, 128, 64, 16))
    nt = T // tt

    def body(d_ref, dn_ref, du_ref):
        i = pl.program_id(0)
        s = lax.broadcasted_iota(jnp.int32, (tt, tt), 0)
        t = lax.broadcasted_iota(jnp.int32, (tt, tt), 1)
        for g, win in enumerate(POOL_WINDOWS):
            sl = slice(g * POOL_GW, (g + 1) * POOL_GW)
            inv_c = 1.0 / jnp.minimum(t + i * tt + 1, win).astype(F32)
            cur = jnp.where((t >= s) & (t < s + win), inv_c, 0.0)
            nxt = jnp.where(t + tt < s + win, 1.0 / win, 0.0)
            dg = d_ref[:, sl]
            dng = jnp.where(i == nt - 1, 0.0, dn_ref[:, sl])
            du_ref[:, sl] = _dot(cur, dg) + _dot(nxt, dng) - dg

    blk = pl.BlockSpec((tt, PW), lambda i: (i, 0))
    return pl.pallas_call(body, name=name, grid=(nt,), in_specs=[blk, pl.BlockSpec((tt, PW), lambda i: (jnp.minimum(i + 1, nt - 1), 0))],
                          out_specs=blk, out_shape=jax.ShapeDtypeStruct((T, PW), F32), compiler_params=_params(("parallel",)))(dd, dd)


def _rope_consts():
    j = jnp.arange(QK_HEAD)
    inv_freq = ROPE_THETA ** (-jnp.arange(0, ROPE, 2, dtype=F32) / ROPE)
    freq = jnp.where(j >= NOPE, inv_freq[(j - NOPE) % (ROPE // 2)], 0.0).astype(F32)
    half = ROPE // 2
    src = jnp.arange(QK_HEAD)[:, None]
    dst = jnp.arange(QK_HEAD)[None, :]
    first = (dst >= NOPE) & (dst < NOPE + half)
    second = dst >= NOPE + half
    p = jnp.where(first & (src == dst + half), -1.0, 0.0) + jnp.where(second & (src == dst - half), 1.0, 0.0)
    return freq.reshape(1, QK_HEAD), p.astype(F32)


def rope_tables(pos_col, freq, name="rope_tables"):
    T = pos_col.shape[0]
    tt = _pick(T, (512, 256, 128, 64, 8))

    def body(p_ref, f_ref, c_ref, s_ref):
        ang = p_ref[...] * f_ref[...]
        rot = lax.broadcasted_iota(jnp.int32, ang.shape, 1) >= NOPE
        c_ref[...] = jnp.where(rot, jnp.cos(ang), 1.0)
        s_ref[...] = jnp.where(rot, jnp.sin(ang), 0.0)

    blk = pl.BlockSpec((tt, QK_HEAD), lambda i: (i, 0))
    sh = jax.ShapeDtypeStruct((T, QK_HEAD), F32)
    return pl.pallas_call(body, name=name, grid=(T // tt,), in_specs=[pl.BlockSpec((tt, 1), lambda i: (i, 0)), pl.BlockSpec((1, QK_HEAD), lambda i: (0, 0))],
                          out_specs=(blk, blk), out_shape=(sh, sh), compiler_params=_params(("parallel",)))(pos_col, freq)


def _permute(x, p):
    hi = x.astype(BF16)
    lo = (x - hi.astype(F32)).astype(BF16)
    pb = p.astype(BF16)
    return jnp.dot(hi, pb, preferred_element_type=F32) + jnp.dot(lo, pb, preferred_element_type=F32)


def _seg_stats(t):
    lane = lax.broadcasted_iota(jnp.int32, t.shape, 1)
    nope = lane < NOPE
    sq = t * t
    r = jnp.where(nope, lax.rsqrt(jnp.sum(jnp.where(nope, sq, 0.0), axis=-1, keepdims=True) / NOPE + EPS),
                  lax.rsqrt(jnp.sum(jnp.where(nope, 0.0, sq), axis=-1, keepdims=True) / ROPE + EPS))
    return nope, r


def headnorm_rope_fwd(t, gain, cos, sin, pmat, name="headnorm_rope_fwd"):
    H, T, W = t.shape
    tt = _pick(T, (2048, 1024, 512, 256, 128, 64, 8))

    def body(t_ref, g_ref, c_ref, s_ref, p_ref, o_ref):
        tv = t_ref[0]
        _, r = _seg_stats(tv)
        y = tv * r * g_ref[...]
        o_ref[0] = (y * c_ref[...] + _permute(y, p_ref[...]) * s_ref[...]).astype(o_ref.dtype)

    blk = pl.BlockSpec((1, tt, W), lambda h, i: (h, i, 0))
    tab = pl.BlockSpec((tt, W), lambda h, i: (i, 0))
    return pl.pallas_call(body, name=name, grid=(H, T // tt),
                          in_specs=[blk, pl.BlockSpec((1, W), lambda h, i: (0, 0)), tab, tab, pl.BlockSpec((W, W), lambda h, i: (0, 0))],
                          out_specs=blk, out_shape=jax.ShapeDtypeStruct((H, T, W), BF16),
                          compiler_params=_params(("parallel", "parallel")))(t, gain.reshape(1, W), cos, sin, pmat)


def headnorm_rope_bwd(t, gain, cos, sin, pmat_t, dout, name="headnorm_rope_bwd"):
    H, T, W = t.shape
    tt = _pick(T, (2048, 1024, 512, 256, 128, 64, 8))

    def body(t_ref, g_ref, c_ref, s_ref, p_ref, do_ref, dt_ref, dg_ref):
        @pl.when((pl.program_id(0) == 0) & (pl.program_id(1) == 0))
        def _():
            dg_ref[...] = jnp.zeros_like(dg_ref)

        tv = t_ref[0]
        dov = do_ref[0]
        nope, r = _seg_stats(tv)
        dy = dov * c_ref[...] + _permute(dov * s_ref[...], p_ref[...])
        xhat = tv * r
        dg_ref[...] += jnp.sum(dy * xhat, axis=0, keepdims=True)
        dxhat = dy * g_ref[...]
        pr = dxhat * xhat
        mean = jnp.where(nope, jnp.sum(jnp.where(nope, pr, 0.0), axis=-1, keepdims=True) / NOPE,
                         jnp.sum(jnp.where(nope, 0.0, pr), axis=-1, keepdims=True) / ROPE)
        dt_ref[0] = r * (dxhat - xhat * mean)

    blk = pl.BlockSpec((1, tt, W), lambda h, i: (h, i, 0))
    tab = pl.BlockSpec((tt, W), lambda h, i: (i, 0))
    gb = pl.BlockSpec((1, W), lambda h, i: (0, 0))
    return pl.pallas_call(body, name=name, grid=(H, T // tt),
                          in_specs=[blk, gb, tab, tab, pl.BlockSpec((W, W), lambda h, i: (0, 0)), blk],
                          out_specs=(blk, gb), out_shape=(jax.ShapeDtypeStruct((H, T, W), F32), jax.ShapeDtypeStruct((1, W), F32)),
                          compiler_params=_params(("arbitrary", "arbitrary")))(t, gain.reshape(1, W), cos, sin, pmat_t, dout)


def sum_heads(x, name="sum_heads"):
    H, T, W = x.shape
    tt = _pick(T, (2048, 1024, 512, 256, 128, 64, 8))

    def body(x_ref, o_ref):
        @pl.when(pl.program_id(1) == 0)
        def _():
            o_ref[...] = jnp.zeros_like(o_ref)

        o_ref[...] += x_ref[0]

    return pl.pallas_call(body, name=name, grid=(T // tt, H), in_specs=[pl.BlockSpec((1, tt, W), lambda i, h: (h, i, 0))],
                          out_specs=pl.BlockSpec((tt, W), lambda i, h: (i, 0)), out_shape=jax.ShapeDtypeStruct((T, W), F32),
                          compiler_params=_params(("parallel", "arbitrary")))(x)


_NT = (((1,), (1,)), ((), ()))
_TN = (((0,), (0,)), ((), ()))
ATTN_SUB = 256


def _attn_fwd_tiles(T):
    tq = _pick(T, (512, 256, 128, 64))
    tk = _pick(T, (2048, 1024, 512, 256, 128, 64))
    return tq, tk, min(ATTN_SUB, tq, tk)


def _attn_bwd_tiles(T):
    tq = _pick(T, (2048, 1024, 512, 256, 128, 64))
    tk = _pick(T, (512, 256, 128, 64))
    return tq, tk, min(ATTN_SUB, tq, tk)


def attn_fwd(q, k, v, name="attn_fwd", tiles=None):
    H, T, DQ = q.shape
    DV = v.shape[2]
    tq, tk, sub = tiles or _attn_fwd_tiles(T)
    nq, nk, nsub = T // tq, T // tk, tk // sub
    scale = DQ ** -0.5

    def body(q_ref, k_ref, v_ref, o_ref, l_ref, m_s, l_s, acc_s):
        i, j = pl.program_id(1), pl.program_id(2)

        @pl.when(j == 0)
        def _():
            m_s[...] = jnp.full_like(m_s, -1e30)
            l_s[...] = jnp.zeros_like(l_s)
            acc_s[...] = jnp.zeros_like(acc_s)

        def tile(rel):
            qv = q_ref[0]
            m_old = m_s[...]
            m_new = m_old
            ss = {}
            for c in range(nsub):
                if rel is not None and c * sub > rel + tq - 1:
                    continue
                s = lax.dot_general(qv, k_ref[0, c * sub:(c + 1) * sub, :], _NT, preferred_element_type=F32) * scale
                if rel is not None and (c + 1) * sub - 1 > rel:
                    row = lax.broadcasted_iota(jnp.int32, s.shape, 0) + rel
                    col = lax.broadcasted_iota(jnp.int32, s.shape, 1) + c * sub
                    s = jnp.where(row >= col, s, -1e30)
                ss[c] = s
                m_new = jnp.maximum(m_new, jnp.max(s, axis=-1, keepdims=True))
            alpha = jnp.exp(m_old - m_new)
            l_new = alpha * l_s[...]
            acc = alpha * acc_s[...]
            for c, s in ss.items():
                p = jnp.exp(s - m_new)
                l_new = l_new + jnp.sum(p, axis=-1, keepdims=True)
                acc = acc + jnp.dot(p.astype(MXU_DTYPE), v_ref[0, c * sub:(c + 1) * sub, :], preferred_element_type=F32)
            l_s[...] = l_new
            acc_s[...] = acc
            m_s[...] = m_new

        rel = i * tq - j * tk
        pl.when(rel >= tk - 1)(lambda: tile(None))
        for r0 in range(0, tk - 1, tq):
            pl.when(rel == r0)(lambda r0=r0: tile(r0))

        @pl.when(j == nk - 1)
        def _():
            o_ref[0] = acc_s[...] / l_s[...]
            l_ref[0] = jnp.broadcast_to(m_s[...] + jnp.log(l_s[...]), (tq, DV))

    last = lambda i: (i * tq + (tq - 1)) // tk
    qb = pl.BlockSpec((1, tq, DQ), lambda h, i, j: (h, i, 0))
    kb = pl.BlockSpec((1, tk, DQ), lambda h, i, j: (h, jnp.minimum(j, last(i)), 0))
    vb = pl.BlockSpec((1, tk, DV), lambda h, i, j: (h, jnp.minimum(j, last(i)), 0))
    ob = pl.BlockSpec((1, tq, DV), lambda h, i, j: (h, i, 0))
    sh = jax.ShapeDtypeStruct((H, T, DV), F32)
    return pl.pallas_call(body, name=name, grid=(H, nq, nk), in_specs=[qb, kb, vb], out_specs=(ob, ob), out_shape=(sh, sh),
                          scratch_shapes=[pltpu.VMEM((tq, 1), F32), pltpu.VMEM((tq, 1), F32), pltpu.VMEM((tq, DV), F32)],
                          compiler_params=_params(("parallel", "parallel", "arbitrary")))(q, k, v)


def attn_bwd(q, k, v, o, lse, do, name="attn_bwd", tiles=None):
    H, T, DQ = q.shape
    DV = v.shape[2]
    tq, tk, sub = tiles or _attn_bwd_tiles(T)
    nq, nk, nsub = T // tq, T // tk, tq // sub
    scale = DQ ** -0.5

    def body(q_ref, k_ref, v_ref, o_ref, l_ref, do_ref, dq_ref, dk_ref, dv_ref, dk_s, dv_s):
        j, i = pl.program_id(1), pl.program_id(2)

        @pl.when((j == 0) & (i == 0))
        def _():
            dq_ref[...] = jnp.zeros_like(dq_ref)

        @pl.when(i == 0)
        def _():
            dk_s[...] = jnp.zeros_like(dk_s)
            dv_s[...] = jnp.zeros_like(dv_s)

        def tile(rel):
            kv, vv = k_ref[0], v_ref[0]
            live = [r for r in range(nsub) if rel is None or (r + 1) * sub - 1 >= rel]
            qs, dos, ss, dps = {}, {}, {}, {}
            for r in live:
                rs = slice(r * sub, (r + 1) * sub)
                qs[r] = q_ref[0, rs, :]
                dos[r] = do_ref[0, rs, :]
                ss[r] = lax.dot_general(qs[r], kv, _NT, preferred_element_type=F32) * scale
                dps[r] = lax.dot_general(dos[r].astype(MXU_DTYPE), vv, _NT, preferred_element_type=F32)
            dk_acc = dk_s[...]
            dv_acc = dv_s[...]
            for r in live:
                rs = slice(r * sub, (r + 1) * sub)
                p = jnp.exp(ss[r] - l_ref[0, rs, 0:1])
                if rel is not None and r * sub < rel + tk - 1:
                    row = lax.broadcasted_iota(jnp.int32, p.shape, 0) + r * sub
                    col = lax.broadcasted_iota(jnp.int32, p.shape, 1) + rel
                    p = jnp.where(row >= col, p, 0.0)
                delta = jnp.sum(dos[r] * o_ref[0, rs, :], axis=-1, keepdims=True)
                ds = (p * (dps[r] - delta) * scale).astype(MXU_DTYPE)
                dv_acc = dv_acc + lax.dot_general(p.astype(MXU_DTYPE), dos[r].astype(MXU_DTYPE), _TN, preferred_element_type=F32)
                dk_acc = dk_acc + lax.dot_general(ds, qs[r], _TN, preferred_element_type=F32)
                rows = pl.ds(pl.multiple_of(i * tq + r * sub, sub), sub)
                dq_ref[0, rows, :] += jnp.dot(ds, kv, preferred_element_type=F32)
            dk_s[...] = dk_acc
            dv_s[...] = dv_acc

        rel = j * tk - i * tq
        pl.when(rel <= 1 - tk)(lambda: tile(None))
        for r0 in range(0, tq, tk):
            pl.when(rel == r0)(lambda r0=r0: tile(r0))

        @pl.when(i == nq - 1)
        def _():
            dk_ref[0] = dk_s[...]
            dv_ref[0] = dv_s[...]

    first = lambda j: (j * tk) // tq
    qi = lambda h, j, i: (h, jnp.maximum(i, first(j)), 0)
    qb = pl.BlockSpec((1, tq, DQ), qi)
    ob = pl.BlockSpec((1, tq, DV), qi)
    kb = pl.BlockSpec((1, tk, DQ), lambda h, j, i: (h, j, 0))
    vb = pl.BlockSpec((1, tk, DV), lambda h, j, i: (h, j, 0))
    dqb = pl.BlockSpec((1, T, DQ), lambda h, j, i: (h, 0, 0))
    return pl.pallas_call(
        body, name=name, grid=(H, nk, nq), in_specs=[qb, kb, vb, ob, ob, ob], out_specs=(dqb, kb, vb),
        out_shape=(jax.ShapeDtypeStruct((H, T, DQ), F32), jax.ShapeDtypeStruct((H, T, DQ), F32), jax.ShapeDtypeStruct((H, T, DV), F32)),
        scratch_shapes=[pltpu.VMEM((tk, DQ), F32), pltpu.VMEM((tk, DV), F32)],
        compiler_params=_params(("parallel", "arbitrary", "arbitrary")))(q, k, v, o, lse, do)


def _mesh_place():
    return lax.axis_index("x"), lax.axis_index("y"), lax.axis_index("c")


def _flip(v, bit):
    return 1 - v if bit else v


def _relations():
    return [((r >> 2) & 1, (r >> 1) & 1, r & 1) for r in range(1, N_DEV)]


def _exchange_semaphores():
    return [pltpu.SemaphoreType.DMA((N_DEV - 1,)), pltpu.SemaphoreType.DMA((N_DEV - 1,)), pltpu.SemaphoreType.DMA(())]


def _gather_plan(x_ref, o_ref, send_sems, recv_sems, local_sem):
    x, y, c = _mesh_place()
    me, sibling = (x, y, c), (x, y, 1 - c)
    chips = [(1 - x, y), (x, 1 - y), (1 - x, 1 - y)]

    def slot(px, py, pc):
        return o_ref.at[4 * px + 2 * py + pc]

    def copy(k, block, to, src=None):
        return pltpu.make_async_remote_copy(
            src_ref=slot(*block) if src is None else src, dst_ref=slot(*block), send_sem=send_sems.at[k], recv_sem=recv_sems.at[k],
            device_id=to, device_id_type=pl.DeviceIdType.MESH)

    mine = pltpu.make_async_copy(x_ref, slot(*me), local_sem)
    first = [copy(0, me, sibling, src=x_ref)] + [copy(1 + j, me, (*chip, c), src=x_ref) for j, chip in enumerate(chips)]
    passed = [copy(4 + j, (*chip, c), sibling) for j, chip in enumerate(chips)]

    def start():
        mine.start()
        for cp in first:
            cp.start()

    def finish():
        for j, chip in enumerate(chips):
            copy(1 + j, (*chip, c), me).wait_recv()
            passed[j].start()
        copy(0, sibling, me).wait_recv()
        for j, chip in enumerate(chips):
            copy(4 + j, (*chip, 1 - c), me).wait_recv()
        for cp in first + passed:
            cp.wait_send()
        mine.wait()

    return start, finish


def _exchange_plan(x_ref, o_ref, send_sems, recv_sems, local_sem):
    x, y, c = _mesh_place()
    me = 4 * x + 2 * y + c
    local = pltpu.make_async_copy(x_ref.at[me], o_ref.at[me], local_sem)
    remote = []
    for r, (bx, by, bc) in enumerate(_relations()):
        px, py, pc = _flip(x, bx), _flip(y, by), _flip(c, bc)
        remote.append(pltpu.make_async_remote_copy(
            src_ref=x_ref.at[4 * px + 2 * py + pc], dst_ref=o_ref.at[me], send_sem=send_sems.at[r], recv_sem=recv_sems.at[r],
            device_id=(px, py, pc), device_id_type=pl.DeviceIdType.MESH))

    def start():
        local.start()
        for cp in remote:
            cp.start()

    def finish():
        for cp in remote:
            cp.wait_recv()
        for cp in remote:
            cp.wait_send()
        local.wait()

    return start, finish


class _Rider:
    def __init__(self, operand, result, plan):
        self.operand, self.result, self.plan = operand, result, plan


def _rider(send, gather):
    assert send is None or gather is None
    if send is not None:
        return _Rider(send, jax.ShapeDtypeStruct(send.shape, send.dtype), _exchange_plan)
    if gather is not None:
        return _Rider(gather, jax.ShapeDtypeStruct((N_DEV, *gather.shape), gather.dtype), _gather_plan)
    return None


def _transfer(rider, name):
    def body(x_ref, o_ref, send_sems, recv_sems, local_sem):
        start, finish = rider.plan(x_ref, o_ref, send_sems, recv_sems, local_sem)
        start()
        finish()

    return pl.pallas_call(body, name=name, out_shape=rider.result, in_specs=[pl.BlockSpec(memory_space=pl.ANY)],
                          out_specs=pl.BlockSpec(memory_space=pl.ANY), scratch_shapes=_exchange_semaphores())(rider.operand)


def all_gather(xs, name="all_gather"):
    return _transfer(_rider(None, xs), name)


def exchange(xs, name="exchange"):
    return _transfer(_rider(xs, None), name)


def adamw(recv, w, m, v, layer=None, name="adamw"):
    _, R, L = recv.shape
    tr = _pick(R, [c for c in (1024, 512, 256, 128, 64, 32, 16) if c * L <= ADAMW_BLOCK_ELEMS] + [8])

    def body(r_ref, w_ref, m_ref, v_ref, g_ref, d_ref, nm_ref, nv_ref):
        g = r_ref[0].astype(F32)
        for s in range(1, N_DEV):
            g = g + r_ref[s].astype(F32)
        m_new = ADAM_B1 * m_ref[...] + (1.0 - ADAM_B1) * g
        v_new = ADAM_B2 * v_ref[...] + (1.0 - ADAM_B2) * jnp.square(g)
        m_hat = m_new / (1.0 - ADAM_B1 ** ADAM_STEP)
        v_hat = v_new / (1.0 - ADAM_B2 ** ADAM_STEP)
        g_ref[...] = g
        d_ref[...] = -ADAM_LR * (m_hat / (jnp.sqrt(v_hat) + ADAM_EPS) + ADAM_WD * w_ref[...])
        nm_ref[...] = m_new
        nv_ref[...] = v_new

    row = pl.BlockSpec((tr, L), lambda i: (i, 0))
    state = row if layer is None else pl.BlockSpec((None, tr, L), lambda i: (layer, i, 0))
    sh = jax.ShapeDtypeStruct((R, L), F32)
    return pl.pallas_call(body, name=name, grid=(R // tr,), in_specs=[pl.BlockSpec((N_DEV, tr, L), lambda i: (0, i, 0)), state, state, state],
                          out_specs=(row, row, row, row), out_shape=(sh, sh, sh, sh), compiler_params=_params(("parallel",)))(recv, w, m, v)


GRAD_WIRE_DTYPE = BF16


class Prefetch:
    def __init__(self, local, order):
        self.local, self.todo, self.blocks, self.full = local, list(order), {}, {}

    def _block(self, key):
        n, i = key
        return self.local[n][i].astype(MXU_DTYPE)

    def _next(self):
        return self.todo.pop(0) if self.todo else None

    def mm(self, a, b, **kw):
        key = self._next()
        if key is None:
            return mm(a, b, **kw)
        out, self.blocks[key] = mm(a, b, gather=self._block(key), **kw)
        return out

    def ffn_hidden_fwd(self, h, wg, wu):
        key = self._next()
        if key is None:
            return ffn_hidden_fwd(h, wg, wu)
        g, u, a, self.blocks[key] = ffn_hidden_fwd(h, wg, wu, gather=self._block(key))
        return g, u, a

    def get(self, n, i):
        while (n, i) not in self.blocks:
            key = self.todo.pop(0)
            self.blocks[key] = all_gather(self._block(key), name="gather_" + key[0])
        if (n, i) not in self.full:
            self.full[(n, i)] = _from_shards(self.blocks[(n, i)], SHARD_AXIS[n] - 1)
        return self.full[(n, i)]


class Params:
    def __init__(self, **thunks):
        self.thunks, self.values = thunks, {}

    def __getitem__(self, k):
        if k not in self.values:
            self.values[k] = self.thunks[k]()
        return self.values[k]


class Wire:
    def __init__(self):
        self.waiting = []

    def post(self, slabs, deliver):
        self.waiting.append((slabs, deliver))

    def mm(self, a, b, **kw):
        if not self.waiting:
            return mm(a, b, **kw)
        slabs, deliver = self.waiting.pop(0)
        out, received = mm(a, b, send=slabs, **kw)
        deliver(received)
        return out

    def flush(self):
        for slabs, deliver in self.waiting:
            deliver(exchange(slabs, name="exchange_rest"))
        self.waiting = []


def _row_slabs(g):
    return g.reshape(N_DEV, g.shape[0] // N_DEV, g.shape[1])


def _col_slabs(g):
    return g.reshape(g.shape[0], N_DEV, g.shape[1] // N_DEV).transpose(1, 0, 2)


def ffn_fwd(x, p, pre):
    h = rms_fwd(x, p["norm"], name="ffn_rms_fwd")
    g, u, a = pre.ffn_hidden_fwd(h, p["w_gate"], p["w_up"])
    y = pre.mm(a, p["w_down"], add=x, scale=0.5, name="ffn_down")
    return y, (x, h, g, u, a)


def ffn_bwd(dy, saved, p, wire):
    x, h, g, u, a = saved
    gain, wg, wu, wd = p["norm"], p["w_gate"], p["w_up"], p["w_down"]
    dg, du = ffn_hidden_bwd(dy, wd, g, u, 0.5)
    dwd = wire.mm(a, dy, ta=True, scale=0.5, out_dtype=GRAD_WIRE_DTYPE, name="ffn_dwd")
    dh = wire.mm(dg, wg, tb=True, name="ffn_dh_gate")
    dh = wire.mm(du, wu, tb=True, add=dh, name="ffn_dh_up")
    dwg = wire.mm(h, dg, ta=True, out_dtype=GRAD_WIRE_DTYPE, col_slabs=N_DEV, name="ffn_dwg")
    dwu = wire.mm(h, du, ta=True, out_dtype=GRAD_WIRE_DTYPE, col_slabs=N_DEV, name="ffn_dwu")
    dx, dgain = rms_bwd(x, gain, dh, dy, name="ffn_rms_bwd")
    return dx, dgain[0], dwg, dwu, _row_slabs(dwd)


HYB_QKVZ = 4 * GDN_HEADS * GDN_D
HYB_AB = 2 * GDN_HEADS
HYB_U = len(POOL_WINDOWS) * POOL_GW


def _hyb_split_w_in(w_in):
    main = jnp.concatenate([w_in[:, :HYB_QKVZ], w_in[:, HYB_QKVZ + HYB_AB:]], axis=1)
    ab = jnp.pad(w_in[:, HYB_QKVZ:HYB_QKVZ + HYB_AB], ((0, 0), (0, LANES - HYB_AB)))
    return main, ab


def _gate_rows(t, n):
    return t.reshape(n, CHUNK, GDN_HEADS).transpose(0, 2, 1).reshape(n * GDN_HEADS, CHUNK)


def _gate_cols(r, n):
    return r.reshape(n, GDN_HEADS, CHUNK).transpose(0, 2, 1).reshape(n * CHUNK, GDN_HEADS)


def hyb_fwd(x, p, pre):
    T = x.shape[0]
    n = T // CHUNK
    h = rms_fwd(x, p["mix_norm"], name="mix_rms_fwd")
    w_main, w_ab = _hyb_split_w_in(p["w_in"])
    pm = pre.mm(h, w_main, name="hyb_in_main")
    pab = pre.mm(h, w_ab, name="hyb_in_gates")
    q, k, v = gdn_prep_fwd(pm, p["conv"])
    a_r = _gate_rows(pab[:, :GDN_HEADS], n)
    b_r = _gate_rows(pab[:, GDN_HEADS:HYB_AB], n)
    alog_c = jnp.tile(p["a_log"], n).reshape(n * GDN_HEADS, 1)
    dt_c = jnp.tile(p["dt_bias"], n).reshape(n * GDN_HEADS, 1)
    gc, beta = gdn_gates_fwd(a_r, b_r, alog_c, dt_c)
    gc3 = gc.reshape(n, GDN_HEADS, CHUNK)
    beta3 = beta.reshape(n, GDN_HEADS, CHUNK)
    o, states = gdr_fwd(q, k, v, gc3, beta3)
    og = gdn_gate_fwd(o, pm, p["out_norm"], z_block=3)
    pool = pool_fwd(pm, p["pool_w"], p["pool_scale"], u_block=4)
    mix = jnp.concatenate([og, pool], axis=1)
    y = pre.mm(mix, p["w_out"], add=x, name="hyb_out")
    return y, dict(x=x, h=h, pm=pm, q=q, k=k, v=v, a_r=a_r, b_r=b_r, alog_c=alog_c, dt_c=dt_c, gc3=gc3, beta3=beta3,
                   o=o, states=states, mix=mix)


def hyb_bwd(dy, s, p, wire):
    T = dy.shape[0]
    n = T // CHUNK
    half = GDN_HEADS * GDN_D
    w_main, w_ab = _hyb_split_w_in(p["w_in"])
    dmix = wire.mm(dy, p["w_out"], tb=True, name="hyb_dmix")
    dog, dpool = dmix[:, :half], dmix[:, half:]
    dw_out = _row_slabs(wire.mm(s["mix"], dy, ta=True, out_dtype=GRAD_WIRE_DTYPE, name="hyb_dwout"))
    do, dz, dout_norm = gdn_gate_bwd(s["o"], s["pm"], p["out_norm"], dog, z_block=3)
    dq, dk, dv, dgc, dbeta = gdr_bwd(s["q"], s["k"], s["v"], s["gc3"], s["beta3"], s["states"], do)
    da_r, db_r, dalog, ddt = gdn_gates_bwd(s["a_r"], s["b_r"], s["alog_c"], s["dt_c"],
                                           dgc.reshape(n * GDN_HEADS, CHUNK), dbeta.reshape(n * GDN_HEADS, CHUNK))
    dpab = jnp.pad(jnp.concatenate([_gate_cols(da_r, n), _gate_cols(db_r, n)], axis=1), ((0, 0), (0, LANES - HYB_AB)))
    dyc, dconv = gdn_prep_bwd(s["pm"], p["conv"], dq, dk, dv)
    dqkv = conv_dx(dyc, p["conv"])
    dd, dpool_w, dpool_scale = pool_bwd_a(s["pm"], p["pool_w"], p["pool_scale"], dpool, u_block=4)
    du = pool_bwd_b(dd)
    dpm = jnp.concatenate([dqkv, dz, du], axis=1)
    dh = wire.mm(dpm, w_main, tb=True, name="hyb_dh_main")
    dh = mm(dpab, w_ab, tb=True, add=dh, name="hyb_dh_gates")
    dw_main = wire.mm(s["h"], dpm, ta=True, name="hyb_dwin_main")
    dw_ab = mm(s["h"], dpab, ta=True, name="hyb_dwin_gates")
    dw_in = _col_slabs(jnp.concatenate([dw_main[:, :HYB_QKVZ], dw_ab[:, :HYB_AB], dw_main[:, HYB_QKVZ:]], axis=1)).astype(GRAD_WIRE_DTYPE)
    dx, dmix = rms_bwd(s["x"], p["mix_norm"], dh, dy, name="mix_rms_bwd")
    grads = dict(mix_norm=dmix[0], w_in=dw_in, conv=dconv, a_log=dalog[:, 0], dt_bias=ddt[:, 0], out_norm=dout_norm[0],
                 pool_w=dpool_w, pool_scale=dpool_scale[0], w_out=dw_out)
    return dx, grads


MLA_LAT = 2 * Q_LORA


def _mla_split_w_in(w_in):
    return w_in[:, :MLA_LAT], jnp.pad(w_in[:, MLA_LAT:], ((0, 0), (0, LANES - ROPE)))


def _to_heads(t, width):
    T = t.shape[0]
    return t.reshape(T, MLA_HEADS, width).transpose(1, 0, 2)


def _from_heads(t):
    H, T, W = t.shape
    return t.transpose(1, 0, 2).reshape(T, H * W)


def mla_fwd(x, p, rope, pre):
    T = x.shape[0]
    cos, sin, pmat = rope
    h = rms_fwd(x, p["mix_norm"], name="mix_rms_fwd")
    w_main, w_pe = _mla_split_w_in(p["w_in"])
    pm = pre.mm(h, w_main, name="mla_in_main")
    ppe = pre.mm(h, w_pe, name="mla_in_pe")
    qn = rms_fwd(pm, p["q_norm"], width=Q_LORA, col_block=0, name="mla_lat_rms_fwd")
    kvn = rms_fwd(pm, p["kv_norm"], width=Q_LORA, col_block=1, name="mla_lat_rms_fwd")
    q3 = _to_heads(pre.mm(qn, p["w_q_up"], name="mla_q_up"), QK_HEAD)
    kv3 = _to_heads(pre.mm(kvn, p["w_kv_up"], name="mla_kv_up"), NOPE + V_HEAD)
    kpe = jnp.broadcast_to(ppe[None, :, :ROPE], (MLA_HEADS, T, ROPE))
    k3 = jnp.concatenate([kv3[..., :NOPE], kpe], axis=-1)
    v3 = kv3[..., NOPE:].astype(MXU_DTYPE)
    qr = headnorm_rope_fwd(q3, p["q_head_norm"], cos, sin, pmat)
    kr = headnorm_rope_fwd(k3, p["k_head_norm"], cos, sin, pmat)
    o3, lse = attn_fwd(qr, kr, v3)
    o = _from_heads(o3).astype(MXU_DTYPE)
    y = pre.mm(o, p["w_out"], add=x, name="mla_out")
    return y, dict(x=x, h=h, pm=pm, qn=qn, kvn=kvn, q3=q3, k3=k3, v3=v3, qr=qr, kr=kr, o3=o3, lse=lse, o=o)


def mla_bwd(dy, s, p, rope, wire):
    cos, sin, pmat = rope
    w_main, w_pe = _mla_split_w_in(p["w_in"])
    do3 = _to_heads(wire.mm(dy, p["w_out"], tb=True, name="mla_do"), V_HEAD)
    dw_out = _row_slabs(wire.mm(s["o"], dy, ta=True, out_dtype=GRAD_WIRE_DTYPE, name="mla_dwout"))
    dqr, dkr, dv3 = attn_bwd(s["qr"], s["kr"], s["v3"], s["o3"], s["lse"], do3)
    dq3, dqhn = headnorm_rope_bwd(s["q3"], p["q_head_norm"], cos, sin, pmat.T, dqr)
    dk3, dkhn = headnorm_rope_bwd(s["k3"], p["k_head_norm"], cos, sin, pmat.T, dkr)
    dppe = jnp.pad(sum_heads(dk3)[:, NOPE:], ((0, 0), (0, LANES - ROPE)))
    dq = _from_heads(dq3)
    dkv = _from_heads(jnp.concatenate([dk3[..., :NOPE], dv3], axis=-1))
    dqn = mm(dq, p["w_q_up"], tb=True, name="mla_dqn")
    dkvn = wire.mm(dkv, p["w_kv_up"], tb=True, name="mla_dkvn")
    dw_q_up = mm(s["qn"], dq, ta=True, out_dtype=GRAD_WIRE_DTYPE, col_slabs=N_DEV, name="mla_dwq_up")
    dw_kv_up = wire.mm(s["kvn"], dkv, ta=True, out_dtype=GRAD_WIRE_DTYPE, col_slabs=N_DEV, name="mla_dwkv_up")
    dqlat, dq_norm = rms_bwd(s["pm"], p["q_norm"], dqn, width=Q_LORA, col_block=0, name="mla_lat_rms_bwd")
    dkvlat, dkv_norm = rms_bwd(s["pm"], p["kv_norm"], dkvn, width=Q_LORA, col_block=1, name="mla_lat_rms_bwd")
    dpm = jnp.concatenate([dqlat, dkvlat], axis=1)
    dh = mm(dpm, w_main, tb=True, name="mla_dh_main")
    dh = mm(dppe, w_pe, tb=True, add=dh, name="mla_dh_pe")
    dw_in = _row_slabs(jnp.concatenate([mm(s["h"], dpm, ta=True, name="mla_dwin_main"),
                                        mm(s["h"], dppe, ta=True, name="mla_dwin_pe")[:, :ROPE]], axis=1)).astype(GRAD_WIRE_DTYPE)
    dx, dmix = rms_bwd(s["x"], p["mix_norm"], dh, dy, name="mix_rms_bwd")
    grads = dict(mix_norm=dmix[0], w_in=dw_in, q_norm=dq_norm[0], kv_norm=dkv_norm[0], w_q_up=dw_q_up, w_kv_up=dw_kv_up,
                 q_head_norm=dqhn[0], k_head_norm=dkhn[0], w_out=dw_out)
    return dx, grads


WEIGHTS = ['ffn1_norm', 'ffn1_w_gate', 'ffn1_w_up', 'ffn1_w_down', 'mix_norm', 'ffn2_norm', 'ffn2_w_gate', 'ffn2_w_up',
           'ffn2_w_down', 'hyb_w_in', 'gdn_conv', 'gdn_a_log', 'gdn_dt_bias', 'gdn_out_norm', 'pool_w', 'pool_scale',
           'hyb_w_out', 'mla_w_in', 'mla_q_norm', 'mla_kv_norm', 'mla_w_q_up', 'mla_w_kv_up', 'mla_q_head_norm',
           'mla_k_head_norm', 'mla_w_out']
SHARD_AXIS = dict(ffn1_norm=None, ffn1_w_gate=2, ffn1_w_up=2, ffn1_w_down=1, mix_norm=None, ffn2_norm=None, ffn2_w_gate=2,
                  ffn2_w_up=2, ffn2_w_down=1, hyb_w_in=2, gdn_conv=2, gdn_a_log=None, gdn_dt_bias=None, gdn_out_norm=None,
                  pool_w=2, pool_scale=None, hyb_w_out=1, mla_w_in=1, mla_q_norm=1, mla_kv_norm=1, mla_w_q_up=2,
                  mla_w_kv_up=2, mla_q_head_norm=None, mla_k_head_norm=None, mla_w_out=1)
GATHER_BF16 = ['ffn1_w_gate', 'ffn1_w_up', 'ffn1_w_down', 'ffn2_w_gate', 'ffn2_w_up', 'ffn2_w_down', 'hyb_w_in', 'pool_w',
               'hyb_w_out', 'mla_w_in', 'mla_w_q_up', 'mla_w_kv_up', 'mla_w_out']
GATHER_F32 = ['gdn_conv', 'mla_q_norm', 'mla_kv_norm']
LARGE = ['ffn1_w_gate', 'ffn1_w_up', 'ffn1_w_down', 'ffn2_w_gate', 'ffn2_w_up', 'ffn2_w_down', 'hyb_w_in', 'hyb_w_out',
         'mla_w_in', 'mla_w_q_up', 'mla_w_kv_up', 'mla_w_out']
SMALL = [n for n in WEIGHTS if n not in LARGE]
SUBLANES = 8


def _pack(flat_list, lead=()):
    flat = jnp.concatenate(flat_list, axis=-1)
    rows = -(-flat.shape[-1] // (LANES * SUBLANES)) * SUBLANES
    flat = jnp.pad(flat, [(0, 0)] * len(lead) + [(0, rows * LANES - flat.shape[-1])])
    return flat.reshape(*lead, rows, LANES)


def _unpack(packed, shapes, lead=()):
    flat = packed.reshape(*lead, -1)
    out, off = [], 0
    for sh in shapes:
        n = math.prod(sh)
        out.append(flat[..., off:off + n].reshape(*lead, *sh))
        off += n
    return out


def _to_slabs(g, axis):
    if axis is None:
        return jnp.broadcast_to(g.reshape(1, -1), (N_DEV, g.size))
    sh = g.shape
    g = g.reshape(*sh[:axis], N_DEV, sh[axis] // N_DEV, *sh[axis + 1:])
    return jnp.moveaxis(g, axis, 0).reshape(N_DEV, -1)


def _from_shards(t, axis):
    t = jnp.moveaxis(t, 0, axis)
    sh = t.shape
    return t.reshape(*sh[:axis], sh[axis] * sh[axis + 1], *sh[axis + 2:])


def _gather_packed(local, names, dtype, name):
    got = all_gather(_pack([local[n].astype(dtype).reshape(-1) for n in names]), name=name)
    parts = _unpack(got, [local[n].shape for n in names], lead=(N_DEV,))
    return {n: _from_shards(t, SHARD_AXIS[n]) for n, t in zip(names, parts)}


def kernel(x, positions, ffn1_norm, ffn1_w_gate, ffn1_w_up, ffn1_w_down, mix_norm, ffn2_norm, ffn2_w_gate, ffn2_w_up, ffn2_w_down, hyb_w_in, gdn_conv, gdn_a_log, gdn_dt_bias, gdn_out_norm, pool_w, pool_scale, hyb_w_out, mla_w_in, mla_q_norm, mla_kv_norm, mla_w_q_up, mla_w_kv_up, mla_q_head_norm, mla_k_head_norm, mla_w_out, loss_target, m_ffn1_norm, m_ffn1_w_gate, m_ffn1_w_up, m_ffn1_w_down, m_mix_norm, m_ffn2_norm, m_ffn2_w_gate, m_ffn2_w_up, m_ffn2_w_down, m_hyb_w_in, m_gdn_conv, m_gdn_a_log, m_gdn_dt_bias, m_gdn_out_norm, m_pool_w, m_pool_scale, m_hyb_w_out, m_mla_w_in, m_mla_q_norm, m_mla_kv_norm, m_mla_w_q_up, m_mla_w_kv_up, m_mla_q_head_norm, m_mla_k_head_norm, m_mla_w_out, v_ffn1_norm, v_ffn1_w_gate, v_ffn1_w_up, v_ffn1_w_down, v_mix_norm, v_ffn2_norm, v_ffn2_w_gate, v_ffn2_w_up, v_ffn2_w_down, v_hyb_w_in, v_gdn_conv, v_gdn_a_log, v_gdn_dt_bias, v_gdn_out_norm, v_pool_w, v_pool_scale, v_hyb_w_out, v_mla_w_in, v_mla_q_norm, v_mla_kv_norm, v_mla_w_q_up, v_mla_w_kv_up, v_mla_q_head_norm, v_mla_k_head_norm, v_mla_w_out):
    given = dict(locals())
    local = {n: given[n] for n in WEIGHTS}
    depth = ffn1_norm.shape[0]
    xs = x[0]
    T = xs.shape[0]

    full = dict(local)
    full.update(_gather_packed(local, GATHER_F32, F32, "gather_f32"))

    order = []
    for layer in range(depth):
        i = layer // 2
        order += [("ffn1_w_gate", layer), ("ffn1_w_up", layer), ("ffn1_w_down", layer)]
        if layer % 2 == 0:
            order += [("hyb_w_in", i), ("pool_w", i), ("hyb_w_out", i)]
        else:
            order += [("mla_w_in", i), ("mla_w_q_up", i), ("mla_w_kv_up", i), ("mla_w_out", i)]
        order += [("ffn2_w_gate", layer), ("ffn2_w_up", layer), ("ffn2_w_down", layer)]
    assert sorted({n for n, _ in order}) == sorted(GATHER_BF16)
    pre = Prefetch(local, order)

    freq, pmat = _rope_consts()
    cos, sin = rope_tables(positions[0].astype(F32).reshape(T, 1), freq)
    rope = (cos, sin, pmat)

    def small(n, i):
        return lambda: full[n][i]

    def large(n, i):
        return lambda: pre.get(n, i)

    def mixer_params(layer):
        i = layer // 2
        if layer % 2 == 0:
            return Params(mix_norm=small("mix_norm", layer), w_in=large("hyb_w_in", i), conv=small("gdn_conv", i), a_log=small("gdn_a_log", i),
                          dt_bias=small("gdn_dt_bias", i), out_norm=small("gdn_out_norm", i), pool_w=large("pool_w", i),
                          pool_scale=small("pool_scale", i), w_out=large("hyb_w_out", i))
        return Params(mix_norm=small("mix_norm", layer), w_in=large("mla_w_in", i), q_norm=small("mla_q_norm", i), kv_norm=small("mla_kv_norm", i),
                      w_q_up=large("mla_w_q_up", i), w_kv_up=large("mla_w_kv_up", i), q_head_norm=small("mla_q_head_norm", i),
                      k_head_norm=small("mla_k_head_norm", i), w_out=large("mla_w_out", i))

    def ffn_params(which, layer):
        return Params(norm=small(which + "_norm", layer), w_gate=large(which + "_w_gate", layer), w_up=large(which + "_w_up", layer),
                      w_down=large(which + "_w_down", layer))

    saved = []
    cur = xs
    for layer in range(depth):
        cur, s1 = ffn_fwd(cur, ffn_params("ffn1", layer), pre)
        if layer % 2 == 0:
            cur, sm = hyb_fwd(cur, mixer_params(layer), pre)
        else:
            cur, sm = mla_fwd(cur, mixer_params(layer), rope, pre)
        cur, s2 = ffn_fwd(cur, ffn_params("ffn2", layer), pre)
        saved.append((s1, sm, s2))

    dcur, loss_local = loss_head(cur, loss_target[0])
    loss = lax.psum(loss_local, ("x", "y", "c"))

    hyb_names = dict(hyb_w_in="w_in", gdn_conv="conv", gdn_a_log="a_log", gdn_dt_bias="dt_bias", gdn_out_norm="out_norm",
                     pool_w="pool_w", pool_scale="pool_scale", hyb_w_out="w_out")
    mla_names = dict(mla_w_in="w_in", mla_q_norm="q_norm", mla_kv_norm="kv_norm", mla_w_q_up="w_q_up", mla_w_kv_up="w_kv_up",
                     mla_q_head_norm="q_head_norm", mla_k_head_norm="k_head_norm", mla_w_out="w_out")
    per_layer = {n: [None] * local[n].shape[0] for n in WEIGHTS}

    wire = Wire()

    def put(n, idx, g):
        if n not in LARGE:
            per_layer[n][idx] = g
            return

        def deliver(received):
            per_layer[n][idx] = adamw(received, local[n], given["m_" + n], given["v_" + n], layer=idx, name="adamw_" + n)

        wire.post(g, deliver)

    for layer in reversed(range(depth)):
        s1, sm, s2 = saved[layer]
        dcur, *gs = ffn_bwd(dcur, s2, ffn_params("ffn2", layer), wire)
        for n, g in zip(("ffn2_norm", "ffn2_w_gate", "ffn2_w_up", "ffn2_w_down"), gs):
            put(n, layer, g)
        if layer % 2 == 0:
            dcur, mg = hyb_bwd(dcur, sm, mixer_params(layer), wire)
            names = hyb_names
        else:
            dcur, mg = mla_bwd(dcur, sm, mixer_params(layer), rope, wire)
            names = mla_names
        put("mix_norm", layer, mg["mix_norm"])
        for n, key in names.items():
            put(n, layer // 2, mg[key])
        dcur, *gs = ffn_bwd(dcur, s1, ffn_params("ffn1", layer), wire)
        for n, g in zip(("ffn1_norm", "ffn1_w_gate", "ffn1_w_up", "ffn1_w_down"), gs):
            put(n, layer, g)
    wire.flush()
    grad_x = dcur[None]

    send = _pack([_to_slabs(jnp.stack(per_layer[n]), SHARD_AXIS[n]) for n in SMALL], lead=(N_DEV,))
    state = [_pack([src[n].reshape(-1) for n in SMALL]) for src in
             (local, {n: given["m_" + n] for n in SMALL}, {n: given["v_" + n] for n in SMALL})]
    small = [_unpack(o, [local[n].shape for n in SMALL]) for o in adamw(exchange(send, name="exchange_small"), *state, name="adamw_small")]

    outs = []
    for j in range(4):
        for n in WEIGHTS:
            outs.append(jnp.stack([t[j] for t in per_layer[n]]) if n in LARGE else small[j][SMALL.index(n)])
    return (loss, grad_x, *outs)
```

```python
import math

import jax
import jax.numpy as jnp
from jax import lax
from jax.experimental import pallas as pl
from jax.experimental.pallas import tpu as pltpu

F32 = jnp.float32
BF16 = jnp.bfloat16
MXU_DTYPE = jnp.bfloat16
HI = lax.Precision.HIGHEST
VMEM_LIMIT = 52 * 1024 * 1024
MM_VMEM_BUDGET = 40 * 1024 * 1024
LANES = 128
N_DEV = 8

EPS = 1e-6
GDN_HEADS = 8
GDN_D = 128
CHUNK = 64
CONV_K = 4
POOL_WINDOWS = (2, 4, 8, 16)
POOL_GW = 256
MLA_HEADS = 16
NOPE = 128
ROPE = 64
QK_HEAD = NOPE + ROPE
V_HEAD = 128
Q_LORA = 512
ROPE_THETA = 10000.0

ADAM_LR = 0.001
ADAM_B1 = 0.9
ADAM_B2 = 0.999
ADAM_EPS = 1e-08
ADAM_WD = 0.01
ADAM_STEP = 10
ADAMW_BLOCK_ELEMS = 128 * 1024


def _pick(n, cands):
    for c in cands:
        if n % c == 0:
            return c
    return n


def _params(sem=None):
    return pltpu.CompilerParams(dimension_semantics=sem, vmem_limit_bytes=VMEM_LIMIT)


def _sigmoid(x):
    return 1.0 / (1.0 + jnp.exp(-x))


def mm(a, b, *, ta=False, tb=False, add=None, scale=None, out_dtype=F32, col_slabs=None, send=None, gather=None, tiles=None, name="mm"):
    if ta:
        K, M = a.shape
    else:
        M, K = a.shape
    if tb:
        N, Kb = b.shape
    else:
        Kb, N = b.shape
    assert K == Kb, (a.shape, b.shape, ta, tb)
    tm = _pick(M, (1024, 512, 256, 128))
    tn = _pick(N if col_slabs is None else N // col_slabs, (1024, 512, 384, 256, 128))

    def vmem_bytes(tk):
        return (2 * tk * (tm * a.dtype.itemsize + tn * b.dtype.itemsize) + tm * tn * (4 + 2 * jnp.dtype(out_dtype).itemsize)
                + (2 * tm * tn * add.dtype.itemsize if add is not None else 0))

    tk = next(c for c in (2048, 1024, 512, 256, 128, K) if K % c == 0 and (c <= 128 or vmem_bytes(c) <= MM_VMEM_BUDGET))
    if tiles is not None:
        tm, tn, tk = tiles
    nk = K // tk
    a_spec = pl.BlockSpec((tk, tm), lambda i, j, k: (k, i)) if ta else pl.BlockSpec((tm, tk), lambda i, j, k: (i, k))
    b_spec = pl.BlockSpec((tn, tk), lambda i, j, k: (j, k)) if tb else pl.BlockSpec((tk, tn), lambda i, j, k: (k, j))
    if col_slabs is None:
        o_spec = pl.BlockSpec((tm, tn), lambda i, j, k: (i, j))
        o_shape = (M, N)
    else:
        assert add is None
        per = N // col_slabs // tn
        o_spec = pl.BlockSpec((None, tm, tn), lambda i, j, k: (j // per, i, j % per))
        o_shape = (col_slabs, M, N // col_slabs)
    dims = (((0 if ta else 1,), (1 if tb else 0,)), ((), ()))
    has_add = add is not None
    rider = _rider(send, gather)
    has_send = rider is not None
    grid = (M // tm, N // tn, nk)

    def body(*refs):
        refs = list(refs)
        a_ref, b_ref = refs[:2]
        c_ref = refs[2] if has_add else None
        n_in = 2 + has_add + has_send
        x_ref = refs[n_in - 1] if has_send else None
        o_ref = refs[n_in]
        r_ref = refs[n_in + 1] if has_send else None
        scratch = refs[n_in + 1 + has_send:]
        acc_ref = scratch[0] if nk > 1 else None
        i, j, k = pl.program_id(0), pl.program_id(1), pl.program_id(2)

        if has_send:
            ride_start, ride_finish = rider.plan(x_ref, r_ref, *scratch[-3:])
            pl.when((i == 0) & (j == 0) & (k == 0))(ride_start)

        prod = lax.dot_general(a_ref[...].astype(MXU_DTYPE), b_ref[...].astype(MXU_DTYPE), dims, preferred_element_type=F32)

        def finish(r):
            if scale is not None:
                r = r * scale
            if has_add:
                r = r + c_ref[...].astype(F32)
            o_ref[...] = r.astype(out_dtype)

        if nk == 1:
            finish(prod)
        else:
            @pl.when(k == 0)
            def _():
                acc_ref[...] = prod

            @pl.when(k > 0)
            def _():
                acc_ref[...] += prod

            @pl.when(k == nk - 1)
            def _():
                finish(acc_ref[...])

        if has_send:
            pl.when((i == grid[0] - 1) & (j == grid[1] - 1) & (k == nk - 1))(ride_finish)

    hbm = pl.BlockSpec(memory_space=pl.ANY)
    ins = [a, b] + ([add] if has_add else []) + ([rider.operand] if has_send else [])
    specs = [a_spec, b_spec] + ([o_spec] if has_add else []) + ([hbm] if has_send else [])
    scratch_shapes = ([pltpu.VMEM((tm, tn), F32)] if nk > 1 else []) + (_exchange_semaphores() if has_send else [])
    o_sds = jax.ShapeDtypeStruct(o_shape, out_dtype)
    return pl.pallas_call(
        body, name=name, grid=grid, in_specs=specs, out_specs=(o_spec, hbm) if has_send else o_spec,
        out_shape=(o_sds, rider.result) if has_send else o_sds, scratch_shapes=scratch_shapes,
        compiler_params=_params(("arbitrary",) * 3 if has_send else ("parallel", "parallel", "arbitrary")))(*ins)


def rms_fwd(x, gain, *, width=None, col_block=0, out_dtype=BF16, name="rms_fwd"):
    T = x.shape[0]
    W = x.shape[1] if width is None else width
    tt = _pick(T, (512, 256, 128, 64, 8))

    def body(x_ref, g_ref, o_ref):
        xv = x_ref[...]
        r = lax.rsqrt(jnp.mean(xv * xv, axis=-1, keepdims=True) + EPS)
        o_ref[...] = (xv * r * g_ref[...]).astype(out_dtype)

    return pl.pallas_call(
        body, name=name, grid=(T // tt,),
        in_specs=[pl.BlockSpec((tt, W), lambda i: (i, col_block)), pl.BlockSpec((1, W), lambda i: (0, 0))],
        out_specs=pl.BlockSpec((tt, W), lambda i: (i, 0)), out_shape=jax.ShapeDtypeStruct((T, W), out_dtype),
        compiler_params=_params(("parallel",)))(x, gain.reshape(1, W))


def rms_bwd(x, gain, dh, res=None, *, width=None, col_block=0, name="rms_bwd"):
    T = x.shape[0]
    W = x.shape[1] if width is None else width
    tt = _pick(T, (256, 128, 64, 8))
    has_res = res is not None

    def body(*refs):
        if has_res:
            x_ref, g_ref, dh_ref, res_ref, dx_ref, dg_ref = refs
        else:
            x_ref, g_ref, dh_ref, dx_ref, dg_ref = refs
        xv = x_ref[...]
        r = lax.rsqrt(jnp.mean(xv * xv, axis=-1, keepdims=True) + EPS)
        xhat = xv * r
        dy = dh_ref[...].astype(F32)
        dxhat = dy * g_ref[...]
        dx = r * (dxhat - xhat * jnp.mean(dxhat * xhat, axis=-1, keepdims=True))
        if has_res:
            dx = dx + res_ref[...]
        dx_ref[...] = dx

        @pl.when(pl.program_id(0) == 0)
        def _():
            dg_ref[...] = jnp.zeros_like(dg_ref)

        dg_ref[...] += jnp.sum(dy * xhat, axis=0, keepdims=True)

    row = pl.BlockSpec((tt, W), lambda i: (i, 0))
    ins = [x, gain.reshape(1, W), dh] + ([res] if has_res else [])
    specs = [pl.BlockSpec((tt, W), lambda i: (i, col_block)), pl.BlockSpec((1, W), lambda i: (0, 0)), row] + ([row] if has_res else [])
    return pl.pallas_call(
        body, name=name, grid=(T // tt,), in_specs=specs,
        out_specs=(row, pl.BlockSpec((1, W), lambda i: (0, 0))),
        out_shape=(jax.ShapeDtypeStruct((T, W), F32), jax.ShapeDtypeStruct((1, W), F32)),
        compiler_params=_params(("arbitrary",)))(*ins)


def ffn_hidden_fwd(h, wg, wu, gather=None, name="ffn_hidden_fwd"):
    T, D = h.shape
    F = wg.shape[1]
    tm = _pick(T, (1024, 512, 256, 128, 64, 8))
    tn = _pick(F, (512, 256, 128))
    grid = (T // tm, F // tn)
    rider = _rider(None, gather)

    def body(h_ref, wg_ref, wu_ref, *refs):
        if rider is not None:
            x_ref, g_ref, u_ref, a_ref, r_ref, *sems = refs
            ride_start, ride_finish = rider.plan(x_ref, r_ref, *sems)
            pl.when((pl.program_id(0) == 0) & (pl.program_id(1) == 0))(ride_start)
        else:
            g_ref, u_ref, a_ref = refs
        hv = h_ref[...].astype(MXU_DTYPE)
        g = jnp.dot(hv, wg_ref[...].astype(MXU_DTYPE), preferred_element_type=F32)
        u = jnp.dot(hv, wu_ref[...].astype(MXU_DTYPE), preferred_element_type=F32)
        g_ref[...] = g.astype(g_ref.dtype)
        u_ref[...] = u.astype(u_ref.dtype)
        a_ref[...] = (g * _sigmoid(g) * u).astype(a_ref.dtype)
        if rider is not None:
            pl.when((pl.program_id(0) == grid[0] - 1) & (pl.program_id(1) == grid[1] - 1))(ride_finish)

    hbm = pl.BlockSpec(memory_space=pl.ANY)
    wb = pl.BlockSpec((D, tn), lambda i, j: (0, j))
    ob = pl.BlockSpec((tm, tn), lambda i, j: (i, j))
    sh = jax.ShapeDtypeStruct((T, F), BF16)
    riding = rider is not None
    return pl.pallas_call(
        body, name=name, grid=grid, in_specs=[pl.BlockSpec((tm, D), lambda i, j: (i, 0)), wb, wb] + ([hbm] if riding else []),
        out_specs=(ob, ob, ob) + ((hbm,) if riding else ()), out_shape=(sh, sh, sh) + ((rider.result,) if riding else ()),
        scratch_shapes=_exchange_semaphores() if riding else [],
        compiler_params=_params(("arbitrary", "arbitrary") if riding else ("parallel", "parallel")))(*([h, wg, wu] + ([gather] if riding else [])))


def ffn_hidden_bwd(dy, wd, g, u, scale, name="ffn_hidden_bwd"):
    T, D = dy.shape
    F = wd.shape[0]
    tm = _pick(T, (1024, 512, 256, 128, 64, 8))
    tn = _pick(F, (512, 256, 128))

    def body(dy_ref, wd_ref, g_ref, u_ref, dg_ref, du_ref):
        da = lax.dot_general(dy_ref[...].astype(MXU_DTYPE), wd_ref[...].astype(MXU_DTYPE), _NT, preferred_element_type=F32) * scale
        gv = g_ref[...].astype(F32)
        s = _sigmoid(gv)
        dg_ref[...] = (da * u_ref[...].astype(F32) * (s * (1.0 + gv * (1.0 - s)))).astype(dg_ref.dtype)
        du_ref[...] = (da * (gv * s)).astype(du_ref.dtype)

    ob = pl.BlockSpec((tm, tn), lambda i, j: (i, j))
    sh = jax.ShapeDtypeStruct((T, F), BF16)
    return pl.pallas_call(body, name=name, grid=(T // tm, F // tn),
                          in_specs=[pl.BlockSpec((tm, D), lambda i, j: (i, 0)), pl.BlockSpec((tn, D), lambda i, j: (j, 0)), ob, ob],
                          out_specs=(ob, ob), out_shape=(sh, sh), compiler_params=_params(("parallel", "parallel")))(dy, wd, g, u)


def loss_head(y, target, name="loss_head"):
    T, D = y.shape
    tt = _pick(T, (512, 256, 128, 64, 8))

    def body(y_ref, t_ref, dy_ref, l_ref):
        e = y_ref[...] - t_ref[...]
        dy_ref[...] = e * (1.0 / D)

        @pl.when(pl.program_id(0) == 0)
        def _():
            l_ref[...] = jnp.zeros_like(l_ref)

        l_ref[...] += 0.5 * jnp.sum(jnp.mean(e * e, axis=-1, keepdims=True))

    row = pl.BlockSpec((tt, D), lambda i: (i, 0))
    dy, l = pl.pallas_call(body, name=name, grid=(T // tt,), in_specs=[row, row],
                           out_specs=(row, pl.BlockSpec((8, LANES), lambda i: (0, 0))),
                           out_shape=(jax.ShapeDtypeStruct((T, D), F32), jax.ShapeDtypeStruct((8, LANES), F32)),
                           compiler_params=_params(("arbitrary",)))(y, target)
    return dy, l[0, 0]


def _shifted(cur, prev, k, row):
    if k == 0:
        return cur
    return jnp.where(row < k, pltpu.roll(prev, k, 0), pltpu.roll(cur, k, 0))


def _conv_pre(x_ref, xp_ref, w_ref, first):
    cur = x_ref[...]
    prev = jnp.where(first, 0.0, xp_ref[...])
    row = lax.broadcasted_iota(jnp.int32, cur.shape, 0)
    xs = [_shifted(cur, prev, CONV_K - 1 - j, row) for j in range(CONV_K)]
    y = xs[0] * w_ref[0:1, :]
    for j in range(1, CONV_K):
        y = y + xs[j] * w_ref[j:j + 1, :]
    return y, xs


def gdn_prep_fwd(pm, conv_w, name="gdn_prep_fwd"):
    T = pm.shape[0]
    HW = GDN_HEADS * GDN_D
    tt = _pick(T, (256, 128, 64, 8))
    qscale = GDN_D ** -0.5

    def body(x_ref, xp_ref, w_ref, q_ref, k_ref, v_ref):
        y, _ = _conv_pre(x_ref, xp_ref, w_ref, pl.program_id(0) == 0)
        s = y * _sigmoid(y)
        for h in range(GDN_HEADS):
            for part, o_ref, sc in ((0, q_ref, qscale), (1, k_ref, 1.0)):
                sl = s[:, part * HW + h * GDN_D: part * HW + (h + 1) * GDN_D]
                r = lax.rsqrt(jnp.sum(sl * sl, axis=-1, keepdims=True) + EPS)
                o_ref[:, h * GDN_D:(h + 1) * GDN_D] = sl * (r * sc)
        v_ref[...] = s[:, 2 * HW:]

    blk = pl.BlockSpec((tt, 3 * HW), lambda i: (i, 0))
    blkp = pl.BlockSpec((tt, 3 * HW), lambda i: (jnp.maximum(i - 1, 0), 0))
    out = pl.BlockSpec((tt, HW), lambda i: (i, 0))
    sh = jax.ShapeDtypeStruct((T, HW), F32)
    return pl.pallas_call(body, name=name, grid=(T // tt,), in_specs=[blk, blkp, pl.BlockSpec((CONV_K, 3 * HW), lambda i: (0, 0))],
                          out_specs=(out, out, out), out_shape=(sh, sh, sh), compiler_params=_params(("parallel",)))(pm, pm, conv_w)


def gdn_prep_bwd(pm, conv_w, dq, dk, dv, name="gdn_prep_bwd"):
    T = pm.shape[0]
    HW = GDN_HEADS * GDN_D
    tt = _pick(T, (256, 128, 64, 8))
    qscale = GDN_D ** -0.5

    def body(x_ref, xp_ref, w_ref, dq_ref, dk_ref, dv_ref, dy_ref, dw_ref):
        y, xs = _conv_pre(x_ref, xp_ref, w_ref, pl.program_id(0) == 0)
        sg = _sigmoid(y)
        s = y * sg
        dsilu = sg * (1.0 + y * (1.0 - sg))
        for h in range(GDN_HEADS):
            for part, d_ref, sc in ((0, dq_ref, qscale), (1, dk_ref, 1.0)):
                lo = part * HW + h * GDN_D
                sl = s[:, lo:lo + GDN_D]
                r = lax.rsqrt(jnp.sum(sl * sl, axis=-1, keepdims=True) + EPS)
                n = sl * r
                dn = d_ref[:, h * GDN_D:(h + 1) * GDN_D] * sc
                ds = r * (dn - n * jnp.sum(dn * n, axis=-1, keepdims=True))
                dy_ref[:, lo:lo + GDN_D] = ds * dsilu[:, lo:lo + GDN_D]
        dy_ref[:, 2 * HW:] = dv_ref[...] * dsilu[:, 2 * HW:]

        @pl.when(pl.program_id(0) == 0)
        def _():
            dw_ref[...] = jnp.zeros_like(dw_ref)

        dyv = dy_ref[...]
        for j in range(CONV_K):
            dw_ref[j:j + 1, :] += jnp.sum(dyv * xs[j], axis=0, keepdims=True)

    blk = pl.BlockSpec((tt, 3 * HW), lambda i: (i, 0))
    blkp = pl.BlockSpec((tt, 3 * HW), lambda i: (jnp.maximum(i - 1, 0), 0))
    hb = pl.BlockSpec((tt, HW), lambda i: (i, 0))
    wb = pl.BlockSpec((CONV_K, 3 * HW), lambda i: (0, 0))
    return pl.pallas_call(body, name=name, grid=(T // tt,), in_specs=[blk, blkp, wb, hb, hb, hb], out_specs=(blk, wb),
                          out_shape=(jax.ShapeDtypeStruct((T, 3 * HW), F32), jax.ShapeDtypeStruct((CONV_K, 3 * HW), F32)),
                          compiler_params=_params(("arbitrary",)))(pm, pm, conv_w, dq, dk, dv)


def conv_dx(dy, conv_w, name="conv_dx"):
    T, W = dy.shape
    tt = _pick(T, (256, 128, 64, 8))
    nt = T // tt

    def body(d_ref, dn_ref, w_ref, dx_ref):
        cur = d_ref[...]
        nxt = jnp.where(pl.program_id(0) == nt - 1, 0.0, dn_ref[...])
        row = lax.broadcasted_iota(jnp.int32, cur.shape, 0)
        acc = cur * w_ref[CONV_K - 1:CONV_K, :]
        for j in range(CONV_K - 1):
            k = CONV_K - 1 - j
            sh = jnp.where(row >= tt - k, pltpu.roll(nxt, tt - k, 0), pltpu.roll(cur, tt - k, 0))
            acc = acc + sh * w_ref[j:j + 1, :]
        dx_ref[...] = acc

    blk = pl.BlockSpec((tt, W), lambda i: (i, 0))
    blkn = pl.BlockSpec((tt, W), lambda i: (jnp.minimum(i + 1, nt - 1), 0))
    return pl.pallas_call(body, name=name, grid=(nt,), in_specs=[blk, blkn, pl.BlockSpec((CONV_K, W), lambda i: (0, 0))], out_specs=blk,
                          out_shape=jax.ShapeDtypeStruct((T, W), F32), compiler_params=_params(("parallel",)))(dy, dy, conv_w)


def _upper_ones(c):
    return (lax.broadcasted_iota(jnp.int32, (c, c), 0) <= lax.broadcasted_iota(jnp.int32, (c, c), 1)).astype(F32)


def gdn_gates_fwd(a_r, b_r, alog_c, dt_c, name="gdn_gates_fwd"):
    R, C = a_r.shape

    def body(a_ref, b_ref, al_ref, dt_ref, gc_ref, beta_ref):
        x = a_ref[...] + dt_ref[...]
        sp = jnp.maximum(x, 0.0) + jnp.log1p(jnp.exp(-jnp.abs(x)))
        g = -jnp.exp(al_ref[...]) * sp
        gc_ref[...] = jnp.dot(g, _upper_ones(C), preferred_element_type=F32, precision=HI)
        beta_ref[...] = _sigmoid(b_ref[...])

    sh = jax.ShapeDtypeStruct((R, C), F32)
    return pl.pallas_call(body, name=name, out_shape=(sh, sh), compiler_params=_params())(a_r, b_r, alog_c, dt_c)


def gdn_gates_bwd(a_r, b_r, alog_c, dt_c, dgc, dbeta, name="gdn_gates_bwd"):
    R, C = a_r.shape

    def body(a_ref, b_ref, al_ref, dt_ref, dgc_ref, dbeta_ref, da_ref, db_ref, dal_ref, ddt_ref):
        x = a_ref[...] + dt_ref[...]
        sp = jnp.maximum(x, 0.0) + jnp.log1p(jnp.exp(-jnp.abs(x)))
        ea = jnp.exp(al_ref[...])
        dg = lax.dot_general(dgc_ref[...], _upper_ones(C), (((1,), (1,)), ((), ())), preferred_element_type=F32, precision=HI)
        dsp = dg * (-ea)
        da = dsp * _sigmoid(x)
        da_ref[...] = da
        beta = _sigmoid(b_ref[...])
        db_ref[...] = dbeta_ref[...] * beta * (1.0 - beta)
        sel = (lax.broadcasted_iota(jnp.int32, (GDN_HEADS, R), 1) % GDN_HEADS == lax.broadcasted_iota(jnp.int32, (GDN_HEADS, R), 0)).astype(F32)
        dal_ref[...] = jnp.sum(jnp.dot(sel, dg * (-ea * sp), preferred_element_type=F32, precision=HI), axis=1, keepdims=True)
        ddt_ref[...] = jnp.sum(jnp.dot(sel, da, preferred_element_type=F32, precision=HI), axis=1, keepdims=True)

    sh = jax.ShapeDtypeStruct((R, C), F32)
    s8 = jax.ShapeDtypeStruct((GDN_HEADS, 1), F32)
    return pl.pallas_call(body, name=name, out_shape=(sh, sh, s8, s8), compiler_params=_params())(a_r, b_r, alog_c, dt_c, dgc, dbeta)


def _dot(a, b):
    return jnp.dot(a, b, preferred_element_type=F32, precision=HI)


def _dot_nt(a, b):
    return lax.dot_general(a, b, (((1,), (1,)), ((), ())), preferred_element_type=F32, precision=HI)


def _dot_tn(a, b):
    return lax.dot_general(a, b, (((0,), (0,)), ((), ())), preferred_element_type=F32, precision=HI)


def _bdot(a, b, dims=(((1,), (0,)), ((), ()))):
    return lax.dot_general(a.astype(MXU_DTYPE), b.astype(MXU_DTYPE), dims, preferred_element_type=F32)


def _bdot_nt(a, b):
    return _bdot(a, b, (((1,), (1,)), ((), ())))


def _bdot_tn(a, b):
    return _bdot(a, b, (((0,), (0,)), ((), ())))


def _unit_lower_inverses(ms):
    c = ms[0].shape[0]
    eye = (lax.broadcasted_iota(jnp.int32, (c, c), 0) == lax.broadcasted_iota(jnp.int32, (c, c), 1)).astype(F32)
    ps = [-m for m in ms]
    ts = [eye + p for p in ps]
    n = 2
    while n < c:
        ps = [_dot(p, p) for p in ps]
        ts = [t + _dot(t, p) for t, p in zip(ts, ps)]
        n *= 2
    return ts


def _col(row, eye):
    c = eye.shape[0]
    return jnp.sum(jnp.where(eye, jnp.broadcast_to(row, (c, c)), 0.0), axis=1, keepdims=True)


def _row(col, eye):
    c = eye.shape[0]
    return jnp.sum(jnp.where(eye, jnp.broadcast_to(col, (c, c)), 0.0), axis=0, keepdims=True)


def _gdr_chunks(q_ref, k_ref, v_ref, gc_ref, b_ref, eye, ii, jj):
    C = eye.shape[0]
    fs = []
    for h in range(GDN_HEADS):
        sl = slice(h * GDN_D, (h + 1) * GDN_D)
        qh, kh, vh = q_ref[:, sl], k_ref[:, sl], v_ref[:, sl]
        gcr, br = gc_ref[0, h:h + 1, :], b_ref[0, h:h + 1, :]
        gcc = _col(gcr, eye)
        bc = _col(br, eye)
        causal = ii >= jj
        decay = jnp.where(causal, jnp.exp(jnp.where(causal, gcc - gcr, 0.0)), 0.0)
        decay_t = jnp.where(ii <= jj, jnp.exp(jnp.where(ii <= jj, gcr - gcc, 0.0)), 0.0)
        kb = kh * bc
        eg = jnp.exp(gcc)
        glast = gcr[:, C - 1:C]
        fs.append(dict(sl=sl, qh=qh, kh=kh, vh=vh, gcc=gcc, bc=bc, decay=decay, decay_t=decay_t, kb=kb, vb=vh * bc, eg=eg,
                       el=jnp.exp(glast), ekd=jnp.exp(glast - gcc), kbg=kb * eg,
                       m=jnp.where(ii > jj, _bdot_nt(kb, kh) * decay, 0.0)))
    for f, tinv in zip(fs, _unit_lower_inverses([f["m"] for f in fs])):
        f["tinv"] = tinv
    for f in fs:
        f["u"] = _dot(f["tinv"], f["vb"])
        f["w"] = _dot(f["tinv"], f["kbg"])
        f["a"] = _bdot_nt(f["qh"], f["kh"]) * f["decay"]
        f["qd"] = f["qh"] * f["eg"]
        f["kd"] = f["kh"] * f["ekd"]
    return fs


def gdr_fwd(q, k, v, gc, beta, name="gdr_fwd"):
    T = q.shape[0]
    H, DK, C = GDN_HEADS, GDN_D, CHUNK
    N = T // C

    def body(q_ref, k_ref, v_ref, gc_ref, b_ref, o_ref, st_ref, s_ref):
        @pl.when(pl.program_id(0) == 0)
        def _():
            s_ref[...] = jnp.zeros_like(s_ref)

        ii = lax.broadcasted_iota(jnp.int32, (C, C), 0)
        jj = lax.broadcasted_iota(jnp.int32, (C, C), 1)
        eye = ii == jj
        fs = _gdr_chunks(q_ref, k_ref, v_ref, gc_ref, b_ref, eye, ii, jj)
        ss = [s_ref[h] for h in range(H)]
        vnews = [f["u"] - _bdot(f["w"], s) for f, s in zip(fs, ss)]
        for h, (f, s, vnew) in enumerate(zip(fs, ss, vnews)):
            st_ref[0, h] = s
            o_ref[:, f["sl"]] = _bdot(f["qd"], s) + _bdot(f["a"], vnew)
            s_ref[h] = s * f["el"] + _bdot_tn(f["kd"], vnew)

    tok = pl.BlockSpec((C, H * DK), lambda n: (n, 0))
    gate = pl.BlockSpec((1, H, C), lambda n: (n, 0, 0))
    return pl.pallas_call(
        body, name=name, grid=(N,), in_specs=[tok, tok, tok, gate, gate],
        out_specs=(tok, pl.BlockSpec((1, H, DK, DK), lambda n: (n, 0, 0, 0))),
        out_shape=(jax.ShapeDtypeStruct((T, H * DK), F32), jax.ShapeDtypeStruct((N, H, DK, DK), F32)),
        scratch_shapes=[pltpu.VMEM((H, DK, DK), F32)], compiler_params=_params(("arbitrary",)))(q, k, v, gc, beta)


def gdr_bwd(q, k, v, gc, beta, states, do, name="gdr_bwd"):
    T = q.shape[0]
    H, DK, C = GDN_HEADS, GDN_D, CHUNK
    N = T // C

    def body(q_ref, k_ref, v_ref, gc_ref, b_ref, st_ref, do_ref, dq_ref, dk_ref, dv_ref, dgc_ref, db_ref, ds_ref):
        @pl.when(pl.program_id(0) == 0)
        def _():
            ds_ref[...] = jnp.zeros_like(ds_ref)

        ii = lax.broadcasted_iota(jnp.int32, (C, C), 0)
        jj = lax.broadcasted_iota(jnp.int32, (C, C), 1)
        eye = ii == jj
        lastj = lax.broadcasted_iota(jnp.int32, (1, C), 1) == C - 1
        fs = _gdr_chunks(q_ref, k_ref, v_ref, gc_ref, b_ref, eye, ii, jj)
        for h, f in enumerate(fs):
            f["s"] = st_ref[0, h]
            f["dsn"] = ds_ref[h]
            f["dout"] = do_ref[:, f["sl"]]
        for f in fs:
            f["vnew"] = f["u"] - _bdot(f["w"], f["s"])
            f["tinv_t"] = f["tinv"].T
            f["a_t"] = _bdot_nt(f["kh"], f["qh"]) * f["decay_t"]
        for f in fs:
            f["dvnew"] = _bdot(f["a_t"], f["dout"]) + _bdot(f["kd"], f["dsn"])
            f["da"] = _bdot_nt(f["dout"], f["vnew"])
            f["da_t"] = _bdot_nt(f["vnew"], f["dout"])
            f["dqd"] = _bdot_nt(f["dout"], f["s"])
            f["dkd"] = _bdot_nt(f["vnew"], f["dsn"])
        for h, f in enumerate(fs):
            ds_ref[h] = _bdot_tn(f["qd"], f["dout"]) - _bdot_tn(f["w"], f["dvnew"]) + f["dsn"] * f["el"]
            f["dw"] = -_bdot_nt(f["dvnew"], f["s"])
        for f in fs:
            f["dvb"] = _dot(f["tinv_t"], f["dvnew"])
            f["dkbg"] = _dot(f["tinv_t"], f["dw"])
        for f in fs:
            f["dm"] = jnp.where(ii > jj, -(_bdot_nt(f["dvb"], f["u"]) + _bdot_nt(f["dkbg"], f["w"])), 0.0)
            f["dm_t"] = jnp.where(ii < jj, -(_bdot_nt(f["u"], f["dvb"]) + _bdot_nt(f["w"], f["dkbg"])), 0.0)
        for h, f in enumerate(fs):
            sl, qh, kh, vh = f["sl"], f["qh"], f["kh"], f["vh"]
            dkd, dqd, dkbg, dvb = f["dkd"], f["dqd"], f["dkbg"], f["dvb"]
            dkk = f["dm"] * f["decay"]
            dkk_t = f["dm_t"] * f["decay_t"]
            dqk = f["da"] * f["decay"]
            dqk_t = f["da_t"] * f["decay_t"]
            e = f["dm"] * f["m"] + f["da"] * f["a"]
            dkb = _bdot(dkk, kh) + dkbg * f["eg"]
            dq_ref[:, sl] = _bdot(dqk, kh) + dqd * f["eg"]
            dk_ref[:, sl] = _bdot(dkk_t, f["kb"]) + _bdot(dqk_t, qh) + dkd * f["ekd"] + dkb * f["bc"]
            dv_ref[:, sl] = dvb * f["bc"]
            skd = jnp.sum(dkd * f["kd"], axis=1, keepdims=True)
            dgc_col = (jnp.sum(e, axis=1, keepdims=True) + jnp.sum(dqd * f["qd"], axis=1, keepdims=True)
                       + jnp.sum(dkbg * f["kbg"], axis=1, keepdims=True) - skd)
            dglast = jnp.sum(f["dsn"] * f["s"]) * f["el"] + jnp.sum(skd)
            dgc_row = _row(dgc_col, eye) - jnp.sum(e, axis=0, keepdims=True)
            dgc_ref[0, h:h + 1, :] = dgc_row + jnp.where(lastj, dglast, 0.0)
            dbeta_col = jnp.sum(dkb * kh, axis=1, keepdims=True) + jnp.sum(dvb * vh, axis=1, keepdims=True)
            db_ref[0, h:h + 1, :] = _row(dbeta_col, eye)

    rev = lambda n: (N - 1 - n, 0)
    tok = pl.BlockSpec((C, H * DK), rev)
    gate = pl.BlockSpec((1, H, C), lambda n: (N - 1 - n, 0, 0))
    tsh = jax.ShapeDtypeStruct((T, H * DK), F32)
    gsh = jax.ShapeDtypeStruct((N, H, C), F32)
    return pl.pallas_call(
        body, name=name, grid=(N,),
        in_specs=[tok, tok, tok, gate, gate, pl.BlockSpec((1, H, DK, DK), lambda n: (N - 1 - n, 0, 0, 0)), tok],
        out_specs=(tok, tok, tok, gate, gate), out_shape=(tsh, tsh, tsh, gsh, gsh),
        scratch_shapes=[pltpu.VMEM((H, DK, DK), F32)], compiler_params=_params(("arbitrary",)))(q, k, v, gc, beta, states, do)


def gdn_gate_fwd(o, pm, out_norm, *, z_block, name="gdn_gate_fwd"):
    T, HW = o.shape
    tt = _pick(T, (256, 128, 64, 8))

    def body(o_ref, z_ref, g_ref, y_ref):
        for h in range(GDN_HEADS):
            sl = slice(h * GDN_D, (h + 1) * GDN_D)
            ov = o_ref[:, sl]
            zv = z_ref[:, sl]
            r = lax.rsqrt(jnp.mean(ov * ov, axis=-1, keepdims=True) + EPS)
            y_ref[:, sl] = (ov * r * g_ref[...] * (zv * _sigmoid(zv))).astype(y_ref.dtype)

    blk = pl.BlockSpec((tt, HW), lambda i: (i, 0))
    return pl.pallas_call(body, name=name, grid=(T // tt,),
                          in_specs=[blk, pl.BlockSpec((tt, HW), lambda i: (i, z_block)), pl.BlockSpec((1, GDN_D), lambda i: (0, 0))],
                          out_specs=blk, out_shape=jax.ShapeDtypeStruct((T, HW), BF16),
                          compiler_params=_params(("parallel",)))(o, pm, out_norm.reshape(1, GDN_D))


def gdn_gate_bwd(o, pm, out_norm, dy, *, z_block, dy_block=0, name="gdn_gate_bwd"):
    T, HW = o.shape
    tt = _pick(T, (256, 128, 64, 8))

    def body(o_ref, z_ref, g_ref, dy_ref, do_ref, dz_ref, dg_ref):
        @pl.when(pl.program_id(0) == 0)
        def _():
            dg_ref[...] = jnp.zeros_like(dg_ref)

        acc = jnp.zeros((1, GDN_D), F32)
        for h in range(GDN_HEADS):
            sl = slice(h * GDN_D, (h + 1) * GDN_D)
            ov = o_ref[:, sl]
            zv = z_ref[:, sl]
            dyv = dy_ref[:, sl]
            r = lax.rsqrt(jnp.mean(ov * ov, axis=-1, keepdims=True) + EPS)
            xhat = ov * r
            sg = _sigmoid(zv)
            sz = zv * sg
            dn = dyv * sz
            dz_ref[:, sl] = dyv * (xhat * g_ref[...]) * (sg * (1.0 + zv * (1.0 - sg)))
            acc = acc + jnp.sum(dn * xhat, axis=0, keepdims=True)
            dxhat = dn * g_ref[...]
            do_ref[:, sl] = r * (dxhat - xhat * jnp.mean(dxhat * xhat, axis=-1, keepdims=True))
        dg_ref[...] += acc

    blk = pl.BlockSpec((tt, HW), lambda i: (i, 0))
    gb = pl.BlockSpec((1, GDN_D), lambda i: (0, 0))
    sh = jax.ShapeDtypeStruct((T, HW), F32)
    return pl.pallas_call(body, name=name, grid=(T // tt,),
                          in_specs=[blk, pl.BlockSpec((tt, HW), lambda i: (i, z_block)), gb, pl.BlockSpec((tt, HW), lambda i: (i, dy_block))],
                          out_specs=(blk, blk, gb), out_shape=(sh, sh, jax.ShapeDtypeStruct((1, GDN_D), F32)),
                          compiler_params=_params(("arbitrary",)))(o, pm, out_norm.reshape(1, GDN_D), dy)


def _pool_bands(tt, win, t0):
    t = lax.broadcasted_iota(jnp.int32, (tt, tt), 0)
    s = lax.broadcasted_iota(jnp.int32, (tt, tt), 1)
    inv = 1.0 / jnp.minimum(t + t0 + 1, win).astype(F32)
    cur = jnp.where((s <= t) & (s > t - win), inv, 0.0)
    prev = jnp.where(s - tt > t - win, inv, 0.0)
    return cur, prev


def _pool_diff(u_ref, up_ref, g, tt, t0, first):
    sl = slice(g * POOL_GW, (g + 1) * POOL_GW)
    cur, prev = _pool_bands(tt, POOL_WINDOWS[g], t0)
    ug = u_ref[:, sl]
    upg = jnp.where(first, 0.0, up_ref[:, sl])
    return _dot(cur, ug) + _dot(prev, upg) - ug


def pool_fwd(pm, pool_w, pool_scale, *, u_block, name="pool_fwd"):
    T = pm.shape[0]
    PW = len(POOL_WINDOWS) * POOL_GW
    tt = _pick(T, (256, 128, 64, 16))

    def body(u_ref, up_ref, w_ref, s_ref, p_ref):
        i = pl.program_id(0)
        for g in range(len(POOL_WINDOWS)):
            sl = slice(g * POOL_GW, (g + 1) * POOL_GW)
            diff = _pool_diff(u_ref, up_ref, g, tt, i * tt, i == 0)
            y = jnp.dot(diff.astype(MXU_DTYPE), w_ref[g].astype(MXU_DTYPE), preferred_element_type=F32)
            p_ref[:, sl] = (y * s_ref[:, sl]).astype(p_ref.dtype)

    return pl.pallas_call(
        body, name=name, grid=(T // tt,),
        in_specs=[pl.BlockSpec((tt, PW), lambda i: (i, u_block)), pl.BlockSpec((tt, PW), lambda i: (jnp.maximum(i - 1, 0), u_block)),
                  pl.BlockSpec((len(POOL_WINDOWS), POOL_GW, POOL_GW), lambda i: (0, 0, 0)), pl.BlockSpec((1, PW), lambda i: (0, 0))],
        out_specs=pl.BlockSpec((tt, PW), lambda i: (i, 0)), out_shape=jax.ShapeDtypeStruct((T, PW), BF16),
        compiler_params=_params(("parallel",)))(pm, pm, pool_w, pool_scale.reshape(1, PW))


def pool_bwd_a(pm, pool_w, pool_scale, dp, *, u_block, dp_block=0, name="pool_bwd_a"):
    T = pm.shape[0]
    G = len(POOL_WINDOWS)
    PW = G * POOL_GW
    tt = _pick(T, (256, 128, 64, 16))

    def body(u_ref, up_ref, w_ref, s_ref, dp_ref, dd_ref, dw_ref, dsc_ref):
        i = pl.program_id(0)

        @pl.when(i == 0)
        def _():
            dw_ref[...] = jnp.zeros_like(dw_ref)
            dsc_ref[...] = jnp.zeros_like(dsc_ref)

        for g in range(G):
            sl = slice(g * POOL_GW, (g + 1) * POOL_GW)
            diff = _pool_diff(u_ref, up_ref, g, tt, i * tt, i == 0).astype(MXU_DTYPE)
            wg = w_ref[g].astype(MXU_DTYPE)
            dpv = dp_ref[:, sl]
            y = jnp.dot(diff, wg, preferred_element_type=F32)
            dsc_ref[:, sl] += jnp.sum(dpv * y, axis=0, keepdims=True)
            dy = (dpv * s_ref[:, sl]).astype(MXU_DTYPE)
            dd_ref[:, sl] = lax.dot_general(dy, wg, (((1,), (1,)), ((), ())), preferred_element_type=F32)
            dw_ref[g] += lax.dot_general(diff, dy, (((0,), (0,)), ((), ())), preferred_element_type=F32)

    wb = pl.BlockSpec((G, POOL_GW, POOL_GW), lambda i: (0, 0, 0))
    sb = pl.BlockSpec((1, PW), lambda i: (0, 0))
    blk = pl.BlockSpec((tt, PW), lambda i: (i, 0))
    return pl.pallas_call(
        body, name=name, grid=(T // tt,),
        in_specs=[pl.BlockSpec((tt, PW), lambda i: (i, u_block)), pl.BlockSpec((tt, PW), lambda i: (jnp.maximum(i - 1, 0), u_block)), wb, sb,
                  pl.BlockSpec((tt, PW), lambda i: (i, dp_block))],
        out_specs=(blk, wb, sb),
        out_shape=(jax.ShapeDtypeStruct((T, PW), F32), jax.ShapeDtypeStruct((G, POOL_GW, POOL_GW), F32), jax.ShapeDtypeStruct((1, PW), F32)),
        compiler_params=_params(("arbitrary",)))(pm, pm, pool_w, pool_scale.reshape(1, PW), dp)


def pool_bwd_b(dd, name="pool_bwd_b"):
    T, PW = dd.shape
    tt = _pick(T, (256, 128, 64, 16))
    nt = T // tt

    def body(d_ref, dn_ref, du_ref):
        i = pl.program_id(0)
        s = lax.broadcasted_iota(jnp.int32, (tt, tt), 0)
        t = lax.broadcasted_iota(jnp.int32, (tt, tt), 1)
        for g, win in enumerate(POOL_WINDOWS):
            sl = slice(g * POOL_GW, (g + 1) * POOL_GW)
            inv_c = 1.0 / jnp.minimum(t + i * tt + 1, win).astype(F32)
            cur = jnp.where((t >= s) & (t < s + win), inv_c, 0.0)
            nxt = jnp.where(t + tt < s + win, 1.0 / win, 0.0)
            dg = d_ref[:, sl]
            dng = jnp.where(i == nt - 1, 0.0, dn_ref[:, sl])
            du_ref[:, sl] = _dot(cur, dg) + _dot(nxt, dng) - dg

    blk = pl.BlockSpec((tt, PW), lambda i: (i, 0))
    return pl.pallas_call(body, name=name, grid=(nt,), in_specs=[blk, pl.BlockSpec((tt, PW), lambda i: (jnp.minimum(i + 1, nt - 1), 0))],
                          out_specs=blk, out_shape=jax.ShapeDtypeStruct((T, PW), F32), compiler_params=_params(("parallel",)))(dd, dd)


def _rope_consts():
    j = jnp.arange(QK_HEAD)
    inv_freq = ROPE_THETA ** (-jnp.arange(0, ROPE, 2, dtype=F32) / ROPE)
    freq = jnp.where(j >= NOPE, inv_freq[(j - NOPE) % (ROPE // 2)], 0.0).astype(F32)
    half = ROPE // 2
    src = jnp.arange(QK_HEAD)[:, None]
    dst = jnp.arange(QK_HEAD)[None, :]
    first = (dst >= NOPE) & (dst < NOPE + half)
    second = dst >= NOPE + half
    p = jnp.where(first & (src == dst + half), -1.0, 0.0) + jnp.where(second & (src == dst - half), 1.0, 0.0)
    return freq.reshape(1, QK_HEAD), p.astype(F32)


def rope_tables(pos_col, freq, name="rope_tables"):
    T = pos_col.shape[0]
    tt = _pick(T, (512, 256, 128, 64, 8))

    def body(p_ref, f_ref, c_ref, s_ref):
        ang = p_ref[...] * f_ref[...]
        rot = lax.broadcasted_iota(jnp.int32, ang.shape, 1) >= NOPE
        c_ref[...] = jnp.where(rot, jnp.cos(ang), 1.0)
        s_ref[...] = jnp.where(rot, jnp.sin(ang), 0.0)

    blk = pl.BlockSpec((tt, QK_HEAD), lambda i: (i, 0))
    sh = jax.ShapeDtypeStruct((T, QK_HEAD), F32)
    return pl.pallas_call(body, name=name, grid=(T // tt,), in_specs=[pl.BlockSpec((tt, 1), lambda i: (i, 0)), pl.BlockSpec((1, QK_HEAD), lambda i: (0, 0))],
                          out_specs=(blk, blk), out_shape=(sh, sh), compiler_params=_params(("parallel",)))(pos_col, freq)


def _permute(x, p):
    hi = x.astype(BF16)
    lo = (x - hi.astype(F32)).astype(BF16)
    pb = p.astype(BF16)
    return jnp.dot(hi, pb, preferred_element_type=F32) + jnp.dot(lo, pb, preferred_element_type=F32)


def _seg_stats(t):
    lane = lax.broadcasted_iota(jnp.int32, t.shape, 1)
    nope = lane < NOPE
    sq = t * t
    r = jnp.where(nope, lax.rsqrt(jnp.sum(jnp.where(nope, sq, 0.0), axis=-1, keepdims=True) / NOPE + EPS),
                  lax.rsqrt(jnp.sum(jnp.where(nope, 0.0, sq), axis=-1, keepdims=True) / ROPE + EPS))
    return nope, r


def headnorm_rope_fwd(t, gain, cos, sin, pmat, name="headnorm_rope_fwd"):
    H, T, W = t.shape
    tt = _pick(T, (2048, 1024, 512, 256, 128, 64, 8))

    def body(t_ref, g_ref, c_ref, s_ref, p_ref, o_ref):
        tv = t_ref[0]
        _, r = _seg_stats(tv)
        y = tv * r * g_ref[...]
        o_ref[0] = (y * c_ref[...] + _permute(y, p_ref[...]) * s_ref[...]).astype(o_ref.dtype)

    blk = pl.BlockSpec((1, tt, W), lambda h, i: (h, i, 0))
    tab = pl.BlockSpec((tt, W), lambda h, i: (i, 0))
    return pl.pallas_call(body, name=name, grid=(H, T // tt),
                          in_specs=[blk, pl.BlockSpec((1, W), lambda h, i: (0, 0)), tab, tab, pl.BlockSpec((W, W), lambda h, i: (0, 0))],
                          out_specs=blk, out_shape=jax.ShapeDtypeStruct((H, T, W), BF16),
                          compiler_params=_params(("parallel", "parallel")))(t, gain.reshape(1, W), cos, sin, pmat)


def headnorm_rope_bwd(t, gain, cos, sin, pmat_t, dout, name="headnorm_rope_bwd"):
    H, T, W = t.shape
    tt = _pick(T, (2048, 1024, 512, 256, 128, 64, 8))

    def body(t_ref, g_ref, c_ref, s_ref, p_ref, do_ref, dt_ref, dg_ref):
        @pl.when((pl.program_id(0) == 0) & (pl.program_id(1) == 0))
        def _():
            dg_ref[...] = jnp.zeros_like(dg_ref)

        tv = t_ref[0]
        dov = do_ref[0]
        nope, r = _seg_stats(tv)
        dy = dov * c_ref[...] + _permute(dov * s_ref[...], p_ref[...])
        xhat = tv * r
        dg_ref[...] += jnp.sum(dy * xhat, axis=0, keepdims=True)
        dxhat = dy * g_ref[...]
        pr = dxhat * xhat
        mean = jnp.where(nope, jnp.sum(jnp.where(nope, pr, 0.0), axis=-1, keepdims=True) / NOPE,
                         jnp.sum(jnp.where(nope, 0.0, pr), axis=-1, keepdims=True) / ROPE)
        dt_ref[0] = r * (dxhat - xhat * mean)

    blk = pl.BlockSpec((1, tt, W), lambda h, i: (h, i, 0))
    tab = pl.BlockSpec((tt, W), lambda h, i: (i, 0))
    gb = pl.BlockSpec((1, W), lambda h, i: (0, 0))
    return pl.pallas_call(body, name=name, grid=(H, T // tt),
                          in_specs=[blk, gb, tab, tab, pl.BlockSpec((W, W), lambda h, i: (0, 0)), blk],
                          out_specs=(blk, gb), out_shape=(jax.ShapeDtypeStruct((H, T, W), F32), jax.ShapeDtypeStruct((1, W), F32)),
                          compiler_params=_params(("arbitrary", "arbitrary")))(t, gain.reshape(1, W), cos, sin, pmat_t, dout)


def sum_heads(x, name="sum_heads"):
    H, T, W = x.shape
    tt = _pick(T, (2048, 1024, 512, 256, 128, 64, 8))

    def body(x_ref, o_ref):
        @pl.when(pl.program_id(1) == 0)
        def _():
            o_ref[...] = jnp.zeros_like(o_ref)

        o_ref[...] += x_ref[0]

    return pl.pallas_call(body, name=name, grid=(T // tt, H), in_specs=[pl.BlockSpec((1, tt, W), lambda i, h: (h, i, 0))],
                          out_specs=pl.BlockSpec((tt, W), lambda i, h: (i, 0)), out_shape=jax.ShapeDtypeStruct((T, W), F32),
                          compiler_params=_params(("parallel", "arbitrary")))(x)


_NT = (((1,), (1,)), ((), ()))
_TN = (((0,), (0,)), ((), ()))
ATTN_SUB = 256


def _attn_fwd_tiles(T):
    tq = _pick(T, (1024, 512, 256, 128, 64))
    tk = _pick(T, (2048, 1024, 512, 256, 128, 64))
    return tq, tk, min(ATTN_SUB, tq, tk)


def _attn_bwd_tiles(T):
    tq = _pick(T, (2048, 1024, 512, 256, 128, 64))
    tk = _pick(T, (512, 256, 128, 64))
    return tq, tk, min(ATTN_SUB, tq, tk)


def attn_fwd(q, k, v, name="attn_fwd", tiles=None):
    H, T, DQ = q.shape
    DV = v.shape[2]
    tq, tk, sub = tiles or _attn_fwd_tiles(T)
    nq, nk, nsub = T // tq, T // tk, tk // sub
    scale = DQ ** -0.5

    def body(q_ref, k_ref, v_ref, o_ref, l_ref, m_s, l_s, acc_s):
        i, j = pl.program_id(1), pl.program_id(2)

        @pl.when(j == 0)
        def _():
            m_s[...] = jnp.full_like(m_s, -1e30)
            l_s[...] = jnp.zeros_like(l_s)
            acc_s[...] = jnp.zeros_like(acc_s)

        def tile(rel):
            qv = q_ref[0]
            m_old = m_s[...]
            m_new = m_old
            ss = {}
            for c in range(nsub):
                if rel is not None and c * sub > rel + tq - 1:
                    continue
                s = lax.dot_general(qv, k_ref[0, c * sub:(c + 1) * sub, :], _NT, preferred_element_type=F32) * scale
                if rel is not None and (c + 1) * sub - 1 > rel:
                    row = lax.broadcasted_iota(jnp.int32, s.shape, 0) + rel
                    col = lax.broadcasted_iota(jnp.int32, s.shape, 1) + c * sub
                    s = jnp.where(row >= col, s, -1e30)
                ss[c] = s
                m_new = jnp.maximum(m_new, jnp.max(s, axis=-1, keepdims=True))
            alpha = jnp.exp(m_old - m_new)
            l_new = alpha * l_s[...]
            acc = alpha * acc_s[...]
            for c, s in ss.items():
                p = jnp.exp(s - m_new)
                l_new = l_new + jnp.sum(p, axis=-1, keepdims=True)
                acc = acc + jnp.dot(p.astype(MXU_DTYPE), v_ref[0, c * sub:(c + 1) * sub, :], preferred_element_type=F32)
            l_s[...] = l_new
            acc_s[...] = acc
            m_s[...] = m_new

        rel = i * tq - j * tk
        pl.when(rel >= tk - 1)(lambda: tile(None))
        for r0 in range(0, tk - 1, tq):
            pl.when(rel == r0)(lambda r0=r0: tile(r0))

        @pl.when(j == nk - 1)
        def _():
            o_ref[0] = acc_s[...] / l_s[...]
            l_ref[0] = jnp.broadcast_to(m_s[...] + jnp.log(l_s[...]), (tq, DV))

    last = lambda i: (i * tq + (tq - 1)) // tk
    qb = pl.BlockSpec((1, tq, DQ), lambda h, i, j: (h, i, 0))
    kb = pl.BlockSpec((1, tk, DQ), lambda h, i, j: (h, jnp.minimum(j, last(i)), 0))
    vb = pl.BlockSpec((1, tk, DV), lambda h, i, j: (h, jnp.minimum(j, last(i)), 0))
    ob = pl.BlockSpec((1, tq, DV), lambda h, i, j: (h, i, 0))
    sh = jax.ShapeDtypeStruct((H, T, DV), F32)
    return pl.pallas_call(body, name=name, grid=(H, nq, nk), in_specs=[qb, kb, vb], out_specs=(ob, ob), out_shape=(sh, sh),
                          scratch_shapes=[pltpu.VMEM((tq, 1), F32), pltpu.VMEM((tq, 1), F32), pltpu.VMEM((tq, DV), F32)],
                          compiler_params=_params(("parallel", "parallel", "arbitrary")))(q, k, v)


def attn_bwd(q, k, v, o, lse, do, name="attn_bwd", tiles=None):
    H, T, DQ = q.shape
    DV = v.shape[2]
    tq, tk, sub = tiles or _attn_bwd_tiles(T)
    nq, nk, nsub = T // tq, T // tk, tq // sub
    scale = DQ ** -0.5

    def body(q_ref, k_ref, v_ref, o_ref, l_ref, do_ref, dq_ref, dk_ref, dv_ref, dk_s, dv_s):
        j, i = pl.program_id(1), pl.program_id(2)

        @pl.when((j == 0) & (i == 0))
        def _():
            dq_ref[...] = jnp.zeros_like(dq_ref)

        @pl.when(i == 0)
        def _():
            dk_s[...] = jnp.zeros_like(dk_s)
            dv_s[...] = jnp.zeros_like(dv_s)

        def tile(rel):
            kv, vv = k_ref[0], v_ref[0]
            live = [r for r in range(nsub) if rel is None or (r + 1) * sub - 1 >= rel]
            qs, dos, ss, dps = {}, {}, {}, {}
            for r in live:
                rs = slice(r * sub, (r + 1) * sub)
                qs[r] = q_ref[0, rs, :]
                dos[r] = do_ref[0, rs, :]
                ss[r] = lax.dot_general(qs[r], kv, _NT, preferred_element_type=F32) * scale
                dps[r] = lax.dot_general(dos[r].astype(MXU_DTYPE), vv, _NT, preferred_element_type=F32)
            dk_acc = dk_s[...]
            dv_acc = dv_s[...]
            for r in live:
                rs = slice(r * sub, (r + 1) * sub)
                p = jnp.exp(ss[r] - l_ref[0, rs, 0:1])
                if rel is not None and r * sub < rel + tk - 1:
                    row = lax.broadcasted_iota(jnp.int32, p.shape, 0) + r * sub
                    col = lax.broadcasted_iota(jnp.int32, p.shape, 1) + rel
                    p = jnp.where(row >= col, p, 0.0)
                delta = jnp.sum(dos[r] * o_ref[0, rs, :], axis=-1, keepdims=True)
                ds = (p * (dps[r] - delta) * scale).astype(MXU_DTYPE)
                dv_acc = dv_acc + lax.dot_general(p.astype(MXU_DTYPE), dos[r].astype(MXU_DTYPE), _TN, preferred_element_type=F32)
                dk_acc = dk_acc + lax.dot_general(ds, qs[r], _TN, preferred_element_type=F32)
                rows = pl.ds(pl.multiple_of(i * tq + r * sub, sub), sub)
                dq_ref[0, rows, :] += jnp.dot(ds, kv, preferred_element_type=F32)
            dk_s[...] = dk_acc
            dv_s[...] = dv_acc

        rel = j * tk - i * tq
        pl.when(rel <= 1 - tk)(lambda: tile(None))
        for r0 in range(0, tq, tk):
            pl.when(rel == r0)(lambda r0=r0: tile(r0))

        @pl.when(i == nq - 1)
        def _():
            dk_ref[0] = dk_s[...]
            dv_ref[0] = dv_s[...]

    first = lambda j: (j * tk) // tq
    qi = lambda h, j, i: (h, jnp.maximum(i, first(j)), 0)
    qb = pl.BlockSpec((1, tq, DQ), qi)
    ob = pl.BlockSpec((1, tq, DV), qi)
    kb = pl.BlockSpec((1, tk, DQ), lambda h, j, i: (h, j, 0))
    vb = pl.BlockSpec((1, tk, DV), lambda h, j, i: (h, j, 0))
    dqb = pl.BlockSpec((1, T, DQ), lambda h, j, i: (h, 0, 0))
    return pl.pallas_call(
        body, name=name, grid=(H, nk, nq), in_specs=[qb, kb, vb, ob, ob, ob], out_specs=(dqb, kb, vb),
        out_shape=(jax.ShapeDtypeStruct((H, T, DQ), F32), jax.ShapeDtypeStruct((H, T, DQ), F32), jax.ShapeDtypeStruct((H, T, DV), F32)),
        scratch_shapes=[pltpu.VMEM((tk, DQ), F32), pltpu.VMEM((tk, DV), F32)],
        compiler_params=_params(("parallel", "arbitrary", "arbitrary")))(q, k, v, o, lse, do)


def _mesh_place():
    return lax.axis_index("x"), lax.axis_index("y"), lax.axis_index("c")


def _flip(v, bit):
    return 1 - v if bit else v


def _relations():
    return [((r >> 2) & 1, (r >> 1) & 1, r & 1) for r in range(1, N_DEV)]


def _exchange_semaphores():
    return [pltpu.SemaphoreType.DMA((N_DEV - 1,)), pltpu.SemaphoreType.DMA((N_DEV - 1,)), pltpu.SemaphoreType.DMA(())]


def _gather_plan(x_ref, o_ref, send_sems, recv_sems, local_sem):
    x, y, c = _mesh_place()
    me, sibling = (x, y, c), (x, y, 1 - c)
    chips = [(1 - x, y), (x, 1 - y), (1 - x, 1 - y)]

    def slot(px, py, pc):
        return o_ref.at[4 * px + 2 * py + pc]

    def copy(k, block, to, src=None):
        return pltpu.make_async_remote_copy(
            src_ref=slot(*block) if src is None else src, dst_ref=slot(*block), send_sem=send_sems.at[k], recv_sem=recv_sems.at[k],
            device_id=to, device_id_type=pl.DeviceIdType.MESH)

    mine = pltpu.make_async_copy(x_ref, slot(*me), local_sem)
    first = [copy(0, me, sibling, src=x_ref)] + [copy(1 + j, me, (*chip, c), src=x_ref) for j, chip in enumerate(chips)]
    passed = [copy(4 + j, (*chip, c), sibling) for j, chip in enumerate(chips)]

    def start():
        mine.start()
        for cp in first:
            cp.start()

    def finish():
        for j, chip in enumerate(chips):
            copy(1 + j, (*chip, c), me).wait_recv()
            passed[j].start()
        copy(0, sibling, me).wait_recv()
        for j, chip in enumerate(chips):
            copy(4 + j, (*chip, 1 - c), me).wait_recv()
        for cp in first + passed:
            cp.wait_send()
        mine.wait()

    return start, finish


def _exchange_plan(x_ref, o_ref, send_sems, recv_sems, local_sem):
    x, y, c = _mesh_place()
    me = 4 * x + 2 * y + c
    local = pltpu.make_async_copy(x_ref.at[me], o_ref.at[me], local_sem)
    remote = []
    for r, (bx, by, bc) in enumerate(_relations()):
        px, py, pc = _flip(x, bx), _flip(y, by), _flip(c, bc)
        remote.append(pltpu.make_async_remote_copy(
            src_ref=x_ref.at[4 * px + 2 * py + pc], dst_ref=o_ref.at[me], send_sem=send_sems.at[r], recv_sem=recv_sems.at[r],
            device_id=(px, py, pc), device_id_type=pl.DeviceIdType.MESH))

    def start():
        local.start()
        for cp in remote:
            cp.start()

    def finish():
        for cp in remote:
            cp.wait_recv()
        for cp in remote:
            cp.wait_send()
        local.wait()

    return start, finish


class _Rider:
    def __init__(self, operand, result, plan):
        self.operand, self.result, self.plan = operand, result, plan


def _rider(send, gather):
    assert send is None or gather is None
    if send is not None:
        return _Rider(send, jax.ShapeDtypeStruct(send.shape, send.dtype), _exchange_plan)
    if gather is not None:
        return _Rider(gather, jax.ShapeDtypeStruct((N_DEV, *gather.shape), gather.dtype), _gather_plan)
    return None


def _transfer(rider, name):
    def body(x_ref, o_ref, send_sems, recv_sems, local_sem):
        start, finish = rider.plan(x_ref, o_ref, send_sems, recv_sems, local_sem)
        start()
        finish()

    return pl.pallas_call(body, name=name, out_shape=rider.result, in_specs=[pl.BlockSpec(memory_space=pl.ANY)],
                          out_specs=pl.BlockSpec(memory_space=pl.ANY), scratch_shapes=_exchange_semaphores())(rider.operand)


def all_gather(xs, name="all_gather"):
    return _transfer(_rider(None, xs), name)


def exchange(xs, name="exchange"):
    return _transfer(_rider(xs, None), name)


def adamw(recv, w, m, v, layer=None, name="adamw"):
    _, R, L = recv.shape
    tr = _pick(R, [c for c in (1024, 512, 256, 128, 64, 32, 16) if c * L <= ADAMW_BLOCK_ELEMS] + [8])

    def body(r_ref, w_ref, m_ref, v_ref, g_ref, d_ref, nm_ref, nv_ref):
        g = r_ref[0].astype(F32)
        for s in range(1, N_DEV):
            g = g + r_ref[s].astype(F32)
        m_new = ADAM_B1 * m_ref[...] + (1.0 - ADAM_B1) * g
        v_new = ADAM_B2 * v_ref[...] + (1.0 - ADAM_B2) * jnp.square(g)
        m_hat = m_new / (1.0 - ADAM_B1 ** ADAM_STEP)
        v_hat = v_new / (1.0 - ADAM_B2 ** ADAM_STEP)
        g_ref[...] = g
        d_ref[...] = -ADAM_LR * (m_hat / (jnp.sqrt(v_hat) + ADAM_EPS) + ADAM_WD * w_ref[...])
        nm_ref[...] = m_new
        nv_ref[...] = v_new

    row = pl.BlockSpec((tr, L), lambda i: (i, 0))
    state = row if layer is None else pl.BlockSpec((None, tr, L), lambda i: (layer, i, 0))
    sh = jax.ShapeDtypeStruct((R, L), F32)
    return pl.pallas_call(body, name=name, grid=(R // tr,), in_specs=[pl.BlockSpec((N_DEV, tr, L), lambda i: (0, i, 0)), state, state, state],
                          out_specs=(row, row, row, row), out_shape=(sh, sh, sh, sh), compiler_params=_params(("parallel",)))(recv, w, m, v)


GRAD_WIRE_DTYPE = BF16


class Prefetch:
    def __init__(self, local, order):
        self.local, self.todo, self.blocks, self.full = local, list(order), {}, {}
        self.where = {(n, i): ((names, i), k) for names, i in order for k, n in enumerate(names)}

    def _block(self, key):
        names, i = key
        return jnp.stack([self.local[n][i] for n in names]).astype(MXU_DTYPE)

    def _next(self):
        return self.todo.pop(0) if self.todo else None

    def mm(self, a, b, **kw):
        key = self._next()
        if key is None:
            return mm(a, b, **kw)
        out, self.blocks[key] = mm(a, b, gather=self._block(key), **kw)
        return out

    def ffn_hidden_fwd(self, h, wg, wu):
        key = self._next()
        if key is None:
            return ffn_hidden_fwd(h, wg, wu)
        g, u, a, self.blocks[key] = ffn_hidden_fwd(h, wg, wu, gather=self._block(key))
        return g, u, a

    def get(self, n, i):
        key, k = self.where[(n, i)]
        while key not in self.blocks:
            nxt = self.todo.pop(0)
            self.blocks[nxt] = all_gather(self._block(nxt), name="gather_" + nxt[0][0])
        if (n, i) not in self.full:
            self.full[(n, i)] = _from_shards(self.blocks[key][:, k], SHARD_AXIS[n] - 1)
        return self.full[(n, i)]


class Params:
    def __init__(self, **thunks):
        self.thunks, self.values = thunks, {}

    def __getitem__(self, k):
        if k not in self.values:
            self.values[k] = self.thunks[k]()
        return self.values[k]


class Wire:
    def __init__(self):
        self.waiting = []

    def post(self, slabs, deliver):
        self.waiting.append((slabs, deliver))

    def mm(self, a, b, **kw):
        if not self.waiting:
            return mm(a, b, **kw)
        slabs, deliver = self.waiting.pop(0)
        out, received = mm(a, b, send=slabs, **kw)
        deliver(received)
        return out

    def flush(self):
        for slabs, deliver in self.waiting:
            deliver(exchange(slabs, name="exchange_rest"))
        self.waiting = []


def _row_slabs(g):
    return g.reshape(N_DEV, g.shape[0] // N_DEV, g.shape[1])


def _col_slabs(g):
    return g.reshape(g.shape[0], N_DEV, g.shape[1] // N_DEV).transpose(1, 0, 2)


def ffn_fwd(x, p, pre):
    h = rms_fwd(x, p["norm"], name="ffn_rms_fwd")
    g, u, a = pre.ffn_hidden_fwd(h, p["w_gate"], p["w_up"])
    y = pre.mm(a, p["w_down"], add=x, scale=0.5, name="ffn_down")
    return y, (x, h, g, u, a)


def ffn_bwd(dy, saved, p, wire, post):
    x, h, g, u, a = saved
    gain, wg, wu, wd = p["norm"], p["w_gate"], p["w_up"], p["w_down"]
    dg, du = ffn_hidden_bwd(dy, wd, g, u, 0.5)
    post("w_down", _row_slabs(wire.mm(a, dy, ta=True, scale=0.5, out_dtype=GRAD_WIRE_DTYPE, name="ffn_dwd")))
    dh = wire.mm(dg, wg, tb=True, name="ffn_dh_gate")
    dh = wire.mm(du, wu, tb=True, add=dh, name="ffn_dh_up")
    post("w_gate", wire.mm(h, dg, ta=True, out_dtype=GRAD_WIRE_DTYPE, col_slabs=N_DEV, name="ffn_dwg"))
    post("w_up", wire.mm(h, du, ta=True, out_dtype=GRAD_WIRE_DTYPE, col_slabs=N_DEV, name="ffn_dwu"))
    dx, dgain = rms_bwd(x, gain, dh, dy, name="ffn_rms_bwd")
    post("norm", dgain[0])
    return dx


HYB_QKVZ = 4 * GDN_HEADS * GDN_D
HYB_AB = 2 * GDN_HEADS
HYB_U = len(POOL_WINDOWS) * POOL_GW


def _hyb_split_w_in(w_in):
    main = jnp.concatenate([w_in[:, :HYB_QKVZ], w_in[:, HYB_QKVZ + HYB_AB:]], axis=1)
    ab = jnp.pad(w_in[:, HYB_QKVZ:HYB_QKVZ + HYB_AB], ((0, 0), (0, LANES - HYB_AB)))
    return main, ab


def _gate_rows(t, n):
    return t.reshape(n, CHUNK, GDN_HEADS).transpose(0, 2, 1).reshape(n * GDN_HEADS, CHUNK)


def _gate_cols(r, n):
    return r.reshape(n, GDN_HEADS, CHUNK).transpose(0, 2, 1).reshape(n * CHUNK, GDN_HEADS)


def hyb_fwd(x, p, pre):
    T = x.shape[0]
    n = T // CHUNK
    h = rms_fwd(x, p["mix_norm"], name="mix_rms_fwd")
    w_main, w_ab = _hyb_split_w_in(p["w_in"])
    pm = pre.mm(h, w_main, name="hyb_in_main")
    pab = pre.mm(h, w_ab, name="hyb_in_gates")
    q, k, v = gdn_prep_fwd(pm, p["conv"])
    a_r = _gate_rows(pab[:, :GDN_HEADS], n)
    b_r = _gate_rows(pab[:, GDN_HEADS:HYB_AB], n)
    alog_c = jnp.tile(p["a_log"], n).reshape(n * GDN_HEADS, 1)
    dt_c = jnp.tile(p["dt_bias"], n).reshape(n * GDN_HEADS, 1)
    gc, beta = gdn_gates_fwd(a_r, b_r, alog_c, dt_c)
    gc3 = gc.reshape(n, GDN_HEADS, CHUNK)
    beta3 = beta.reshape(n, GDN_HEADS, CHUNK)
    o, states = gdr_fwd(q, k, v, gc3, beta3)
    og = gdn_gate_fwd(o, pm, p["out_norm"], z_block=3)
    pool = pool_fwd(pm, p["pool_w"], p["pool_scale"], u_block=4)
    mix = jnp.concatenate([og, pool], axis=1)
    y = pre.mm(mix, p["w_out"], add=x, name="hyb_out")
    return y, dict(x=x, h=h, pm=pm, q=q, k=k, v=v, a_r=a_r, b_r=b_r, alog_c=alog_c, dt_c=dt_c, gc3=gc3, beta3=beta3,
                   o=o, states=states, mix=mix)


def hyb_bwd(dy, s, p, wire):
    T = dy.shape[0]
    n = T // CHUNK
    half = GDN_HEADS * GDN_D
    w_main, w_ab = _hyb_split_w_in(p["w_in"])
    dmix = wire.mm(dy, p["w_out"], tb=True, name="hyb_dmix")
    dw_out = _row_slabs(wire.mm(s["mix"], dy, ta=True, out_dtype=GRAD_WIRE_DTYPE, name="hyb_dwout"))
    do, dz, dout_norm = gdn_gate_bwd(s["o"], s["pm"], p["out_norm"], dmix, z_block=3, dy_block=0)
    dq, dk, dv, dgc, dbeta = gdr_bwd(s["q"], s["k"], s["v"], s["gc3"], s["beta3"], s["states"], do)
    da_r, db_r, dalog, ddt = gdn_gates_bwd(s["a_r"], s["b_r"], s["alog_c"], s["dt_c"],
                                           dgc.reshape(n * GDN_HEADS, CHUNK), dbeta.reshape(n * GDN_HEADS, CHUNK))
    dpab = jnp.pad(jnp.concatenate([_gate_cols(da_r, n), _gate_cols(db_r, n)], axis=1), ((0, 0), (0, LANES - HYB_AB)))
    dyc, dconv = gdn_prep_bwd(s["pm"], p["conv"], dq, dk, dv)
    dqkv = conv_dx(dyc, p["conv"])
    dd, dpool_w, dpool_scale = pool_bwd_a(s["pm"], p["pool_w"], p["pool_scale"], dmix, u_block=4, dp_block=1)
    du = pool_bwd_b(dd)
    dpm = jnp.concatenate([dqkv, dz, du], axis=1)
    dh = wire.mm(dpm, w_main, tb=True, name="hyb_dh_main")
    dh = mm(dpab, w_ab, tb=True, add=dh, name="hyb_dh_gates")
    dw_main = wire.mm(s["h"], dpm, ta=True, name="hyb_dwin_main")
    dw_ab = mm(s["h"], dpab, ta=True, name="hyb_dwin_gates")
    dw_in = _col_slabs(jnp.concatenate([dw_main[:, :HYB_QKVZ], dw_ab[:, :HYB_AB], dw_main[:, HYB_QKVZ:]], axis=1)).astype(GRAD_WIRE_DTYPE)
    dx, dmix = rms_bwd(s["x"], p["mix_norm"], dh, dy, name="mix_rms_bwd")
    grads = dict(mix_norm=dmix[0], w_in=dw_in, conv=dconv, a_log=dalog[:, 0], dt_bias=ddt[:, 0], out_norm=dout_norm[0],
                 pool_w=dpool_w, pool_scale=dpool_scale[0], w_out=dw_out)
    return dx, grads


MLA_LAT = 2 * Q_LORA


def _mla_split_w_in(w_in):
    return w_in[:, :MLA_LAT], jnp.pad(w_in[:, MLA_LAT:], ((0, 0), (0, LANES - ROPE)))


def _to_heads(t, width):
    T = t.shape[0]
    return t.reshape(T, MLA_HEADS, width).transpose(1, 0, 2)


def _from_heads(t):
    H, T, W = t.shape
    return t.transpose(1, 0, 2).reshape(T, H * W)


def mla_fwd(x, p, rope, pre):
    T = x.shape[0]
    cos, sin, pmat = rope
    h = rms_fwd(x, p["mix_norm"], name="mix_rms_fwd")
    w_main, w_pe = _mla_split_w_in(p["w_in"])
    pm = pre.mm(h, w_main, name="mla_in_main")
    ppe = pre.mm(h, w_pe, name="mla_in_pe")
    qn = rms_fwd(pm, p["q_norm"], width=Q_LORA, col_block=0, name="mla_lat_rms_fwd")
    kvn = rms_fwd(pm, p["kv_norm"], width=Q_LORA, col_block=1, name="mla_lat_rms_fwd")
    q3 = _to_heads(pre.mm(qn, p["w_q_up"], name="mla_q_up"), QK_HEAD)
    kv3 = pre.mm(kvn, p["w_kv_up"], col_slabs=MLA_HEADS, name="mla_kv_up")
    kpe = jnp.broadcast_to(ppe[None, :, :ROPE], (MLA_HEADS, T, ROPE))
    k3 = jnp.concatenate([kv3[..., :NOPE], kpe], axis=-1)
    v3 = kv3[..., NOPE:].astype(MXU_DTYPE)
    qr = headnorm_rope_fwd(q3, p["q_head_norm"], cos, sin, pmat)
    kr = headnorm_rope_fwd(k3, p["k_head_norm"], cos, sin, pmat)
    o3, lse = attn_fwd(qr, kr, v3)
    o = _from_heads(o3).astype(MXU_DTYPE)
    y = pre.mm(o, p["w_out"], add=x, name="mla_out")
    return y, dict(x=x, h=h, pm=pm, qn=qn, kvn=kvn, q3=q3, k3=k3, v3=v3, qr=qr, kr=kr, o3=o3, lse=lse, o=o)


def mla_bwd(dy, s, p, rope, wire):
    cos, sin, pmat = rope
    w_main, w_pe = _mla_split_w_in(p["w_in"])
    do3 = _to_heads(wire.mm(dy, p["w_out"], tb=True, name="mla_do"), V_HEAD)
    dw_out = _row_slabs(wire.mm(s["o"], dy, ta=True, out_dtype=GRAD_WIRE_DTYPE, name="mla_dwout"))
    dqr, dkr, dv3 = attn_bwd(s["qr"], s["kr"], s["v3"], s["o3"], s["lse"], do3)
    dq3, dqhn = headnorm_rope_bwd(s["q3"], p["q_head_norm"], cos, sin, pmat.T, dqr)
    dk3, dkhn = headnorm_rope_bwd(s["k3"], p["k_head_norm"], cos, sin, pmat.T, dkr)
    dppe = jnp.pad(sum_heads(dk3)[:, NOPE:], ((0, 0), (0, LANES - ROPE)))
    dq = _from_heads(dq3)
    dkv = _from_heads(jnp.concatenate([dk3[..., :NOPE], dv3], axis=-1))
    dqn = mm(dq, p["w_q_up"], tb=True, name="mla_dqn")
    dkvn = wire.mm(dkv, p["w_kv_up"], tb=True, name="mla_dkvn")
    dw_q_up = mm(s["qn"], dq, ta=True, out_dtype=GRAD_WIRE_DTYPE, col_slabs=N_DEV, name="mla_dwq_up")
    dw_kv_up = wire.mm(s["kvn"], dkv, ta=True, out_dtype=GRAD_WIRE_DTYPE, col_slabs=N_DEV, name="mla_dwkv_up")
    dqlat, dq_norm = rms_bwd(s["pm"], p["q_norm"], dqn, width=Q_LORA, col_block=0, name="mla_lat_rms_bwd")
    dkvlat, dkv_norm = rms_bwd(s["pm"], p["kv_norm"], dkvn, width=Q_LORA, col_block=1, name="mla_lat_rms_bwd")
    dpm = jnp.concatenate([dqlat, dkvlat], axis=1)
    dh = mm(dpm, w_main, tb=True, name="mla_dh_main")
    dh = mm(dppe, w_pe, tb=True, add=dh, name="mla_dh_pe")
    dw_in = _row_slabs(jnp.concatenate([mm(s["h"], dpm, ta=True, name="mla_dwin_main"),
                                        mm(s["h"], dppe, ta=True, name="mla_dwin_pe")[:, :ROPE]], axis=1)).astype(GRAD_WIRE_DTYPE)
    dx, dmix = rms_bwd(s["x"], p["mix_norm"], dh, dy, name="mix_rms_bwd")
    grads = dict(mix_norm=dmix[0], w_in=dw_in, q_norm=dq_norm[0], kv_norm=dkv_norm[0], w_q_up=dw_q_up, w_kv_up=dw_kv_up,
                 q_head_norm=dqhn[0], k_head_norm=dkhn[0], w_out=dw_out)
    return dx, grads


WEIGHTS = ['ffn1_norm', 'ffn1_w_gate', 'ffn1_w_up', 'ffn1_w_down', 'mix_norm', 'ffn2_norm', 'ffn2_w_gate', 'ffn2_w_up',
           'ffn2_w_down', 'hyb_w_in', 'gdn_conv', 'gdn_a_log', 'gdn_dt_bias', 'gdn_out_norm', 'pool_w', 'pool_scale',
           'hyb_w_out', 'mla_w_in', 'mla_q_norm', 'mla_kv_norm', 'mla_w_q_up', 'mla_w_kv_up', 'mla_q_head_norm',
           'mla_k_head_norm', 'mla_w_out']
SHARD_AXIS = dict(ffn1_norm=None, ffn1_w_gate=2, ffn1_w_up=2, ffn1_w_down=1, mix_norm=None, ffn2_norm=None, ffn2_w_gate=2,
                  ffn2_w_up=2, ffn2_w_down=1, hyb_w_in=2, gdn_conv=2, gdn_a_log=None, gdn_dt_bias=None, gdn_out_norm=None,
                  pool_w=2, pool_scale=None, hyb_w_out=1, mla_w_in=1, mla_q_norm=1, mla_kv_norm=1, mla_w_q_up=2,
                  mla_w_kv_up=2, mla_q_head_norm=None, mla_k_head_norm=None, mla_w_out=1)
GATHER_BF16 = ['ffn1_w_gate', 'ffn1_w_up', 'ffn1_w_down', 'ffn2_w_gate', 'ffn2_w_up', 'ffn2_w_down', 'hyb_w_in', 'pool_w',
               'hyb_w_out', 'mla_w_in', 'mla_w_q_up', 'mla_w_kv_up', 'mla_w_out']
GATHER_F32 = ['gdn_conv', 'mla_q_norm', 'mla_kv_norm']
LARGE = ['ffn1_w_gate', 'ffn1_w_up', 'ffn1_w_down', 'ffn2_w_gate', 'ffn2_w_up', 'ffn2_w_down', 'hyb_w_in', 'hyb_w_out',
         'mla_w_in', 'mla_w_q_up', 'mla_w_kv_up', 'mla_w_out']
SMALL = [n for n in WEIGHTS if n not in LARGE]
SUBLANES = 8


def _pack(flat_list, lead=()):
    flat = jnp.concatenate(flat_list, axis=-1)
    rows = -(-flat.shape[-1] // (LANES * SUBLANES)) * SUBLANES
    flat = jnp.pad(flat, [(0, 0)] * len(lead) + [(0, rows * LANES - flat.shape[-1])])
    return flat.reshape(*lead, rows, LANES)


def _unpack(packed, shapes, lead=()):
    flat = packed.reshape(*lead, -1)
    out, off = [], 0
    for sh in shapes:
        n = math.prod(sh)
        out.append(flat[..., off:off + n].reshape(*lead, *sh))
        off += n
    return out


def _to_slabs(g, axis):
    if axis is None:
        return jnp.broadcast_to(g.reshape(1, -1), (N_DEV, g.size))
    sh = g.shape
    g = g.reshape(*sh[:axis], N_DEV, sh[axis] // N_DEV, *sh[axis + 1:])
    return jnp.moveaxis(g, axis, 0).reshape(N_DEV, -1)


def _from_shards(t, axis):
    t = jnp.moveaxis(t, 0, axis)
    sh = t.shape
    return t.reshape(*sh[:axis], sh[axis] * sh[axis + 1], *sh[axis + 2:])


def _gather_packed(local, names, dtype, name):
    got = all_gather(_pack([local[n].astype(dtype).reshape(-1) for n in names]), name=name)
    parts = _unpack(got, [local[n].shape for n in names], lead=(N_DEV,))
    return {n: _from_shards(t, SHARD_AXIS[n]) for n, t in zip(names, parts)}


def kernel(x, positions, ffn1_norm, ffn1_w_gate, ffn1_w_up, ffn1_w_down, mix_norm, ffn2_norm, ffn2_w_gate, ffn2_w_up, ffn2_w_down, hyb_w_in, gdn_conv, gdn_a_log, gdn_dt_bias, gdn_out_norm, pool_w, pool_scale, hyb_w_out, mla_w_in, mla_q_norm, mla_kv_norm, mla_w_q_up, mla_w_kv_up, mla_q_head_norm, mla_k_head_norm, mla_w_out, loss_target, m_ffn1_norm, m_ffn1_w_gate, m_ffn1_w_up, m_ffn1_w_down, m_mix_norm, m_ffn2_norm, m_ffn2_w_gate, m_ffn2_w_up, m_ffn2_w_down, m_hyb_w_in, m_gdn_conv, m_gdn_a_log, m_gdn_dt_bias, m_gdn_out_norm, m_pool_w, m_pool_scale, m_hyb_w_out, m_mla_w_in, m_mla_q_norm, m_mla_kv_norm, m_mla_w_q_up, m_mla_w_kv_up, m_mla_q_head_norm, m_mla_k_head_norm, m_mla_w_out, v_ffn1_norm, v_ffn1_w_gate, v_ffn1_w_up, v_ffn1_w_down, v_mix_norm, v_ffn2_norm, v_ffn2_w_gate, v_ffn2_w_up, v_ffn2_w_down, v_hyb_w_in, v_gdn_conv, v_gdn_a_log, v_gdn_dt_bias, v_gdn_out_norm, v_pool_w, v_pool_scale, v_hyb_w_out, v_mla_w_in, v_mla_q_norm, v_mla_kv_norm, v_mla_w_q_up, v_mla_w_kv_up, v_mla_q_head_norm, v_mla_k_head_norm, v_mla_w_out):
    given = dict(locals())
    local = {n: given[n] for n in WEIGHTS}
    depth = ffn1_norm.shape[0]
    xs = x[0]
    T = xs.shape[0]

    full = dict(local)
    full.update(_gather_packed(local, GATHER_F32, F32, "gather_f32"))

    order = []
    for layer in range(depth):
        i = layer // 2
        order += [(("ffn1_w_gate", "ffn1_w_up"), layer), (("ffn1_w_down",), layer)]
        if layer % 2 == 0:
            order += [(("hyb_w_in",), i), (("pool_w",), i), (("hyb_w_out",), i)]
        else:
            order += [(("mla_w_in",), i), (("mla_w_q_up",), i), (("mla_w_kv_up",), i), (("mla_w_out",), i)]
        order += [(("ffn2_w_gate", "ffn2_w_up"), layer), (("ffn2_w_down",), layer)]
    assert sorted({n for names, _ in order for n in names}) == sorted(GATHER_BF16)
    pre = Prefetch(local, order)

    freq, pmat = _rope_consts()
    cos, sin = rope_tables(positions[0].astype(F32).reshape(T, 1), freq)
    rope = (cos, sin, pmat)

    def small(n, i):
        return lambda: full[n][i]

    def large(n, i):
        return lambda: pre.get(n, i)

    def mixer_params(layer):
        i = layer // 2
        if layer % 2 == 0:
            return Params(mix_norm=small("mix_norm", layer), w_in=large("hyb_w_in", i), conv=small("gdn_conv", i), a_log=small("gdn_a_log", i),
                          dt_bias=small("gdn_dt_bias", i), out_norm=small("gdn_out_norm", i), pool_w=large("pool_w", i),
                          pool_scale=small("pool_scale", i), w_out=large("hyb_w_out", i))
        return Params(mix_norm=small("mix_norm", layer), w_in=large("mla_w_in", i), q_norm=small("mla_q_norm", i), kv_norm=small("mla_kv_norm", i),
                      w_q_up=large("mla_w_q_up", i), w_kv_up=large("mla_w_kv_up", i), q_head_norm=small("mla_q_head_norm", i),
                      k_head_norm=small("mla_k_head_norm", i), w_out=large("mla_w_out", i))

    def ffn_params(which, layer):
        return Params(norm=small(which + "_norm", layer), w_gate=large(which + "_w_gate", layer), w_up=large(which + "_w_up", layer),
                      w_down=large(which + "_w_down", layer))

    saved = []
    cur = xs
    for layer in range(depth):
        cur, s1 = ffn_fwd(cur, ffn_params("ffn1", layer), pre)
        if layer % 2 == 0:
            cur, sm = hyb_fwd(cur, mixer_params(layer), pre)
        else:
            cur, sm = mla_fwd(cur, mixer_params(layer), rope, pre)
        cur, s2 = ffn_fwd(cur, ffn_params("ffn2", layer), pre)
        saved.append((s1, sm, s2))

    dcur, loss_local = loss_head(cur, loss_target[0])
    loss = lax.psum(loss_local, ("x", "y", "c"))

    hyb_names = dict(hyb_w_in="w_in", gdn_conv="conv", gdn_a_log="a_log", gdn_dt_bias="dt_bias", gdn_out_norm="out_norm",
                     pool_w="pool_w", pool_scale="pool_scale", hyb_w_out="w_out")
    mla_names = dict(mla_w_in="w_in", mla_q_norm="q_norm", mla_kv_norm="kv_norm", mla_w_q_up="w_q_up", mla_w_kv_up="w_kv_up",
                     mla_q_head_norm="q_head_norm", mla_k_head_norm="k_head_norm", mla_w_out="w_out")
    per_layer = {n: [None] * local[n].shape[0] for n in WEIGHTS}

    wire = Wire()

    def put(n, idx, g):
        if n not in LARGE:
            per_layer[n][idx] = g
            return

        def deliver(received):
            per_layer[n][idx] = adamw(received, local[n], given["m_" + n], given["v_" + n], layer=idx, name="adamw_" + n)

        wire.post(g, deliver)

    for layer in reversed(range(depth)):
        s1, sm, s2 = saved[layer]
        dcur = ffn_bwd(dcur, s2, ffn_params("ffn2", layer), wire, lambda n, g, layer=layer: put("ffn2_" + n, layer, g))
        if layer % 2 == 0:
            dcur, mg = hyb_bwd(dcur, sm, mixer_params(layer), wire)
            names = hyb_names
        else:
            dcur, mg = mla_bwd(dcur, sm, mixer_params(layer), rope, wire)
            names = mla_names
        put("mix_norm", layer, mg["mix_norm"])
        for n, key in names.items():
            put(n, layer // 2, mg[key])
        dcur = ffn_bwd(dcur, s1, ffn_params("ffn1", layer), wire, lambda n, g, layer=layer: put("ffn1_" + n, layer, g))
    wire.flush()
    grad_x = dcur[None]

    send = _pack([_to_slabs(jnp.stack(per_layer[n]), SHARD_AXIS[n]) for n in SMALL], lead=(N_DEV,))
    state = [_pack([src[n].reshape(-1) for n in SMALL]) for src in
             (local, {n: given["m_" + n] for n in SMALL}, {n: given["v_" + n] for n in SMALL})]
    small = [_unpack(o, [local[n].shape for n in SMALL]) for o in adamw(exchange(send, name="exchange_small"), *state, name="adamw_small")]

    outs = []
    for j in range(4):
        for n in WEIGHTS:
            outs.append(jnp.stack([t[j] for t in per_layer[n]]) if n in LARGE else small[j][SMALL.index(n)])
    return (loss, grad_x, *outs)
```

```python
import math

import jax
import jax.numpy as jnp
from jax import lax
from jax.experimental import pallas as pl
from jax.experimental.pallas import tpu as pltpu

F32 = jnp.float32
BF16 = jnp.bfloat16
MXU_DTYPE = jnp.bfloat16
HI = lax.Precision.HIGHEST
VMEM_LIMIT = 52 * 1024 * 1024
MM_VMEM_BUDGET = 40 * 1024 * 1024
LANES = 128
N_DEV = 8

EPS = 1e-6
GDN_HEADS = 8
GDN_D = 128
CHUNK = 64
CONV_K = 4
POOL_WINDOWS = (2, 4, 8, 16)
POOL_GW = 256
MLA_HEADS = 16
NOPE = 128
ROPE = 64
QK_HEAD = NOPE + ROPE
V_HEAD = 128
Q_LORA = 512
ROPE_THETA = 10000.0

ADAM_LR = 0.001
ADAM_B1 = 0.9
ADAM_B2 = 0.999
ADAM_EPS = 1e-08
ADAM_WD = 0.01
ADAM_STEP = 10
ADAMW_BLOCK_ELEMS = 128 * 1024


def _pick(n, cands):
    for c in cands:
        if n % c == 0:
            return c
    return n


def _params(sem=None):
    return pltpu.CompilerParams(dimension_semantics=sem, vmem_limit_bytes=VMEM_LIMIT)


def _sigmoid(x):
    return 1.0 / (1.0 + jnp.exp(-x))


def mm(a, b, *, ta=False, tb=False, add=None, scale=None, out_dtype=F32, col_slabs=None, send=None, gather=None, tiles=None, name="mm"):
    if ta:
        K, M = a.shape
    else:
        M, K = a.shape
    if tb:
        N, Kb = b.shape
    else:
        Kb, N = b.shape
    assert K == Kb, (a.shape, b.shape, ta, tb)
    tm = _pick(M, (1024, 512, 256, 128))
    tn = _pick(N if col_slabs is None else N // col_slabs, (1024, 512, 384, 256, 128))

    def vmem_bytes(tk):
        return (2 * tk * (tm * a.dtype.itemsize + tn * b.dtype.itemsize) + tm * tn * (4 + 2 * jnp.dtype(out_dtype).itemsize)
                + (2 * tm * tn * add.dtype.itemsize if add is not None else 0))

    tk = next(c for c in (2048, 1024, 512, 256, 128, K) if K % c == 0 and (c <= 128 or vmem_bytes(c) <= MM_VMEM_BUDGET))
    if tiles is not None:
        tm, tn, tk = tiles
    nk = K // tk
    a_spec = pl.BlockSpec((tk, tm), lambda i, j, k: (k, i)) if ta else pl.BlockSpec((tm, tk), lambda i, j, k: (i, k))
    b_spec = pl.BlockSpec((tn, tk), lambda i, j, k: (j, k)) if tb else pl.BlockSpec((tk, tn), lambda i, j, k: (k, j))
    if col_slabs is None:
        o_spec = pl.BlockSpec((tm, tn), lambda i, j, k: (i, j))
        o_shape = (M, N)
    else:
        assert add is None
        per = N // col_slabs // tn
        o_spec = pl.BlockSpec((None, tm, tn), lambda i, j, k: (j // per, i, j % per))
        o_shape = (col_slabs, M, N // col_slabs)
    dims = (((0 if ta else 1,), (1 if tb else 0,)), ((), ()))
    has_add = add is not None
    rider = _rider(send, gather)
    has_send = rider is not None
    grid = (M // tm, N // tn, nk)

    def body(*refs):
        refs = list(refs)
        a_ref, b_ref = refs[:2]
        c_ref = refs[2] if has_add else None
        n_in = 2 + has_add + has_send
        x_ref = refs[n_in - 1] if has_send else None
        o_ref = refs[n_in]
        r_ref = refs[n_in + 1] if has_send else None
        scratch = refs[n_in + 1 + has_send:]
        acc_ref = scratch[0] if nk > 1 else None
        i, j, k = pl.program_id(0), pl.program_id(1), pl.program_id(2)

        if has_send:
            ride_start, ride_finish = rider.plan(x_ref, r_ref, *scratch[-3:])
            pl.when((i == 0) & (j == 0) & (k == 0))(ride_start)

        prod = lax.dot_general(a_ref[...].astype(MXU_DTYPE), b_ref[...].astype(MXU_DTYPE), dims, preferred_element_type=F32)

        def finish(r):
            if scale is not None:
                r = r * scale
            if has_add:
                r = r + c_ref[...].astype(F32)
            o_ref[...] = r.astype(out_dtype)

        if nk == 1:
            finish(prod)
        else:
            @pl.when(k == 0)
            def _():
                acc_ref[...] = prod

            @pl.when(k > 0)
            def _():
                acc_ref[...] += prod

            @pl.when(k == nk - 1)
            def _():
                finish(acc_ref[...])

        if has_send:
            pl.when((i == grid[0] - 1) & (j == grid[1] - 1) & (k == nk - 1))(ride_finish)

    hbm = pl.BlockSpec(memory_space=pl.ANY)
    ins = [a, b] + ([add] if has_add else []) + ([rider.operand] if has_send else [])
    specs = [a_spec, b_spec] + ([o_spec] if has_add else []) + ([hbm] if has_send else [])
    scratch_shapes = ([pltpu.VMEM((tm, tn), F32)] if nk > 1 else []) + (_exchange_semaphores() if has_send else [])
    o_sds = jax.ShapeDtypeStruct(o_shape, out_dtype)
    return pl.pallas_call(
        body, name=name, grid=grid, in_specs=specs, out_specs=(o_spec, hbm) if has_send else o_spec,
        out_shape=(o_sds, rider.result) if has_send else o_sds, scratch_shapes=scratch_shapes,
        compiler_params=_params(("arbitrary",) * 3 if has_send else ("parallel", "parallel", "arbitrary")))(*ins)


def rms_fwd(x, gain, *, width=None, col_block=0, out_dtype=BF16, name="rms_fwd"):
    T = x.shape[0]
    W = x.shape[1] if width is None else width
    tt = _pick(T, (512, 256, 128, 64, 8))

    def body(x_ref, g_ref, o_ref):
        xv = x_ref[...]
        r = lax.rsqrt(jnp.mean(xv * xv, axis=-1, keepdims=True) + EPS)
        o_ref[...] = (xv * r * g_ref[...]).astype(out_dtype)

    return pl.pallas_call(
        body, name=name, grid=(T // tt,),
        in_specs=[pl.BlockSpec((tt, W), lambda i: (i, col_block)), pl.BlockSpec((1, W), lambda i: (0, 0))],
        out_specs=pl.BlockSpec((tt, W), lambda i: (i, 0)), out_shape=jax.ShapeDtypeStruct((T, W), out_dtype),
        compiler_params=_params(("parallel",)))(x, gain.reshape(1, W))


def rms_bwd(x, gain, dh, res=None, *, width=None, col_block=0, name="rms_bwd"):
    T = x.shape[0]
    W = x.shape[1] if width is None else width
    tt = _pick(T, (256, 128, 64, 8))
    has_res = res is not None

    def body(*refs):
        if has_res:
            x_ref, g_ref, dh_ref, res_ref, dx_ref, dg_ref = refs
        else:
            x_ref, g_ref, dh_ref, dx_ref, dg_ref = refs
        xv = x_ref[...]
        r = lax.rsqrt(jnp.mean(xv * xv, axis=-1, keepdims=True) + EPS)
        xhat = xv * r
        dy = dh_ref[...].astype(F32)
        dxhat = dy * g_ref[...]
        dx = r * (dxhat - xhat * jnp.mean(dxhat * xhat, axis=-1, keepdims=True))
        if has_res:
            dx = dx + res_ref[...]
        dx_ref[...] = dx

        @pl.when(pl.program_id(0) == 0)
        def _():
            dg_ref[...] = jnp.zeros_like(dg_ref)

        dg_ref[...] += jnp.sum(dy * xhat, axis=0, keepdims=True)

    row = pl.BlockSpec((tt, W), lambda i: (i, 0))
    ins = [x, gain.reshape(1, W), dh] + ([res] if has_res else [])
    specs = [pl.BlockSpec((tt, W), lambda i: (i, col_block)), pl.BlockSpec((1, W), lambda i: (0, 0)), row] + ([row] if has_res else [])
    return pl.pallas_call(
        body, name=name, grid=(T // tt,), in_specs=specs,
        out_specs=(row, pl.BlockSpec((1, W), lambda i: (0, 0))),
        out_shape=(jax.ShapeDtypeStruct((T, W), F32), jax.ShapeDtypeStruct((1, W), F32)),
        compiler_params=_params(("arbitrary",)))(*ins)


def ffn_hidden_fwd(h, wg, wu, gather=None, name="ffn_hidden_fwd"):
    T, D = h.shape
    F = wg.shape[1]
    tm = _pick(T, (1024, 512, 256, 128, 64, 8))
    tn = _pick(F, (512, 256, 128))
    grid = (T // tm, F // tn)
    rider = _rider(None, gather)

    def body(h_ref, wg_ref, wu_ref, *refs):
        if rider is not None:
            x_ref, g_ref, u_ref, a_ref, r_ref, *sems = refs
            ride_start, ride_finish = rider.plan(x_ref, r_ref, *sems)
            pl.when((pl.program_id(0) == 0) & (pl.program_id(1) == 0))(ride_start)
        else:
            g_ref, u_ref, a_ref = refs
        hv = h_ref[...].astype(MXU_DTYPE)
        g = jnp.dot(hv, wg_ref[...].astype(MXU_DTYPE), preferred_element_type=F32)
        u = jnp.dot(hv, wu_ref[...].astype(MXU_DTYPE), preferred_element_type=F32)
        g_ref[...] = g.astype(g_ref.dtype)
        u_ref[...] = u.astype(u_ref.dtype)
        a_ref[...] = (g * _sigmoid(g) * u).astype(a_ref.dtype)
        if rider is not None:
            pl.when((pl.program_id(0) == grid[0] - 1) & (pl.program_id(1) == grid[1] - 1))(ride_finish)

    hbm = pl.BlockSpec(memory_space=pl.ANY)
    wb = pl.BlockSpec((D, tn), lambda i, j: (0, j))
    ob = pl.BlockSpec((tm, tn), lambda i, j: (i, j))
    sh = jax.ShapeDtypeStruct((T, F), BF16)
    riding = rider is not None
    return pl.pallas_call(
        body, name=name, grid=grid, in_specs=[pl.BlockSpec((tm, D), lambda i, j: (i, 0)), wb, wb] + ([hbm] if riding else []),
        out_specs=(ob, ob, ob) + ((hbm,) if riding else ()), out_shape=(sh, sh, sh) + ((rider.result,) if riding else ()),
        scratch_shapes=_exchange_semaphores() if riding else [],
        compiler_params=_params(("arbitrary", "arbitrary") if riding else ("parallel", "parallel")))(*([h, wg, wu] + ([gather] if riding else [])))


def ffn_hidden_bwd(dy, wd, g, u, scale, name="ffn_hidden_bwd"):
    T, D = dy.shape
    F = wd.shape[0]
    tm = _pick(T, (1024, 512, 256, 128, 64, 8))
    tn = _pick(F, (512, 256, 128))

    def body(dy_ref, wd_ref, g_ref, u_ref, dg_ref, du_ref):
        da = lax.dot_general(dy_ref[...].astype(MXU_DTYPE), wd_ref[...].astype(MXU_DTYPE), _NT, preferred_element_type=F32) * scale
        gv = g_ref[...].astype(F32)
        s = _sigmoid(gv)
        dg_ref[...] = (da * u_ref[...].astype(F32) * (s * (1.0 + gv * (1.0 - s)))).astype(dg_ref.dtype)
        du_ref[...] = (da * (gv * s)).astype(du_ref.dtype)

    ob = pl.BlockSpec((tm, tn), lambda i, j: (i, j))
    sh = jax.ShapeDtypeStruct((T, F), BF16)
    return pl.pallas_call(body, name=name, grid=(T // tm, F // tn),
                          in_specs=[pl.BlockSpec((tm, D), lambda i, j: (i, 0)), pl.BlockSpec((tn, D), lambda i, j: (j, 0)), ob, ob],
                          out_specs=(ob, ob), out_shape=(sh, sh), compiler_params=_params(("parallel", "parallel")))(dy, wd, g, u)


def loss_head(y, target, name="loss_head"):
    T, D = y.shape
    tt = _pick(T, (512, 256, 128, 64, 8))

    def body(y_ref, t_ref, dy_ref, l_ref):
        e = y_ref[...] - t_ref[...]
        dy_ref[...] = e * (1.0 / D)

        @pl.when(pl.program_id(0) == 0)
        def _():
            l_ref[...] = jnp.zeros_like(l_ref)

        l_ref[...] += 0.5 * jnp.sum(jnp.mean(e * e, axis=-1, keepdims=True))

    row = pl.BlockSpec((tt, D), lambda i: (i, 0))
    dy, l = pl.pallas_call(body, name=name, grid=(T // tt,), in_specs=[row, row],
                           out_specs=(row, pl.BlockSpec((8, LANES), lambda i: (0, 0))),
                           out_shape=(jax.ShapeDtypeStruct((T, D), F32), jax.ShapeDtypeStruct((8, LANES), F32)),
                           compiler_params=_params(("arbitrary",)))(y, target)
    return dy, l[0, 0]


def _shifted(cur, prev, k, row):
    if k == 0:
        return cur
    return jnp.where(row < k, pltpu.roll(prev, k, 0), pltpu.roll(cur, k, 0))


def _conv_pre(x_ref, xp_ref, w_ref, first):
    cur = x_ref[...]
    prev = jnp.where(first, 0.0, xp_ref[...])
    row = lax.broadcasted_iota(jnp.int32, cur.shape, 0)
    xs = [_shifted(cur, prev, CONV_K - 1 - j, row) for j in range(CONV_K)]
    y = xs[0] * w_ref[0:1, :]
    for j in range(1, CONV_K):
        y = y + xs[j] * w_ref[j:j + 1, :]
    return y, xs


def gdn_prep_fwd(pm, conv_w, name="gdn_prep_fwd"):
    T = pm.shape[0]
    HW = GDN_HEADS * GDN_D
    tt = _pick(T, (256, 128, 64, 8))
    qscale = GDN_D ** -0.5

    def body(x_ref, xp_ref, w_ref, q_ref, k_ref, v_ref):
        y, _ = _conv_pre(x_ref, xp_ref, w_ref, pl.program_id(0) == 0)
        s = y * _sigmoid(y)
        for h in range(GDN_HEADS):
            for part, o_ref, sc in ((0, q_ref, qscale), (1, k_ref, 1.0)):
                sl = s[:, part * HW + h * GDN_D: part * HW + (h + 1) * GDN_D]
                r = lax.rsqrt(jnp.sum(sl * sl, axis=-1, keepdims=True) + EPS)
                o_ref[:, h * GDN_D:(h + 1) * GDN_D] = sl * (r * sc)
        v_ref[...] = s[:, 2 * HW:]

    blk = pl.BlockSpec((tt, 3 * HW), lambda i: (i, 0))
    blkp = pl.BlockSpec((tt, 3 * HW), lambda i: (jnp.maximum(i - 1, 0), 0))
    out = pl.BlockSpec((tt, HW), lambda i: (i, 0))
    sh = jax.ShapeDtypeStruct((T, HW), F32)
    return pl.pallas_call(body, name=name, grid=(T // tt,), in_specs=[blk, blkp, pl.BlockSpec((CONV_K, 3 * HW), lambda i: (0, 0))],
                          out_specs=(out, out, out), out_shape=(sh, sh, sh), compiler_params=_params(("parallel",)))(pm, pm, conv_w)


def gdn_prep_bwd(pm, conv_w, dq, dk, dv, name="gdn_prep_bwd"):
    T = pm.shape[0]
    HW = GDN_HEADS * GDN_D
    tt = _pick(T, (256, 128, 64, 8))
    qscale = GDN_D ** -0.5

    def body(x_ref, xp_ref, w_ref, dq_ref, dk_ref, dv_ref, dy_ref, dw_ref):
        y, xs = _conv_pre(x_ref, xp_ref, w_ref, pl.program_id(0) == 0)
        sg = _sigmoid(y)
        s = y * sg
        dsilu = sg * (1.0 + y * (1.0 - sg))
        for h in range(GDN_HEADS):
            for part, d_ref, sc in ((0, dq_ref, qscale), (1, dk_ref, 1.0)):
                lo = part * HW + h * GDN_D
                sl = s[:, lo:lo + GDN_D]
                r = lax.rsqrt(jnp.sum(sl * sl, axis=-1, keepdims=True) + EPS)
                n = sl * r
                dn = d_ref[:, h * GDN_D:(h + 1) * GDN_D] * sc
                ds = r * (dn - n * jnp.sum(dn * n, axis=-1, keepdims=True))
                dy_ref[:, lo:lo + GDN_D] = ds * dsilu[:, lo:lo + GDN_D]
        dy_ref[:, 2 * HW:] = dv_ref[...] * dsilu[:, 2 * HW:]

        @pl.when(pl.program_id(0) == 0)
        def _():
            dw_ref[...] = jnp.zeros_like(dw_ref)

        dyv = dy_ref[...]
        for j in range(CONV_K):
            dw_ref[j:j + 1, :] += jnp.sum(dyv * xs[j], axis=0, keepdims=True)

    blk = pl.BlockSpec((tt, 3 * HW), lambda i: (i, 0))
    blkp = pl.BlockSpec((tt, 3 * HW), lambda i: (jnp.maximum(i - 1, 0), 0))
    hb = pl.BlockSpec((tt, HW), lambda i: (i, 0))
    wb = pl.BlockSpec((CONV_K, 3 * HW), lambda i: (0, 0))
    return pl.pallas_call(body, name=name, grid=(T // tt,), in_specs=[blk, blkp, wb, hb, hb, hb], out_specs=(blk, wb),
                          out_shape=(jax.ShapeDtypeStruct((T, 3 * HW), F32), jax.ShapeDtypeStruct((CONV_K, 3 * HW), F32)),
                          compiler_params=_params(("arbitrary",)))(pm, pm, conv_w, dq, dk, dv)


def conv_dx(dy, conv_w, name="conv_dx"):
    T, W = dy.shape
    tt = _pick(T, (256, 128, 64, 8))
    nt = T // tt

    def body(d_ref, dn_ref, w_ref, dx_ref):
        cur = d_ref[...]
        nxt = jnp.where(pl.program_id(0) == nt - 1, 0.0, dn_ref[...])
        row = lax.broadcasted_iota(jnp.int32, cur.shape, 0)
        acc = cur * w_ref[CONV_K - 1:CONV_K, :]
        for j in range(CONV_K - 1):
            k = CONV_K - 1 - j
            sh = jnp.where(row >= tt - k, pltpu.roll(nxt, tt - k, 0), pltpu.roll(cur, tt - k, 0))
            acc = acc + sh * w_ref[j:j + 1, :]
        dx_ref[...] = acc

    blk = pl.BlockSpec((tt, W), lambda i: (i, 0))
    blkn = pl.BlockSpec((tt, W), lambda i: (jnp.minimum(i + 1, nt - 1), 0))
    return pl.pallas_call(body, name=name, grid=(nt,), in_specs=[blk, blkn, pl.BlockSpec((CONV_K, W), lambda i: (0, 0))], out_specs=blk,
                          out_shape=jax.ShapeDtypeStruct((T, W), F32), compiler_params=_params(("parallel",)))(dy, dy, conv_w)


def _upper_ones(c):
    return (lax.broadcasted_iota(jnp.int32, (c, c), 0) <= lax.broadcasted_iota(jnp.int32, (c, c), 1)).astype(F32)


def gdn_gates_fwd(a_r, b_r, alog_c, dt_c, name="gdn_gates_fwd"):
    R, C = a_r.shape

    def body(a_ref, b_ref, al_ref, dt_ref, gc_ref, beta_ref):
        x = a_ref[...] + dt_ref[...]
        sp = jnp.maximum(x, 0.0) + jnp.log1p(jnp.exp(-jnp.abs(x)))
        g = -jnp.exp(al_ref[...]) * sp
        gc_ref[...] = jnp.dot(g, _upper_ones(C), preferred_element_type=F32, precision=HI)
        beta_ref[...] = _sigmoid(b_ref[...])

    sh = jax.ShapeDtypeStruct((R, C), F32)
    return pl.pallas_call(body, name=name, out_shape=(sh, sh), compiler_params=_params())(a_r, b_r, alog_c, dt_c)


def gdn_gates_bwd(a_r, b_r, alog_c, dt_c, dgc, dbeta, name="gdn_gates_bwd"):
    R, C = a_r.shape

    def body(a_ref, b_ref, al_ref, dt_ref, dgc_ref, dbeta_ref, da_ref, db_ref, dal_ref, ddt_ref):
        x = a_ref[...] + dt_ref[...]
        sp = jnp.maximum(x, 0.0) + jnp.log1p(jnp.exp(-jnp.abs(x)))
        ea = jnp.exp(al_ref[...])
        dg = lax.dot_general(dgc_ref[...], _upper_ones(C), (((1,), (1,)), ((), ())), preferred_element_type=F32, precision=HI)
        dsp = dg * (-ea)
        da = dsp * _sigmoid(x)
        da_ref[...] = da
        beta = _sigmoid(b_ref[...])
        db_ref[...] = dbeta_ref[...] * beta * (1.0 - beta)
        sel = (lax.broadcasted_iota(jnp.int32, (GDN_HEADS, R), 1) % GDN_HEADS == lax.broadcasted_iota(jnp.int32, (GDN_HEADS, R), 0)).astype(F32)
        dal_ref[...] = jnp.sum(jnp.dot(sel, dg * (-ea * sp), preferred_element_type=F32, precision=HI), axis=1, keepdims=True)
        ddt_ref[...] = jnp.sum(jnp.dot(sel, da, preferred_element_type=F32, precision=HI), axis=1, keepdims=True)

    sh = jax.ShapeDtypeStruct((R, C), F32)
    s8 = jax.ShapeDtypeStruct((GDN_HEADS, 1), F32)
    return pl.pallas_call(body, name=name, out_shape=(sh, sh, s8, s8), compiler_params=_params())(a_r, b_r, alog_c, dt_c, dgc, dbeta)


def _dot(a, b):
    return jnp.dot(a, b, preferred_element_type=F32, precision=HI)


def _dot_nt(a, b):
    return lax.dot_general(a, b, (((1,), (1,)), ((), ())), preferred_element_type=F32, precision=HI)


def _dot_tn(a, b):
    return lax.dot_general(a, b, (((0,), (0,)), ((), ())), preferred_element_type=F32, precision=HI)


def _bdot(a, b, dims=(((1,), (0,)), ((), ()))):
    return lax.dot_general(a.astype(MXU_DTYPE), b.astype(MXU_DTYPE), dims, preferred_element_type=F32)


def _bdot_nt(a, b):
    return _bdot(a, b, (((1,), (1,)), ((), ())))


def _bdot_tn(a, b):
    return _bdot(a, b, (((0,), (0,)), ((), ())))


def _unit_lower_inverses(ms):
    c = ms[0].shape[0]
    eye = (lax.broadcasted_iota(jnp.int32, (c, c), 0) == lax.broadcasted_iota(jnp.int32, (c, c), 1)).astype(F32)
    ps = [-m for m in ms]
    ts = [eye + p for p in ps]
    n = 2
    while n < c:
        ps = [_dot(p, p) for p in ps]
        ts = [t + _dot(t, p) for t, p in zip(ts, ps)]
        n *= 2
    return ts


def _col(row, eye):
    c = eye.shape[0]
    return jnp.sum(jnp.where(eye, jnp.broadcast_to(row, (c, c)), 0.0), axis=1, keepdims=True)


def _row(col, eye):
    c = eye.shape[0]
    return jnp.sum(jnp.where(eye, jnp.broadcast_to(col, (c, c)), 0.0), axis=0, keepdims=True)


def _gdr_chunks(q_ref, k_ref, v_ref, gc_ref, b_ref, eye, ii, jj):
    C = eye.shape[0]
    fs = []
    for h in range(GDN_HEADS):
        sl = slice(h * GDN_D, (h + 1) * GDN_D)
        qh, kh, vh = q_ref[:, sl], k_ref[:, sl], v_ref[:, sl]
        gcr, br = gc_ref[0, h:h + 1, :], b_ref[0, h:h + 1, :]
        gcc = _col(gcr, eye)
        bc = _col(br, eye)
        causal = ii >= jj
        decay = jnp.where(causal, jnp.exp(jnp.where(causal, gcc - gcr, 0.0)), 0.0)
        decay_t = jnp.where(ii <= jj, jnp.exp(jnp.where(ii <= jj, gcr - gcc, 0.0)), 0.0)
        kb = kh * bc
        eg = jnp.exp(gcc)
        glast = gcr[:, C - 1:C]
        fs.append(dict(sl=sl, qh=qh, kh=kh, vh=vh, gcc=gcc, bc=bc, decay=decay, decay_t=decay_t, kb=kb, vb=vh * bc, eg=eg,
                       el=jnp.exp(glast), ekd=jnp.exp(glast - gcc), kbg=kb * eg,
                       m=jnp.where(ii > jj, _bdot_nt(kb, kh) * decay, 0.0)))
    for f, tinv in zip(fs, _unit_lower_inverses([f["m"] for f in fs])):
        f["tinv"] = tinv
    for f in fs:
        f["u"] = _dot(f["tinv"], f["vb"])
        f["w"] = _dot(f["tinv"], f["kbg"])
        f["a"] = _bdot_nt(f["qh"], f["kh"]) * f["decay"]
        f["qd"] = f["qh"] * f["eg"]
        f["kd"] = f["kh"] * f["ekd"]
    return fs


def gdr_fwd(q, k, v, gc, beta, name="gdr_fwd"):
    T = q.shape[0]
    H, DK, C = GDN_HEADS, GDN_D, CHUNK
    N = T // C

    def body(q_ref, k_ref, v_ref, gc_ref, b_ref, o_ref, st_ref, s_ref):
        @pl.when(pl.program_id(0) == 0)
        def _():
            s_ref[...] = jnp.zeros_like(s_ref)

        ii = lax.broadcasted_iota(jnp.int32, (C, C), 0)
        jj = lax.broadcasted_iota(jnp.int32, (C, C), 1)
        eye = ii == jj
        fs = _gdr_chunks(q_ref, k_ref, v_ref, gc_ref, b_ref, eye, ii, jj)
        ss = [s_ref[h] for h in range(H)]
        vnews = [f["u"] - _bdot(f["w"], s) for f, s in zip(fs, ss)]
        for h, (f, s, vnew) in enumerate(zip(fs, ss, vnews)):
            st_ref[0, h] = s
            o_ref[:, f["sl"]] = _bdot(f["qd"], s) + _bdot(f["a"], vnew)
            s_ref[h] = s * f["el"] + _bdot_tn(f["kd"], vnew)

    tok = pl.BlockSpec((C, H * DK), lambda n: (n, 0))
    gate = pl.BlockSpec((1, H, C), lambda n: (n, 0, 0))
    return pl.pallas_call(
        body, name=name, grid=(N,), in_specs=[tok, tok, tok, gate, gate],
        out_specs=(tok, pl.BlockSpec((1, H, DK, DK), lambda n: (n, 0, 0, 0))),
        out_shape=(jax.ShapeDtypeStruct((T, H * DK), F32), jax.ShapeDtypeStruct((N, H, DK, DK), F32)),
        scratch_shapes=[pltpu.VMEM((H, DK, DK), F32)], compiler_params=_params(("arbitrary",)))(q, k, v, gc, beta)


def gdr_bwd(q, k, v, gc, beta, states, do, name="gdr_bwd"):
    T = q.shape[0]
    H, DK, C = GDN_HEADS, GDN_D, CHUNK
    N = T // C

    def body(q_ref, k_ref, v_ref, gc_ref, b_ref, st_ref, do_ref, dq_ref, dk_ref, dv_ref, dgc_ref, db_ref, ds_ref):
        @pl.when(pl.program_id(0) == 0)
        def _():
            ds_ref[...] = jnp.zeros_like(ds_ref)

        ii = lax.broadcasted_iota(jnp.int32, (C, C), 0)
        jj = lax.broadcasted_iota(jnp.int32, (C, C), 1)
        eye = ii == jj
        lastj = lax.broadcasted_iota(jnp.int32, (1, C), 1) == C - 1
        fs = _gdr_chunks(q_ref, k_ref, v_ref, gc_ref, b_ref, eye, ii, jj)
        for h, f in enumerate(fs):
            f["s"] = st_ref[0, h]
            f["dsn"] = ds_ref[h]
            f["dout"] = do_ref[:, f["sl"]]
        for f in fs:
            f["vnew"] = f["u"] - _bdot(f["w"], f["s"])
            f["tinv_t"] = f["tinv"].T
            f["a_t"] = _bdot_nt(f["kh"], f["qh"]) * f["decay_t"]
        for f in fs:
            f["dvnew"] = _bdot(f["a_t"], f["dout"]) + _bdot(f["kd"], f["dsn"])
            f["da"] = _bdot_nt(f["dout"], f["vnew"])
            f["da_t"] = _bdot_nt(f["vnew"], f["dout"])
            f["dqd"] = _bdot_nt(f["dout"], f["s"])
            f["dkd"] = _bdot_nt(f["vnew"], f["dsn"])
        for h, f in enumerate(fs):
            ds_ref[h] = _bdot_tn(f["qd"], f["dout"]) - _bdot_tn(f["w"], f["dvnew"]) + f["dsn"] * f["el"]
            f["dw"] = -_bdot_nt(f["dvnew"], f["s"])
        for f in fs:
            f["dvb"] = _dot(f["tinv_t"], f["dvnew"])
            f["dkbg"] = _dot(f["tinv_t"], f["dw"])
        for f in fs:
            f["dm"] = jnp.where(ii > jj, -(_bdot_nt(f["dvb"], f["u"]) + _bdot_nt(f["dkbg"], f["w"])), 0.0)
            f["dm_t"] = jnp.where(ii < jj, -(_bdot_nt(f["u"], f["dvb"]) + _bdot_nt(f["w"], f["dkbg"])), 0.0)
        for h, f in enumerate(fs):
            sl, qh, kh, vh = f["sl"], f["qh"], f["kh"], f["vh"]
            dkd, dqd, dkbg, dvb = f["dkd"], f["dqd"], f["dkbg"], f["dvb"]
            dkk = f["dm"] * f["decay"]
            dkk_t = f["dm_t"] * f["decay_t"]
            dqk = f["da"] * f["decay"]
            dqk_t = f["da_t"] * f["decay_t"]
            e = f["dm"] * f["m"] + f["da"] * f["a"]
            dkb = _bdot(dkk, kh) + dkbg * f["eg"]
            dq_ref[:, sl] = _bdot(dqk, kh) + dqd * f["eg"]
            dk_ref[:, sl] = _bdot(dkk_t, f["kb"]) + _bdot(dqk_t, qh) + dkd * f["ekd"] + dkb * f["bc"]
            dv_ref[:, sl] = dvb * f["bc"]
            skd = jnp.sum(dkd * f["kd"], axis=1, keepdims=True)
            dgc_col = (jnp.sum(e, axis=1, keepdims=True) + jnp.sum(dqd * f["qd"], axis=1, keepdims=True)
                       + jnp.sum(dkbg * f["kbg"], axis=1, keepdims=True) - skd)
            dglast = jnp.sum(f["dsn"] * f["s"]) * f["el"] + jnp.sum(skd)
            dgc_row = _row(dgc_col, eye) - jnp.sum(e, axis=0, keepdims=True)
            dgc_ref[0, h:h + 1, :] = dgc_row + jnp.where(lastj, dglast, 0.0)
            dbeta_col = jnp.sum(dkb * kh, axis=1, keepdims=True) + jnp.sum(dvb * vh, axis=1, keepdims=True)
            db_ref[0, h:h + 1, :] = _row(dbeta_col, eye)

    rev = lambda n: (N - 1 - n, 0)
    tok = pl.BlockSpec((C, H * DK), rev)
    gate = pl.BlockSpec((1, H, C), lambda n: (N - 1 - n, 0, 0))
    tsh = jax.ShapeDtypeStruct((T, H * DK), F32)
    gsh = jax.ShapeDtypeStruct((N, H, C), F32)
    return pl.pallas_call(
        body, name=name, grid=(N,),
        in_specs=[tok, tok, tok, gate, gate, pl.BlockSpec((1, H, DK, DK), lambda n: (N - 1 - n, 0, 0, 0)), tok],
        out_specs=(tok, tok, tok, gate, gate), out_shape=(tsh, tsh, tsh, gsh, gsh),
        scratch_shapes=[pltpu.VMEM((H, DK, DK), F32)], compiler_params=_params(("arbitrary",)))(q, k, v, gc, beta, states, do)


def gdn_gate_fwd(o, pm, out_norm, *, z_block, name="gdn_gate_fwd"):
    T, HW = o.shape
    tt = _pick(T, (256, 128, 64, 8))

    def body(o_ref, z_ref, g_ref, y_ref):
        for h in range(GDN_HEADS):
            sl = slice(h * GDN_D, (h + 1) * GDN_D)
            ov = o_ref[:, sl]
            zv = z_ref[:, sl]
            r = lax.rsqrt(jnp.mean(ov * ov, axis=-1, keepdims=True) + EPS)
            y_ref[:, sl] = (ov * r * g_ref[...] * (zv * _sigmoid(zv))).astype(y_ref.dtype)

    blk = pl.BlockSpec((tt, HW), lambda i: (i, 0))
    return pl.pallas_call(body, name=name, grid=(T // tt,),
                          in_specs=[blk, pl.BlockSpec((tt, HW), lambda i: (i, z_block)), pl.BlockSpec((1, GDN_D), lambda i: (0, 0))],
                          out_specs=blk, out_shape=jax.ShapeDtypeStruct((T, HW), BF16),
                          compiler_params=_params(("parallel",)))(o, pm, out_norm.reshape(1, GDN_D))


def gdn_gate_bwd(o, pm, out_norm, dy, *, z_block, dy_block=0, name="gdn_gate_bwd"):
    T, HW = o.shape
    tt = _pick(T, (256, 128, 64, 8))

    def body(o_ref, z_ref, g_ref, dy_ref, do_ref, dz_ref, dg_ref):
        @pl.when(pl.program_id(0) == 0)
        def _():
            dg_ref[...] = jnp.zeros_like(dg_ref)

        acc = jnp.zeros((1, GDN_D), F32)
        for h in range(GDN_HEADS):
            sl = slice(h * GDN_D, (h + 1) * GDN_D)
            ov = o_ref[:, sl]
            zv = z_ref[:, sl]
            dyv = dy_ref[:, sl]
            r = lax.rsqrt(jnp.mean(ov * ov, axis=-1, keepdims=True) + EPS)
            xhat = ov * r
            sg = _sigmoid(zv)
            sz = zv * sg
            dn = dyv * sz
            dz_ref[:, sl] = dyv * (xhat * g_ref[...]) * (sg * (1.0 + zv * (1.0 - sg)))
            acc = acc + jnp.sum(dn * xhat, axis=0, keepdims=True)
            dxhat = dn * g_ref[...]
            do_ref[:, sl] = r * (dxhat - xhat * jnp.mean(dxhat * xhat, axis=-1, keepdims=True))
        dg_ref[...] += acc

    blk = pl.BlockSpec((tt, HW), lambda i: (i, 0))
    gb = pl.BlockSpec((1, GDN_D), lambda i: (0, 0))
    sh = jax.ShapeDtypeStruct((T, HW), F32)
    return pl.pallas_call(body, name=name, grid=(T // tt,),
                          in_specs=[blk, pl.BlockSpec((tt, HW), lambda i: (i, z_block)), gb, pl.BlockSpec((tt, HW), lambda i: (i, dy_block))],
                          out_specs=(blk, blk, gb), out_shape=(sh, sh, jax.ShapeDtypeStruct((1, GDN_D), F32)),
                          compiler_params=_params(("arbitrary",)))(o, pm, out_norm.reshape(1, GDN_D), dy)


def _pool_bands(tt, win, t0):
    t = lax.broadcasted_iota(jnp.int32, (tt, tt), 0)
    s = lax.broadcasted_iota(jnp.int32, (tt, tt), 1)
    inv = 1.0 / jnp.minimum(t + t0 + 1, win).astype(F32)
    cur = jnp.where((s <= t) & (s > t - win), inv, 0.0)
    prev = jnp.where(s - tt > t - win, inv, 0.0)
    return cur, prev


def _pool_diff(u_ref, up_ref, g, tt, t0, first):
    sl = slice(g * POOL_GW, (g + 1) * POOL_GW)
    cur, prev = _pool_bands(tt, POOL_WINDOWS[g], t0)
    ug = u_ref[:, sl]
    upg = jnp.where(first, 0.0, up_ref[:, sl])
    return _dot(cur, ug) + _dot(prev, upg) - ug


def pool_fwd(pm, pool_w, pool_scale, *, u_block, name="pool_fwd"):
    T = pm.shape[0]
    PW = len(POOL_WINDOWS) * POOL_GW
    tt = _pick(T, (256, 128, 64, 16))

    def body(u_ref, up_ref, w_ref, s_ref, p_ref):
        i = pl.program_id(0)
        for g in range(len(POOL_WINDOWS)):
            sl = slice(g * POOL_GW, (g + 1) * POOL_GW)
            diff = _pool_diff(u_ref, up_ref, g, tt, i * tt, i == 0)
            y = jnp.dot(diff.astype(MXU_DTYPE), w_ref[g].astype(MXU_DTYPE), preferred_element_type=F32)
            p_ref[:, sl] = (y * s_ref[:, sl]).astype(p_ref.dtype)

    return pl.pallas_call(
        body, name=name, grid=(T // tt,),
        in_specs=[pl.BlockSpec((tt, PW), lambda i: (i, u_block)), pl.BlockSpec((tt, PW), lambda i: (jnp.maximum(i - 1, 0), u_block)),
                  pl.BlockSpec((len(POOL_WINDOWS), POOL_GW, POOL_GW), lambda i: (0, 0, 0)), pl.BlockSpec((1, PW), lambda i: (0, 0))],
        out_specs=pl.BlockSpec((tt, PW), lambda i: (i, 0)), out_shape=jax.ShapeDtypeStruct((T, PW), BF16),
        compiler_params=_params(("parallel",)))(pm, pm, pool_w, pool_scale.reshape(1, PW))


def pool_bwd_a(pm, pool_w, pool_scale, dp, *, u_block, dp_block=0, name="pool_bwd_a"):
    T = pm.shape[0]
    G = len(POOL_WINDOWS)
    PW = G * POOL_GW
    tt = _pick(T, (256, 128, 64, 16))

    def body(u_ref, up_ref, w_ref, s_ref, dp_ref, dd_ref, dw_ref, dsc_ref):
        i = pl.program_id(0)

        @pl.when(i == 0)
        def _():
            dw_ref[...] = jnp.zeros_like(dw_ref)
            dsc_ref[...] = jnp.zeros_like(dsc_ref)

        for g in range(G):
            sl = slice(g * POOL_GW, (g + 1) * POOL_GW)
            diff = _pool_diff(u_ref, up_ref, g, tt, i * tt, i == 0).astype(MXU_DTYPE)
            wg = w_ref[g].astype(MXU_DTYPE)
            dpv = dp_ref[:, sl]
            y = jnp.dot(diff, wg, preferred_element_type=F32)
            dsc_ref[:, sl] += jnp.sum(dpv * y, axis=0, keepdims=True)
            dy = (dpv * s_ref[:, sl]).astype(MXU_DTYPE)
            dd_ref[:, sl] = lax.dot_general(dy, wg, (((1,), (1,)), ((), ())), preferred_element_type=F32)
            dw_ref[g] += lax.dot_general(diff, dy, (((0,), (0,)), ((), ())), preferred_element_type=F32)

    wb = pl.BlockSpec((G, POOL_GW, POOL_GW), lambda i: (0, 0, 0))
    sb = pl.BlockSpec((1, PW), lambda i: (0, 0))
    blk = pl.BlockSpec((tt, PW), lambda i: (i, 0))
    return pl.pallas_call(
        body, name=name, grid=(T // tt,),
        in_specs=[pl.BlockSpec((tt, PW), lambda i: (i, u_block)), pl.BlockSpec((tt, PW), lambda i: (jnp.maximum(i - 1, 0), u_block)), wb, sb,
                  pl.BlockSpec((tt, PW), lambda i: (i, dp_block))],
        out_specs=(blk, wb, sb),
        out_shape=(jax.ShapeDtypeStruct((T, PW), F32), jax.ShapeDtypeStruct((G, POOL_GW, POOL_GW), F32), jax.ShapeDtypeStruct((1, PW), F32)),
        compiler_params=_params(("arbitrary",)))(pm, pm, pool_w, pool_scale.reshape(1, PW), dp)


def pool_bwd_b(dd, name="pool_bwd_b"):
    T, PW = dd.shape
    tt = _pick(T, (256, 128, 64, 16))
    nt = T // tt

    def body(d_ref, dn_ref, du_ref):
        i = pl.program_id(0)
        s = lax.broadcasted_iota(jnp.int32, (tt, tt), 0)
        t = lax.broadcasted_iota(jnp.int32, (tt, tt), 1)
        for g, win in enumerate(POOL_WINDOWS):
            sl = slice(g * POOL_GW, (g + 1) * POOL_GW)
            inv_c = 1.0 / jnp.minimum(t + i * tt + 1, win).astype(F32)
            cur = jnp.where((t >= s) & (t < s + win), inv_c, 0.0)
            nxt = jnp.where(t + tt < s + win, 1.0 / win, 0.0)
            dg = d_ref[:, sl]
            dng = jnp.where(i == nt - 1, 0.0, dn_ref[:, sl])
            du_ref[:, sl] = _dot(cur, dg) + _dot(nxt, dng) - dg

    blk = pl.BlockSpec((tt, PW), lambda i: (i, 0))
    return pl.pallas_call(body, name=name, grid=(nt,), in_specs=[blk, pl.BlockSpec((tt, PW), lambda i: (jnp.minimum(i + 1, nt - 1), 0))],
                          out_specs=blk, out_shape=jax.ShapeDtypeStruct((T, PW), F32), compiler_params=_params(("parallel",)))(dd, dd)


def _rope_consts():
    j = jnp.arange(QK_HEAD)
    inv_freq = ROPE_THETA ** (-jnp.arange(0, ROPE, 2, dtype=F32) / ROPE)
    freq = jnp.where(j >= NOPE, inv_freq[(j - NOPE) % (ROPE // 2)], 0.0).astype(F32)
    half = ROPE // 2
    src = jnp.arange(QK_HEAD)[:, None]
    dst = jnp.arange(QK_HEAD)[None, :]
    first = (dst >= NOPE) & (dst < NOPE + half)
    second = dst >= NOPE + half
    p = jnp.where(first & (src == dst + half), -1.0, 0.0) + jnp.where(second & (src == dst - half), 1.0, 0.0)
    return freq.reshape(1, QK_HEAD), p.astype(F32)


def rope_tables(pos_col, freq, name="rope_tables"):
    T = pos_col.shape[0]
    tt = _pick(T, (512, 256, 128, 64, 8))

    def body(p_ref, f_ref, c_ref, s_ref):
        ang = p_ref[...] * f_ref[...]
        rot = lax.broadcasted_iota(jnp.int32, ang.shape, 1) >= NOPE
        c_ref[...] = jnp.where(rot, jnp.cos(ang), 1.0)
        s_ref[...] = jnp.where(rot, jnp.sin(ang), 0.0)

    blk = pl.BlockSpec((tt, QK_HEAD), lambda i: (i, 0))
    sh = jax.ShapeDtypeStruct((T, QK_HEAD), F32)
    return pl.pallas_call(body, name=name, grid=(T // tt,), in_specs=[pl.BlockSpec((tt, 1), lambda i: (i, 0)), pl.BlockSpec((1, QK_HEAD), lambda i: (0, 0))],
                          out_specs=(blk, blk), out_shape=(sh, sh), compiler_params=_params(("parallel",)))(pos_col, freq)


def _permute(x, p):
    hi = x.astype(BF16)
    lo = (x - hi.astype(F32)).astype(BF16)
    pb = p.astype(BF16)
    return jnp.dot(hi, pb, preferred_element_type=F32) + jnp.dot(lo, pb, preferred_element_type=F32)


def _seg_stats(t):
    lane = lax.broadcasted_iota(jnp.int32, t.shape, 1)
    nope = lane < NOPE
    sq = t * t
    r = jnp.where(nope, lax.rsqrt(jnp.sum(jnp.where(nope, sq, 0.0), axis=-1, keepdims=True) / NOPE + EPS),
                  lax.rsqrt(jnp.sum(jnp.where(nope, 0.0, sq), axis=-1, keepdims=True) / ROPE + EPS))
    return nope, r


def headnorm_rope_fwd(t, gain, cos, sin, pmat, out_scale=1.0, name="headnorm_rope_fwd"):
    H, T, W = t.shape
    tt = _pick(T, (2048, 1024, 512, 256, 128, 64, 8))

    def body(t_ref, g_ref, c_ref, s_ref, p_ref, o_ref):
        tv = t_ref[0]
        _, r = _seg_stats(tv)
        y = tv * r * g_ref[...]
        o_ref[0] = ((y * c_ref[...] + _permute(y, p_ref[...]) * s_ref[...]) * out_scale).astype(o_ref.dtype)

    blk = pl.BlockSpec((1, tt, W), lambda h, i: (h, i, 0))
    tab = pl.BlockSpec((tt, W), lambda h, i: (i, 0))
    return pl.pallas_call(body, name=name, grid=(H, T // tt),
                          in_specs=[blk, pl.BlockSpec((1, W), lambda h, i: (0, 0)), tab, tab, pl.BlockSpec((W, W), lambda h, i: (0, 0))],
                          out_specs=blk, out_shape=jax.ShapeDtypeStruct((H, T, W), BF16),
                          compiler_params=_params(("parallel", "parallel")))(t, gain.reshape(1, W), cos, sin, pmat)


def headnorm_rope_bwd(t, gain, cos, sin, pmat_t, dout, out_scale=1.0, name="headnorm_rope_bwd"):
    H, T, W = t.shape
    tt = _pick(T, (2048, 1024, 512, 256, 128, 64, 8))

    def body(t_ref, g_ref, c_ref, s_ref, p_ref, do_ref, dt_ref, dg_ref):
        @pl.when((pl.program_id(0) == 0) & (pl.program_id(1) == 0))
        def _():
            dg_ref[...] = jnp.zeros_like(dg_ref)

        tv = t_ref[0]
        dov = do_ref[0] * out_scale
        nope, r = _seg_stats(tv)
        dy = dov * c_ref[...] + _permute(dov * s_ref[...], p_ref[...])
        xhat = tv * r
        dg_ref[...] += jnp.sum(dy * xhat, axis=0, keepdims=True)
        dxhat = dy * g_ref[...]
        pr = dxhat * xhat
        mean = jnp.where(nope, jnp.sum(jnp.where(nope, pr, 0.0), axis=-1, keepdims=True) / NOPE,
                         jnp.sum(jnp.where(nope, 0.0, pr), axis=-1, keepdims=True) / ROPE)
        dt_ref[0] = r * (dxhat - xhat * mean)

    blk = pl.BlockSpec((1, tt, W), lambda h, i: (h, i, 0))
    tab = pl.BlockSpec((tt, W), lambda h, i: (i, 0))
    gb = pl.BlockSpec((1, W), lambda h, i: (0, 0))
    return pl.pallas_call(body, name=name, grid=(H, T // tt),
                          in_specs=[blk, gb, tab, tab, pl.BlockSpec((W, W), lambda h, i: (0, 0)), blk],
                          out_specs=(blk, gb), out_shape=(jax.ShapeDtypeStruct((H, T, W), F32), jax.ShapeDtypeStruct((1, W), F32)),
                          compiler_params=_params(("arbitrary", "arbitrary")))(t, gain.reshape(1, W), cos, sin, pmat_t, dout)


def sum_heads(x, name="sum_heads"):
    H, T, W = x.shape
    tt = _pick(T, (2048, 1024, 512, 256, 128, 64, 8))

    def body(x_ref, o_ref):
        @pl.when(pl.program_id(1) == 0)
        def _():
            o_ref[...] = jnp.zeros_like(o_ref)

        o_ref[...] += x_ref[0]

    return pl.pallas_call(body, name=name, grid=(T // tt, H), in_specs=[pl.BlockSpec((1, tt, W), lambda i, h: (h, i, 0))],
                          out_specs=pl.BlockSpec((tt, W), lambda i, h: (i, 0)), out_shape=jax.ShapeDtypeStruct((T, W), F32),
                          compiler_params=_params(("parallel", "arbitrary")))(x)


_NT = (((1,), (1,)), ((), ()))
_TN = (((0,), (0,)), ((), ()))
ATTN_SUB = 256


def _attn_fwd_tiles(T):
    tq = _pick(T, (1024, 512, 256, 128, 64))
    tk = _pick(T, (2048, 1024, 512, 256, 128, 64))
    return tq, tk, min(ATTN_SUB, tq, tk)


def _attn_bwd_tiles(T):
    tq = _pick(T, (2048, 1024, 512, 256, 128, 64))
    tk = _pick(T, (512, 256, 128, 64))
    return tq, tk, min(ATTN_SUB, tq, tk)


def attn_fwd(q, k, v, name="attn_fwd", tiles=None):
    H, T, DQ = q.shape
    DV = v.shape[2]
    tq, tk, sub = tiles or _attn_fwd_tiles(T)
    nq, nk, nsub = T // tq, T // tk, tk // sub

    def body(q_ref, k_ref, v_ref, o_ref, l_ref, m_s, l_s, acc_s):
        i, j = pl.program_id(1), pl.program_id(2)

        @pl.when(j == 0)
        def _():
            m_s[...] = jnp.full_like(m_s, -1e30)
            l_s[...] = jnp.zeros_like(l_s)
            acc_s[...] = jnp.zeros_like(acc_s)

        def tile(rel):
            qv = q_ref[0]
            m_old = m_s[...]
            m_new = m_old
            ss = {}
            for c in range(nsub):
                if rel is not None and c * sub > rel + tq - 1:
                    continue
                s = lax.dot_general(qv, k_ref[0, c * sub:(c + 1) * sub, :], _NT, preferred_element_type=F32)
                if rel is not None and (c + 1) * sub - 1 > rel:
                    row = lax.broadcasted_iota(jnp.int32, s.shape, 0) + rel
                    col = lax.broadcasted_iota(jnp.int32, s.shape, 1) + c * sub
                    s = jnp.where(row >= col, s, -1e30)
                ss[c] = s
                m_new = jnp.maximum(m_new, jnp.max(s, axis=-1, keepdims=True))
            alpha = jnp.exp(m_old - m_new)
            l_new = alpha * l_s[...]
            acc = alpha * acc_s[...]
            for c, s in ss.items():
                p = jnp.exp(s - m_new)
                l_new = l_new + jnp.sum(p, axis=-1, keepdims=True)
                acc = acc + jnp.dot(p.astype(MXU_DTYPE), v_ref[0, c * sub:(c + 1) * sub, :], preferred_element_type=F32)
            l_s[...] = l_new
            acc_s[...] = acc
            m_s[...] = m_new

        rel = i * tq - j * tk
        pl.when(rel >= tk - 1)(lambda: tile(None))
        for r0 in range(0, tk - 1, tq):
            pl.when(rel == r0)(lambda r0=r0: tile(r0))

        @pl.when(j == nk - 1)
        def _():
            o_ref[0] = acc_s[...] / l_s[...]
            l_ref[0] = jnp.broadcast_to(m_s[...] + jnp.log(l_s[...]), (tq, DV))

    last = lambda i: (i * tq + (tq - 1)) // tk
    qb = pl.BlockSpec((1, tq, DQ), lambda h, i, j: (h, i, 0))
    kb = pl.BlockSpec((1, tk, DQ), lambda h, i, j: (h, jnp.minimum(j, last(i)), 0))
    vb = pl.BlockSpec((1, tk, DV), lambda h, i, j: (h, jnp.minimum(j, last(i)), 0))
    ob = pl.BlockSpec((1, tq, DV), lambda h, i, j: (h, i, 0))
    sh = jax.ShapeDtypeStruct((H, T, DV), F32)
    return pl.pallas_call(body, name=name, grid=(H, nq, nk), in_specs=[qb, kb, vb], out_specs=(ob, ob), out_shape=(sh, sh),
                          scratch_shapes=[pltpu.VMEM((tq, 1), F32), pltpu.VMEM((tq, 1), F32), pltpu.VMEM((tq, DV), F32)],
                          compiler_params=_params(("parallel", "parallel", "arbitrary")))(q, k, v)


def attn_bwd(q, k, v, o, lse, do, name="attn_bwd", tiles=None):
    H, T, DQ = q.shape
    DV = v.shape[2]
    tq, tk, sub = tiles or _attn_bwd_tiles(T)
    nq, nk, nsub = T // tq, T // tk, tq // sub

    def body(q_ref, k_ref, v_ref, o_ref, l_ref, do_ref, dq_ref, dk_ref, dv_ref, dk_s, dv_s):
        j, i = pl.program_id(1), pl.program_id(2)

        @pl.when((j == 0) & (i == 0))
        def _():
            dq_ref[...] = jnp.zeros_like(dq_ref)

        @pl.when(i == 0)
        def _():
            dk_s[...] = jnp.zeros_like(dk_s)
            dv_s[...] = jnp.zeros_like(dv_s)

        def tile(rel):
            kv, vv = k_ref[0], v_ref[0]
            live = [r for r in range(nsub) if rel is None or (r + 1) * sub - 1 >= rel]
            qs, dos, ss, dps = {}, {}, {}, {}
            for r in live:
                rs = slice(r * sub, (r + 1) * sub)
                qs[r] = q_ref[0, rs, :]
                dos[r] = do_ref[0, rs, :]
                ss[r] = lax.dot_general(qs[r], kv, _NT, preferred_element_type=F32)
                dps[r] = lax.dot_general(dos[r].astype(MXU_DTYPE), vv, _NT, preferred_element_type=F32)
            dk_acc = dk_s[...]
            dv_acc = dv_s[...]
            for r in live:
                rs = slice(r * sub, (r + 1) * sub)
                p = jnp.exp(ss[r] - l_ref[0, rs, 0:1])
                if rel is not None and r * sub < rel + tk - 1:
                    row = lax.broadcasted_iota(jnp.int32, p.shape, 0) + r * sub
                    col = lax.broadcasted_iota(jnp.int32, p.shape, 1) + rel
                    p = jnp.where(row >= col, p, 0.0)
                delta = jnp.sum(dos[r] * o_ref[0, rs, :], axis=-1, keepdims=True)
                ds = (p * (dps[r] - delta)).astype(MXU_DTYPE)
                dv_acc = dv_acc + lax.dot_general(p.astype(MXU_DTYPE), dos[r].astype(MXU_DTYPE), _TN, preferred_element_type=F32)
                dk_acc = dk_acc + lax.dot_general(ds, qs[r], _TN, preferred_element_type=F32)
                rows = pl.ds(pl.multiple_of(i * tq + r * sub, sub), sub)
                dq_ref[0, rows, :] += jnp.dot(ds, kv, preferred_element_type=F32)
            dk_s[...] = dk_acc
            dv_s[...] = dv_acc

        rel = j * tk - i * tq
        pl.when(rel <= 1 - tk)(lambda: tile(None))
        for r0 in range(0, tq, tk):
            pl.when(rel == r0)(lambda r0=r0: tile(r0))

        @pl.when(i == nq - 1)
        def _():
            dk_ref[0] = dk_s[...]
            dv_ref[0] = dv_s[...]

    first = lambda j: (j * tk) // tq
    qi = lambda h, j, i: (h, jnp.maximum(i, first(j)), 0)
    qb = pl.BlockSpec((1, tq, DQ), qi)
    ob = pl.BlockSpec((1, tq, DV), qi)
    kb = pl.BlockSpec((1, tk, DQ), lambda h, j, i: (h, j, 0))
    vb = pl.BlockSpec((1, tk, DV), lambda h, j, i: (h, j, 0))
    dqb = pl.BlockSpec((1, T, DQ), lambda h, j, i: (h, 0, 0))
    return pl.pallas_call(
        body, name=name, grid=(H, nk, nq), in_specs=[qb, kb, vb, ob, ob, ob], out_specs=(dqb, kb, vb),
        out_shape=(jax.ShapeDtypeStruct((H, T, DQ), F32), jax.ShapeDtypeStruct((H, T, DQ), F32), jax.ShapeDtypeStruct((H, T, DV), F32)),
        scratch_shapes=[pltpu.VMEM((tk, DQ), F32), pltpu.VMEM((tk, DV), F32)],
        compiler_params=_params(("parallel", "arbitrary", "arbitrary")))(q, k, v, o, lse, do)


def _mesh_place():
    return lax.axis_index("x"), lax.axis_index("y"), lax.axis_index("c")


def _flip(v, bit):
    return 1 - v if bit else v


def _relations():
    return [((r >> 2) & 1, (r >> 1) & 1, r & 1) for r in range(1, N_DEV)]


def _exchange_semaphores():
    return [pltpu.SemaphoreType.DMA((N_DEV - 1,)), pltpu.SemaphoreType.DMA((N_DEV - 1,)), pltpu.SemaphoreType.DMA(())]


def _gather_plan(x_ref, o_ref, send_sems, recv_sems, local_sem):
    x, y, c = _mesh_place()
    me, sibling = (x, y, c), (x, y, 1 - c)
    chips = [(1 - x, y), (x, 1 - y), (1 - x, 1 - y)]

    def slot(px, py, pc):
        return o_ref.at[4 * px + 2 * py + pc]

    def copy(k, block, to, src=None):
        return pltpu.make_async_remote_copy(
            src_ref=slot(*block) if src is None else src, dst_ref=slot(*block), send_sem=send_sems.at[k], recv_sem=recv_sems.at[k],
            device_id=to, device_id_type=pl.DeviceIdType.MESH)

    mine = pltpu.make_async_copy(x_ref, slot(*me), local_sem)
    first = [copy(0, me, sibling, src=x_ref)] + [copy(1 + j, me, (*chip, c), src=x_ref) for j, chip in enumerate(chips)]
    passed = [copy(4 + j, (*chip, c), sibling) for j, chip in enumerate(chips)]

    def start():
        mine.start()
        for cp in first:
            cp.start()

    def finish():
        for j, chip in enumerate(chips):
            copy(1 + j, (*chip, c), me).wait_recv()
            passed[j].start()
        copy(0, sibling, me).wait_recv()
        for j, chip in enumerate(chips):
            copy(4 + j, (*chip, 1 - c), me).wait_recv()
        for cp in first + passed:
            cp.wait_send()
        mine.wait()

    return start, finish


def _exchange_plan(x_ref, o_ref, send_sems, recv_sems, local_sem):
    x, y, c = _mesh_place()
    me = 4 * x + 2 * y + c
    local = pltpu.make_async_copy(x_ref.at[me], o_ref.at[me], local_sem)
    remote = []
    for r, (bx, by, bc) in enumerate(_relations()):
        px, py, pc = _flip(x, bx), _flip(y, by), _flip(c, bc)
        remote.append(pltpu.make_async_remote_copy(
            src_ref=x_ref.at[4 * px + 2 * py + pc], dst_ref=o_ref.at[me], send_sem=send_sems.at[r], recv_sem=recv_sems.at[r],
            device_id=(px, py, pc), device_id_type=pl.DeviceIdType.MESH))

    def start():
        local.start()
        for cp in remote:
            cp.start()

    def finish():
        for cp in remote:
            cp.wait_recv()
        for cp in remote:
            cp.wait_send()
        local.wait()

    return start, finish


class _Rider:
    def __init__(self, operand, result, plan):
        self.operand, self.result, self.plan = operand, result, plan


def _rider(send, gather):
    assert send is None or gather is None
    if send is not None:
        return _Rider(send, jax.ShapeDtypeStruct(send.shape, send.dtype), _exchange_plan)
    if gather is not None:
        return _Rider(gather, jax.ShapeDtypeStruct((N_DEV, *gather.shape), gather.dtype), _gather_plan)
    return None


def _transfer(rider, name):
    def body(x_ref, o_ref, send_sems, recv_sems, local_sem):
        start, finish = rider.plan(x_ref, o_ref, send_sems, recv_sems, local_sem)
        start()
        finish()

    return pl.pallas_call(body, name=name, out_shape=rider.result, in_specs=[pl.BlockSpec(memory_space=pl.ANY)],
                          out_specs=pl.BlockSpec(memory_space=pl.ANY), scratch_shapes=_exchange_semaphores())(rider.operand)


def all_gather(xs, name="all_gather"):
    return _transfer(_rider(None, xs), name)


def exchange(xs, name="exchange"):
    return _transfer(_rider(xs, None), name)


def adamw(recv, w, m, v, layer=None, name="adamw"):
    _, R, L = recv.shape
    tr = _pick(R, [c for c in (1024, 512, 256, 128, 64, 32, 16) if c * L <= ADAMW_BLOCK_ELEMS] + [8])

    def body(r_ref, w_ref, m_ref, v_ref, g_ref, d_ref, nm_ref, nv_ref):
        g = r_ref[0].astype(F32)
        for s in range(1, N_DEV):
            g = g + r_ref[s].astype(F32)
        m_new = ADAM_B1 * m_ref[...] + (1.0 - ADAM_B1) * g
        v_new = ADAM_B2 * v_ref[...] + (1.0 - ADAM_B2) * jnp.square(g)
        m_hat = m_new / (1.0 - ADAM_B1 ** ADAM_STEP)
        v_hat = v_new / (1.0 - ADAM_B2 ** ADAM_STEP)
        g_ref[...] = g
        d_ref[...] = -ADAM_LR * (m_hat / (jnp.sqrt(v_hat) + ADAM_EPS) + ADAM_WD * w_ref[...])
        nm_ref[...] = m_new
        nv_ref[...] = v_new

    row = pl.BlockSpec((tr, L), lambda i: (i, 0))
    state = row if layer is None else pl.BlockSpec((None, tr, L), lambda i: (layer, i, 0))
    sh = jax.ShapeDtypeStruct((R, L), F32)
    return pl.pallas_call(body, name=name, grid=(R // tr,), in_specs=[pl.BlockSpec((N_DEV, tr, L), lambda i: (0, i, 0)), state, state, state],
                          out_specs=(row, row, row, row), out_shape=(sh, sh, sh, sh), compiler_params=_params(("parallel",)))(recv, w, m, v)


GRAD_WIRE_DTYPE = BF16


class Prefetch:
    def __init__(self, local, order):
        self.local, self.todo, self.blocks, self.full = local, list(order), {}, {}
        self.where = {(n, i): ((names, i), k) for names, i in order for k, n in enumerate(names)}

    def _block(self, key):
        names, i = key
        return jnp.stack([self.local[n][i] for n in names]).astype(MXU_DTYPE)

    def _next(self):
        return self.todo.pop(0) if self.todo else None

    def mm(self, a, b, **kw):
        key = self._next()
        if key is None:
            return mm(a, b, **kw)
        out, self.blocks[key] = mm(a, b, gather=self._block(key), **kw)
        return out

    def ffn_hidden_fwd(self, h, wg, wu):
        key = self._next()
        if key is None:
            return ffn_hidden_fwd(h, wg, wu)
        g, u, a, self.blocks[key] = ffn_hidden_fwd(h, wg, wu, gather=self._block(key))
        return g, u, a

    def get(self, n, i):
        key, k = self.where[(n, i)]
        while key not in self.blocks:
            nxt = self.todo.pop(0)
            self.blocks[nxt] = all_gather(self._block(nxt), name="gather_" + nxt[0][0])
        if (n, i) not in self.full:
            self.full[(n, i)] = _from_shards(self.blocks[key][:, k], SHARD_AXIS[n] - 1)
        return self.full[(n, i)]


class Params:
    def __init__(self, **thunks):
        self.thunks, self.values = thunks, {}

    def __getitem__(self, k):
        if k not in self.values:
            self.values[k] = self.thunks[k]()
        return self.values[k]


class Wire:
    def __init__(self):
        self.waiting = []

    def post(self, slabs, deliver):
        self.waiting.append((slabs, deliver))

    def mm(self, a, b, **kw):
        if not self.waiting:
            return mm(a, b, **kw)
        slabs, deliver = self.waiting.pop(0)
        out, received = mm(a, b, send=slabs, **kw)
        deliver(received)
        return out

    def flush(self):
        for slabs, deliver in self.waiting:
            deliver(exchange(slabs, name="exchange_rest"))
        self.waiting = []


def _row_slabs(g):
    return g.reshape(N_DEV, g.shape[0] // N_DEV, g.shape[1])


def _col_slabs(g):
    return g.reshape(g.shape[0], N_DEV, g.shape[1] // N_DEV).transpose(1, 0, 2)


def ffn_fwd(x, p, pre):
    h = rms_fwd(x, p["norm"], name="ffn_rms_fwd")
    g, u, a = pre.ffn_hidden_fwd(h, p["w_gate"], p["w_up"])
    y = pre.mm(a, p["w_down"], add=x, scale=0.5, name="ffn_down")
    return y, (x, h, g, u, a)


def ffn_bwd(dy, saved, p, wire, post):
    x, h, g, u, a = saved
    gain, wg, wu, wd = p["norm"], p["w_gate"], p["w_up"], p["w_down"]
    dg, du = ffn_hidden_bwd(dy, wd, g, u, 0.5)
    post("w_down", _row_slabs(wire.mm(a, dy, ta=True, scale=0.5, out_dtype=GRAD_WIRE_DTYPE, name="ffn_dwd")))
    dh = wire.mm(dg, wg, tb=True, name="ffn_dh_gate")
    dh = wire.mm(du, wu, tb=True, add=dh, name="ffn_dh_up")
    post("w_gate", wire.mm(h, dg, ta=True, out_dtype=GRAD_WIRE_DTYPE, col_slabs=N_DEV, name="ffn_dwg"))
    post("w_up", wire.mm(h, du, ta=True, out_dtype=GRAD_WIRE_DTYPE, col_slabs=N_DEV, name="ffn_dwu"))
    dx, dgain = rms_bwd(x, gain, dh, dy, name="ffn_rms_bwd")
    post("norm", dgain[0])
    return dx


HYB_QKVZ = 4 * GDN_HEADS * GDN_D
HYB_AB = 2 * GDN_HEADS
HYB_U = len(POOL_WINDOWS) * POOL_GW


def _hyb_split_w_in(w_in):
    main = jnp.concatenate([w_in[:, :HYB_QKVZ], w_in[:, HYB_QKVZ + HYB_AB:]], axis=1)
    ab = jnp.pad(w_in[:, HYB_QKVZ:HYB_QKVZ + HYB_AB], ((0, 0), (0, LANES - HYB_AB)))
    return main, ab


def _gate_rows(t, n):
    return t.reshape(n, CHUNK, GDN_HEADS).transpose(0, 2, 1).reshape(n * GDN_HEADS, CHUNK)


def _gate_cols(r, n):
    return r.reshape(n, GDN_HEADS, CHUNK).transpose(0, 2, 1).reshape(n * CHUNK, GDN_HEADS)


def hyb_fwd(x, p, pre):
    T = x.shape[0]
    n = T // CHUNK
    h = rms_fwd(x, p["mix_norm"], name="mix_rms_fwd")
    w_main, w_ab = _hyb_split_w_in(p["w_in"])
    pm = pre.mm(h, w_main, name="hyb_in_main")
    pab = pre.mm(h, w_ab, name="hyb_in_gates")
    q, k, v = gdn_prep_fwd(pm, p["conv"])
    a_r = _gate_rows(pab[:, :GDN_HEADS], n)
    b_r = _gate_rows(pab[:, GDN_HEADS:HYB_AB], n)
    alog_c = jnp.tile(p["a_log"], n).reshape(n * GDN_HEADS, 1)
    dt_c = jnp.tile(p["dt_bias"], n).reshape(n * GDN_HEADS, 1)
    gc, beta = gdn_gates_fwd(a_r, b_r, alog_c, dt_c)
    gc3 = gc.reshape(n, GDN_HEADS, CHUNK)
    beta3 = beta.reshape(n, GDN_HEADS, CHUNK)
    o, states = gdr_fwd(q, k, v, gc3, beta3)
    og = gdn_gate_fwd(o, pm, p["out_norm"], z_block=3)
    pool = pool_fwd(pm, p["pool_w"], p["pool_scale"], u_block=4)
    mix = jnp.concatenate([og, pool], axis=1)
    y = pre.mm(mix, p["w_out"], add=x, name="hyb_out")
    return y, dict(x=x, h=h, pm=pm, q=q, k=k, v=v, a_r=a_r, b_r=b_r, alog_c=alog_c, dt_c=dt_c, gc3=gc3, beta3=beta3,
                   o=o, states=states, mix=mix)


def hyb_bwd(dy, s, p, wire):
    T = dy.shape[0]
    n = T // CHUNK
    half = GDN_HEADS * GDN_D
    w_main, w_ab = _hyb_split_w_in(p["w_in"])
    dmix = wire.mm(dy, p["w_out"], tb=True, name="hyb_dmix")
    dw_out = _row_slabs(wire.mm(s["mix"], dy, ta=True, out_dtype=GRAD_WIRE_DTYPE, name="hyb_dwout"))
    do, dz, dout_norm = gdn_gate_bwd(s["o"], s["pm"], p["out_norm"], dmix, z_block=3, dy_block=0)
    dq, dk, dv, dgc, dbeta = gdr_bwd(s["q"], s["k"], s["v"], s["gc3"], s["beta3"], s["states"], do)
    da_r, db_r, dalog, ddt = gdn_gates_bwd(s["a_r"], s["b_r"], s["alog_c"], s["dt_c"],
                                           dgc.reshape(n * GDN_HEADS, CHUNK), dbeta.reshape(n * GDN_HEADS, CHUNK))
    dpab = jnp.pad(jnp.concatenate([_gate_cols(da_r, n), _gate_cols(db_r, n)], axis=1), ((0, 0), (0, LANES - HYB_AB)))
    dyc, dconv = gdn_prep_bwd(s["pm"], p["conv"], dq, dk, dv)
    dqkv = conv_dx(dyc, p["conv"])
    dd, dpool_w, dpool_scale = pool_bwd_a(s["pm"], p["pool_w"], p["pool_scale"], dmix, u_block=4, dp_block=1)
    du = pool_bwd_b(dd)
    dpm = jnp.concatenate([dqkv, dz, du], axis=1)
    dh = wire.mm(dpm, w_main, tb=True, name="hyb_dh_main")
    dh = mm(dpab, w_ab, tb=True, add=dh, name="hyb_dh_gates")
    dw_main = wire.mm(s["h"], dpm, ta=True, name="hyb_dwin_main")
    dw_ab = mm(s["h"], dpab, ta=True, name="hyb_dwin_gates")
    dw_in = _col_slabs(jnp.concatenate([dw_main[:, :HYB_QKVZ], dw_ab[:, :HYB_AB], dw_main[:, HYB_QKVZ:]], axis=1)).astype(GRAD_WIRE_DTYPE)
    dx, dmix = rms_bwd(s["x"], p["mix_norm"], dh, dy, name="mix_rms_bwd")
    grads = dict(mix_norm=dmix[0], w_in=dw_in, conv=dconv, a_log=dalog[:, 0], dt_bias=ddt[:, 0], out_norm=dout_norm[0],
                 pool_w=dpool_w, pool_scale=dpool_scale[0], w_out=dw_out)
    return dx, grads


MLA_LAT = 2 * Q_LORA
ATTN_SCALE = QK_HEAD ** -0.5


def _mla_split_w_in(w_in):
    return w_in[:, :MLA_LAT], jnp.pad(w_in[:, MLA_LAT:], ((0, 0), (0, LANES - ROPE)))


def _to_heads(t, width):
    T = t.shape[0]
    return t.reshape(T, MLA_HEADS, width).transpose(1, 0, 2)


def _from_heads(t):
    H, T, W = t.shape
    return t.transpose(1, 0, 2).reshape(T, H * W)


def mla_fwd(x, p, rope, pre):
    T = x.shape[0]
    cos, sin, pmat = rope
    h = rms_fwd(x, p["mix_norm"], name="mix_rms_fwd")
    w_main, w_pe = _mla_split_w_in(p["w_in"])
    pm = pre.mm(h, w_main, name="mla_in_main")
    ppe = pre.mm(h, w_pe, name="mla_in_pe")
    qn = rms_fwd(pm, p["q_norm"], width=Q_LORA, col_block=0, name="mla_lat_rms_fwd")
    kvn = rms_fwd(pm, p["kv_norm"], width=Q_LORA, col_block=1, name="mla_lat_rms_fwd")
    q3 = _to_heads(pre.mm(qn, p["w_q_up"], name="mla_q_up"), QK_HEAD)
    kv3 = pre.mm(kvn, p["w_kv_up"], col_slabs=MLA_HEADS, name="mla_kv_up")
    kpe = jnp.broadcast_to(ppe[None, :, :ROPE], (MLA_HEADS, T, ROPE))
    k3 = jnp.concatenate([kv3[..., :NOPE], kpe], axis=-1)
    v3 = kv3[..., NOPE:].astype(MXU_DTYPE)
    qr = headnorm_rope_fwd(q3, p["q_head_norm"], cos, sin, pmat, out_scale=ATTN_SCALE)
    kr = headnorm_rope_fwd(k3, p["k_head_norm"], cos, sin, pmat)
    o3, lse = attn_fwd(qr, kr, v3)
    o = _from_heads(o3).astype(MXU_DTYPE)
    y = pre.mm(o, p["w_out"], add=x, name="mla_out")
    return y, dict(x=x, h=h, pm=pm, qn=qn, kvn=kvn, q3=q3, k3=k3, v3=v3, qr=qr, kr=kr, o3=o3, lse=lse, o=o)


def mla_bwd(dy, s, p, rope, wire):
    cos, sin, pmat = rope
    w_main, w_pe = _mla_split_w_in(p["w_in"])
    do3 = _to_heads(wire.mm(dy, p["w_out"], tb=True, name="mla_do"), V_HEAD)
    dw_out = _row_slabs(wire.mm(s["o"], dy, ta=True, out_dtype=GRAD_WIRE_DTYPE, name="mla_dwout"))
    dqr, dkr, dv3 = attn_bwd(s["qr"], s["kr"], s["v3"], s["o3"], s["lse"], do3)
    dq3, dqhn = headnorm_rope_bwd(s["q3"], p["q_head_norm"], cos, sin, pmat.T, dqr, out_scale=ATTN_SCALE)
    dk3, dkhn = headnorm_rope_bwd(s["k3"], p["k_head_norm"], cos, sin, pmat.T, dkr)
    dppe = jnp.pad(sum_heads(dk3)[:, NOPE:], ((0, 0), (0, LANES - ROPE)))
    dq = _from_heads(dq3)
    dkv = _from_heads(jnp.concatenate([dk3[..., :NOPE], dv3], axis=-1))
    dqn = mm(dq, p["w_q_up"], tb=True, name="mla_dqn")
    dkvn = wire.mm(dkv, p["w_kv_up"], tb=True, name="mla_dkvn")
    dw_q_up = mm(s["qn"], dq, ta=True, out_dtype=GRAD_WIRE_DTYPE, col_slabs=N_DEV, name="mla_dwq_up")
    dw_kv_up = wire.mm(s["kvn"], dkv, ta=True, out_dtype=GRAD_WIRE_DTYPE, col_slabs=N_DEV, name="mla_dwkv_up")
    dqlat, dq_norm = rms_bwd(s["pm"], p["q_norm"], dqn, width=Q_LORA, col_block=0, name="mla_lat_rms_bwd")
    dkvlat, dkv_norm = rms_bwd(s["pm"], p["kv_norm"], dkvn, width=Q_LORA, col_block=1, name="mla_lat_rms_bwd")
    dpm = jnp.concatenate([dqlat, dkvlat], axis=1)
    dh = mm(dpm, w_main, tb=True, name="mla_dh_main")
    dh = mm(dppe, w_pe, tb=True, add=dh, name="mla_dh_pe")
    dw_in = _row_slabs(jnp.concatenate([mm(s["h"], dpm, ta=True, name="mla_dwin_main"),
                                        mm(s["h"], dppe, ta=True, name="mla_dwin_pe")[:, :ROPE]], axis=1)).astype(GRAD_WIRE_DTYPE)
    dx, dmix = rms_bwd(s["x"], p["mix_norm"], dh, dy, name="mix_rms_bwd")
    grads = dict(mix_norm=dmix[0], w_in=dw_in, q_norm=dq_norm[0], kv_norm=dkv_norm[0], w_q_up=dw_q_up, w_kv_up=dw_kv_up,
                 q_head_norm=dqhn[0], k_head_norm=dkhn[0], w_out=dw_out)
    return dx, grads


WEIGHTS = ['ffn1_norm', 'ffn1_w_gate', 'ffn1_w_up', 'ffn1_w_down', 'mix_norm', 'ffn2_norm', 'ffn2_w_gate', 'ffn2_w_up',
           'ffn2_w_down', 'hyb_w_in', 'gdn_conv', 'gdn_a_log', 'gdn_dt_bias', 'gdn_out_norm', 'pool_w', 'pool_scale',
           'hyb_w_out', 'mla_w_in', 'mla_q_norm', 'mla_kv_norm', 'mla_w_q_up', 'mla_w_kv_up', 'mla_q_head_norm',
           'mla_k_head_norm', 'mla_w_out']
SHARD_AXIS = dict(ffn1_norm=None, ffn1_w_gate=2, ffn1_w_up=2, ffn1_w_down=1, mix_norm=None, ffn2_norm=None, ffn2_w_gate=2,
                  ffn2_w_up=2, ffn2_w_down=1, hyb_w_in=2, gdn_conv=2, gdn_a_log=None, gdn_dt_bias=None, gdn_out_norm=None,
                  pool_w=2, pool_scale=None, hyb_w_out=1, mla_w_in=1, mla_q_norm=1, mla_kv_norm=1, mla_w_q_up=2,
                  mla_w_kv_up=2, mla_q_head_norm=None, mla_k_head_norm=None, mla_w_out=1)
GATHER_BF16 = ['ffn1_w_gate', 'ffn1_w_up', 'ffn1_w_down', 'ffn2_w_gate', 'ffn2_w_up', 'ffn2_w_down', 'hyb_w_in', 'pool_w',
               'hyb_w_out', 'mla_w_in', 'mla_w_q_up', 'mla_w_kv_up', 'mla_w_out']
GATHER_F32 = ['gdn_conv', 'mla_q_norm', 'mla_kv_norm']
LARGE = ['ffn1_w_gate', 'ffn1_w_up', 'ffn1_w_down', 'ffn2_w_gate', 'ffn2_w_up', 'ffn2_w_down', 'hyb_w_in', 'hyb_w_out',
         'mla_w_in', 'mla_w_q_up', 'mla_w_kv_up', 'mla_w_out']
SMALL = [n for n in WEIGHTS if n not in LARGE]
SUBLANES = 8


def _pack(flat_list, lead=()):
    flat = jnp.concatenate(flat_list, axis=-1)
    rows = -(-flat.shape[-1] // (LANES * SUBLANES)) * SUBLANES
    flat = jnp.pad(flat, [(0, 0)] * len(lead) + [(0, rows * LANES - flat.shape[-1])])
    return flat.reshape(*lead, rows, LANES)


def _unpack(packed, shapes, lead=()):
    flat = packed.reshape(*lead, -1)
    out, off = [], 0
    for sh in shapes:
        n = math.prod(sh)
        out.append(flat[..., off:off + n].reshape(*lead, *sh))
        off += n
    return out


def _to_slabs(g, axis):
    if axis is None:
        return jnp.broadcast_to(g.reshape(1, -1), (N_DEV, g.size))
    sh = g.shape
    g = g.reshape(*sh[:axis], N_DEV, sh[axis] // N_DEV, *sh[axis + 1:])
    return jnp.moveaxis(g, axis, 0).reshape(N_DEV, -1)


def _from_shards(t, axis):
    t = jnp.moveaxis(t, 0, axis)
    sh = t.shape
    return t.reshape(*sh[:axis], sh[axis] * sh[axis + 1], *sh[axis + 2:])


def _gather_packed(local, names, dtype, name):
    got = all_gather(_pack([local[n].astype(dtype).reshape(-1) for n in names]), name=name)
    parts = _unpack(got, [local[n].shape for n in names], lead=(N_DEV,))
    return {n: _from_shards(t, SHARD_AXIS[n]) for n, t in zip(names, parts)}


def kernel(x, positions, ffn1_norm, ffn1_w_gate, ffn1_w_up, ffn1_w_down, mix_norm, ffn2_norm, ffn2_w_gate, ffn2_w_up, ffn2_w_down, hyb_w_in, gdn_conv, gdn_a_log, gdn_dt_bias, gdn_out_norm, pool_w, pool_scale, hyb_w_out, mla_w_in, mla_q_norm, mla_kv_norm, mla_w_q_up, mla_w_kv_up, mla_q_head_norm, mla_k_head_norm, mla_w_out, loss_target, m_ffn1_norm, m_ffn1_w_gate, m_ffn1_w_up, m_ffn1_w_down, m_mix_norm, m_ffn2_norm, m_ffn2_w_gate, m_ffn2_w_up, m_ffn2_w_down, m_hyb_w_in, m_gdn_conv, m_gdn_a_log, m_gdn_dt_bias, m_gdn_out_norm, m_pool_w, m_pool_scale, m_hyb_w_out, m_mla_w_in, m_mla_q_norm, m_mla_kv_norm, m_mla_w_q_up, m_mla_w_kv_up, m_mla_q_head_norm, m_mla_k_head_norm, m_mla_w_out, v_ffn1_norm, v_ffn1_w_gate, v_ffn1_w_up, v_ffn1_w_down, v_mix_norm, v_ffn2_norm, v_ffn2_w_gate, v_ffn2_w_up, v_ffn2_w_down, v_hyb_w_in, v_gdn_conv, v_gdn_a_log, v_gdn_dt_bias, v_gdn_out_norm, v_pool_w, v_pool_scale, v_hyb_w_out, v_mla_w_in, v_mla_q_norm, v_mla_kv_norm, v_mla_w_q_up, v_mla_w_kv_up, v_mla_q_head_norm, v_mla_k_head_norm, v_mla_w_out):
    given = dict(locals())
    local = {n: given[n] for n in WEIGHTS}
    depth = ffn1_norm.shape[0]
    xs = x[0]
    T = xs.shape[0]

    full = dict(local)
    full.update(_gather_packed(local, GATHER_F32, F32, "gather_f32"))

    order = []
    for layer in range(depth):
        i = layer // 2
        order += [(("ffn1_w_gate", "ffn1_w_up"), layer), (("ffn1_w_down",), layer)]
        if layer % 2 == 0:
            order += [(("hyb_w_in",), i), (("pool_w",), i), (("hyb_w_out",), i)]
        else:
            order += [(("mla_w_in",), i), (("mla_w_q_up",), i), (("mla_w_kv_up",), i), (("mla_w_out",), i)]
        order += [(("ffn2_w_gate", "ffn2_w_up"), layer), (("ffn2_w_down",), layer)]
    assert sorted({n for names, _ in order for n in names}) == sorted(GATHER_BF16)
    pre = Prefetch(local, order)

    freq, pmat = _rope_consts()
    cos, sin = rope_tables(positions[0].astype(F32).reshape(T, 1), freq)
    rope = (cos, sin, pmat)

    def small(n, i):
        return lambda: full[n][i]

    def large(n, i):
        return lambda: pre.get(n, i)

    def mixer_params(layer):
        i = layer // 2
        if layer % 2 == 0:
            return Params(mix_norm=small("mix_norm", layer), w_in=large("hyb_w_in", i), conv=small("gdn_conv", i), a_log=small("gdn_a_log", i),
                          dt_bias=small("gdn_dt_bias", i), out_norm=small("gdn_out_norm", i), pool_w=large("pool_w", i),
                          pool_scale=small("pool_scale", i), w_out=large("hyb_w_out", i))
        return Params(mix_norm=small("mix_norm", layer), w_in=large("mla_w_in", i), q_norm=small("mla_q_norm", i), kv_norm=small("mla_kv_norm", i),
                      w_q_up=large("mla_w_q_up", i), w_kv_up=large("mla_w_kv_up", i), q_head_norm=small("mla_q_head_norm", i),
                      k_head_norm=small("mla_k_head_norm", i), w_out=large("mla_w_out", i))

    def ffn_params(which, layer):
        return Params(norm=small(which + "_norm", layer), w_gate=large(which + "_w_gate", layer), w_up=large(which + "_w_up", layer),
                      w_down=large(which + "_w_down", layer))

    saved = []
    cur = xs
    for layer in range(depth):
        cur, s1 = ffn_fwd(cur, ffn_params("ffn1", layer), pre)
        if layer % 2 == 0:
            cur, sm = hyb_fwd(cur, mixer_params(layer), pre)
        else:
            cur, sm = mla_fwd(cur, mixer_params(layer), rope, pre)
        cur, s2 = ffn_fwd(cur, ffn_params("ffn2", layer), pre)
        saved.append((s1, sm, s2))

    dcur, loss_local = loss_head(cur, loss_target[0])
    loss = lax.psum(loss_local, ("x", "y", "c"))

    hyb_names = dict(hyb_w_in="w_in", gdn_conv="conv", gdn_a_log="a_log", gdn_dt_bias="dt_bias", gdn_out_norm="out_norm",
                     pool_w="pool_w", pool_scale="pool_scale", hyb_w_out="w_out")
    mla_names = dict(mla_w_in="w_in", mla_q_norm="q_norm", mla_kv_norm="kv_norm", mla_w_q_up="w_q_up", mla_w_kv_up="w_kv_up",
                     mla_q_head_norm="q_head_norm", mla_k_head_norm="k_head_norm", mla_w_out="w_out")
    per_layer = {n: [None] * local[n].shape[0] for n in WEIGHTS}

    wire = Wire()

    def put(n, idx, g):
        if n not in LARGE:
            per_layer[n][idx] = g
            return

        def deliver(received):
            per_layer[n][idx] = adamw(received, local[n], given["m_" + n], given["v_" + n], layer=idx, name="adamw_" + n)

        wire.post(g, deliver)

    for layer in reversed(range(depth)):
        s1, sm, s2 = saved[layer]
        dcur = ffn_bwd(dcur, s2, ffn_params("ffn2", layer), wire, lambda n, g, layer=layer: put("ffn2_" + n, layer, g))
        if layer % 2 == 0:
            dcur, mg = hyb_bwd(dcur, sm, mixer_params(layer), wire)
            names = hyb_names
        else:
            dcur, mg = mla_bwd(dcur, sm, mixer_params(layer), rope, wire)
            names = mla_names
        put("mix_norm", layer, mg["mix_norm"])
        for n, key in names.items():
            put(n, layer // 2, mg[key])
        dcur = ffn_bwd(dcur, s1, ffn_params("ffn1", layer), wire, lambda n, g, layer=layer: put("ffn1_" + n, layer, g))
    wire.flush()
    grad_x = dcur[None]

    send = _pack([_to_slabs(jnp.stack(per_layer[n]), SHARD_AXIS[n]) for n in SMALL], lead=(N_DEV,))
    state = [_pack([src[n].reshape(-1) for n in SMALL]) for src in
             (local, {n: given["m_" + n] for n in SMALL}, {n: given["v_" + n] for n in SMALL})]
    small = [_unpack(o, [local[n].shape for n in SMALL]) for o in adamw(exchange(send, name="exchange_small"), *state, name="adamw_small")]

    outs = []
    for j in range(4):
        for n in WEIGHTS:
            outs.append(jnp.stack([t[j] for t in per_layer[n]]) if n in LARGE else small[j][SMALL.index(n)])
    return (loss, grad_x, *outs)
```

```python
import math

import jax
import jax.numpy as jnp
from jax import lax
from jax.experimental import pallas as pl
from jax.experimental.pallas import tpu as pltpu

F32 = jnp.float32
BF16 = jnp.bfloat16
MXU_DTYPE = jnp.bfloat16
HI = lax.Precision.HIGHEST
VMEM_LIMIT = 52 * 1024 * 1024
MM_VMEM_BUDGET = 40 * 1024 * 1024
LANES = 128
N_DEV = 8

EPS = 1e-6
GDN_HEADS = 8
GDN_D = 128
CHUNK = 64
CONV_K = 4
POOL_WINDOWS = (2, 4, 8, 16)
POOL_GW = 256
MLA_HEADS = 16
NOPE = 128
ROPE = 64
QK_HEAD = NOPE + ROPE
V_HEAD = 128
Q_LORA = 512
ROPE_THETA = 10000.0

ADAM_LR = 0.001
ADAM_B1 = 0.9
ADAM_B2 = 0.999
ADAM_EPS = 1e-08
ADAM_WD = 0.01
ADAM_STEP = 10
ADAMW_BLOCK_ELEMS = 128 * 1024


def _pick(n, cands):
    for c in cands:
        if n % c == 0:
            return c
    return n


def _params(sem=None):
    return pltpu.CompilerParams(dimension_semantics=sem, vmem_limit_bytes=VMEM_LIMIT)


def _sigmoid(x):
    return 1.0 / (1.0 + jnp.exp(-x))


def mm(a, b, *, ta=False, tb=False, add=None, scale=None, out_dtype=F32, col_slabs=None, send=None, gather=None, tiles=None, name="mm"):
    if ta:
        K, M = a.shape
    else:
        M, K = a.shape
    if tb:
        N, Kb = b.shape
    else:
        Kb, N = b.shape
    assert K == Kb, (a.shape, b.shape, ta, tb)
    tm = _pick(M, (1024, 512, 256, 128))
    tn = _pick(N if col_slabs is None else N // col_slabs, (1024, 512, 384, 256, 128))

    def vmem_bytes(tk):
        return (2 * tk * (tm * a.dtype.itemsize + tn * b.dtype.itemsize) + tm * tn * (4 + 2 * jnp.dtype(out_dtype).itemsize)
                + (2 * tm * tn * add.dtype.itemsize if add is not None else 0))

    tk = next(c for c in (2048, 1024, 512, 256, 128, K) if K % c == 0 and (c <= 128 or vmem_bytes(c) <= MM_VMEM_BUDGET))
    if tiles is not None:
        tm, tn, tk = tiles
    nk = K // tk
    a_spec = pl.BlockSpec((tk, tm), lambda i, j, k: (k, i)) if ta else pl.BlockSpec((tm, tk), lambda i, j, k: (i, k))
    b_spec = pl.BlockSpec((tn, tk), lambda i, j, k: (j, k)) if tb else pl.BlockSpec((tk, tn), lambda i, j, k: (k, j))
    if col_slabs is None:
        o_spec = pl.BlockSpec((tm, tn), lambda i, j, k: (i, j))
        o_shape = (M, N)
    else:
        assert add is None
        per = N // col_slabs // tn
        o_spec = pl.BlockSpec((None, tm, tn), lambda i, j, k: (j // per, i, j % per))
        o_shape = (col_slabs, M, N // col_slabs)
    dims = (((0 if ta else 1,), (1 if tb else 0,)), ((), ()))
    has_add = add is not None
    rider = _rider(send, gather)
    has_send = rider is not None
    grid = (M // tm, N // tn, nk)

    def body(*refs):
        refs = list(refs)
        a_ref, b_ref = refs[:2]
        c_ref = refs[2] if has_add else None
        n_in = 2 + has_add + has_send
        x_ref = refs[n_in - 1] if has_send else None
        o_ref = refs[n_in]
        r_ref = refs[n_in + 1] if has_send else None
        scratch = refs[n_in + 1 + has_send:]
        acc_ref = scratch[0] if nk > 1 else None
        i, j, k = pl.program_id(0), pl.program_id(1), pl.program_id(2)

        if has_send:
            ride_start, ride_finish = rider.plan(x_ref, r_ref, *scratch[-3:])
            pl.when((i == 0) & (j == 0) & (k == 0))(ride_start)

        prod = lax.dot_general(a_ref[...].astype(MXU_DTYPE), b_ref[...].astype(MXU_DTYPE), dims, preferred_element_type=F32)

        def finish(r):
            if scale is not None:
                r = r * scale
            if has_add:
                r = r + c_ref[...].astype(F32)
            o_ref[...] = r.astype(out_dtype)

        if nk == 1:
            finish(prod)
        else:
            @pl.when(k == 0)
            def _():
                acc_ref[...] = prod

            @pl.when(k > 0)
            def _():
                acc_ref[...] += prod

            @pl.when(k == nk - 1)
            def _():
                finish(acc_ref[...])

        if has_send:
            pl.when((i == grid[0] - 1) & (j == grid[1] - 1) & (k == nk - 1))(ride_finish)

    hbm = pl.BlockSpec(memory_space=pl.ANY)
    ins = [a, b] + ([add] if has_add else []) + ([rider.operand] if has_send else [])
    specs = [a_spec, b_spec] + ([o_spec] if has_add else []) + ([hbm] if has_send else [])
    scratch_shapes = ([pltpu.VMEM((tm, tn), F32)] if nk > 1 else []) + (_exchange_semaphores() if has_send else [])
    o_sds = jax.ShapeDtypeStruct(o_shape, out_dtype)
    return pl.pallas_call(
        body, name=name, grid=grid, in_specs=specs, out_specs=(o_spec, hbm) if has_send else o_spec,
        out_shape=(o_sds, rider.result) if has_send else o_sds, scratch_shapes=scratch_shapes,
        compiler_params=_params(("arbitrary",) * 3 if has_send else ("parallel", "parallel", "arbitrary")))(*ins)


def rms_fwd(x, gain, *, width=None, col_block=0, out_dtype=BF16, name="rms_fwd"):
    T = x.shape[0]
    W = x.shape[1] if width is None else width
    tt = _pick(T, (512, 256, 128, 64, 8))

    def body(x_ref, g_ref, o_ref):
        xv = x_ref[...]
        r = lax.rsqrt(jnp.mean(xv * xv, axis=-1, keepdims=True) + EPS)
        o_ref[...] = (xv * r * g_ref[...]).astype(out_dtype)

    return pl.pallas_call(
        body, name=name, grid=(T // tt,),
        in_specs=[pl.BlockSpec((tt, W), lambda i: (i, col_block)), pl.BlockSpec((1, W), lambda i: (0, 0))],
        out_specs=pl.BlockSpec((tt, W), lambda i: (i, 0)), out_shape=jax.ShapeDtypeStruct((T, W), out_dtype),
        compiler_params=_params(("parallel",)))(x, gain.reshape(1, W))


def rms_bwd(x, gain, dh, res=None, *, width=None, col_block=0, name="rms_bwd"):
    T = x.shape[0]
    W = x.shape[1] if width is None else width
    tt = _pick(T, (256, 128, 64, 8))
    has_res = res is not None

    def body(*refs):
        if has_res:
            x_ref, g_ref, dh_ref, res_ref, dx_ref, dg_ref = refs
        else:
            x_ref, g_ref, dh_ref, dx_ref, dg_ref = refs
        xv = x_ref[...]
        r = lax.rsqrt(jnp.mean(xv * xv, axis=-1, keepdims=True) + EPS)
        xhat = xv * r
        dy = dh_ref[...].astype(F32)
        dxhat = dy * g_ref[...]
        dx = r * (dxhat - xhat * jnp.mean(dxhat * xhat, axis=-1, keepdims=True))
        if has_res:
            dx = dx + res_ref[...]
        dx_ref[...] = dx

        @pl.when(pl.program_id(0) == 0)
        def _():
            dg_ref[...] = jnp.zeros_like(dg_ref)

        dg_ref[...] += jnp.sum(dy * xhat, axis=0, keepdims=True)

    row = pl.BlockSpec((tt, W), lambda i: (i, 0))
    ins = [x, gain.reshape(1, W), dh] + ([res] if has_res else [])
    specs = [pl.BlockSpec((tt, W), lambda i: (i, col_block)), pl.BlockSpec((1, W), lambda i: (0, 0)), row] + ([row] if has_res else [])
    return pl.pallas_call(
        body, name=name, grid=(T // tt,), in_specs=specs,
        out_specs=(row, pl.BlockSpec((1, W), lambda i: (0, 0))),
        out_shape=(jax.ShapeDtypeStruct((T, W), F32), jax.ShapeDtypeStruct((1, W), F32)),
        compiler_params=_params(("arbitrary",)))(*ins)


def ffn_hidden_fwd(h, wg, wu, gather=None, name="ffn_hidden_fwd"):
    T, D = h.shape
    F = wg.shape[1]
    tm = _pick(T, (1024, 512, 256, 128, 64, 8))
    tn = _pick(F, (512, 256, 128))
    grid = (T // tm, F // tn)
    rider = _rider(None, gather)

    def body(h_ref, wg_ref, wu_ref, *refs):
        if rider is not None:
            x_ref, g_ref, u_ref, a_ref, r_ref, *sems = refs
            ride_start, ride_finish = rider.plan(x_ref, r_ref, *sems)
            pl.when((pl.program_id(0) == 0) & (pl.program_id(1) == 0))(ride_start)
        else:
            g_ref, u_ref, a_ref = refs
        hv = h_ref[...].astype(MXU_DTYPE)
        g = jnp.dot(hv, wg_ref[...].astype(MXU_DTYPE), preferred_element_type=F32)
        u = jnp.dot(hv, wu_ref[...].astype(MXU_DTYPE), preferred_element_type=F32)
        g_ref[...] = g.astype(g_ref.dtype)
        u_ref[...] = u.astype(u_ref.dtype)
        a_ref[...] = (g * _sigmoid(g) * u).astype(a_ref.dtype)
        if rider is not None:
            pl.when((pl.program_id(0) == grid[0] - 1) & (pl.program_id(1) == grid[1] - 1))(ride_finish)

    hbm = pl.BlockSpec(memory_space=pl.ANY)
    wb = pl.BlockSpec((D, tn), lambda i, j: (0, j))
    ob = pl.BlockSpec((tm, tn), lambda i, j: (i, j))
    sh = jax.ShapeDtypeStruct((T, F), BF16)
    riding = rider is not None
    return pl.pallas_call(
        body, name=name, grid=grid, in_specs=[pl.BlockSpec((tm, D), lambda i, j: (i, 0)), wb, wb] + ([hbm] if riding else []),
        out_specs=(ob, ob, ob) + ((hbm,) if riding else ()), out_shape=(sh, sh, sh) + ((rider.result,) if riding else ()),
        scratch_shapes=_exchange_semaphores() if riding else [],
        compiler_params=_params(("arbitrary", "arbitrary") if riding else ("parallel", "parallel")))(*([h, wg, wu] + ([gather] if riding else [])))


def ffn_hidden_bwd(dy, wd, g, u, scale, name="ffn_hidden_bwd"):
    T, D = dy.shape
    F = wd.shape[0]
    tm = _pick(T, (1024, 512, 256, 128, 64, 8))
    tn = _pick(F, (512, 256, 128))

    nsub = 4 if tm % 64 == 0 else 1
    sub = tm // nsub

    def body(dy_ref, wd_ref, g_ref, u_ref, dg_ref, du_ref):
        wv = wd_ref[...].astype(MXU_DTYPE)
        das = [lax.dot_general(dy_ref[r * sub:(r + 1) * sub, :].astype(MXU_DTYPE), wv, _NT, preferred_element_type=F32)
               for r in range(nsub)]
        for r in range(nsub):
            rs = slice(r * sub, (r + 1) * sub)
            da = das[r] * scale
            gv = g_ref[rs, :].astype(F32)
            s = _sigmoid(gv)
            dg_ref[rs, :] = (da * u_ref[rs, :].astype(F32) * (s * (1.0 + gv * (1.0 - s)))).astype(dg_ref.dtype)
            du_ref[rs, :] = (da * (gv * s)).astype(du_ref.dtype)

    ob = pl.BlockSpec((tm, tn), lambda i, j: (i, j))
    sh = jax.ShapeDtypeStruct((T, F), BF16)
    return pl.pallas_call(body, name=name, grid=(T // tm, F // tn),
                          in_specs=[pl.BlockSpec((tm, D), lambda i, j: (i, 0)), pl.BlockSpec((tn, D), lambda i, j: (j, 0)), ob, ob],
                          out_specs=(ob, ob), out_shape=(sh, sh), compiler_params=_params(("parallel", "parallel")))(dy, wd, g, u)


def loss_head(y, target, name="loss_head"):
    T, D = y.shape
    tt = _pick(T, (512, 256, 128, 64, 8))

    def body(y_ref, t_ref, dy_ref, l_ref):
        e = y_ref[...] - t_ref[...]
        dy_ref[...] = e * (1.0 / D)

        @pl.when(pl.program_id(0) == 0)
        def _():
            l_ref[...] = jnp.zeros_like(l_ref)

        l_ref[...] += 0.5 * jnp.sum(jnp.mean(e * e, axis=-1, keepdims=True))

    row = pl.BlockSpec((tt, D), lambda i: (i, 0))
    dy, l = pl.pallas_call(body, name=name, grid=(T // tt,), in_specs=[row, row],
                           out_specs=(row, pl.BlockSpec((8, LANES), lambda i: (0, 0))),
                           out_shape=(jax.ShapeDtypeStruct((T, D), F32), jax.ShapeDtypeStruct((8, LANES), F32)),
                           compiler_params=_params(("arbitrary",)))(y, target)
    return dy, l[0, 0]


def _shifted(cur, prev, k, row):
    if k == 0:
        return cur
    return jnp.where(row < k, pltpu.roll(prev, k, 0), pltpu.roll(cur, k, 0))


def _conv_pre(x_ref, xp_ref, w_ref, first):
    cur = x_ref[...]
    prev = jnp.where(first, 0.0, xp_ref[...])
    row = lax.broadcasted_iota(jnp.int32, cur.shape, 0)
    xs = [_shifted(cur, prev, CONV_K - 1 - j, row) for j in range(CONV_K)]
    y = xs[0] * w_ref[0:1, :]
    for j in range(1, CONV_K):
        y = y + xs[j] * w_ref[j:j + 1, :]
    return y, xs


def gdn_prep_fwd(pm, conv_w, name="gdn_prep_fwd"):
    T = pm.shape[0]
    HW = GDN_HEADS * GDN_D
    tt = _pick(T, (256, 128, 64, 8))
    qscale = GDN_D ** -0.5

    def body(x_ref, xp_ref, w_ref, q_ref, k_ref, v_ref):
        y, _ = _conv_pre(x_ref, xp_ref, w_ref, pl.program_id(0) == 0)
        s = y * _sigmoid(y)
        for h in range(GDN_HEADS):
            for part, o_ref, sc in ((0, q_ref, qscale), (1, k_ref, 1.0)):
                sl = s[:, part * HW + h * GDN_D: part * HW + (h + 1) * GDN_D]
                r = lax.rsqrt(jnp.sum(sl * sl, axis=-1, keepdims=True) + EPS)
                o_ref[:, h * GDN_D:(h + 1) * GDN_D] = sl * (r * sc)
        v_ref[...] = s[:, 2 * HW:]

    blk = pl.BlockSpec((tt, 3 * HW), lambda i: (i, 0))
    blkp = pl.BlockSpec((tt, 3 * HW), lambda i: (jnp.maximum(i - 1, 0), 0))
    out = pl.BlockSpec((tt, HW), lambda i: (i, 0))
    sh = jax.ShapeDtypeStruct((T, HW), F32)
    return pl.pallas_call(body, name=name, grid=(T // tt,), in_specs=[blk, blkp, pl.BlockSpec((CONV_K, 3 * HW), lambda i: (0, 0))],
                          out_specs=(out, out, out), out_shape=(sh, sh, sh), compiler_params=_params(("parallel",)))(pm, pm, conv_w)


def gdn_prep_bwd(pm, conv_w, dq, dk, dv, name="gdn_prep_bwd"):
    T = pm.shape[0]
    HW = GDN_HEADS * GDN_D
    tt = _pick(T, (256, 128, 64, 8))
    qscale = GDN_D ** -0.5

    def body(x_ref, xp_ref, w_ref, dq_ref, dk_ref, dv_ref, dy_ref, dw_ref):
        y, xs = _conv_pre(x_ref, xp_ref, w_ref, pl.program_id(0) == 0)
        sg = _sigmoid(y)
        s = y * sg
        dsilu = sg * (1.0 + y * (1.0 - sg))
        for h in range(GDN_HEADS):
            for part, d_ref, sc in ((0, dq_ref, qscale), (1, dk_ref, 1.0)):
                lo = part * HW + h * GDN_D
                sl = s[:, lo:lo + GDN_D]
                r = lax.rsqrt(jnp.sum(sl * sl, axis=-1, keepdims=True) + EPS)
                n = sl * r
                dn = d_ref[:, h * GDN_D:(h + 1) * GDN_D] * sc
                ds = r * (dn - n * jnp.sum(dn * n, axis=-1, keepdims=True))
                dy_ref[:, lo:lo + GDN_D] = ds * dsilu[:, lo:lo + GDN_D]
        dy_ref[:, 2 * HW:] = dv_ref[...] * dsilu[:, 2 * HW:]

        @pl.when(pl.program_id(0) == 0)
        def _():
            dw_ref[...] = jnp.zeros_like(dw_ref)

        dyv = dy_ref[...]
        for j in range(CONV_K):
            dw_ref[j:j + 1, :] += jnp.sum(dyv * xs[j], axis=0, keepdims=True)

    blk = pl.BlockSpec((tt, 3 * HW), lambda i: (i, 0))
    blkp = pl.BlockSpec((tt, 3 * HW), lambda i: (jnp.maximum(i - 1, 0), 0))
    hb = pl.BlockSpec((tt, HW), lambda i: (i, 0))
    wb = pl.BlockSpec((CONV_K, 3 * HW), lambda i: (0, 0))
    return pl.pallas_call(body, name=name, grid=(T // tt,), in_specs=[blk, blkp, wb, hb, hb, hb], out_specs=(blk, wb),
                          out_shape=(jax.ShapeDtypeStruct((T, 3 * HW), F32), jax.ShapeDtypeStruct((CONV_K, 3 * HW), F32)),
                          compiler_params=_params(("arbitrary",)))(pm, pm, conv_w, dq, dk, dv)


def conv_dx(dy, conv_w, name="conv_dx"):
    T, W = dy.shape
    tt = _pick(T, (256, 128, 64, 8))
    nt = T // tt

    def body(d_ref, dn_ref, w_ref, dx_ref):
        cur = d_ref[...]
        nxt = jnp.where(pl.program_id(0) == nt - 1, 0.0, dn_ref[...])
        row = lax.broadcasted_iota(jnp.int32, cur.shape, 0)
        acc = cur * w_ref[CONV_K - 1:CONV_K, :]
        for j in range(CONV_K - 1):
            k = CONV_K - 1 - j
            sh = jnp.where(row >= tt - k, pltpu.roll(nxt, tt - k, 0), pltpu.roll(cur, tt - k, 0))
            acc = acc + sh * w_ref[j:j + 1, :]
        dx_ref[...] = acc

    blk = pl.BlockSpec((tt, W), lambda i: (i, 0))
    blkn = pl.BlockSpec((tt, W), lambda i: (jnp.minimum(i + 1, nt - 1), 0))
    return pl.pallas_call(body, name=name, grid=(nt,), in_specs=[blk, blkn, pl.BlockSpec((CONV_K, W), lambda i: (0, 0))], out_specs=blk,
                          out_shape=jax.ShapeDtypeStruct((T, W), F32), compiler_params=_params(("parallel",)))(dy, dy, conv_w)


def _upper_ones(c):
    return (lax.broadcasted_iota(jnp.int32, (c, c), 0) <= lax.broadcasted_iota(jnp.int32, (c, c), 1)).astype(F32)


def gdn_gates_fwd(a_r, b_r, alog_c, dt_c, name="gdn_gates_fwd"):
    R, C = a_r.shape

    def body(a_ref, b_ref, al_ref, dt_ref, gc_ref, beta_ref):
        x = a_ref[...] + dt_ref[...]
        sp = jnp.maximum(x, 0.0) + jnp.log1p(jnp.exp(-jnp.abs(x)))
        g = -jnp.exp(al_ref[...]) * sp
        gc_ref[...] = jnp.dot(g, _upper_ones(C), preferred_element_type=F32, precision=HI)
        beta_ref[...] = _sigmoid(b_ref[...])

    sh = jax.ShapeDtypeStruct((R, C), F32)
    return pl.pallas_call(body, name=name, out_shape=(sh, sh), compiler_params=_params())(a_r, b_r, alog_c, dt_c)


def gdn_gates_bwd(a_r, b_r, alog_c, dt_c, dgc, dbeta, name="gdn_gates_bwd"):
    R, C = a_r.shape

    def body(a_ref, b_ref, al_ref, dt_ref, dgc_ref, dbeta_ref, da_ref, db_ref, dal_ref, ddt_ref):
        x = a_ref[...] + dt_ref[...]
        sp = jnp.maximum(x, 0.0) + jnp.log1p(jnp.exp(-jnp.abs(x)))
        ea = jnp.exp(al_ref[...])
        dg = lax.dot_general(dgc_ref[...], _upper_ones(C), (((1,), (1,)), ((), ())), preferred_element_type=F32, precision=HI)
        dsp = dg * (-ea)
        da = dsp * _sigmoid(x)
        da_ref[...] = da
        beta = _sigmoid(b_ref[...])
        db_ref[...] = dbeta_ref[...] * beta * (1.0 - beta)
        sel = (lax.broadcasted_iota(jnp.int32, (GDN_HEADS, R), 1) % GDN_HEADS == lax.broadcasted_iota(jnp.int32, (GDN_HEADS, R), 0)).astype(F32)
        dal_ref[...] = jnp.sum(jnp.dot(sel, dg * (-ea * sp), preferred_element_type=F32, precision=HI), axis=1, keepdims=True)
        ddt_ref[...] = jnp.sum(jnp.dot(sel, da, preferred_element_type=F32, precision=HI), axis=1, keepdims=True)

    sh = jax.ShapeDtypeStruct((R, C), F32)
    s8 = jax.ShapeDtypeStruct((GDN_HEADS, 1), F32)
    return pl.pallas_call(body, name=name, out_shape=(sh, sh, s8, s8), compiler_params=_params())(a_r, b_r, alog_c, dt_c, dgc, dbeta)


def _dot(a, b):
    return jnp.dot(a, b, preferred_element_type=F32, precision=HI)


def _dot_nt(a, b):
    return lax.dot_general(a, b, (((1,), (1,)), ((), ())), preferred_element_type=F32, precision=HI)


def _dot_tn(a, b):
    return lax.dot_general(a, b, (((0,), (0,)), ((), ())), preferred_element_type=F32, precision=HI)


def _bdot(a, b, dims=(((1,), (0,)), ((), ()))):
    return lax.dot_general(a.astype(MXU_DTYPE), b.astype(MXU_DTYPE), dims, preferred_element_type=F32)


def _bdot_nt(a, b):
    return _bdot(a, b, (((1,), (1,)), ((), ())))


def _bdot_tn(a, b):
    return _bdot(a, b, (((0,), (0,)), ((), ())))


def _unit_lower_inverses(ms):
    c = ms[0].shape[0]
    eye = (lax.broadcasted_iota(jnp.int32, (c, c), 0) == lax.broadcasted_iota(jnp.int32, (c, c), 1)).astype(F32)
    ps = [-m for m in ms]
    ts = [eye + p for p in ps]
    n = 2
    while n < c:
        ps = [_dot(p, p) for p in ps]
        ts = [t + _dot(t, p) for t, p in zip(ts, ps)]
        n *= 2
    return ts


def _col(row, eye):
    c = eye.shape[0]
    return jnp.sum(jnp.where(eye, jnp.broadcast_to(row, (c, c)), 0.0), axis=1, keepdims=True)


def _row(col, eye):
    c = eye.shape[0]
    return jnp.sum(jnp.where(eye, jnp.broadcast_to(col, (c, c)), 0.0), axis=0, keepdims=True)


def _gdr_chunks(q_ref, k_ref, v_ref, gc_ref, b_ref, eye, ii, jj):
    C = eye.shape[0]
    fs = []
    for h in range(GDN_HEADS):
        sl = slice(h * GDN_D, (h + 1) * GDN_D)
        qh, kh, vh = q_ref[:, sl], k_ref[:, sl], v_ref[:, sl]
        gcr, br = gc_ref[0, h:h + 1, :], b_ref[0, h:h + 1, :]
        gcc = _col(gcr, eye)
        bc = _col(br, eye)
        causal = ii >= jj
        decay = jnp.where(causal, jnp.exp(jnp.where(causal, gcc - gcr, 0.0)), 0.0)
        decay_t = jnp.where(ii <= jj, jnp.exp(jnp.where(ii <= jj, gcr - gcc, 0.0)), 0.0)
        kb = kh * bc
        eg = jnp.exp(gcc)
        glast = gcr[:, C - 1:C]
        fs.append(dict(sl=sl, qh=qh, kh=kh, vh=vh, gcc=gcc, bc=bc, decay=decay, decay_t=decay_t, kb=kb, vb=vh * bc, eg=eg,
                       el=jnp.exp(glast), ekd=jnp.exp(glast - gcc), kbg=kb * eg,
                       m=jnp.where(ii > jj, _bdot_nt(kb, kh) * decay, 0.0)))
    for f, tinv in zip(fs, _unit_lower_inverses([f["m"] for f in fs])):
        f["tinv"] = tinv
    for f in fs:
        f["u"] = _dot(f["tinv"], f["vb"])
        f["w"] = _dot(f["tinv"], f["kbg"])
        f["a"] = _bdot_nt(f["qh"], f["kh"]) * f["decay"]
        f["qd"] = f["qh"] * f["eg"]
        f["kd"] = f["kh"] * f["ekd"]
    return fs


def gdr_fwd(q, k, v, gc, beta, name="gdr_fwd"):
    T = q.shape[0]
    H, DK, C = GDN_HEADS, GDN_D, CHUNK
    N = T // C

    def body(q_ref, k_ref, v_ref, gc_ref, b_ref, o_ref, st_ref, s_ref):
        @pl.when(pl.program_id(0) == 0)
        def _():
            s_ref[...] = jnp.zeros_like(s_ref)

        ii = lax.broadcasted_iota(jnp.int32, (C, C), 0)
        jj = lax.broadcasted_iota(jnp.int32, (C, C), 1)
        eye = ii == jj
        fs = _gdr_chunks(q_ref, k_ref, v_ref, gc_ref, b_ref, eye, ii, jj)
        ss = [s_ref[h] for h in range(H)]
        vnews = [f["u"] - _bdot(f["w"], s) for f, s in zip(fs, ss)]
        for h, (f, s, vnew) in enumerate(zip(fs, ss, vnews)):
            st_ref[0, h] = s
            o_ref[:, f["sl"]] = _bdot(f["qd"], s) + _bdot(f["a"], vnew)
            s_ref[h] = s * f["el"] + _bdot_tn(f["kd"], vnew)

    tok = pl.BlockSpec((C, H * DK), lambda n: (n, 0))
    gate = pl.BlockSpec((1, H, C), lambda n: (n, 0, 0))
    return pl.pallas_call(
        body, name=name, grid=(N,), in_specs=[tok, tok, tok, gate, gate],
        out_specs=(tok, pl.BlockSpec((1, H, DK, DK), lambda n: (n, 0, 0, 0))),
        out_shape=(jax.ShapeDtypeStruct((T, H * DK), F32), jax.ShapeDtypeStruct((N, H, DK, DK), F32)),
        scratch_shapes=[pltpu.VMEM((H, DK, DK), F32)], compiler_params=_params(("arbitrary",)))(q, k, v, gc, beta)


def gdr_bwd(q, k, v, gc, beta, states, do, name="gdr_bwd"):
    T = q.shape[0]
    H, DK, C = GDN_HEADS, GDN_D, CHUNK
    N = T // C

    def body(q_ref, k_ref, v_ref, gc_ref, b_ref, st_ref, do_ref, dq_ref, dk_ref, dv_ref, dgc_ref, db_ref, ds_ref):
        @pl.when(pl.program_id(0) == 0)
        def _():
            ds_ref[...] = jnp.zeros_like(ds_ref)

        ii = lax.broadcasted_iota(jnp.int32, (C, C), 0)
        jj = lax.broadcasted_iota(jnp.int32, (C, C), 1)
        eye = ii == jj
        lastj = lax.broadcasted_iota(jnp.int32, (1, C), 1) == C - 1
        fs = _gdr_chunks(q_ref, k_ref, v_ref, gc_ref, b_ref, eye, ii, jj)
        for h, f in enumerate(fs):
            f["s"] = st_ref[0, h]
            f["dsn"] = ds_ref[h]
            f["dout"] = do_ref[:, f["sl"]]
        for f in fs:
            f["vnew"] = f["u"] - _bdot(f["w"], f["s"])
            f["tinv_t"] = f["tinv"].T
            f["a_t"] = _bdot_nt(f["kh"], f["qh"]) * f["decay_t"]
        for f in fs:
            f["dvnew"] = _bdot(f["a_t"], f["dout"]) + _bdot(f["kd"], f["dsn"])
            f["da"] = _bdot_nt(f["dout"], f["vnew"])
            f["da_t"] = _bdot_nt(f["vnew"], f["dout"])
            f["dqd"] = _bdot_nt(f["dout"], f["s"])
            f["dkd"] = _bdot_nt(f["vnew"], f["dsn"])
        for h, f in enumerate(fs):
            ds_ref[h] = _bdot_tn(f["qd"], f["dout"]) - _bdot_tn(f["w"], f["dvnew"]) + f["dsn"] * f["el"]
            f["dw"] = -_bdot_nt(f["dvnew"], f["s"])
        for f in fs:
            f["dvb"] = _dot(f["tinv_t"], f["dvnew"])
            f["dkbg"] = _dot(f["tinv_t"], f["dw"])
        for f in fs:
            f["dm"] = jnp.where(ii > jj, -(_bdot_nt(f["dvb"], f["u"]) + _bdot_nt(f["dkbg"], f["w"])), 0.0)
            f["dm_t"] = jnp.where(ii < jj, -(_bdot_nt(f["u"], f["dvb"]) + _bdot_nt(f["w"], f["dkbg"])), 0.0)
        for h, f in enumerate(fs):
            sl, qh, kh, vh = f["sl"], f["qh"], f["kh"], f["vh"]
            dkd, dqd, dkbg, dvb = f["dkd"], f["dqd"], f["dkbg"], f["dvb"]
            dkk = f["dm"] * f["decay"]
            dkk_t = f["dm_t"] * f["decay_t"]
            dqk = f["da"] * f["decay"]
            dqk_t = f["da_t"] * f["decay_t"]
            e = f["dm"] * f["m"] + f["da"] * f["a"]
            dkb = _bdot(dkk, kh) + dkbg * f["eg"]
            dq_ref[:, sl] = _bdot(dqk, kh) + dqd * f["eg"]
            dk_ref[:, sl] = _bdot(dkk_t, f["kb"]) + _bdot(dqk_t, qh) + dkd * f["ekd"] + dkb * f["bc"]
            dv_ref[:, sl] = dvb * f["bc"]
            skd = jnp.sum(dkd * f["kd"], axis=1, keepdims=True)
            dgc_col = (jnp.sum(e, axis=1, keepdims=True) + jnp.sum(dqd * f["qd"], axis=1, keepdims=True)
                       + jnp.sum(dkbg * f["kbg"], axis=1, keepdims=True) - skd)
            dglast = jnp.sum(f["dsn"] * f["s"]) * f["el"] + jnp.sum(skd)
            dgc_row = _row(dgc_col, eye) - jnp.sum(e, axis=0, keepdims=True)
            dgc_ref[0, h:h + 1, :] = dgc_row + jnp.where(lastj, dglast, 0.0)
            dbeta_col = jnp.sum(dkb * kh, axis=1, keepdims=True) + jnp.sum(dvb * vh, axis=1, keepdims=True)
            db_ref[0, h:h + 1, :] = _row(dbeta_col, eye)

    rev = lambda n: (N - 1 - n, 0)
    tok = pl.BlockSpec((C, H * DK), rev)
    gate = pl.BlockSpec((1, H, C), lambda n: (N - 1 - n, 0, 0))
    tsh = jax.ShapeDtypeStruct((T, H * DK), F32)
    gsh = jax.ShapeDtypeStruct((N, H, C), F32)
    return pl.pallas_call(
        body, name=name, grid=(N,),
        in_specs=[tok, tok, tok, gate, gate, pl.BlockSpec((1, H, DK, DK), lambda n: (N - 1 - n, 0, 0, 0)), tok],
        out_specs=(tok, tok, tok, gate, gate), out_shape=(tsh, tsh, tsh, gsh, gsh),
        scratch_shapes=[pltpu.VMEM((H, DK, DK), F32)], compiler_params=_params(("arbitrary",)))(q, k, v, gc, beta, states, do)


def gdn_gate_fwd(o, pm, out_norm, *, z_block, name="gdn_gate_fwd"):
    T, HW = o.shape
    tt = _pick(T, (256, 128, 64, 8))

    def body(o_ref, z_ref, g_ref, y_ref):
        for h in range(GDN_HEADS):
            sl = slice(h * GDN_D, (h + 1) * GDN_D)
            ov = o_ref[:, sl]
            zv = z_ref[:, sl]
            r = lax.rsqrt(jnp.mean(ov * ov, axis=-1, keepdims=True) + EPS)
            y_ref[:, sl] = (ov * r * g_ref[...] * (zv * _sigmoid(zv))).astype(y_ref.dtype)

    blk = pl.BlockSpec((tt, HW), lambda i: (i, 0))
    return pl.pallas_call(body, name=name, grid=(T // tt,),
                          in_specs=[blk, pl.BlockSpec((tt, HW), lambda i: (i, z_block)), pl.BlockSpec((1, GDN_D), lambda i: (0, 0))],
                          out_specs=blk, out_shape=jax.ShapeDtypeStruct((T, HW), BF16),
                          compiler_params=_params(("parallel",)))(o, pm, out_norm.reshape(1, GDN_D))


def gdn_gate_bwd(o, pm, out_norm, dy, *, z_block, dy_block=0, name="gdn_gate_bwd"):
    T, HW = o.shape
    tt = _pick(T, (256, 128, 64, 8))

    def body(o_ref, z_ref, g_ref, dy_ref, do_ref, dz_ref, dg_ref):
        @pl.when(pl.program_id(0) == 0)
        def _():
            dg_ref[...] = jnp.zeros_like(dg_ref)

        acc = jnp.zeros((1, GDN_D), F32)
        for h in range(GDN_HEADS):
            sl = slice(h * GDN_D, (h + 1) * GDN_D)
            ov = o_ref[:, sl]
            zv = z_ref[:, sl]
            dyv = dy_ref[:, sl]
            r = lax.rsqrt(jnp.mean(ov * ov, axis=-1, keepdims=True) + EPS)
            xhat = ov * r
            sg = _sigmoid(zv)
            sz = zv * sg
            dn = dyv * sz
            dz_ref[:, sl] = dyv * (xhat * g_ref[...]) * (sg * (1.0 + zv * (1.0 - sg)))
            acc = acc + jnp.sum(dn * xhat, axis=0, keepdims=True)
            dxhat = dn * g_ref[...]
            do_ref[:, sl] = r * (dxhat - xhat * jnp.mean(dxhat * xhat, axis=-1, keepdims=True))
        dg_ref[...] += acc

    blk = pl.BlockSpec((tt, HW), lambda i: (i, 0))
    gb = pl.BlockSpec((1, GDN_D), lambda i: (0, 0))
    sh = jax.ShapeDtypeStruct((T, HW), F32)
    return pl.pallas_call(body, name=name, grid=(T // tt,),
                          in_specs=[blk, pl.BlockSpec((tt, HW), lambda i: (i, z_block)), gb, pl.BlockSpec((tt, HW), lambda i: (i, dy_block))],
                          out_specs=(blk, blk, gb), out_shape=(sh, sh, jax.ShapeDtypeStruct((1, GDN_D), F32)),
                          compiler_params=_params(("arbitrary",)))(o, pm, out_norm.reshape(1, GDN_D), dy)


def _pool_bands(tt, win, t0):
    t = lax.broadcasted_iota(jnp.int32, (tt, tt), 0)
    s = lax.broadcasted_iota(jnp.int32, (tt, tt), 1)
    inv = 1.0 / jnp.minimum(t + t0 + 1, win).astype(F32)
    cur = jnp.where((s <= t) & (s > t - win), inv, 0.0)
    prev = jnp.where(s - tt > t - win, inv, 0.0)
    return cur, prev


def _pool_diff(u_ref, up_ref, g, tt, t0, first):
    sl = slice(g * POOL_GW, (g + 1) * POOL_GW)
    cur, prev = _pool_bands(tt, POOL_WINDOWS[g], t0)
    ug = u_ref[:, sl]
    upg = jnp.where(first, 0.0, up_ref[:, sl])
    return _dot(cur, ug) + _dot(prev, upg) - ug


def pool_fwd(pm, pool_w, pool_scale, *, u_block, name="pool_fwd"):
    T = pm.shape[0]
    PW = len(POOL_WINDOWS) * POOL_GW
    tt = _pick(T, (256, 128, 64, 16))

    def body(u_ref, up_ref, w_ref, s_ref, p_ref):
        i = pl.program_id(0)
        for g in range(len(POOL_WINDOWS)):
            sl = slice(g * POOL_GW, (g + 1) * POOL_GW)
            diff = _pool_diff(u_ref, up_ref, g, tt, i * tt, i == 0)
            y = jnp.dot(diff.astype(MXU_DTYPE), w_ref[g].astype(MXU_DTYPE), preferred_element_type=F32)
            p_ref[:, sl] = (y * s_ref[:, sl]).astype(p_ref.dtype)

    return pl.pallas_call(
        body, name=name, grid=(T // tt,),
        in_specs=[pl.BlockSpec((tt, PW), lambda i: (i, u_block)), pl.BlockSpec((tt, PW), lambda i: (jnp.maximum(i - 1, 0), u_block)),
                  pl.BlockSpec((len(POOL_WINDOWS), POOL_GW, POOL_GW), lambda i: (0, 0, 0)), pl.BlockSpec((1, PW), lambda i: (0, 0))],
        out_specs=pl.BlockSpec((tt, PW), lambda i: (i, 0)), out_shape=jax.ShapeDtypeStruct((T, PW), BF16),
        compiler_params=_params(("parallel",)))(pm, pm, pool_w, pool_scale.reshape(1, PW))


def pool_bwd_a(pm, pool_w, pool_scale, dp, *, u_block, dp_block=0, name="pool_bwd_a"):
    T = pm.shape[0]
    G = len(POOL_WINDOWS)
    PW = G * POOL_GW
    tt = _pick(T, (256, 128, 64, 16))

    def body(u_ref, up_ref, w_ref, s_ref, dp_ref, dd_ref, dw_ref, dsc_ref):
        i = pl.program_id(0)

        @pl.when(i == 0)
        def _():
            dw_ref[...] = jnp.zeros_like(dw_ref)
            dsc_ref[...] = jnp.zeros_like(dsc_ref)

        for g in range(G):
            sl = slice(g * POOL_GW, (g + 1) * POOL_GW)
            diff = _pool_diff(u_ref, up_ref, g, tt, i * tt, i == 0).astype(MXU_DTYPE)
            wg = w_ref[g].astype(MXU_DTYPE)
            dpv = dp_ref[:, sl]
            y = jnp.dot(diff, wg, preferred_element_type=F32)
            dsc_ref[:, sl] += jnp.sum(dpv * y, axis=0, keepdims=True)
            dy = (dpv * s_ref[:, sl]).astype(MXU_DTYPE)
            dd_ref[:, sl] = lax.dot_general(dy, wg, (((1,), (1,)), ((), ())), preferred_element_type=F32)
            dw_ref[g] += lax.dot_general(diff, dy, (((0,), (0,)), ((), ())), preferred_element_type=F32)

    wb = pl.BlockSpec((G, POOL_GW, POOL_GW), lambda i: (0, 0, 0))
    sb = pl.BlockSpec((1, PW), lambda i: (0, 0))
    blk = pl.BlockSpec((tt, PW), lambda i: (i, 0))
    return pl.pallas_call(
        body, name=name, grid=(T // tt,),
        in_specs=[pl.BlockSpec((tt, PW), lambda i: (i, u_block)), pl.BlockSpec((tt, PW), lambda i: (jnp.maximum(i - 1, 0), u_block)), wb, sb,
                  pl.BlockSpec((tt, PW), lambda i: (i, dp_block))],
        out_specs=(blk, wb, sb),
        out_shape=(jax.ShapeDtypeStruct((T, PW), F32), jax.ShapeDtypeStruct((G, POOL_GW, POOL_GW), F32), jax.ShapeDtypeStruct((1, PW), F32)),
        compiler_params=_params(("arbitrary",)))(pm, pm, pool_w, pool_scale.reshape(1, PW), dp)


def pool_bwd_b(dd, name="pool_bwd_b"):
    T, PW = dd.shape
    tt = _pick(T, (256, 128, 64, 16))
    nt = T // tt

    def body(d_ref, dn_ref, du_ref):
        i = pl.program_id(0)
        s = lax.broadcasted_iota(jnp.int32, (tt, tt), 0)
        t = lax.broadcasted_iota(jnp.int32, (tt, tt), 1)
        for g, win in enumerate(POOL_WINDOWS):
            sl = slice(g * POOL_GW, (g + 1) * POOL_GW)
            inv_c = 1.0 / jnp.minimum(t + i * tt + 1, win).astype(F32)
            cur = jnp.where((t >= s) & (t < s + win), inv_c, 0.0)
            nxt = jnp.where(t + tt < s + win, 1.0 / win, 0.0)
            dg = d_ref[:, sl]
            dng = jnp.where(i == nt - 1, 0.0, dn_ref[:, sl])
            du_ref[:, sl] = _dot(cur, dg) + _dot(nxt, dng) - dg

    blk = pl.BlockSpec((tt, PW), lambda i: (i, 0))
    return pl.pallas_call(body, name=name, grid=(nt,), in_specs=[blk, pl.BlockSpec((tt, PW), lambda i: (jnp.minimum(i + 1, nt - 1), 0))],
                          out_specs=blk, out_shape=jax.ShapeDtypeStruct((T, PW), F32), compiler_params=_params(("parallel",)))(dd, dd)


def _rope_consts():
    j = jnp.arange(QK_HEAD)
    inv_freq = ROPE_THETA ** (-jnp.arange(0, ROPE, 2, dtype=F32) / ROPE)
    freq = jnp.where(j >= NOPE, inv_freq[(j - NOPE) % (ROPE // 2)], 0.0).astype(F32)
    half = ROPE // 2
    src = jnp.arange(QK_HEAD)[:, None]
    dst = jnp.arange(QK_HEAD)[None, :]
    first = (dst >= NOPE) & (dst < NOPE + half)
    second = dst >= NOPE + half
    p = jnp.where(first & (src == dst + half), -1.0, 0.0) + jnp.where(second & (src == dst - half), 1.0, 0.0)
    return freq.reshape(1, QK_HEAD), p.astype(F32)


def rope_tables(pos_col, freq, name="rope_tables"):
    T = pos_col.shape[0]
    tt = _pick(T, (512, 256, 128, 64, 8))

    def body(p_ref, f_ref, c_ref, s_ref):
        ang = p_ref[...] * f_ref[...]
        rot = lax.broadcasted_iota(jnp.int32, ang.shape, 1) >= NOPE
        c_ref[...] = jnp.where(rot, jnp.cos(ang), 1.0)
        s_ref[...] = jnp.where(rot, jnp.sin(ang), 0.0)

    blk = pl.BlockSpec((tt, QK_HEAD), lambda i: (i, 0))
    sh = jax.ShapeDtypeStruct((T, QK_HEAD), F32)
    return pl.pallas_call(body, name=name, grid=(T // tt,), in_specs=[pl.BlockSpec((tt, 1), lambda i: (i, 0)), pl.BlockSpec((1, QK_HEAD), lambda i: (0, 0))],
                          out_specs=(blk, blk), out_shape=(sh, sh), compiler_params=_params(("parallel",)))(pos_col, freq)


def _permute(x, p):
    hi = x.astype(BF16)
    lo = (x - hi.astype(F32)).astype(BF16)
    pb = p.astype(BF16)
    return jnp.dot(hi, pb, preferred_element_type=F32) + jnp.dot(lo, pb, preferred_element_type=F32)


def _seg_stats(t):
    lane = lax.broadcasted_iota(jnp.int32, t.shape, 1)
    nope = lane < NOPE
    sq = t * t
    r = jnp.where(nope, lax.rsqrt(jnp.sum(jnp.where(nope, sq, 0.0), axis=-1, keepdims=True) / NOPE + EPS),
                  lax.rsqrt(jnp.sum(jnp.where(nope, 0.0, sq), axis=-1, keepdims=True) / ROPE + EPS))
    return nope, r


def headnorm_rope_fwd(t, gain, cos, sin, pmat, out_scale=1.0, name="headnorm_rope_fwd"):
    H, T, W = t.shape
    tt = _pick(T, (2048, 1024, 512, 256, 128, 64, 8))

    def body(t_ref, g_ref, c_ref, s_ref, p_ref, o_ref):
        tv = t_ref[0]
        _, r = _seg_stats(tv)
        y = tv * r * g_ref[...]
        o_ref[0] = ((y * c_ref[...] + _permute(y, p_ref[...]) * s_ref[...]) * out_scale).astype(o_ref.dtype)

    blk = pl.BlockSpec((1, tt, W), lambda h, i: (h, i, 0))
    tab = pl.BlockSpec((tt, W), lambda h, i: (i, 0))
    return pl.pallas_call(body, name=name, grid=(H, T // tt),
                          in_specs=[blk, pl.BlockSpec((1, W), lambda h, i: (0, 0)), tab, tab, pl.BlockSpec((W, W), lambda h, i: (0, 0))],
                          out_specs=blk, out_shape=jax.ShapeDtypeStruct((H, T, W), BF16),
                          compiler_params=_params(("parallel", "parallel")))(t, gain.reshape(1, W), cos, sin, pmat)


def headnorm_rope_bwd(t, gain, cos, sin, pmat_t, dout, out_scale=1.0, name="headnorm_rope_bwd"):
    H, T, W = t.shape
    tt = _pick(T, (2048, 1024, 512, 256, 128, 64, 8))

    def body(t_ref, g_ref, c_ref, s_ref, p_ref, do_ref, dt_ref, dg_ref):
        @pl.when((pl.program_id(0) == 0) & (pl.program_id(1) == 0))
        def _():
            dg_ref[...] = jnp.zeros_like(dg_ref)

        tv = t_ref[0]
        dov = do_ref[0] * out_scale
        nope, r = _seg_stats(tv)
        dy = dov * c_ref[...] + _permute(dov * s_ref[...], p_ref[...])
        xhat = tv * r
        dg_ref[...] += jnp.sum(dy * xhat, axis=0, keepdims=True)
        dxhat = dy * g_ref[...]
        pr = dxhat * xhat
        mean = jnp.where(nope, jnp.sum(jnp.where(nope, pr, 0.0), axis=-1, keepdims=True) / NOPE,
                         jnp.sum(jnp.where(nope, 0.0, pr), axis=-1, keepdims=True) / ROPE)
        dt_ref[0] = r * (dxhat - xhat * mean)

    blk = pl.BlockSpec((1, tt, W), lambda h, i: (h, i, 0))
    tab = pl.BlockSpec((tt, W), lambda h, i: (i, 0))
    gb = pl.BlockSpec((1, W), lambda h, i: (0, 0))
    return pl.pallas_call(body, name=name, grid=(H, T // tt),
                          in_specs=[blk, gb, tab, tab, pl.BlockSpec((W, W), lambda h, i: (0, 0)), blk],
                          out_specs=(blk, gb), out_shape=(jax.ShapeDtypeStruct((H, T, W), F32), jax.ShapeDtypeStruct((1, W), F32)),
                          compiler_params=_params(("arbitrary", "arbitrary")))(t, gain.reshape(1, W), cos, sin, pmat_t, dout)


def sum_heads(x, name="sum_heads"):
    H, T, W = x.shape
    tt = _pick(T, (2048, 1024, 512, 256, 128, 64, 8))

    def body(x_ref, o_ref):
        @pl.when(pl.program_id(1) == 0)
        def _():
            o_ref[...] = jnp.zeros_like(o_ref)

        o_ref[...] += x_ref[0]

    return pl.pallas_call(body, name=name, grid=(T // tt, H), in_specs=[pl.BlockSpec((1, tt, W), lambda i, h: (h, i, 0))],
                          out_specs=pl.BlockSpec((tt, W), lambda i, h: (i, 0)), out_shape=jax.ShapeDtypeStruct((T, W), F32),
                          compiler_params=_params(("parallel", "arbitrary")))(x)


_NT = (((1,), (1,)), ((), ()))
_TN = (((0,), (0,)), ((), ()))
ATTN_SUB = 256


def _attn_fwd_tiles(T):
    tq = _pick(T, (1024, 512, 256, 128, 64))
    tk = _pick(T, (2048, 1024, 512, 256, 128, 64))
    return tq, tk, min(ATTN_SUB, tq, tk)


def _attn_bwd_tiles(T):
    tq = _pick(T, (2048, 1024, 512, 256, 128, 64))
    tk = _pick(T, (512, 256, 128, 64))
    return tq, tk, min(2 * ATTN_SUB, tq, tk)


def attn_fwd(q, k, v, name="attn_fwd", tiles=None):
    H, T, DQ = q.shape
    DV = v.shape[2]
    tq, tk, sub = tiles or _attn_fwd_tiles(T)
    nq, nk, nsub = T // tq, T // tk, tk // sub

    def body(q_ref, k_ref, v_ref, o_ref, l_ref, m_s, l_s, acc_s):
        i, j = pl.program_id(1), pl.program_id(2)

        @pl.when(j == 0)
        def _():
            m_s[...] = jnp.full_like(m_s, -1e30)
            l_s[...] = jnp.zeros_like(l_s)
            acc_s[...] = jnp.zeros_like(acc_s)

        def tile(rel):
            qv = q_ref[0]
            m_old = m_s[...]
            m_new = m_old
            ss = {}
            for c in range(nsub):
                if rel is not None and c * sub > rel + tq - 1:
                    continue
                s = lax.dot_general(qv, k_ref[0, c * sub:(c + 1) * sub, :], _NT, preferred_element_type=F32)
                if rel is not None and (c + 1) * sub - 1 > rel:
                    row = lax.broadcasted_iota(jnp.int32, s.shape, 0) + rel
                    col = lax.broadcasted_iota(jnp.int32, s.shape, 1) + c * sub
                    s = jnp.where(row >= col, s, -1e30)
                ss[c] = s
                m_new = jnp.maximum(m_new, jnp.max(s, axis=-1, keepdims=True))
            alpha = jnp.exp(m_old - m_new)
            l_new = alpha * l_s[...]
            acc = alpha * acc_s[...]
            for c, s in ss.items():
                p = jnp.exp(s - m_new)
                l_new = l_new + jnp.sum(p, axis=-1, keepdims=True)
                acc = acc + jnp.dot(p.astype(MXU_DTYPE), v_ref[0, c * sub:(c + 1) * sub, :], preferred_element_type=F32)
            l_s[...] = l_new
            acc_s[...] = acc
            m_s[...] = m_new

        rel = i * tq - j * tk
        pl.when(rel >= tk - 1)(lambda: tile(None))
        for r0 in range(0, tk - 1, tq):
            pl.when(rel == r0)(lambda r0=r0: tile(r0))

        @pl.when(j == nk - 1)
        def _():
            o_ref[0] = acc_s[...] / l_s[...]
            l_ref[0] = jnp.broadcast_to(m_s[...] + jnp.log(l_s[...]), (tq, DV))

    last = lambda i: (i * tq + (tq - 1)) // tk
    qb = pl.BlockSpec((1, tq, DQ), lambda h, i, j: (h, i, 0))
    kb = pl.BlockSpec((1, tk, DQ), lambda h, i, j: (h, jnp.minimum(j, last(i)), 0))
    vb = pl.BlockSpec((1, tk, DV), lambda h, i, j: (h, jnp.minimum(j, last(i)), 0))
    ob = pl.BlockSpec((1, tq, DV), lambda h, i, j: (h, i, 0))
    sh = jax.ShapeDtypeStruct((H, T, DV), F32)
    return pl.pallas_call(body, name=name, grid=(H, nq, nk), in_specs=[qb, kb, vb], out_specs=(ob, ob), out_shape=(sh, sh),
                          scratch_shapes=[pltpu.VMEM((tq, 1), F32), pltpu.VMEM((tq, 1), F32), pltpu.VMEM((tq, DV), F32)],
                          compiler_params=_params(("parallel", "parallel", "arbitrary")))(q, k, v)


def attn_bwd(q, k, v, o, lse, do, name="attn_bwd", tiles=None):
    H, T, DQ = q.shape
    DV = v.shape[2]
    tq, tk, sub = tiles or _attn_bwd_tiles(T)
    nq, nk, nsub = T // tq, T // tk, tq // sub

    def body(q_ref, k_ref, v_ref, o_ref, l_ref, do_ref, dq_ref, dk_ref, dv_ref, dk_s, dv_s):
        j, i = pl.program_id(1), pl.program_id(2)

        @pl.when((j == 0) & (i == 0))
        def _():
            dq_ref[...] = jnp.zeros_like(dq_ref)

        @pl.when(i == 0)
        def _():
            dk_s[...] = jnp.zeros_like(dk_s)
            dv_s[...] = jnp.zeros_like(dv_s)

        def tile(rel):
            kv, vv = k_ref[0], v_ref[0]
            live = [r for r in range(nsub) if rel is None or (r + 1) * sub - 1 >= rel]
            qs, dos, ss, dps = {}, {}, {}, {}
            for r in live:
                rs = slice(r * sub, (r + 1) * sub)
                qs[r] = q_ref[0, rs, :]
                dos[r] = do_ref[0, rs, :]
                ss[r] = lax.dot_general(qs[r], kv, _NT, preferred_element_type=F32)
                dps[r] = lax.dot_general(dos[r].astype(MXU_DTYPE), vv, _NT, preferred_element_type=F32)
            dk_acc = dk_s[...]
            dv_acc = dv_s[...]
            for r in live:
                rs = slice(r * sub, (r + 1) * sub)
                p = jnp.exp(ss[r] - l_ref[0, rs, 0:1])
                if rel is not None and r * sub < rel + tk - 1:
                    row = lax.broadcasted_iota(jnp.int32, p.shape, 0) + r * sub
                    col = lax.broadcasted_iota(jnp.int32, p.shape, 1) + rel
                    p = jnp.where(row >= col, p, 0.0)
                delta = jnp.sum(dos[r] * o_ref[0, rs, :], axis=-1, keepdims=True)
                ds = (p * (dps[r] - delta)).astype(MXU_DTYPE)
                dv_acc = dv_acc + lax.dot_general(p.astype(MXU_DTYPE), dos[r].astype(MXU_DTYPE), _TN, preferred_element_type=F32)
                dk_acc = dk_acc + lax.dot_general(ds, qs[r], _TN, preferred_element_type=F32)
                rows = pl.ds(pl.multiple_of(i * tq + r * sub, sub), sub)
                dq_ref[0, rows, :] += jnp.dot(ds, kv, preferred_element_type=F32)
            dk_s[...] = dk_acc
            dv_s[...] = dv_acc

        rel = j * tk - i * tq
        pl.when(rel <= 1 - tk)(lambda: tile(None))
        for r0 in range(0, tq, tk):
            pl.when(rel == r0)(lambda r0=r0: tile(r0))

        @pl.when(i == nq - 1)
        def _():
            dk_ref[0] = dk_s[...]
            dv_ref[0] = dv_s[...]

    first = lambda j: (j * tk) // tq
    qi = lambda h, j, i: (h, jnp.maximum(i, first(j)), 0)
    qb = pl.BlockSpec((1, tq, DQ), qi)
    ob = pl.BlockSpec((1, tq, DV), qi)
    kb = pl.BlockSpec((1, tk, DQ), lambda h, j, i: (h, j, 0))
    vb = pl.BlockSpec((1, tk, DV), lambda h, j, i: (h, j, 0))
    dqb = pl.BlockSpec((1, T, DQ), lambda h, j, i: (h, 0, 0))
    return pl.pallas_call(
        body, name=name, grid=(H, nk, nq), in_specs=[qb, kb, vb, ob, ob, ob], out_specs=(dqb, kb, vb),
        out_shape=(jax.ShapeDtypeStruct((H, T, DQ), F32), jax.ShapeDtypeStruct((H, T, DQ), F32), jax.ShapeDtypeStruct((H, T, DV), F32)),
        scratch_shapes=[pltpu.VMEM((tk, DQ), F32), pltpu.VMEM((tk, DV), F32)],
        compiler_params=_params(("parallel", "arbitrary", "arbitrary")))(q, k, v, o, lse, do)


def _mesh_place():
    return lax.axis_index("x"), lax.axis_index("y"), lax.axis_index("c")


def _flip(v, bit):
    return 1 - v if bit else v


def _relations():
    return [((r >> 2) & 1, (r >> 1) & 1, r & 1) for r in range(1, N_DEV)]


def _exchange_semaphores():
    return [pltpu.SemaphoreType.DMA((N_DEV - 1,)), pltpu.SemaphoreType.DMA((N_DEV - 1,)), pltpu.SemaphoreType.DMA(())]


def _gather_plan(x_ref, o_ref, send_sems, recv_sems, local_sem):
    x, y, c = _mesh_place()
    me, sibling = (x, y, c), (x, y, 1 - c)
    chips = [(1 - x, y), (x, 1 - y), (1 - x, 1 - y)]

    def slot(px, py, pc):
        return o_ref.at[4 * px + 2 * py + pc]

    def copy(k, block, to, src=None):
        return pltpu.make_async_remote_copy(
            src_ref=slot(*block) if src is None else src, dst_ref=slot(*block), send_sem=send_sems.at[k], recv_sem=recv_sems.at[k],
            device_id=to, device_id_type=pl.DeviceIdType.MESH)

    mine = pltpu.make_async_copy(x_ref, slot(*me), local_sem)
    first = [copy(0, me, sibling, src=x_ref)] + [copy(1 + j, me, (*chip, c), src=x_ref) for j, chip in enumerate(chips)]
    passed = [copy(4 + j, (*chip, c), sibling) for j, chip in enumerate(chips)]

    def start():
        mine.start()
        for cp in first:
            cp.start()

    def finish():
        for j, chip in enumerate(chips):
            copy(1 + j, (*chip, c), me).wait_recv()
            passed[j].start()
        copy(0, sibling, me).wait_recv()
        for j, chip in enumerate(chips):
            copy(4 + j, (*chip, 1 - c), me).wait_recv()
        for cp in first + passed:
            cp.wait_send()
        mine.wait()

    return start, finish


def _exchange_plan(x_ref, o_ref, send_sems, recv_sems, local_sem):
    x, y, c = _mesh_place()
    me = 4 * x + 2 * y + c
    local = pltpu.make_async_copy(x_ref.at[me], o_ref.at[me], local_sem)
    remote = []
    for r, (bx, by, bc) in enumerate(_relations()):
        px, py, pc = _flip(x, bx), _flip(y, by), _flip(c, bc)
        remote.append(pltpu.make_async_remote_copy(
            src_ref=x_ref.at[4 * px + 2 * py + pc], dst_ref=o_ref.at[me], send_sem=send_sems.at[r], recv_sem=recv_sems.at[r],
            device_id=(px, py, pc), device_id_type=pl.DeviceIdType.MESH))

    def start():
        local.start()
        for cp in remote:
            cp.start()

    def finish():
        for cp in remote:
            cp.wait_recv()
        for cp in remote:
            cp.wait_send()
        local.wait()

    return start, finish


class _Rider:
    def __init__(self, operand, result, plan):
        self.operand, self.result, self.plan = operand, result, plan


def _rider(send, gather):
    assert send is None or gather is None
    if send is not None:
        return _Rider(send, jax.ShapeDtypeStruct(send.shape, send.dtype), _exchange_plan)
    if gather is not None:
        return _Rider(gather, jax.ShapeDtypeStruct((N_DEV, *gather.shape), gather.dtype), _gather_plan)
    return None


def _transfer(rider, name):
    def body(x_ref, o_ref, send_sems, recv_sems, local_sem):
        start, finish = rider.plan(x_ref, o_ref, send_sems, recv_sems, local_sem)
        start()
        finish()

    return pl.pallas_call(body, name=name, out_shape=rider.result, in_specs=[pl.BlockSpec(memory_space=pl.ANY)],
                          out_specs=pl.BlockSpec(memory_space=pl.ANY), scratch_shapes=_exchange_semaphores())(rider.operand)


def all_gather(xs, name="all_gather"):
    return _transfer(_rider(None, xs), name)


def exchange(xs, name="exchange"):
    return _transfer(_rider(xs, None), name)


def adamw(recv, w, m, v, layer=None, name="adamw"):
    _, R, L = recv.shape
    tr = _pick(R, [c for c in (1024, 512, 256, 128, 64, 32, 16) if c * L <= ADAMW_BLOCK_ELEMS] + [8])

    def body(r_ref, w_ref, m_ref, v_ref, g_ref, d_ref, nm_ref, nv_ref):
        g = r_ref[0].astype(F32)
        for s in range(1, N_DEV):
            g = g + r_ref[s].astype(F32)
        m_new = ADAM_B1 * m_ref[...] + (1.0 - ADAM_B1) * g
        v_new = ADAM_B2 * v_ref[...] + (1.0 - ADAM_B2) * jnp.square(g)
        m_hat = m_new / (1.0 - ADAM_B1 ** ADAM_STEP)
        v_hat = v_new / (1.0 - ADAM_B2 ** ADAM_STEP)
        g_ref[...] = g
        d_ref[...] = -ADAM_LR * (m_hat / (jnp.sqrt(v_hat) + ADAM_EPS) + ADAM_WD * w_ref[...])
        nm_ref[...] = m_new
        nv_ref[...] = v_new

    row = pl.BlockSpec((tr, L), lambda i: (i, 0))
    state = row if layer is None else pl.BlockSpec((None, tr, L), lambda i: (layer, i, 0))
    sh = jax.ShapeDtypeStruct((R, L), F32)
    return pl.pallas_call(body, name=name, grid=(R // tr,), in_specs=[pl.BlockSpec((N_DEV, tr, L), lambda i: (0, i, 0)), state, state, state],
                          out_specs=(row, row, row, row), out_shape=(sh, sh, sh, sh), compiler_params=_params(("parallel",)))(recv, w, m, v)


GRAD_WIRE_DTYPE = BF16


class Prefetch:
    def __init__(self, local, order):
        self.local, self.todo, self.blocks, self.full = local, list(order), {}, {}
        self.where = {(n, i): ((names, i), k) for names, i in order for k, n in enumerate(names)}

    def _block(self, key):
        names, i = key
        return jnp.stack([self.local[n][i] for n in names]).astype(MXU_DTYPE)

    def _next(self):
        return self.todo.pop(0) if self.todo else None

    def mm(self, a, b, **kw):
        key = self._next()
        if key is None:
            return mm(a, b, **kw)
        out, self.blocks[key] = mm(a, b, gather=self._block(key), **kw)
        return out

    def ffn_hidden_fwd(self, h, wg, wu):
        key = self._next()
        if key is None:
            return ffn_hidden_fwd(h, wg, wu)
        g, u, a, self.blocks[key] = ffn_hidden_fwd(h, wg, wu, gather=self._block(key))
        return g, u, a

    def get(self, n, i):
        key, k = self.where[(n, i)]
        while key not in self.blocks:
            nxt = self.todo.pop(0)
            self.blocks[nxt] = all_gather(self._block(nxt), name="gather_" + nxt[0][0])
        if (n, i) not in self.full:
            self.full[(n, i)] = _from_shards(self.blocks[key][:, k], SHARD_AXIS[n] - 1)
        return self.full[(n, i)]


class Params:
    def __init__(self, **thunks):
        self.thunks, self.values = thunks, {}

    def __getitem__(self, k):
        if k not in self.values:
            self.values[k] = self.thunks[k]()
        return self.values[k]


class Wire:
    def __init__(self):
        self.waiting = []

    def post(self, slabs, deliver):
        self.waiting.append((slabs, deliver))

    def mm(self, a, b, **kw):
        if not self.waiting:
            return mm(a, b, **kw)
        slabs, deliver = self.waiting.pop(0)
        out, received = mm(a, b, send=slabs, **kw)
        deliver(received)
        return out

    def flush(self):
        for slabs, deliver in self.waiting:
            deliver(exchange(slabs, name="exchange_rest"))
        self.waiting = []


def _row_slabs(g):
    return g.reshape(N_DEV, g.shape[0] // N_DEV, g.shape[1])


def _col_slabs(g):
    return g.reshape(g.shape[0], N_DEV, g.shape[1] // N_DEV).transpose(1, 0, 2)


def ffn_fwd(x, p, pre):
    h = rms_fwd(x, p["norm"], name="ffn_rms_fwd")
    g, u, a = pre.ffn_hidden_fwd(h, p["w_gate"], p["w_up"])
    y = pre.mm(a, p["w_down"], add=x, scale=0.5, name="ffn_down")
    return y, (x, h, g, u, a)


def ffn_bwd(dy, saved, p, wire, post):
    x, h, g, u, a = saved
    gain, wg, wu, wd = p["norm"], p["w_gate"], p["w_up"], p["w_down"]
    dg, du = ffn_hidden_bwd(dy, wd, g, u, 0.5)
    post("w_down", _row_slabs(wire.mm(a, dy, ta=True, scale=0.5, out_dtype=GRAD_WIRE_DTYPE, name="ffn_dwd")))
    dh = wire.mm(dg, wg, tb=True, name="ffn_dh_gate")
    dh = wire.mm(du, wu, tb=True, add=dh, name="ffn_dh_up")
    post("w_gate", wire.mm(h, dg, ta=True, out_dtype=GRAD_WIRE_DTYPE, col_slabs=N_DEV, name="ffn_dwg"))
    post("w_up", wire.mm(h, du, ta=True, out_dtype=GRAD_WIRE_DTYPE, col_slabs=N_DEV, name="ffn_dwu"))
    dx, dgain = rms_bwd(x, gain, dh, dy, name="ffn_rms_bwd")
    post("norm", dgain[0])
    return dx


HYB_QKVZ = 4 * GDN_HEADS * GDN_D
HYB_AB = 2 * GDN_HEADS
HYB_U = len(POOL_WINDOWS) * POOL_GW


def _hyb_split_w_in(w_in):
    main = jnp.concatenate([w_in[:, :HYB_QKVZ], w_in[:, HYB_QKVZ + HYB_AB:]], axis=1)
    ab = jnp.pad(w_in[:, HYB_QKVZ:HYB_QKVZ + HYB_AB], ((0, 0), (0, LANES - HYB_AB)))
    return main, ab


def _gate_rows(t, n):
    return t.reshape(n, CHUNK, GDN_HEADS).transpose(0, 2, 1).reshape(n * GDN_HEADS, CHUNK)


def _gate_cols(r, n):
    return r.reshape(n, GDN_HEADS, CHUNK).transpose(0, 2, 1).reshape(n * CHUNK, GDN_HEADS)


def hyb_fwd(x, p, pre):
    T = x.shape[0]
    n = T // CHUNK
    h = rms_fwd(x, p["mix_norm"], name="mix_rms_fwd")
    w_main, w_ab = _hyb_split_w_in(p["w_in"])
    pm = pre.mm(h, w_main, name="hyb_in_main")
    pab = pre.mm(h, w_ab, name="hyb_in_gates")
    q, k, v = gdn_prep_fwd(pm, p["conv"])
    a_r = _gate_rows(pab[:, :GDN_HEADS], n)
    b_r = _gate_rows(pab[:, GDN_HEADS:HYB_AB], n)
    alog_c = jnp.tile(p["a_log"], n).reshape(n * GDN_HEADS, 1)
    dt_c = jnp.tile(p["dt_bias"], n).reshape(n * GDN_HEADS, 1)
    gc, beta = gdn_gates_fwd(a_r, b_r, alog_c, dt_c)
    gc3 = gc.reshape(n, GDN_HEADS, CHUNK)
    beta3 = beta.reshape(n, GDN_HEADS, CHUNK)
    o, states = gdr_fwd(q, k, v, gc3, beta3)
    og = gdn_gate_fwd(o, pm, p["out_norm"], z_block=3)
    pool = pool_fwd(pm, p["pool_w"], p["pool_scale"], u_block=4)
    mix = jnp.concatenate([og, pool], axis=1)
    y = pre.mm(mix, p["w_out"], add=x, name="hyb_out")
    return y, dict(x=x, h=h, pm=pm, q=q, k=k, v=v, a_r=a_r, b_r=b_r, alog_c=alog_c, dt_c=dt_c, gc3=gc3, beta3=beta3,
                   o=o, states=states, mix=mix)


def hyb_bwd(dy, s, p, wire):
    T = dy.shape[0]
    n = T // CHUNK
    half = GDN_HEADS * GDN_D
    w_main, w_ab = _hyb_split_w_in(p["w_in"])
    dmix = wire.mm(dy, p["w_out"], tb=True, name="hyb_dmix")
    dw_out = _row_slabs(wire.mm(s["mix"], dy, ta=True, out_dtype=GRAD_WIRE_DTYPE, name="hyb_dwout"))
    do, dz, dout_norm = gdn_gate_bwd(s["o"], s["pm"], p["out_norm"], dmix, z_block=3, dy_block=0)
    dq, dk, dv, dgc, dbeta = gdr_bwd(s["q"], s["k"], s["v"], s["gc3"], s["beta3"], s["states"], do)
    da_r, db_r, dalog, ddt = gdn_gates_bwd(s["a_r"], s["b_r"], s["alog_c"], s["dt_c"],
                                           dgc.reshape(n * GDN_HEADS, CHUNK), dbeta.reshape(n * GDN_HEADS, CHUNK))
    dpab = jnp.pad(jnp.concatenate([_gate_cols(da_r, n), _gate_cols(db_r, n)], axis=1), ((0, 0), (0, LANES - HYB_AB)))
    dyc, dconv = gdn_prep_bwd(s["pm"], p["conv"], dq, dk, dv)
    dqkv = conv_dx(dyc, p["conv"])
    dd, dpool_w, dpool_scale = pool_bwd_a(s["pm"], p["pool_w"], p["pool_scale"], dmix, u_block=4, dp_block=1)
    du = pool_bwd_b(dd)
    dpm = jnp.concatenate([dqkv, dz, du], axis=1)
    dh = wire.mm(dpm, w_main, tb=True, name="hyb_dh_main")
    dh = mm(dpab, w_ab, tb=True, add=dh, name="hyb_dh_gates")
    dw_main = wire.mm(s["h"], dpm, ta=True, name="hyb_dwin_main")
    dw_ab = mm(s["h"], dpab, ta=True, name="hyb_dwin_gates")
    dw_in = _col_slabs(jnp.concatenate([dw_main[:, :HYB_QKVZ], dw_ab[:, :HYB_AB], dw_main[:, HYB_QKVZ:]], axis=1)).astype(GRAD_WIRE_DTYPE)
    dx, dmix = rms_bwd(s["x"], p["mix_norm"], dh, dy, name="mix_rms_bwd")
    grads = dict(mix_norm=dmix[0], w_in=dw_in, conv=dconv, a_log=dalog[:, 0], dt_bias=ddt[:, 0], out_norm=dout_norm[0],
                 pool_w=dpool_w, pool_scale=dpool_scale[0], w_out=dw_out)
    return dx, grads


MLA_LAT = 2 * Q_LORA
ATTN_SCALE = QK_HEAD ** -0.5


def _mla_split_w_in(w_in):
    return w_in[:, :MLA_LAT], jnp.pad(w_in[:, MLA_LAT:], ((0, 0), (0, LANES - ROPE)))


def _to_heads(t, width):
    T = t.shape[0]
    return t.reshape(T, MLA_HEADS, width).transpose(1, 0, 2)


def _from_heads(t):
    H, T, W = t.shape
    return t.transpose(1, 0, 2).reshape(T, H * W)


def mla_fwd(x, p, rope, pre):
    T = x.shape[0]
    cos, sin, pmat = rope
    h = rms_fwd(x, p["mix_norm"], name="mix_rms_fwd")
    w_main, w_pe = _mla_split_w_in(p["w_in"])
    pm = pre.mm(h, w_main, name="mla_in_main")
    ppe = pre.mm(h, w_pe, name="mla_in_pe")
    qn = rms_fwd(pm, p["q_norm"], width=Q_LORA, col_block=0, name="mla_lat_rms_fwd")
    kvn = rms_fwd(pm, p["kv_norm"], width=Q_LORA, col_block=1, name="mla_lat_rms_fwd")
    q3 = _to_heads(pre.mm(qn, p["w_q_up"], name="mla_q_up"), QK_HEAD)
    kv3 = pre.mm(kvn, p["w_kv_up"], col_slabs=MLA_HEADS, name="mla_kv_up")
    kpe = jnp.broadcast_to(ppe[None, :, :ROPE], (MLA_HEADS, T, ROPE))
    k3 = jnp.concatenate([kv3[..., :NOPE], kpe], axis=-1)
    v3 = kv3[..., NOPE:].astype(MXU_DTYPE)
    qr = headnorm_rope_fwd(q3, p["q_head_norm"], cos, sin, pmat, out_scale=ATTN_SCALE)
    kr = headnorm_rope_fwd(k3, p["k_head_norm"], cos, sin, pmat)
    o3, lse = attn_fwd(qr, kr, v3)
    o = _from_heads(o3).astype(MXU_DTYPE)
    y = pre.mm(o, p["w_out"], add=x, name="mla_out")
    return y, dict(x=x, h=h, pm=pm, qn=qn, kvn=kvn, q3=q3, k3=k3, v3=v3, qr=qr, kr=kr, o3=o3, lse=lse, o=o)


def mla_bwd(dy, s, p, rope, wire):
    cos, sin, pmat = rope
    w_main, w_pe = _mla_split_w_in(p["w_in"])
    do3 = _to_heads(wire.mm(dy, p["w_out"], tb=True, name="mla_do"), V_HEAD)
    dw_out = _row_slabs(wire.mm(s["o"], dy, ta=True, out_dtype=GRAD_WIRE_DTYPE, name="mla_dwout"))
    dqr, dkr, dv3 = attn_bwd(s["qr"], s["kr"], s["v3"], s["o3"], s["lse"], do3)
    dq3, dqhn = headnorm_rope_bwd(s["q3"], p["q_head_norm"], cos, sin, pmat.T, dqr, out_scale=ATTN_SCALE)
    dk3, dkhn = headnorm_rope_bwd(s["k3"], p["k_head_norm"], cos, sin, pmat.T, dkr)
    dppe = jnp.pad(sum_heads(dk3)[:, NOPE:], ((0, 0), (0, LANES - ROPE)))
    dq = _from_heads(dq3)
    dkv = _from_heads(jnp.concatenate([dk3[..., :NOPE], dv3], axis=-1))
    dqn = mm(dq, p["w_q_up"], tb=True, name="mla_dqn")
    dkvn = wire.mm(dkv, p["w_kv_up"], tb=True, name="mla_dkvn")
    dw_q_up = mm(s["qn"], dq, ta=True, out_dtype=GRAD_WIRE_DTYPE, col_slabs=N_DEV, name="mla_dwq_up")
    dw_kv_up = wire.mm(s["kvn"], dkv, ta=True, out_dtype=GRAD_WIRE_DTYPE, col_slabs=N_DEV, name="mla_dwkv_up")
    dqlat, dq_norm = rms_bwd(s["pm"], p["q_norm"], dqn, width=Q_LORA, col_block=0, name="mla_lat_rms_bwd")
    dkvlat, dkv_norm = rms_bwd(s["pm"], p["kv_norm"], dkvn, width=Q_LORA, col_block=1, name="mla_lat_rms_bwd")
    dpm = jnp.concatenate([dqlat, dkvlat], axis=1)
    dh = mm(dpm, w_main, tb=True, name="mla_dh_main")
    dh = mm(dppe, w_pe, tb=True, add=dh, name="mla_dh_pe")
    dw_in = _row_slabs(jnp.concatenate([mm(s["h"], dpm, ta=True, name="mla_dwin_main"),
                                        mm(s["h"], dppe, ta=True, name="mla_dwin_pe")[:, :ROPE]], axis=1)).astype(GRAD_WIRE_DTYPE)
    dx, dmix = rms_bwd(s["x"], p["mix_norm"], dh, dy, name="mix_rms_bwd")
    grads = dict(mix_norm=dmix[0], w_in=dw_in, q_norm=dq_norm[0], kv_norm=dkv_norm[0], w_q_up=dw_q_up, w_kv_up=dw_kv_up,
                 q_head_norm=dqhn[0], k_head_norm=dkhn[0], w_out=dw_out)
    return dx, grads


WEIGHTS = ['ffn1_norm', 'ffn1_w_gate', 'ffn1_w_up', 'ffn1_w_down', 'mix_norm', 'ffn2_norm', 'ffn2_w_gate', 'ffn2_w_up',
           'ffn2_w_down', 'hyb_w_in', 'gdn_conv', 'gdn_a_log', 'gdn_dt_bias', 'gdn_out_norm', 'pool_w', 'pool_scale',
           'hyb_w_out', 'mla_w_in', 'mla_q_norm', 'mla_kv_norm', 'mla_w_q_up', 'mla_w_kv_up', 'mla_q_head_norm',
           'mla_k_head_norm', 'mla_w_out']
SHARD_AXIS = dict(ffn1_norm=None, ffn1_w_gate=2, ffn1_w_up=2, ffn1_w_down=1, mix_norm=None, ffn2_norm=None, ffn2_w_gate=2,
                  ffn2_w_up=2, ffn2_w_down=1, hyb_w_in=2, gdn_conv=2, gdn_a_log=None, gdn_dt_bias=None, gdn_out_norm=None,
                  pool_w=2, pool_scale=None, hyb_w_out=1, mla_w_in=1, mla_q_norm=1, mla_kv_norm=1, mla_w_q_up=2,
                  mla_w_kv_up=2, mla_q_head_norm=None, mla_k_head_norm=None, mla_w_out=1)
GATHER_BF16 = ['ffn1_w_gate', 'ffn1_w_up', 'ffn1_w_down', 'ffn2_w_gate', 'ffn2_w_up', 'ffn2_w_down', 'hyb_w_in', 'pool_w',
               'hyb_w_out', 'mla_w_in', 'mla_w_q_up', 'mla_w_kv_up', 'mla_w_out']
GATHER_F32 = ['gdn_conv', 'mla_q_norm', 'mla_kv_norm']
LARGE = ['ffn1_w_gate', 'ffn1_w_up', 'ffn1_w_down', 'ffn2_w_gate', 'ffn2_w_up', 'ffn2_w_down', 'hyb_w_in', 'hyb_w_out',
         'mla_w_in', 'mla_w_q_up', 'mla_w_kv_up', 'mla_w_out']
SMALL = [n for n in WEIGHTS if n not in LARGE]
SUBLANES = 8


def _pack(flat_list, lead=()):
    flat = jnp.concatenate(flat_list, axis=-1)
    rows = -(-flat.shape[-1] // (LANES * SUBLANES)) * SUBLANES
    flat = jnp.pad(flat, [(0, 0)] * len(lead) + [(0, rows * LANES - flat.shape[-1])])
    return flat.reshape(*lead, rows, LANES)


def _unpack(packed, shapes, lead=()):
    flat = packed.reshape(*lead, -1)
    out, off = [], 0
    for sh in shapes:
        n = math.prod(sh)
        out.append(flat[..., off:off + n].reshape(*lead, *sh))
        off += n
    return out


def _to_slabs(g, axis):
    if axis is None:
        return jnp.broadcast_to(g.reshape(1, -1), (N_DEV, g.size))
    sh = g.shape
    g = g.reshape(*sh[:axis], N_DEV, sh[axis] // N_DEV, *sh[axis + 1:])
    return jnp.moveaxis(g, axis, 0).reshape(N_DEV, -1)


def _from_shards(t, axis):
    t = jnp.moveaxis(t, 0, axis)
    sh = t.shape
    return t.reshape(*sh[:axis], sh[axis] * sh[axis + 1], *sh[axis + 2:])


def _gather_packed(local, names, dtype, name):
    got = all_gather(_pack([local[n].astype(dtype).reshape(-1) for n in names]), name=name)
    parts = _unpack(got, [local[n].shape for n in names], lead=(N_DEV,))
    return {n: _from_shards(t, SHARD_AXIS[n]) for n, t in zip(names, parts)}


def kernel(x, positions, ffn1_norm, ffn1_w_gate, ffn1_w_up, ffn1_w_down, mix_norm, ffn2_norm, ffn2_w_gate, ffn2_w_up, ffn2_w_down, hyb_w_in, gdn_conv, gdn_a_log, gdn_dt_bias, gdn_out_norm, pool_w, pool_scale, hyb_w_out, mla_w_in, mla_q_norm, mla_kv_norm, mla_w_q_up, mla_w_kv_up, mla_q_head_norm, mla_k_head_norm, mla_w_out, loss_target, m_ffn1_norm, m_ffn1_w_gate, m_ffn1_w_up, m_ffn1_w_down, m_mix_norm, m_ffn2_norm, m_ffn2_w_gate, m_ffn2_w_up, m_ffn2_w_down, m_hyb_w_in, m_gdn_conv, m_gdn_a_log, m_gdn_dt_bias, m_gdn_out_norm, m_pool_w, m_pool_scale, m_hyb_w_out, m_mla_w_in, m_mla_q_norm, m_mla_kv_norm, m_mla_w_q_up, m_mla_w_kv_up, m_mla_q_head_norm, m_mla_k_head_norm, m_mla_w_out, v_ffn1_norm, v_ffn1_w_gate, v_ffn1_w_up, v_ffn1_w_down, v_mix_norm, v_ffn2_norm, v_ffn2_w_gate, v_ffn2_w_up, v_ffn2_w_down, v_hyb_w_in, v_gdn_conv, v_gdn_a_log, v_gdn_dt_bias, v_gdn_out_norm, v_pool_w, v_pool_scale, v_hyb_w_out, v_mla_w_in, v_mla_q_norm, v_mla_kv_norm, v_mla_w_q_up, v_mla_w_kv_up, v_mla_q_head_norm, v_mla_k_head_norm, v_mla_w_out):
    given = dict(locals())
    local = {n: given[n] for n in WEIGHTS}
    depth = ffn1_norm.shape[0]
    xs = x[0]
    T = xs.shape[0]

    full = dict(local)
    full.update(_gather_packed(local, GATHER_F32, F32, "gather_f32"))

    order = []
    for layer in range(depth):
        i = layer // 2
        order += [(("ffn1_w_gate", "ffn1_w_up"), layer), (("ffn1_w_down",), layer)]
        if layer % 2 == 0:
            order += [(("hyb_w_in",), i), (("pool_w",), i), (("hyb_w_out",), i)]
        else:
            order += [(("mla_w_in",), i), (("mla_w_q_up",), i), (("mla_w_kv_up",), i), (("mla_w_out",), i)]
        order += [(("ffn2_w_gate", "ffn2_w_up"), layer), (("ffn2_w_down",), layer)]
    assert sorted({n for names, _ in order for n in names}) == sorted(GATHER_BF16)
    pre = Prefetch(local, order)

    freq, pmat = _rope_consts()
    cos, sin = rope_tables(positions[0].astype(F32).reshape(T, 1), freq)
    rope = (cos, sin, pmat)

    def small(n, i):
        return lambda: full[n][i]

    def large(n, i):
        return lambda: pre.get(n, i)

    def mixer_params(layer):
        i = layer // 2
        if layer % 2 == 0:
            return Params(mix_norm=small("mix_norm", layer), w_in=large("hyb_w_in", i), conv=small("gdn_conv", i), a_log=small("gdn_a_log", i),
                          dt_bias=small("gdn_dt_bias", i), out_norm=small("gdn_out_norm", i), pool_w=large("pool_w", i),
                          pool_scale=small("pool_scale", i), w_out=large("hyb_w_out", i))
        return Params(mix_norm=small("mix_norm", layer), w_in=large("mla_w_in", i), q_norm=small("mla_q_norm", i), kv_norm=small("mla_kv_norm", i),
                      w_q_up=large("mla_w_q_up", i), w_kv_up=large("mla_w_kv_up", i), q_head_norm=small("mla_q_head_norm", i),
                      k_head_norm=small("mla_k_head_norm", i), w_out=large("mla_w_out", i))

    def ffn_params(which, layer):
        return Params(norm=small(which + "_norm", layer), w_gate=large(which + "_w_gate", layer), w_up=large(which + "_w_up", layer),
                      w_down=large(which + "_w_down", layer))

    saved = []
    cur = xs
    for layer in range(depth):
        cur, s1 = ffn_fwd(cur, ffn_params("ffn1", layer), pre)
        if layer % 2 == 0:
            cur, sm = hyb_fwd(cur, mixer_params(layer), pre)
        else:
            cur, sm = mla_fwd(cur, mixer_params(layer), rope, pre)
        cur, s2 = ffn_fwd(cur, ffn_params("ffn2", layer), pre)
        saved.append((s1, sm, s2))

    dcur, loss_local = loss_head(cur, loss_target[0])
    loss = lax.psum(loss_local, ("x", "y", "c"))

    hyb_names = dict(hyb_w_in="w_in", gdn_conv="conv", gdn_a_log="a_log", gdn_dt_bias="dt_bias", gdn_out_norm="out_norm",
                     pool_w="pool_w", pool_scale="pool_scale", hyb_w_out="w_out")
    mla_names = dict(mla_w_in="w_in", mla_q_norm="q_norm", mla_kv_norm="kv_norm", mla_w_q_up="w_q_up", mla_w_kv_up="w_kv_up",
                     mla_q_head_norm="q_head_norm", mla_k_head_norm="k_head_norm", mla_w_out="w_out")
    per_layer = {n: [None] * local[n].shape[0] for n in WEIGHTS}

    wire = Wire()

    def put(n, idx, g):
        if n not in LARGE:
            per_layer[n][idx] = g
            return

        def deliver(received):
            per_layer[n][idx] = adamw(received, local[n], given["m_" + n], given["v_" + n], layer=idx, name="adamw_" + n)

        wire.post(g, deliver)

    for layer in reversed(range(depth)):
        s1, sm, s2 = saved[layer]
        dcur = ffn_bwd(dcur, s2, ffn_params("ffn2", layer), wire, lambda n, g, layer=layer: put("ffn2_" + n, layer, g))
        if layer % 2 == 0:
            dcur, mg = hyb_bwd(dcur, sm, mixer_params(layer), wire)
            names = hyb_names
        else:
            dcur, mg = mla_bwd(dcur, sm, mixer_params(layer), rope, wire)
            names = mla_names
        put("mix_norm", layer, mg["mix_norm"])
        for n, key in names.items():
            put(n, layer // 2, mg[key])
        dcur = ffn_bwd(dcur, s1, ffn_params("ffn1", layer), wire, lambda n, g, layer=layer: put("ffn1_" + n, layer, g))
    wire.flush()
    grad_x = dcur[None]

    send = _pack([_to_slabs(jnp.stack(per_layer[n]), SHARD_AXIS[n]) for n in SMALL], lead=(N_DEV,))
    state = [_pack([src[n].reshape(-1) for n in SMALL]) for src in
             (local, {n: given["m_" + n] for n in SMALL}, {n: given["v_" + n] for n in SMALL})]
    small = [_unpack(o, [local[n].shape for n in SMALL]) for o in adamw(exchange(send, name="exchange_small"), *state, name="adamw_small")]

    outs = []
    for j in range(4):
        for n in WEIGHTS:
            outs.append(jnp.stack([t[j] for t in per_layer[n]]) if n in LARGE else small[j][SMALL.index(n)])
    return (loss, grad_x, *outs)
```
